```python
import math
import jax, jax.numpy as jnp
from jax import lax
import numpy as np

D_MODEL = 1024
BATCH = 8
SEQ = 4096
DEPTH = 1

N_META = 16
D_MIX = D_MODEL
HEAD_DIM = 64
D_FOX = D_MIX // 2
D_RWKV = D_MIX - D_FOX
H_FOX = D_FOX // HEAD_DIM
H_RWKV = D_RWKV // HEAD_DIM
RANK_W = 64
RANK_A = 64
Q_BLOCK = 128
NORM_EPS = 1e-6
GN_EPS = 64e-5
NEG_INF = -1e30

FOX_SIZES = (D_FOX, D_FOX, D_FOX, H_FOX, D_FOX)
RWKV_SIZES = (D_RWKV, D_RWKV, D_RWKV, RANK_W, RANK_A, D_RWKV)
D_FOX_IN = sum(FOX_SIZES)
D_RWKV_IN = sum(RWKV_SIZES)
D_IN = D_FOX_IN + D_RWKV_IN
D_SHIFT = 3 * D_RWKV + RANK_W + RANK_A

kernel_name = "hymba_fox_rwkv7_hybrid_block"


def _split(p, sizes):
    offs = [int(o) for o in np.cumsum(sizes)[:-1]]
    return jnp.split(p, offs, axis=-1)


def rmsnorm(x, w):
    xf = x.astype(jnp.float32)
    y = xf * lax.rsqrt(jnp.mean(xf * xf, axis=-1, keepdims=True) + NORM_EPS)
    return (y * w.astype(jnp.float32)).astype(x.dtype)


def _heads(t, n_heads):
    b, l, _ = t.shape
    return t.reshape(b, l, n_heads, -1).transpose(0, 2, 1, 3)


def _fox_block(q, k, v, cq, ck, q_start):
    qb, kb = q.shape[2], k.shape[2]
    s = jnp.einsum('bhqd,bhkd->bhqk', q, k).astype(jnp.float32) / math.sqrt(HEAD_DIM)
    s = s + cq[..., :, None] - ck[..., None, :]
    q_pos = q_start + jnp.arange(qb)
    k_pos = jnp.arange(kb)
    s = jnp.where(k_pos[None, :] <= q_pos[:, None], s, NEG_INF)
    p = jax.nn.softmax(s, axis=-1)
    return jnp.einsum('bhqk,bhkd->bhqd', p, v.astype(jnp.float32))


def fox_branch(p, b_f):
    b, l, _ = p.shape
    q, k, v, fl, z = _split(p, FOX_SIZES)
    q, k, v = _heads(q, H_FOX), _heads(k, H_FOX), _heads(v, H_FOX)
    log_f = jax.nn.log_sigmoid(fl.astype(jnp.float32) + b_f.astype(jnp.float32))
    c = jnp.cumsum(log_f, axis=1).transpose(0, 2, 1)
    n_real = l - N_META
    bounds = [(0, N_META)] + [(N_META + i * Q_BLOCK, min(N_META + (i + 1) * Q_BLOCK, l))
                              for i in range(-(-n_real // Q_BLOCK))]
    outs = [_fox_block(q[:, :, s0:s1], k[:, :, :s1], v[:, :, :s1], c[:, :, s0:s1], c[:, :, :s1], s0)
            for (s0, s1) in bounds]
    o = jnp.concatenate(outs, axis=2).transpose(0, 2, 1, 3).reshape(b, l, D_FOX)
    return (o * jax.nn.silu(z.astype(jnp.float32))).astype(p.dtype)


def _rwkv7_step(S, inp):
    r_t, w_t, k_t, v_t, a_t, b_t = inp
    sa = jnp.einsum('bhij,bhj->bhi', S, a_t)
    S = S * w_t[:, :, None, :] + sa[..., None] * b_t[:, :, None, :] + v_t[..., None] * k_t[:, :, None, :]
    y = jnp.einsum('bhij,bhj->bhi', S, r_t)
    return S, y


def rwkv_branch(p, mu, w0, w_up, a0, a_up, k_k, k_a, r_k, gn_w, gn_b):
    b, l, _ = p.shape
    f32 = jnp.float32
    ps, z = p[..., :D_SHIFT], p[..., D_SHIFT:]
    prev = jnp.pad(ps, ((0, 0), (1, 0), (0, 0)))[:, :-1]
    ps = (ps + mu * (prev - ps)).astype(f32)
    r, k, v, wd, ad = _split(ps, (D_RWKV, D_RWKV, D_RWKV, RANK_W, RANK_A))
    w = -jax.nn.softplus(-(w0.astype(f32) + jnp.tanh(wd) @ w_up.astype(f32))) - 0.5
    decay = jnp.exp(-jnp.exp(w))
    a = jax.nn.sigmoid(a0.astype(f32) + ad @ a_up.astype(f32))
    kk = (k * k_k.astype(f32)).reshape(b, l, H_RWKV, HEAD_DIM)
    kk = kk * lax.rsqrt(jnp.sum(kk * kk, axis=-1, keepdims=True) + 1e-12)
    k = k * (1.0 + (a - 1.0) * k_a.astype(f32))
    hd = lambda t: t.reshape(b, l, H_RWKV, HEAD_DIM)
    r, k, v, decay, a = hd(r), hd(k), hd(v), hd(decay), hd(a)
    a_vec, b_vec = -kk, kk * a
    tm = lambda t: jnp.moveaxis(t, 1, 0)
    S0 = jnp.zeros((b, H_RWKV, HEAD_DIM, HEAD_DIM), f32)
    _, y = lax.scan(_rwkv7_step, S0, (tm(r), tm(decay), tm(k), tm(v), tm(a_vec), tm(b_vec)))
    y = jnp.moveaxis(y, 0, 1)
    mean = jnp.mean(y, axis=-1, keepdims=True)
    var = jnp.mean(jnp.square(y - mean), axis=-1, keepdims=True)
    y = ((y - mean) * lax.rsqrt(var + GN_EPS)).reshape(b, l, D_RWKV)
    y = y * gn_w.astype(f32) + gn_b.astype(f32)
    bonus = jnp.sum(r * k * r_k.astype(f32), axis=-1, keepdims=True) * v
    y = y + bonus.reshape(b, l, D_RWKV)
    return (y * jax.nn.silu(z.astype(f32))).astype(p.dtype)


def setup_inputs(seed: int = 0) -> dict:
    key = jax.random.key(seed)
    ks = jax.random.split(key, 20)
    nrm = jax.random.normal
    x = nrm(ks[0], (BATCH, SEQ, D_MODEL), jnp.float32)
    meta = nrm(ks[1], (N_META, D_MODEL), jnp.float32)
    norm_w = 1.0 + 0.1 * nrm(ks[2], (DEPTH, D_MODEL), jnp.float32)
    w_in = nrm(ks[3], (DEPTH, D_MODEL, D_IN), jnp.float32) * D_MODEL ** -0.5
    b_f = jax.random.uniform(ks[4], (DEPTH, H_FOX), jnp.float32, 1.0, 5.0)
    mu_shift = jax.random.uniform(ks[5], (DEPTH, D_SHIFT), jnp.float32)
    w0 = jax.random.uniform(ks[6], (DEPTH, D_RWKV), jnp.float32, -7.0, -1.0)
    w_up = 0.5 * nrm(ks[7], (DEPTH, RANK_W, D_RWKV), jnp.float32) * RANK_W ** -0.5
    a0 = 0.5 * nrm(ks[8], (DEPTH, D_RWKV), jnp.float32)
    a_up = 0.5 * nrm(ks[9], (DEPTH, RANK_A, D_RWKV), jnp.float32) * RANK_A ** -0.5
    k_k = 0.85 + 0.05 * nrm(ks[10], (DEPTH, D_RWKV), jnp.float32)
    k_a = 1.0 + 0.05 * nrm(ks[11], (DEPTH, D_RWKV), jnp.float32)
    r_k = 0.1 * nrm(ks[12], (DEPTH, H_RWKV, HEAD_DIM), jnp.float32)
    gn_w = 1.0 + 0.1 * nrm(ks[13], (DEPTH, D_RWKV), jnp.float32)
    gn_b = 0.02 * nrm(ks[14], (DEPTH, D_RWKV), jnp.float32)
    w_out = nrm(ks[15], (DEPTH, D_MIX, D_MODEL), jnp.float32) * D_MIX ** -0.5
    final_norm_w = 1.0 + 0.1 * nrm(ks[16], (D_MODEL,), jnp.float32)
    return {"x": x, "meta": meta, "norm_w": norm_w, "w_in": w_in, "b_f": b_f,
            "mu_shift": mu_shift, "w0": w0, "w_up": w_up, "a0": a0, "a_up": a_up,
            "k_k": k_k, "k_a": k_a, "r_k": r_k, "gn_w": gn_w, "gn_b": gn_b,
            "w_out": w_out, "final_norm_w": final_norm_w}


def reference(x, meta, norm_w, w_in, b_f, mu_shift, w0, w_up, a0, a_up, k_k, k_a, r_k,
              gn_w, gn_b, w_out, final_norm_w):
    b = x.shape[0]
    h = jnp.concatenate([jnp.broadcast_to(meta[None].astype(x.dtype), (b, N_META, x.shape[-1])), x], axis=1)
    for l in range(DEPTH):
        u = rmsnorm(h, norm_w[l])
        p = u @ w_in[l]
        p_fox, p_rwkv = p[..., :D_FOX_IN], p[..., D_FOX_IN:]
        y_fox = fox_branch(p_fox, b_f[l])
        y_rwkv = rwkv_branch(p_rwkv, mu_shift[l], w0[l], w_up[l], a0[l], a_up[l],
                             k_k[l], k_a[l], r_k[l], gn_w[l], gn_b[l])
        h = h + jnp.concatenate([y_fox, y_rwkv], axis=-1) @ w_out[l]
    h = rmsnorm(h, final_norm_w)
    return h[:, N_META:]
```

```python
import functools

import numpy as np
import jax
import jax.numpy as jnp
from jax import lax
from jax.experimental import pallas as pl
from jax.experimental.pallas import tpu as pltpu

F32 = jnp.float32
BF16 = jnp.bfloat16

D_MODEL = 1024
N_META = 16
HEAD_DIM = 64
N_HEADS = 8
D_BRANCH = N_HEADS * HEAD_DIM
N_PAIRS = N_HEADS // 2
RANK = 64
NORM_EPS = 1e-6
GN_EPS = 64e-5
KK_EPS = 1e-12
NEG = -1e30

LANES = 128
PREFIX_ROWS = 128
CHUNK = 64
PAIR = 2 * HEAD_DIM
VMEM_LIMIT = 56 * 1024 * 1024

N_WIDE = 8
COL_WA = N_WIDE * D_BRANCH
COL_FL = COL_WA + LANES
N_COLS = COL_FL + LANES

N_CPARTS = 3


def _dot(a, b):
    return jnp.dot(a, b, preferred_element_type=F32)


def _dot_hi(a, b):
    return jnp.dot(a, b, preferred_element_type=F32, precision=lax.Precision.HIGHEST)


def _dot_nt(a, b):
    return lax.dot_general(a, b, (((1,), (1,)), ((), ())), preferred_element_type=F32)


def _dot_tn(a, b):
    return lax.dot_general(a, b, (((0,), (0,)), ((), ())), preferred_element_type=F32)


def _softplus(x):
    return jnp.maximum(x, 0.0) + jnp.log(1.0 + jnp.exp(-jnp.abs(x)))


def _sigmoid(x):
    return 1.0 / (1.0 + jnp.exp(-x))


def _params(sem):
    return pltpu.CompilerParams(dimension_semantics=sem, vmem_limit_bytes=VMEM_LIMIT)


def _in_proj_kernel(x_ref, nw_ref, w_ref, *out_refs):
    x = x_ref[...]
    u = x * lax.rsqrt(jnp.mean(x * x, axis=-1, keepdims=True) + NORM_EPS) * nw_ref[...]
    ub = u.astype(BF16)
    for g in range(N_WIDE):
        o = out_refs[g]
        o[...] = _dot(ub, w_ref[:, g * D_BRANCH:(g + 1) * D_BRANCH]).astype(o.dtype)
    out_refs[N_WIDE][...] = _dot(ub, w_ref[:, COL_WA:COL_WA + LANES])
    out_refs[N_WIDE + 1][...] = _dot(ub, w_ref[:, COL_FL:COL_FL + LANES])


def _in_proj(rows, norm_w, w_all, tm):
    n = rows.shape[0]
    wide_dtypes = (BF16, BF16, BF16, F32, F32, F32, F32, F32)
    out_shape = [jax.ShapeDtypeStruct((n, D_BRANCH), dt) for dt in wide_dtypes]
    out_shape += [jax.ShapeDtypeStruct((n, LANES), F32)] * 2
    out_specs = [pl.BlockSpec((tm, D_BRANCH), lambda i: (i, 0))] * N_WIDE
    out_specs += [pl.BlockSpec((tm, LANES), lambda i: (i, 0))] * 2
    return pl.pallas_call(
        _in_proj_kernel,
        grid=(n // tm,),
        in_specs=[pl.BlockSpec((tm, D_MODEL), lambda i: (i, 0)),
                  pl.BlockSpec((1, D_MODEL), lambda i: (0, 0)),
                  pl.BlockSpec((D_MODEL, N_COLS), lambda i: (0, 0))],
        out_specs=out_specs,
        out_shape=out_shape,
        compiler_params=_params(("parallel",)),
        name="in_proj",
    )(rows, norm_w, w_all)


def _select_matrices():
    pq = np.zeros((N_HEADS, 2 * LANES, LANES), np.float32)
    pk = np.zeros((N_HEADS, 2 * LANES, LANES), np.float32)
    for h in range(N_HEADS):
        base = (h % 2) * HEAD_DIM
        for d in range(HEAD_DIM):
            pq[h, base + d, d] = 0.125
            pk[h, base + d, d] = 1.0
        for part in range(N_CPARTS):
            pq[h, LANES + part * N_HEADS + h, HEAD_DIM + part] = 1.0
            pq[h, LANES + N_CPARTS * N_HEADS + h, HEAD_DIM + N_CPARTS + part] = 1.0
            pk[h, LANES + N_CPARTS * N_HEADS + h, HEAD_DIM + part] = 1.0
            pk[h, LANES + part * N_HEADS + h, HEAD_DIM + N_CPARTS + part] = -1.0
    return jnp.asarray(pq, BF16), jnp.asarray(pk, BF16)


def _fox_prep_kernel(q_ref, k_ref, v_ref, fl_ref, bf_ref, c0_ref, pq_ref, pk_ref,
                     qo_ref, ko_ref, vo_ref, cl_ref, carry_ref, *, tr, n_pad):
    i = pl.program_id(1)

    @pl.when(i == 0)
    def _():
        carry_ref[...] = c0_ref[...]

    x = fl_ref[0] + bf_ref[...]
    logf = jnp.minimum(x, 0.0) - jnp.log(1.0 + jnp.exp(-jnp.abs(x)))
    lane = lax.broadcasted_iota(jnp.int32, (tr, LANES), 1)
    row = lax.broadcasted_iota(jnp.int32, (tr, LANES), 0) + i * tr
    valid = lane < N_HEADS
    if n_pad:
        valid = valid & (row >= n_pad)
    logf = jnp.where(valid, logf, 0.0)
    r2 = lax.broadcasted_iota(jnp.int32, (tr, tr), 0)
    c2 = lax.broadcasted_iota(jnp.int32, (tr, tr), 1)
    tri = jnp.where(r2 >= c2, 1.0, 0.0).astype(F32)
    cum = _dot_hi(tri, logf) + carry_ref[...]
    carry_ref[...] = cum[tr - 1:tr, :]

    @pl.when(i == pl.num_programs(1) - 1)
    def _():
        cl_ref[0] = cum[tr - 1:tr, :]

    p1 = cum.astype(BF16).astype(F32)
    rem = cum - p1
    p2 = rem.astype(BF16).astype(F32)
    p3 = (rem - p2).astype(BF16).astype(F32)
    ones = jnp.where((lane >= N_CPARTS * N_HEADS) & (lane < (N_CPARTS + 1) * N_HEADS), 1.0, 0.0)
    cbits = p1 + pltpu.roll(p2, N_HEADS, 1) + pltpu.roll(p3, 2 * N_HEADS, 1) + ones
    cbits_k = cbits
    if n_pad:
        cbits_k = jnp.where((row < n_pad) & (lane < N_HEADS), -NEG, cbits)
    cq = cbits.astype(BF16)
    ck = cbits_k.astype(BF16)
    lane_lo = lane < HEAD_DIM
    for h in range(N_HEADS):
        sl = slice((h // 2) * PAIR, (h // 2 + 1) * PAIR)
        xq = jnp.concatenate([q_ref[0, :, sl], cq], axis=1)
        xk = jnp.concatenate([k_ref[0, :, sl], ck], axis=1)
        qo_ref[0, h] = _dot(xq, pq_ref[h]).astype(BF16)
        ko_ref[0, h] = _dot(xk, pk_ref[h]).astype(BF16)
        vp = v_ref[0, :, sl]
        keep = lane_lo if h % 2 == 0 else jnp.logical_not(lane_lo)
        vo_ref[0, h] = jnp.where(keep, vp, jnp.zeros_like(vp))


def _fox_prep(q, k, v, fl, bf_pad, c0, pq, pk, tr, n_pad):
    b, l, _ = q.shape
    wide = pl.BlockSpec((1, tr, D_BRANCH), lambda bi, i: (bi, i, 0))
    head_out = pl.BlockSpec((1, N_HEADS, tr, LANES), lambda bi, i: (bi, 0, i, 0))
    row128 = pl.BlockSpec((1, LANES), lambda bi, i: (0, 0))
    sel = pl.BlockSpec((N_HEADS, 2 * LANES, LANES), lambda bi, i: (0, 0, 0))
    return pl.pallas_call(
        functools.partial(_fox_prep_kernel, tr=tr, n_pad=n_pad),
        grid=(b, l // tr),
        in_specs=[wide, wide, wide,
                  pl.BlockSpec((1, tr, LANES), lambda bi, i: (bi, i, 0)),
                  row128, row128, sel, sel],
        out_specs=[head_out, head_out, head_out,
                   pl.BlockSpec((1, 1, LANES), lambda bi, i: (bi, 0, 0))],
        out_shape=[jax.ShapeDtypeStruct((b, N_HEADS, l, LANES), BF16)] * 3
                  + [jax.ShapeDtypeStruct((b, 1, LANES), F32)],
        scratch_shapes=[pltpu.VMEM((1, LANES), F32)],
        compiler_params=_params(("parallel", "arbitrary")),
        name="fox_prep",
    )(q, k, v, fl, bf_pad, c0, pq, pk)


def _attn_kernel(q_ref, kpre_ref, vpre_ref, k_ref, v_ref, o_ref, *, tq):
    qi = pl.program_id(2)
    row = lax.broadcasted_iota(jnp.int32, (tq, tq), 0)
    col = lax.broadcasted_iota(jnp.int32, (tq, tq), 1)
    causal = col <= row
    out = None
    for h in range(2):
        q = q_ref[0, h]

        def step(carry, kc, vc, mask, q=q):
            m, l, acc = carry
            s = _dot_nt(q, kc)
            if mask is not None:
                s = jnp.where(mask, s, NEG)
            m_new = jnp.maximum(m, jnp.max(s, axis=-1, keepdims=True))
            alpha = jnp.exp(m - m_new)
            p = jnp.exp(s - m_new)
            l = alpha * l + jnp.sum(p, axis=-1, keepdims=True)
            acc = alpha * acc + _dot(p.astype(BF16), vc)
            return m_new, l, acc

        carry = (jnp.full((tq, 1), NEG, F32), jnp.zeros((tq, 1), F32), jnp.zeros((tq, LANES), F32))
        carry = step(carry, kpre_ref[0, h], vpre_ref[0, h], None)

        def body(j, carry, h=h, step=step):
            off = pl.multiple_of(j * tq, tq)
            return step(carry, k_ref[0, h, pl.ds(off, tq), :], v_ref[0, h, pl.ds(off, tq), :], None)

        carry = lax.fori_loop(0, qi, body, carry)
        off = pl.multiple_of(qi * tq, tq)
        m, l, acc = step(carry, k_ref[0, h, pl.ds(off, tq), :], v_ref[0, h, pl.ds(off, tq), :], causal)
        o_h = acc / l
        out = o_h if out is None else out + o_h
    o_ref[0] = out


def _fox_attn(qp, kp, vp, kpre, vpre, tq):
    b, _, l, _ = qp.shape
    pre = pl.BlockSpec((1, 2, PREFIX_ROWS, LANES), lambda bi, p, qi: (0, p, 0, 0))
    full = pl.BlockSpec((1, 2, l, LANES), lambda bi, p, qi: (bi, p, 0, 0))
    return pl.pallas_call(
        functools.partial(_attn_kernel, tq=tq),
        grid=(b, N_PAIRS, l // tq),
        in_specs=[pl.BlockSpec((1, 2, tq, LANES), lambda bi, p, qi: (bi, p, qi, 0)),
                  pre, pre, full, full],
        out_specs=pl.BlockSpec((1, tq, PAIR), lambda bi, p, qi: (bi, qi, p)),
        out_shape=jax.ShapeDtypeStruct((b, l, D_BRANCH), F32),
        compiler_params=_params(("parallel", "parallel", "parallel")),
        name="fox_attn",
    )(qp, kpre, vpre, kp, vp)


def _split2(x):
    hi = x.astype(BF16)
    lo = (x - hi.astype(F32)).astype(BF16)
    return hi, lo


def _rwkv_kernel(r_ref, k_ref, v_ref, z_ref, wa_ref,
                 mu_r_ref, mu_k_ref, mu_v_ref, mu_wa_ref, w0_ref, wup_ref, a0_ref, aup_ref,
                 kk_ref, ka_ref, rk_ref, gnw_ref, gnb_ref, ones_ref,
                 z0_ref, pr_ref, pk_ref, pv_ref, pwa_ref,
                 y_ref, zf_ref, lr_ref, lk_ref, lv_ref, lwa_ref,
                 state_ref, sr_ref, sk_ref, sv_ref, swa_ref, yacc_ref, *, tr):
    t = pl.program_id(1)
    last = t == pl.num_programs(1) - 1

    @pl.when(t == 0)
    def _():
        state_ref[...] = z0_ref[...]
        sr_ref[...] = pr_ref[...]
        sk_ref[...] = pk_ref[...]
        sv_ref[...] = pv_ref[...]
        swa_ref[...] = pwa_ref[...]

    def shifted(cur, prev_ref, mu_ref):
        first = lax.broadcasted_iota(jnp.int32, cur.shape, 0) == 0
        prev = jnp.where(first, prev_ref[...], pltpu.roll(cur, 1, 0))
        prev_ref[...] = cur[tr - 1:tr, :]
        return cur + mu_ref[...] * (prev - cur)

    xr = shifted(r_ref[0], sr_ref, mu_r_ref)
    xk = shifted(k_ref[0], sk_ref, mu_k_ref)
    xv = shifted(v_ref[0], sv_ref, mu_v_ref)
    xwa = shifted(wa_ref[0], swa_ref, mu_wa_ref)

    @pl.when(last)
    def _():
        lr_ref[0] = sr_ref[...]
        lk_ref[0] = sk_ref[...]
        lv_ref[0] = sv_ref[...]
        lwa_ref[0] = swa_ref[...]

    ones_bd = ones_ref[...]

    def seg_sum(x):
        hi, lo = _split2(x)
        return _dot(hi, ones_bd) + _dot(lo, ones_bd)

    w_lin = w0_ref[...] + _dot_hi(jnp.tanh(xwa), wup_ref[...])
    a_lin = a0_ref[...] + _dot_hi(xwa, aup_ref[...])
    w = -_softplus(-w_lin) - 0.5
    ld = -jnp.exp(w)
    a = _sigmoid(a_lin)
    kk = xk * kk_ref[...]
    kk = kk * lax.rsqrt(seg_sum(kk * kk) + KK_EPS)
    kmod = xk * (1.0 + (a - 1.0) * ka_ref[...])
    bvec = kk * a

    r2 = lax.broadcasted_iota(jnp.int32, (tr, tr), 0)
    c2 = lax.broadcasted_iota(jnp.int32, (tr, tr), 1)
    tri = jnp.where((r2 >= c2) & (r2 // CHUNK == c2 // CHUNK), 1.0, 0.0).astype(F32)
    lw = _dot_hi(tri, ld)
    w_inv = jnp.exp(-lw)
    rt = xr * jnp.exp(lw)
    at = -kk * jnp.exp(lw - ld)
    bt = bvec * w_inv
    kt = kmod * w_inv

    lane = lax.broadcasted_iota(jnp.int32, (CHUNK, PAIR), 1)
    head_a = lane < HEAD_DIM
    tau_r = lax.broadcasted_iota(jnp.int32, (PAIR, PAIR), 0)
    tau_c = lax.broadcasted_iota(jnp.int32, (PAIR, PAIR), 1)
    strict = (tau_r % CHUNK) > (tau_c % CHUNK)
    incl = (tau_r % CHUNK) >= (tau_c % CHUNK)
    eye = tau_r == tau_c
    eye_f = jnp.where(eye, 1.0, 0.0).astype(F32)

    def stack(x):
        return jnp.concatenate([jnp.where(head_a, x, 0.0), jnp.where(head_a, 0.0, x)], axis=0)

    for c in range(tr // CHUNK):
        rows = slice(c * CHUNK, (c + 1) * CHUNK)
        lw_end = lw[(c + 1) * CHUNK - 1:(c + 1) * CHUNK, :]
        w_end = jnp.exp(lw_end)
        for p in range(N_PAIRS):
            cols = slice(p * PAIR, (p + 1) * PAIR)
            wc = w_end[:, cols]
            rs_f = stack(rt[rows, cols])
            a_s = stack(at[rows, cols]).astype(BF16)
            b_s = stack(bt[rows, cols]).astype(BF16)
            k_s = stack(kt[rows, cols]).astype(BF16)
            v_s = stack(xv[rows, cols]).astype(BF16)
            bh_s = stack(bt[rows, cols] * wc).astype(BF16)
            kh_s = stack(kt[rows, cols] * wc).astype(BF16)
            r_s = rs_f.astype(BF16)

            a_ab = jnp.where(strict, _dot_nt(a_s, b_s), 0.0)
            a_ak = jnp.where(strict, _dot_nt(a_s, k_s), 0.0).astype(BF16)
            a_rb = jnp.where(incl, _dot_nt(r_s, b_s), 0.0).astype(BF16)
            a_rk = jnp.where(incl, _dot_nt(r_s, k_s), 0.0).astype(BF16)

            pw = a_ab
            tinv = eye_f + a_ab
            for _ in range(5):
                pb = pw.astype(BF16)
                pw = _dot(pb, pb)
                tinv = tinv + _dot(tinv.astype(BF16), pw.astype(BF16))
            tb = tinv.astype(BF16)

            ap = _dot(tb, a_s).astype(BF16)
            vp = _dot(tb, _dot(a_ak, v_s).astype(BF16)).astype(BF16)
            m_mat = jnp.where(eye, wc, 0.0) + _dot_tn(bh_s, ap)
            g_mat = _dot_tn(bh_s, vp) + _dot_tn(kh_s, v_s)
            rp = rs_f + _dot(a_rb, ap)
            y0 = _dot(a_rb, vp) + _dot(a_rk, v_s)

            zb = state_ref[p].astype(BF16)
            ys = _dot(rp.astype(BF16), zb) + y0
            state_ref[p] = _dot(m_mat.astype(BF16), zb) + g_mat
            yacc_ref[rows, cols] = ys[:CHUNK] + ys[CHUNK:]

    @pl.when(last)
    def _():
        zf_ref[0] = state_ref[...]

    y = yacc_ref[...]
    inv_n = 1.0 / HEAD_DIM
    mean = seg_sum(y) * inv_n
    d = y - mean
    var = seg_sum(d * d) * inv_n
    yn = d * lax.rsqrt(var + GN_EPS) * gnw_ref[...] + gnb_ref[...]
    bonus = seg_sum(xr * kmod * rk_ref[...]) * xv
    z = z_ref[0]
    y_ref[0] = ((yn + bonus) * (z * _sigmoid(z))).astype(y_ref.dtype)


def _rwkv(r, k, v, z, wa, prm, z0, prev, tr):
    b, l, _ = r.shape
    wide = pl.BlockSpec((1, tr, D_BRANCH), lambda bi, t: (bi, t, 0))
    narrow = pl.BlockSpec((1, tr, LANES), lambda bi, t: (bi, t, 0))
    row_w = pl.BlockSpec((1, D_BRANCH), lambda bi, t: (0, 0))
    row_n = pl.BlockSpec((1, LANES), lambda bi, t: (0, 0))
    up = pl.BlockSpec((LANES, D_BRANCH), lambda bi, t: (0, 0))
    ones = pl.BlockSpec((D_BRANCH, D_BRANCH), lambda bi, t: (0, 0))
    st_in = pl.BlockSpec((N_PAIRS, PAIR, PAIR), lambda bi, t: (0, 0, 0))
    st_out = pl.BlockSpec((1, N_PAIRS, PAIR, PAIR), lambda bi, t: (bi, 0, 0, 0))
    lrow_w = pl.BlockSpec((1, 1, D_BRANCH), lambda bi, t: (bi, 0, 0))
    lrow_n = pl.BlockSpec((1, 1, LANES), lambda bi, t: (bi, 0, 0))
    return pl.pallas_call(
        functools.partial(_rwkv_kernel, tr=tr),
        grid=(b, l // tr),
        in_specs=[wide, wide, wide, wide, narrow,
                  row_w, row_w, row_w, row_n, row_w, up, row_w, up,
                  row_w, row_w, row_w, row_w, row_w, ones,
                  st_in, row_w, row_w, row_w, row_n],
        out_specs=[wide, st_out, lrow_w, lrow_w, lrow_w, lrow_n],
        out_shape=[jax.ShapeDtypeStruct((b, l, D_BRANCH), BF16),
                   jax.ShapeDtypeStruct((b, N_PAIRS, PAIR, PAIR), F32),
                   jax.ShapeDtypeStruct((b, 1, D_BRANCH), F32),
                   jax.ShapeDtypeStruct((b, 1, D_BRANCH), F32),
                   jax.ShapeDtypeStruct((b, 1, D_BRANCH), F32),
                   jax.ShapeDtypeStruct((b, 1, LANES), F32)],
        scratch_shapes=[pltpu.VMEM((N_PAIRS, PAIR, PAIR), F32),
                        pltpu.VMEM((1, D_BRANCH), F32), pltpu.VMEM((1, D_BRANCH), F32),
                        pltpu.VMEM((1, D_BRANCH), F32), pltpu.VMEM((1, LANES), F32),
                        pltpu.VMEM((tr, D_BRANCH), F32)],
        compiler_params=_params(("parallel", "arbitrary")),
        name="rwkv",
    )(r, k, v, z, wa, *prm, z0, *prev)


def _out_proj_kernel(o_ref, zf_ref, yr_ref, x_ref, w_ref, fnw_ref, out_ref):
    z = zf_ref[...]
    yf = (o_ref[...] * (z * _sigmoid(z))).astype(BF16)
    mix = _dot(yf, w_ref[:D_BRANCH, :]) + _dot(yr_ref[...], w_ref[D_BRANCH:, :])
    h = x_ref[...] + mix
    out_ref[...] = h * lax.rsqrt(jnp.mean(h * h, axis=-1, keepdims=True) + NORM_EPS) * fnw_ref[...]


def _out_proj(o, zf, yr, x, w_out, fnw, tm):
    n = x.shape[0]
    half = pl.BlockSpec((tm, D_BRANCH), lambda i: (i, 0))
    full = pl.BlockSpec((tm, D_MODEL), lambda i: (i, 0))
    return pl.pallas_call(
        _out_proj_kernel,
        grid=(n // tm,),
        in_specs=[half, half, half, full,
                  pl.BlockSpec((D_MODEL, D_MODEL), lambda i: (0, 0)),
                  pl.BlockSpec((1, D_MODEL), lambda i: (0, 0))],
        out_specs=full,
        out_shape=jax.ShapeDtypeStruct((n, D_MODEL), F32),
        compiler_params=_params(("parallel",)),
        name="out_proj",
    )(o, zf, yr, x, w_out, fnw)


def _tiles(b, l):
    rows = b * l
    tm = 512 if rows % 512 == 0 else 256
    return tm, 256, 256, 128


def kernel(x, meta, norm_w, w_in, b_f, mu_shift, w0, w_up, a0, a_up, k_k, k_a, r_k, gn_w, gn_b,
           w_out, final_norm_w):
    b, l, d = x.shape
    assert d == D_MODEL and norm_w.shape[0] == 1 and l % 256 == 0
    tm, t_prep, t_attn, t_rwkv = _tiles(b, l)

    wi = w_in[0]
    o = 0
    cols = {}
    for name, width in (("q", D_BRANCH), ("k", D_BRANCH), ("v", D_BRANCH), ("fl", N_HEADS), ("zf", D_BRANCH),
                        ("r", D_BRANCH), ("rk", D_BRANCH), ("rv", D_BRANCH), ("wd", RANK), ("ad", RANK),
                        ("zr", D_BRANCH)):
        cols[name] = wi[:, o:o + width]
        o += width
    w_all = jnp.concatenate(
        [cols[n] for n in ("q", "k", "v", "zf", "r", "rk", "rv", "zr", "wd", "ad", "fl")]
        + [jnp.zeros((D_MODEL, LANES - N_HEADS), F32)], axis=1).astype(BF16)

    row = lambda vec: vec.reshape(1, -1).astype(F32)
    mu = mu_shift[0]
    mu_r, mu_k, mu_v = (row(mu[i * D_BRANCH:(i + 1) * D_BRANCH]) for i in range(3))
    mu_wa = row(mu[3 * D_BRANCH:])
    zeros_up = jnp.zeros((RANK, D_BRANCH), F32)
    wup_pad = jnp.concatenate([w_up[0], zeros_up], axis=0)
    aup_pad = jnp.concatenate([zeros_up, a_up[0]], axis=0)
    hid = np.arange(D_BRANCH) // HEAD_DIM
    ones_bd = jnp.asarray(hid[:, None] == hid[None, :], BF16)
    rwkv_prm = (mu_r, mu_k, mu_v, mu_wa, row(w0[0]), wup_pad, row(a0[0]), aup_pad,
                row(k_k[0]), row(k_a[0]), row(r_k[0]), row(gn_w[0]), row(gn_b[0]), ones_bd)
    bf_pad = jnp.concatenate([b_f[0], jnp.zeros((LANES - N_HEADS,), F32)]).reshape(1, LANES)
    pq, pk = _select_matrices()
    nw = row(norm_w[0])

    pre_rows = jnp.concatenate([jnp.zeros((PREFIX_ROWS - N_META, D_MODEL), F32), meta.astype(F32)], axis=0)
    pq_, pk_, pv_, _, pr_, prk_, prv_, pzr_, pwa_, pfl_ = _in_proj(pre_rows, nw, w_all, PREFIX_ROWS)
    lead = lambda a: a[None]
    _, kpre, vpre, c_pre = _fox_prep(lead(pq_), lead(pk_), lead(pv_), lead(pfl_), bf_pad,
                                     jnp.zeros((1, LANES), F32), pq, pk, PREFIX_ROWS, PREFIX_ROWS - N_META)
    zero_w = jnp.zeros((1, D_BRANCH), F32)
    _, z_pre, lr, lk, lv, lwa = _rwkv(
        lead(pr_), lead(prk_), lead(prv_), lead(pzr_), lead(pwa_), rwkv_prm,
        jnp.zeros((N_PAIRS, PAIR, PAIR), F32), (zero_w, zero_w, zero_w, jnp.zeros((1, LANES), F32)), PREFIX_ROWS)

    xf = x.reshape(b * l, D_MODEL)
    q_, k_, v_, zf_, r_, rk_, rv_, zr_, wa_, fl_ = _in_proj(xf, nw, w_all, tm)
    bl = lambda a: a.reshape(b, l, a.shape[-1])
    qp, kp, vp, _ = _fox_prep(bl(q_), bl(k_), bl(v_), bl(fl_), bf_pad, c_pre[0], pq, pk, t_prep, 0)
    o_attn = _fox_attn(qp, kp, vp, kpre, vpre, t_attn)
    y_rwkv, _, _, _, _, _ = _rwkv(bl(r_), bl(rk_), bl(rv_), bl(zr_), bl(wa_), rwkv_prm,
                                  z_pre[0], (lr[0], lk[0], lv[0], lwa[0]), t_rwkv)
    out = _out_proj(o_attn.reshape(b * l, D_BRANCH), zf_, y_rwkv.reshape(b * l, D_BRANCH), xf,
                    w_out[0].astype(BF16), row(final_norm_w), tm)
    return out.reshape(b, l, D_MODEL)
```

```python
import functools

import numpy as np
import jax
import jax.numpy as jnp
from jax import lax
from jax.experimental import pallas as pl
from jax.experimental.pallas import tpu as pltpu

F32 = jnp.float32
BF16 = jnp.bfloat16

D_MODEL = 1024
N_META = 16
HEAD_DIM = 64
N_HEADS = 8
D_BRANCH = N_HEADS * HEAD_DIM
N_PAIRS = N_HEADS // 2
RANK = 64
NORM_EPS = 1e-6
GN_EPS = 64e-5
KK_EPS = 1e-12
NEG = -1e30

LANES = 128
PREFIX_ROWS = 128
CHUNK = 64
PAIR = 2 * HEAD_DIM
VMEM_LIMIT = 56 * 1024 * 1024

N_WIDE = 8
COL_WA = N_WIDE * D_BRANCH
COL_FL = COL_WA + LANES
N_COLS = COL_FL + LANES

N_CPARTS = 3
ONES_LANE = (HEAD_DIM, 0)


def _dot(a, b):
    return jnp.dot(a, b, preferred_element_type=F32)


def _dot_hi(a, b):
    return jnp.dot(a, b, preferred_element_type=F32, precision=lax.Precision.HIGHEST)


def _dot_nt(a, b):
    return lax.dot_general(a, b, (((1,), (1,)), ((), ())), preferred_element_type=F32)


def _dot_tn(a, b):
    return lax.dot_general(a, b, (((0,), (0,)), ((), ())), preferred_element_type=F32)


def _softplus(x):
    return jnp.maximum(x, 0.0) + jnp.log(1.0 + jnp.exp(-jnp.abs(x)))


def _sigmoid(x):
    return 1.0 / (1.0 + jnp.exp(-x))


def _params(sem):
    return pltpu.CompilerParams(dimension_semantics=sem, vmem_limit_bytes=VMEM_LIMIT)


def _in_proj_kernel(x_ref, nw_ref, w_ref, *out_refs):
    x = x_ref[...]
    u = x * lax.rsqrt(jnp.mean(x * x, axis=-1, keepdims=True) + NORM_EPS) * nw_ref[...]
    ub = u.astype(BF16)
    for g in range(N_WIDE):
        o = out_refs[g]
        o[...] = _dot(ub, w_ref[:, g * D_BRANCH:(g + 1) * D_BRANCH]).astype(o.dtype)
    out_refs[N_WIDE][...] = _dot(ub, w_ref[:, COL_WA:COL_WA + LANES])
    out_refs[N_WIDE + 1][...] = _dot(ub, w_ref[:, COL_FL:COL_FL + LANES])


def _in_proj(rows, norm_w, w_all, tm):
    n = rows.shape[0]
    wide_dtypes = (BF16, BF16, BF16, F32, F32, F32, F32, F32)
    out_shape = [jax.ShapeDtypeStruct((n, D_BRANCH), dt) for dt in wide_dtypes]
    out_shape += [jax.ShapeDtypeStruct((n, LANES), F32)] * 2
    out_specs = [pl.BlockSpec((tm, D_BRANCH), lambda i: (i, 0))] * N_WIDE
    out_specs += [pl.BlockSpec((tm, LANES), lambda i: (i, 0))] * 2
    return pl.pallas_call(
        _in_proj_kernel,
        grid=(n // tm,),
        in_specs=[pl.BlockSpec((tm, D_MODEL), lambda i: (i, 0)),
                  pl.BlockSpec((1, D_MODEL), lambda i: (0, 0)),
                  pl.BlockSpec((D_MODEL, N_COLS), lambda i: (0, 0))],
        out_specs=out_specs,
        out_shape=out_shape,
        compiler_params=_params(("parallel",)),
        name="in_proj",
    )(rows, norm_w, w_all)


def _select_matrices():
    pq = np.zeros((N_HEADS, 2 * LANES, LANES), np.float32)
    pk = np.zeros((N_HEADS, 2 * LANES, LANES), np.float32)
    for h in range(N_HEADS):
        base = (h % 2) * HEAD_DIM
        for d in range(HEAD_DIM):
            pq[h, base + d, d] = 0.125
            pk[h, base + d, d] = 1.0
        for part in range(N_CPARTS):
            pq[h, LANES + part * N_HEADS + h, HEAD_DIM + part] = 1.0
            pq[h, LANES + N_CPARTS * N_HEADS + h, HEAD_DIM + N_CPARTS + part] = 1.0
            pk[h, LANES + N_CPARTS * N_HEADS + h, HEAD_DIM + part] = 1.0
            pk[h, LANES + part * N_HEADS + h, HEAD_DIM + N_CPARTS + part] = -1.0
    return jnp.asarray(pq, BF16), jnp.asarray(pk, BF16)


def _fox_prep_kernel(q_ref, k_ref, v_ref, fl_ref, bf_ref, c0_ref, pq_ref, pk_ref,
                     qo_ref, ko_ref, vo_ref, cl_ref, carry_ref, *, tr, n_pad):
    i = pl.program_id(1)

    @pl.when(i == 0)
    def _():
        carry_ref[...] = c0_ref[...]

    x = fl_ref[0] + bf_ref[...]
    logf = jnp.minimum(x, 0.0) - jnp.log(1.0 + jnp.exp(-jnp.abs(x)))
    lane = lax.broadcasted_iota(jnp.int32, (tr, LANES), 1)
    row = lax.broadcasted_iota(jnp.int32, (tr, LANES), 0) + i * tr
    valid = lane < N_HEADS
    if n_pad:
        valid = valid & (row >= n_pad)
    logf = jnp.where(valid, logf, 0.0)
    r2 = lax.broadcasted_iota(jnp.int32, (tr, tr), 0)
    c2 = lax.broadcasted_iota(jnp.int32, (tr, tr), 1)
    tri = jnp.where(r2 >= c2, 1.0, 0.0).astype(F32)
    cum = _dot_hi(tri, logf) + carry_ref[...]
    carry_ref[...] = cum[tr - 1:tr, :]

    @pl.when(i == pl.num_programs(1) - 1)
    def _():
        cl_ref[0] = cum[tr - 1:tr, :]

    p1 = cum.astype(BF16).astype(F32)
    rem = cum - p1
    p2 = rem.astype(BF16).astype(F32)
    p3 = (rem - p2).astype(BF16).astype(F32)
    ones = jnp.where((lane >= N_CPARTS * N_HEADS) & (lane < (N_CPARTS + 1) * N_HEADS), 1.0, 0.0)
    cbits = p1 + pltpu.roll(p2, N_HEADS, 1) + pltpu.roll(p3, 2 * N_HEADS, 1) + ones
    cbits_k = cbits
    if n_pad:
        cbits_k = jnp.where((row < n_pad) & (lane < N_HEADS), -NEG, cbits)
    cq = cbits.astype(BF16)
    ck = cbits_k.astype(BF16)
    lane_lo = lane < HEAD_DIM
    for h in range(N_HEADS):
        sl = slice((h // 2) * PAIR, (h // 2 + 1) * PAIR)
        xq = jnp.concatenate([q_ref[0, :, sl], cq], axis=1)
        xk = jnp.concatenate([k_ref[0, :, sl], ck], axis=1)
        qo_ref[0, h] = _dot(xq, pq_ref[h]).astype(BF16)
        ko_ref[0, h] = _dot(xk, pk_ref[h]).astype(BF16)
        vp = v_ref[0, :, sl]
        keep = lane_lo if h % 2 == 0 else jnp.logical_not(lane_lo)
        one = jnp.where(lane == ONES_LANE[h % 2], 1.0, 0.0).astype(BF16)
        vo_ref[0, h] = jnp.where(keep, vp, one)


def _fox_prep(q, k, v, fl, bf_pad, c0, pq, pk, tr, n_pad):
    b, l, _ = q.shape
    wide = pl.BlockSpec((1, tr, D_BRANCH), lambda bi, i: (bi, i, 0))
    head_out = pl.BlockSpec((1, N_HEADS, tr, LANES), lambda bi, i: (bi, 0, i, 0))
    row128 = pl.BlockSpec((1, LANES), lambda bi, i: (0, 0))
    sel = pl.BlockSpec((N_HEADS, 2 * LANES, LANES), lambda bi, i: (0, 0, 0))
    return pl.pallas_call(
        functools.partial(_fox_prep_kernel, tr=tr, n_pad=n_pad),
        grid=(b, l // tr),
        in_specs=[wide, wide, wide,
                  pl.BlockSpec((1, tr, LANES), lambda bi, i: (bi, i, 0)),
                  row128, row128, sel, sel],
        out_specs=[head_out, head_out, head_out,
                   pl.BlockSpec((1, 1, LANES), lambda bi, i: (bi, 0, 0))],
        out_shape=[jax.ShapeDtypeStruct((b, N_HEADS, l, LANES), BF16)] * 3
                  + [jax.ShapeDtypeStruct((b, 1, LANES), F32)],
        scratch_shapes=[pltpu.VMEM((1, LANES), F32)],
        compiler_params=_params(("parallel", "arbitrary")),
        name="fox_prep",
    )(q, k, v, fl, bf_pad, c0, pq, pk)


def _attn_kernel(q_ref, kpre_ref, vpre_ref, k_ref, v_ref, o_ref, *, tq):
    qi = pl.program_id(2)
    qs = (q_ref[0, 0], q_ref[0, 1])

    def step(carry, kv, mask):
        new = []
        for h in range(2):
            m, acc = carry[h]
            kc, vc = kv[h]
            s = _dot_nt(qs[h], kc)
            if mask is not None:
                s = jnp.where(mask, s, NEG)
            m_new = jnp.maximum(m, jnp.max(s, axis=-1, keepdims=True))
            alpha = jnp.exp(m - m_new)
            p = jnp.exp(s - m_new).astype(BF16)
            new.append((m_new, alpha * acc + _dot(p, vc)))
        return tuple(new)

    def chunk(off):
        return tuple((k_ref[0, h, pl.ds(off, tq), :], v_ref[0, h, pl.ds(off, tq), :]) for h in range(2))

    carry = tuple((jnp.full((tq, 1), NEG, F32), jnp.zeros((tq, LANES), F32)) for _ in range(2))
    carry = step(carry, tuple((kpre_ref[0, h], vpre_ref[0, h]) for h in range(2)), None)
    carry = lax.fori_loop(0, qi, lambda j, c: step(c, chunk(pl.multiple_of(j * tq, tq)), None), carry)
    row = lax.broadcasted_iota(jnp.int32, (tq, tq), 0)
    col = lax.broadcasted_iota(jnp.int32, (tq, tq), 1)
    (_, acc0), (_, acc1) = step(carry, chunk(pl.multiple_of(qi * tq, tq)), col <= row)
    l0 = acc0[:, ONES_LANE[0]:ONES_LANE[0] + 1]
    l1 = acc1[:, ONES_LANE[1]:ONES_LANE[1] + 1]
    lane = lax.broadcasted_iota(jnp.int32, (tq, LANES), 1)
    o_ref[0] = jnp.where(lane < HEAD_DIM, acc0 / l0, acc1 / l1)


def _fox_attn(qp, kp, vp, kpre, vpre, tq):
    b, _, l, _ = qp.shape
    pre = pl.BlockSpec((1, 2, PREFIX_ROWS, LANES), lambda bi, p, qi: (0, p, 0, 0))
    full = pl.BlockSpec((1, 2, l, LANES), lambda bi, p, qi: (bi, p, 0, 0))
    return pl.pallas_call(
        functools.partial(_attn_kernel, tq=tq),
        grid=(b, N_PAIRS, l // tq),
        in_specs=[pl.BlockSpec((1, 2, tq, LANES), lambda bi, p, qi: (bi, p, qi, 0)),
                  pre, pre, full, full],
        out_specs=pl.BlockSpec((1, tq, PAIR), lambda bi, p, qi: (bi, qi, p)),
        out_shape=jax.ShapeDtypeStruct((b, l, D_BRANCH), F32),
        compiler_params=_params(("parallel", "parallel", "parallel")),
        name="fox_attn",
    )(qp, kpre, vpre, kp, vp)


def _split2(x):
    hi = x.astype(BF16)
    lo = (x - hi.astype(F32)).astype(BF16)
    return hi, lo


def _rwkv_kernel(r_ref, k_ref, v_ref, z_ref, wa_ref,
                 mu_r_ref, mu_k_ref, mu_v_ref, mu_wa_ref, w0_ref, wup_ref, a0_ref, aup_ref,
                 kk_ref, ka_ref, rk_ref, gnw_ref, gnb_ref, ones_ref,
                 z0_ref, pr_ref, pk_ref, pv_ref, pwa_ref,
                 y_ref, zf_ref, lr_ref, lk_ref, lv_ref, lwa_ref,
                 state_ref, sr_ref, sk_ref, sv_ref, swa_ref, yacc_ref, *, tr):
    t = pl.program_id(1)
    last = t == pl.num_programs(1) - 1

    @pl.when(t == 0)
    def _():
        state_ref[...] = z0_ref[...]
        sr_ref[...] = pr_ref[...]
        sk_ref[...] = pk_ref[...]
        sv_ref[...] = pv_ref[...]
        swa_ref[...] = pwa_ref[...]

    def shifted(cur, prev_ref, mu_ref):
        first = lax.broadcasted_iota(jnp.int32, cur.shape, 0) == 0
        prev = jnp.where(first, prev_ref[...], pltpu.roll(cur, 1, 0))
        prev_ref[...] = cur[tr - 1:tr, :]
        return cur + mu_ref[...] * (prev - cur)

    xr = shifted(r_ref[0], sr_ref, mu_r_ref)
    xk = shifted(k_ref[0], sk_ref, mu_k_ref)
    xv = shifted(v_ref[0], sv_ref, mu_v_ref)
    xwa = shifted(wa_ref[0], swa_ref, mu_wa_ref)

    @pl.when(last)
    def _():
        lr_ref[0] = sr_ref[...]
        lk_ref[0] = sk_ref[...]
        lv_ref[0] = sv_ref[...]
        lwa_ref[0] = swa_ref[...]

    ones_bd = ones_ref[...]

    def seg_sum(x):
        hi, lo = _split2(x)
        return _dot(hi, ones_bd) + _dot(lo, ones_bd)

    w_lin = w0_ref[...] + _dot_hi(jnp.tanh(xwa), wup_ref[...])
    a_lin = a0_ref[...] + _dot_hi(xwa, aup_ref[...])
    w = -_softplus(-w_lin) - 0.5
    ld = -jnp.exp(w)
    a = _sigmoid(a_lin)
    kk = xk * kk_ref[...]
    kk = kk * lax.rsqrt(seg_sum(kk * kk) + KK_EPS)
    kmod = xk * (1.0 + (a - 1.0) * ka_ref[...])
    bvec = kk * a

    r2 = lax.broadcasted_iota(jnp.int32, (tr, tr), 0)
    c2 = lax.broadcasted_iota(jnp.int32, (tr, tr), 1)
    tri = jnp.where((r2 >= c2) & (r2 // CHUNK == c2 // CHUNK), 1.0, 0.0).astype(F32)
    lw = _dot_hi(tri, ld)
    w_inv = jnp.exp(-lw)
    rt = xr * jnp.exp(lw)
    at = -kk * jnp.exp(lw - ld)
    bt = bvec * w_inv
    kt = kmod * w_inv

    lane = lax.broadcasted_iota(jnp.int32, (CHUNK, PAIR), 1)
    head_a = lane < HEAD_DIM
    tau_r = lax.broadcasted_iota(jnp.int32, (PAIR, PAIR), 0)
    tau_c = lax.broadcasted_iota(jnp.int32, (PAIR, PAIR), 1)
    strict = (tau_r % CHUNK) > (tau_c % CHUNK)
    incl = (tau_r % CHUNK) >= (tau_c % CHUNK)
    eye = tau_r == tau_c
    eye_f = jnp.where(eye, 1.0, 0.0).astype(F32)

    def stack(x):
        return jnp.concatenate([jnp.where(head_a, x, 0.0), jnp.where(head_a, 0.0, x)], axis=0)

    for c in range(tr // CHUNK):
        rows = slice(c * CHUNK, (c + 1) * CHUNK)
        lw_end = lw[(c + 1) * CHUNK - 1:(c + 1) * CHUNK, :]
        w_end = jnp.exp(lw_end)
        for p in range(N_PAIRS):
            cols = slice(p * PAIR, (p + 1) * PAIR)
            wc = w_end[:, cols]
            rs_f = stack(rt[rows, cols])
            a_s = stack(at[rows, cols]).astype(BF16)
            b_s = stack(bt[rows, cols]).astype(BF16)
            k_s = stack(kt[rows, cols]).astype(BF16)
            v_s = stack(xv[rows, cols]).astype(BF16)
            bh_s = stack(bt[rows, cols] * wc).astype(BF16)
            kh_s = stack(kt[rows, cols] * wc).astype(BF16)
            r_s = rs_f.astype(BF16)

            a_ab = jnp.where(strict, _dot_nt(a_s, b_s), 0.0)
            a_ak = jnp.where(strict, _dot_nt(a_s, k_s), 0.0).astype(BF16)
            a_rb = jnp.where(incl, _dot_nt(r_s, b_s), 0.0).astype(BF16)
            a_rk = jnp.where(incl, _dot_nt(r_s, k_s), 0.0).astype(BF16)

            pw = a_ab
            tinv = eye_f + a_ab
            for _ in range(5):
                pb = pw.astype(BF16)
                pw = _dot(pb, pb)
                tinv = tinv + _dot(tinv.astype(BF16), pw.astype(BF16))
            tb = tinv.astype(BF16)

            ap = _dot(tb, a_s).astype(BF16)
            vp = _dot(tb, _dot(a_ak, v_s).astype(BF16)).astype(BF16)
            m_mat = jnp.where(eye, wc, 0.0) + _dot_tn(bh_s, ap)
            g_mat = _dot_tn(bh_s, vp) + _dot_tn(kh_s, v_s)
            rp = rs_f + _dot(a_rb, ap)
            y0 = _dot(a_rb, vp) + _dot(a_rk, v_s)

            zb = state_ref[p].astype(BF16)
            ys = _dot(rp.astype(BF16), zb) + y0
            state_ref[p] = _dot(m_mat.astype(BF16), zb) + g_mat
            yacc_ref[rows, cols] = ys[:CHUNK] + ys[CHUNK:]

    @pl.when(last)
    def _():
        zf_ref[0] = state_ref[...]

    y = yacc_ref[...]
    inv_n = 1.0 / HEAD_DIM
    mean = seg_sum(y) * inv_n
    d = y - mean
    var = seg_sum(d * d) * inv_n
    yn = d * lax.rsqrt(var + GN_EPS) * gnw_ref[...] + gnb_ref[...]
    bonus = seg_sum(xr * kmod * rk_ref[...]) * xv
    z = z_ref[0]
    y_ref[0] = ((yn + bonus) * (z * _sigmoid(z))).astype(y_ref.dtype)


def _rwkv(r, k, v, z, wa, prm, z0, prev, tr):
    b, l, _ = r.shape
    wide = pl.BlockSpec((1, tr, D_BRANCH), lambda bi, t: (bi, t, 0))
    narrow = pl.BlockSpec((1, tr, LANES), lambda bi, t: (bi, t, 0))
    row_w = pl.BlockSpec((1, D_BRANCH), lambda bi, t: (0, 0))
    row_n = pl.BlockSpec((1, LANES), lambda bi, t: (0, 0))
    up = pl.BlockSpec((LANES, D_BRANCH), lambda bi, t: (0, 0))
    ones = pl.BlockSpec((D_BRANCH, D_BRANCH), lambda bi, t: (0, 0))
    st_in = pl.BlockSpec((N_PAIRS, PAIR, PAIR), lambda bi, t: (0, 0, 0))
    st_out = pl.BlockSpec((1, N_PAIRS, PAIR, PAIR), lambda bi, t: (bi, 0, 0, 0))
    lrow_w = pl.BlockSpec((1, 1, D_BRANCH), lambda bi, t: (bi, 0, 0))
    lrow_n = pl.BlockSpec((1, 1, LANES), lambda bi, t: (bi, 0, 0))
    return pl.pallas_call(
        functools.partial(_rwkv_kernel, tr=tr),
        grid=(b, l // tr),
        in_specs=[wide, wide, wide, wide, narrow,
                  row_w, row_w, row_w, row_n, row_w, up, row_w, up,
                  row_w, row_w, row_w, row_w, row_w, ones,
                  st_in, row_w, row_w, row_w, row_n],
        out_specs=[wide, st_out, lrow_w, lrow_w, lrow_w, lrow_n],
        out_shape=[jax.ShapeDtypeStruct((b, l, D_BRANCH), BF16),
                   jax.ShapeDtypeStruct((b, N_PAIRS, PAIR, PAIR), F32),
                   jax.ShapeDtypeStruct((b, 1, D_BRANCH), F32),
                   jax.ShapeDtypeStruct((b, 1, D_BRANCH), F32),
                   jax.ShapeDtypeStruct((b, 1, D_BRANCH), F32),
                   jax.ShapeDtypeStruct((b, 1, LANES), F32)],
        scratch_shapes=[pltpu.VMEM((N_PAIRS, PAIR, PAIR), F32),
                        pltpu.VMEM((1, D_BRANCH), F32), pltpu.VMEM((1, D_BRANCH), F32),
                        pltpu.VMEM((1, D_BRANCH), F32), pltpu.VMEM((1, LANES), F32),
                        pltpu.VMEM((tr, D_BRANCH), F32)],
        compiler_params=_params(("parallel", "arbitrary")),
        name="rwkv",
    )(r, k, v, z, wa, *prm, z0, *prev)


def _out_proj_kernel(o_ref, zf_ref, yr_ref, x_ref, w_ref, fnw_ref, out_ref):
    z = zf_ref[...]
    yf = (o_ref[...] * (z * _sigmoid(z))).astype(BF16)
    mix = _dot(yf, w_ref[:D_BRANCH, :]) + _dot(yr_ref[...], w_ref[D_BRANCH:, :])
    h = x_ref[...] + mix
    out_ref[...] = h * lax.rsqrt(jnp.mean(h * h, axis=-1, keepdims=True) + NORM_EPS) * fnw_ref[...]


def _out_proj(o, zf, yr, x, w_out, fnw, tm):
    n = x.shape[0]
    half = pl.BlockSpec((tm, D_BRANCH), lambda i: (i, 0))
    full = pl.BlockSpec((tm, D_MODEL), lambda i: (i, 0))
    return pl.pallas_call(
        _out_proj_kernel,
        grid=(n // tm,),
        in_specs=[half, half, half, full,
                  pl.BlockSpec((D_MODEL, D_MODEL), lambda i: (0, 0)),
                  pl.BlockSpec((1, D_MODEL), lambda i: (0, 0))],
        out_specs=full,
        out_shape=jax.ShapeDtypeStruct((n, D_MODEL), F32),
        compiler_params=_params(("parallel",)),
        name="out_proj",
    )(o, zf, yr, x, w_out, fnw)


def _tiles(b, l):
    rows = b * l
    tm = 512 if rows % 512 == 0 else 256
    return tm, 256, 512, 128


def kernel(x, meta, norm_w, w_in, b_f, mu_shift, w0, w_up, a0, a_up, k_k, k_a, r_k, gn_w, gn_b,
           w_out, final_norm_w):
    b, l, d = x.shape
    assert d == D_MODEL and norm_w.shape[0] == 1 and l % 256 == 0
    tm, t_prep, t_attn, t_rwkv = _tiles(b, l)

    wi = w_in[0]
    o = 0
    cols = {}
    for name, width in (("q", D_BRANCH), ("k", D_BRANCH), ("v", D_BRANCH), ("fl", N_HEADS), ("zf", D_BRANCH),
                        ("r", D_BRANCH), ("rk", D_BRANCH), ("rv", D_BRANCH), ("wd", RANK), ("ad", RANK),
                        ("zr", D_BRANCH)):
        cols[name] = wi[:, o:o + width]
        o += width
    w_all = jnp.concatenate(
        [cols[n] for n in ("q", "k", "v", "zf", "r", "rk", "rv", "zr", "wd", "ad", "fl")]
        + [jnp.zeros((D_MODEL, LANES - N_HEADS), F32)], axis=1).astype(BF16)

    row = lambda vec: vec.reshape(1, -1).astype(F32)
    mu = mu_shift[0]
    mu_r, mu_k, mu_v = (row(mu[i * D_BRANCH:(i + 1) * D_BRANCH]) for i in range(3))
    mu_wa = row(mu[3 * D_BRANCH:])
    zeros_up = jnp.zeros((RANK, D_BRANCH), F32)
    wup_pad = jnp.concatenate([w_up[0], zeros_up], axis=0)
    aup_pad = jnp.concatenate([zeros_up, a_up[0]], axis=0)
    hid = np.arange(D_BRANCH) // HEAD_DIM
    ones_bd = jnp.asarray(hid[:, None] == hid[None, :], BF16)
    rwkv_prm = (mu_r, mu_k, mu_v, mu_wa, row(w0[0]), wup_pad, row(a0[0]), aup_pad,
                row(k_k[0]), row(k_a[0]), row(r_k[0]), row(gn_w[0]), row(gn_b[0]), ones_bd)
    bf_pad = jnp.concatenate([b_f[0], jnp.zeros((LANES - N_HEADS,), F32)]).reshape(1, LANES)
    pq, pk = _select_matrices()
    nw = row(norm_w[0])

    pre_rows = jnp.concatenate([jnp.zeros((PREFIX_ROWS - N_META, D_MODEL), F32), meta.astype(F32)], axis=0)
    pq_, pk_, pv_, _, pr_, prk_, prv_, pzr_, pwa_, pfl_ = _in_proj(pre_rows, nw, w_all, PREFIX_ROWS)
    lead = lambda a: a[None]
    _, kpre, vpre, c_pre = _fox_prep(lead(pq_), lead(pk_), lead(pv_), lead(pfl_), bf_pad,
                                     jnp.zeros((1, LANES), F32), pq, pk, PREFIX_ROWS, PREFIX_ROWS - N_META)
    zero_w = jnp.zeros((1, D_BRANCH), F32)
    _, z_pre, lr, lk, lv, lwa = _rwkv(
        lead(pr_), lead(prk_), lead(prv_), lead(pzr_), lead(pwa_), rwkv_prm,
        jnp.zeros((N_PAIRS, PAIR, PAIR), F32), (zero_w, zero_w, zero_w, jnp.zeros((1, LANES), F32)), PREFIX_ROWS)

    xf = x.reshape(b * l, D_MODEL)
    q_, k_, v_, zf_, r_, rk_, rv_, zr_, wa_, fl_ = _in_proj(xf, nw, w_all, tm)
    bl = lambda a: a.reshape(b, l, a.shape[-1])
    qp, kp, vp, _ = _fox_prep(bl(q_), bl(k_), bl(v_), bl(fl_), bf_pad, c_pre[0], pq, pk, t_prep, 0)
    o_attn = _fox_attn(qp, kp, vp, kpre, vpre, t_attn)
    y_rwkv, _, _, _, _, _ = _rwkv(bl(r_), bl(rk_), bl(rv_), bl(zr_), bl(wa_), rwkv_prm,
                                  z_pre[0], (lr[0], lk[0], lv[0], lwa[0]), t_rwkv)
    out = _out_proj(o_attn.reshape(b * l, D_BRANCH), zf_, y_rwkv.reshape(b * l, D_BRANCH), xf,
                    w_out[0].astype(BF16), row(final_norm_w), tm)
    return out.reshape(b, l, D_MODEL)
```

```python
import functools

import numpy as np
import jax
import jax.numpy as jnp
from jax import lax
from jax.experimental import pallas as pl
from jax.experimental.pallas import tpu as pltpu

F32 = jnp.float32
BF16 = jnp.bfloat16

D_MODEL = 1024
N_META = 16
HEAD_DIM = 64
N_HEADS = 8
D_BRANCH = N_HEADS * HEAD_DIM
N_PAIRS = N_HEADS // 2
RANK = 64
NORM_EPS = 1e-6
GN_EPS = 64e-5
KK_EPS = 1e-12
NEG = -1e30

LANES = 128
PREFIX_ROWS = 128
CHUNK = 64
PAIR = 2 * HEAD_DIM
VMEM_LIMIT = 56 * 1024 * 1024

N_WIDE = 8
COL_WA = N_WIDE * D_BRANCH
COL_FL = COL_WA + LANES
N_COLS = COL_FL + LANES

N_CPARTS = 3
ONES_LANE = (HEAD_DIM, 0)


def _dot(a, b):
    return jnp.dot(a, b, preferred_element_type=F32)


def _pieces(x, n):
    out = []
    for _ in range(n - 1):
        p = x.astype(BF16)
        out.append(p)
        x = x - p.astype(F32)
    out.append(x.astype(BF16))
    return out


def _dot_exact_lhs(a_bf16, x, n):
    acc = None
    for p in _pieces(x, n):
        t = _dot(a_bf16, p)
        acc = t if acc is None else acc + t
    return acc


def _dot_split(x, w_hi, w_lo):
    x_hi, x_lo = _pieces(x, 2)
    return _dot(x_hi, w_hi) + _dot(x_lo, w_hi) + _dot(x_hi, w_lo)


def _dot_nt(a, b):
    return lax.dot_general(a, b, (((1,), (1,)), ((), ())), preferred_element_type=F32)


def _dot_tn(a, b):
    return lax.dot_general(a, b, (((0,), (0,)), ((), ())), preferred_element_type=F32)


def _softplus(x):
    return jnp.maximum(x, 0.0) + jnp.log(1.0 + jnp.exp(-jnp.abs(x)))


def _sigmoid(x):
    return 1.0 / (1.0 + jnp.exp(-x))


def _params(sem):
    return pltpu.CompilerParams(dimension_semantics=sem, vmem_limit_bytes=VMEM_LIMIT)


def _in_proj_kernel(x_ref, nw_ref, w_ref, *out_refs):
    x = x_ref[...]
    u = x * lax.rsqrt(jnp.mean(x * x, axis=-1, keepdims=True) + NORM_EPS) * nw_ref[...]
    ub = u.astype(BF16)
    for g in range(N_WIDE):
        o = out_refs[g]
        o[...] = _dot(ub, w_ref[:, g * D_BRANCH:(g + 1) * D_BRANCH]).astype(o.dtype)
    out_refs[N_WIDE][...] = _dot(ub, w_ref[:, COL_WA:COL_WA + LANES])
    out_refs[N_WIDE + 1][...] = _dot(ub, w_ref[:, COL_FL:COL_FL + LANES])


def _in_proj(rows, norm_w, w_all, tm):
    n = rows.shape[0]
    wide_dtypes = (BF16, BF16, BF16, F32, F32, F32, F32, F32)
    out_shape = [jax.ShapeDtypeStruct((n, D_BRANCH), dt) for dt in wide_dtypes]
    out_shape += [jax.ShapeDtypeStruct((n, LANES), F32)] * 2
    out_specs = [pl.BlockSpec((tm, D_BRANCH), lambda i: (i, 0))] * N_WIDE
    out_specs += [pl.BlockSpec((tm, LANES), lambda i: (i, 0))] * 2
    return pl.pallas_call(
        _in_proj_kernel,
        grid=(n // tm,),
        in_specs=[pl.BlockSpec((tm, D_MODEL), lambda i: (i, 0)),
                  pl.BlockSpec((1, D_MODEL), lambda i: (0, 0)),
                  pl.BlockSpec((D_MODEL, N_COLS), lambda i: (0, 0))],
        out_specs=out_specs,
        out_shape=out_shape,
        compiler_params=_params(("parallel",)),
        name="in_proj",
    )(rows, norm_w, w_all)


def _select_matrices():
    pq = np.zeros((N_HEADS, 2 * LANES, LANES), np.float32)
    pk = np.zeros((N_HEADS, 2 * LANES, LANES), np.float32)
    for h in range(N_HEADS):
        base = (h % 2) * HEAD_DIM
        for d in range(HEAD_DIM):
            pq[h, base + d, d] = 0.125
            pk[h, base + d, d] = 1.0
        for part in range(N_CPARTS):
            pq[h, LANES + part * N_HEADS + h, HEAD_DIM + part] = 1.0
            pq[h, LANES + N_CPARTS * N_HEADS + h, HEAD_DIM + N_CPARTS + part] = 1.0
            pk[h, LANES + N_CPARTS * N_HEADS + h, HEAD_DIM + part] = 1.0
            pk[h, LANES + part * N_HEADS + h, HEAD_DIM + N_CPARTS + part] = -1.0
    return jnp.asarray(pq, BF16), jnp.asarray(pk, BF16)


def _fox_prep_kernel(q_ref, k_ref, v_ref, fl_ref, bf_ref, c0_ref, pq_ref, pk_ref,
                     qo_ref, ko_ref, vo_ref, cl_ref, carry_ref, *, tr, n_pad):
    i = pl.program_id(1)

    @pl.when(i == 0)
    def _():
        carry_ref[...] = c0_ref[...]

    x = fl_ref[0] + bf_ref[...]
    logf = jnp.minimum(x, 0.0) - jnp.log(1.0 + jnp.exp(-jnp.abs(x)))
    lane = lax.broadcasted_iota(jnp.int32, (tr, LANES), 1)
    row = lax.broadcasted_iota(jnp.int32, (tr, LANES), 0) + i * tr
    valid = lane < N_HEADS
    if n_pad:
        valid = valid & (row >= n_pad)
    logf = jnp.where(valid, logf, 0.0)
    r2 = lax.broadcasted_iota(jnp.int32, (tr, tr), 0)
    c2 = lax.broadcasted_iota(jnp.int32, (tr, tr), 1)
    tri = jnp.where(r2 >= c2, 1.0, 0.0).astype(BF16)
    cum = _dot_exact_lhs(tri, logf, 3) + carry_ref[...]
    carry_ref[...] = cum[tr - 1:tr, :]

    @pl.when(i == pl.num_programs(1) - 1)
    def _():
        cl_ref[0] = cum[tr - 1:tr, :]

    p1 = cum.astype(BF16).astype(F32)
    rem = cum - p1
    p2 = rem.astype(BF16).astype(F32)
    p3 = (rem - p2).astype(BF16).astype(F32)
    ones = jnp.where((lane >= N_CPARTS * N_HEADS) & (lane < (N_CPARTS + 1) * N_HEADS), 1.0, 0.0)
    cbits = p1 + pltpu.roll(p2, N_HEADS, 1) + pltpu.roll(p3, 2 * N_HEADS, 1) + ones
    cbits_k = cbits
    if n_pad:
        cbits_k = jnp.where((row < n_pad) & (lane < N_HEADS), -NEG, cbits)
    cq = cbits.astype(BF16)
    ck = cbits_k.astype(BF16)
    lane_lo = lane < HEAD_DIM
    for h in range(N_HEADS):
        sl = slice((h // 2) * PAIR, (h // 2 + 1) * PAIR)
        xq = jnp.concatenate([q_ref[0, :, sl], cq], axis=1)
        xk = jnp.concatenate([k_ref[0, :, sl], ck], axis=1)
        qo_ref[0, h] = _dot(xq, pq_ref[h]).astype(BF16)
        ko_ref[0, h] = _dot(xk, pk_ref[h]).astype(BF16)
        vp = v_ref[0, :, sl]
        keep = lane_lo if h % 2 == 0 else jnp.logical_not(lane_lo)
        one = jnp.where(lane == ONES_LANE[h % 2], 1.0, 0.0).astype(BF16)
        vo_ref[0, h] = jnp.where(keep, vp, one)


def _fox_prep(q, k, v, fl, bf_pad, c0, pq, pk, tr, n_pad):
    b, l, _ = q.shape
    wide = pl.BlockSpec((1, tr, D_BRANCH), lambda bi, i: (bi, i, 0))
    head_out = pl.BlockSpec((1, N_HEADS, tr, LANES), lambda bi, i: (bi, 0, i, 0))
    row128 = pl.BlockSpec((1, LANES), lambda bi, i: (0, 0))
    sel = pl.BlockSpec((N_HEADS, 2 * LANES, LANES), lambda bi, i: (0, 0, 0))
    return pl.pallas_call(
        functools.partial(_fox_prep_kernel, tr=tr, n_pad=n_pad),
        grid=(b, l // tr),
        in_specs=[wide, wide, wide,
                  pl.BlockSpec((1, tr, LANES), lambda bi, i: (bi, i, 0)),
                  row128, row128, sel, sel],
        out_specs=[head_out, head_out, head_out,
                   pl.BlockSpec((1, 1, LANES), lambda bi, i: (bi, 0, 0))],
        out_shape=[jax.ShapeDtypeStruct((b, N_HEADS, l, LANES), BF16)] * 3
                  + [jax.ShapeDtypeStruct((b, 1, LANES), F32)],
        scratch_shapes=[pltpu.VMEM((1, LANES), F32)],
        compiler_params=_params(("parallel", "arbitrary")),
        name="fox_prep",
    )(q, k, v, fl, bf_pad, c0, pq, pk)


def _attn_kernel(q_ref, kpre_ref, vpre_ref, k_ref, v_ref, o_ref, s_ref, *, tq):
    qi = pl.program_id(2)
    heads = range(2)
    qs = [q_ref[0, h] for h in heads]

    def softmax_pv(s, rmax, vc, m, acc):
        m_new = jnp.maximum(m, rmax)
        alpha = jnp.exp(m - m_new)
        p = jnp.exp(s - m_new).astype(BF16)
        return m_new, alpha * acc + _dot(p, vc)

    def rowmax(s):
        return jnp.max(s, axis=-1, keepdims=True)

    def k_chunk(h, idx):
        return k_ref[0, h, pl.ds(pl.multiple_of(idx * tq, tq), tq), :]

    def v_chunk(h, idx):
        return v_ref[0, h, pl.ds(pl.multiple_of(idx * tq, tq), tq), :]

    m = [jnp.full((tq, 1), NEG, F32) for _ in heads]
    acc = [jnp.zeros((tq, LANES), F32) for _ in heads]
    for h in heads:
        s = _dot_nt(qs[h], kpre_ref[0, h])
        m[h], acc[h] = softmax_pv(s, rowmax(s), vpre_ref[0, h], m[h], acc[h])

    row = lax.broadcasted_iota(jnp.int32, (tq, tq), 0)
    col = lax.broadcasted_iota(jnp.int32, (tq, tq), 1)
    rmax = []
    for h in heads:
        s = jnp.where(col <= row, _dot_nt(qs[h], k_chunk(h, qi)), NEG)
        s_ref[h] = s
        rmax.append(rowmax(s))

    def body(j, carry):
        v_idx = jnp.where(j == 0, qi, j - 1)
        s_cur = [s_ref[h] for h in heads]
        s_new = [_dot_nt(qs[h], k_chunk(h, j)) for h in heads]
        out = []
        for h in heads:
            m_h, acc_h, rmax_h = carry[h]
            m_h, acc_h = softmax_pv(s_cur[h], rmax_h, v_chunk(h, v_idx), m_h, acc_h)
            s_ref[h] = s_new[h]
            out.append((m_h, acc_h, rowmax(s_new[h])))
        return tuple(out)

    carry = lax.fori_loop(0, qi, body, tuple((m[h], acc[h], rmax[h]) for h in heads))
    v_idx = jnp.where(qi > 0, qi - 1, qi)
    for h in heads:
        m_h, acc_h, rmax_h = carry[h]
        _, acc[h] = softmax_pv(s_ref[h], rmax_h, v_chunk(h, v_idx), m_h, acc_h)
    l0 = acc[0][:, ONES_LANE[0]:ONES_LANE[0] + 1]
    l1 = acc[1][:, ONES_LANE[1]:ONES_LANE[1] + 1]
    lane = lax.broadcasted_iota(jnp.int32, (tq, LANES), 1)
    o_ref[0] = jnp.where(lane < HEAD_DIM, acc[0] / l0, acc[1] / l1)


def _fox_attn(qp, kp, vp, kpre, vpre, tq):
    b, _, l, _ = qp.shape
    pre = pl.BlockSpec((1, 2, PREFIX_ROWS, LANES), lambda bi, p, qi: (0, p, 0, 0))
    full = pl.BlockSpec((1, 2, l, LANES), lambda bi, p, qi: (bi, p, 0, 0))
    return pl.pallas_call(
        functools.partial(_attn_kernel, tq=tq),
        grid=(b, N_PAIRS, l // tq),
        in_specs=[pl.BlockSpec((1, 2, tq, LANES), lambda bi, p, qi: (bi, p, qi, 0)),
                  pre, pre, full, full],
        out_specs=pl.BlockSpec((1, tq, PAIR), lambda bi, p, qi: (bi, qi, p)),
        out_shape=jax.ShapeDtypeStruct((b, l, D_BRANCH), F32),
        scratch_shapes=[pltpu.VMEM((2, tq, tq), F32)],
        compiler_params=_params(("parallel", "parallel", "parallel")),
        name="fox_attn",
    )(qp, kpre, vpre, kp, vp)


def _rwkv_kernel(r_ref, k_ref, v_ref, z_ref, wa_ref,
                 mu_r_ref, mu_k_ref, mu_v_ref, mu_wa_ref, w0_ref, wuph_ref, wupl_ref,
                 a0_ref, auph_ref, aupl_ref,
                 kk_ref, ka_ref, rk_ref, gnw_ref, gnb_ref, ones_ref,
                 z0_ref, pr_ref, pk_ref, pv_ref, pwa_ref,
                 y_ref, zf_ref, lr_ref, lk_ref, lv_ref, lwa_ref,
                 state_ref, sr_ref, sk_ref, sv_ref, swa_ref, yacc_ref, *, tr):
    t = pl.program_id(1)
    last = t == pl.num_programs(1) - 1

    @pl.when(t == 0)
    def _():
        state_ref[...] = z0_ref[...]
        sr_ref[...] = pr_ref[...]
        sk_ref[...] = pk_ref[...]
        sv_ref[...] = pv_ref[...]
        swa_ref[...] = pwa_ref[...]

    def shifted(cur, prev_ref, mu_ref):
        first = lax.broadcasted_iota(jnp.int32, cur.shape, 0) == 0
        prev = jnp.where(first, prev_ref[...], pltpu.roll(cur, 1, 0))
        prev_ref[...] = cur[tr - 1:tr, :]
        return cur + mu_ref[...] * (prev - cur)

    xr = shifted(r_ref[0], sr_ref, mu_r_ref)
    xk = shifted(k_ref[0], sk_ref, mu_k_ref)
    xv = shifted(v_ref[0], sv_ref, mu_v_ref)
    xwa = shifted(wa_ref[0], swa_ref, mu_wa_ref)

    @pl.when(last)
    def _():
        lr_ref[0] = sr_ref[...]
        lk_ref[0] = sk_ref[...]
        lv_ref[0] = sv_ref[...]
        lwa_ref[0] = swa_ref[...]

    ones_bd = ones_ref[...]

    def seg_sum(x, n):
        acc = None
        for piece in _pieces(x, n):
            t = _dot(piece, ones_bd)
            acc = t if acc is None else acc + t
        return acc

    w_lin = w0_ref[...] + _dot_split(jnp.tanh(xwa), wuph_ref[...], wupl_ref[...])
    a_lin = a0_ref[...] + _dot_split(xwa, auph_ref[...], aupl_ref[...])
    w = -_softplus(-w_lin) - 0.5
    ld = -jnp.exp(w)
    a = _sigmoid(a_lin)
    kk = xk * kk_ref[...]
    kk = kk * lax.rsqrt(seg_sum(kk * kk, 1) + KK_EPS)
    kmod = xk * (1.0 + (a - 1.0) * ka_ref[...])
    bvec = kk * a

    r2 = lax.broadcasted_iota(jnp.int32, (tr, tr), 0)
    c2 = lax.broadcasted_iota(jnp.int32, (tr, tr), 1)
    tri = jnp.where((r2 >= c2) & (r2 // CHUNK == c2 // CHUNK), 1.0, 0.0).astype(BF16)
    lw = _dot_exact_lhs(tri, ld, 3)
    w_inv = jnp.exp(-lw)
    rt = xr * jnp.exp(lw)
    at = -kk * jnp.exp(lw - ld)
    bt = bvec * w_inv
    kt = kmod * w_inv

    lane = lax.broadcasted_iota(jnp.int32, (CHUNK, PAIR), 1)
    head_a = lane < HEAD_DIM
    tau_r = lax.broadcasted_iota(jnp.int32, (PAIR, PAIR), 0)
    tau_c = lax.broadcasted_iota(jnp.int32, (PAIR, PAIR), 1)
    strict = (tau_r % CHUNK) > (tau_c % CHUNK)
    incl = (tau_r % CHUNK) >= (tau_c % CHUNK)
    eye = tau_r == tau_c
    eye_f = jnp.where(eye, 1.0, 0.0).astype(F32)

    def stack(x):
        return jnp.concatenate([jnp.where(head_a, x, 0.0), jnp.where(head_a, 0.0, x)], axis=0)

    n_chunks = tr // CHUNK
    units = [(c, p) for c in range(n_chunks) for p in range(N_PAIRS)]
    w_end = [jnp.exp(lw[(c + 1) * CHUNK - 1:(c + 1) * CHUNK, :]) for c in range(n_chunks)]

    def tile(x, c, p):
        return x[c * CHUNK:(c + 1) * CHUNK, p * PAIR:(p + 1) * PAIR]

    wc = [w_end[c][:, p * PAIR:(p + 1) * PAIR] for c, p in units]
    rs_f = [stack(tile(rt, c, p)) for c, p in units]
    r_s = [x.astype(BF16) for x in rs_f]
    a_s = [stack(tile(at, c, p)).astype(BF16) for c, p in units]
    b_s = [stack(tile(bt, c, p)).astype(BF16) for c, p in units]
    k_s = [stack(tile(kt, c, p)).astype(BF16) for c, p in units]
    v_s = [stack(tile(xv, c, p)).astype(BF16) for c, p in units]
    bh_s = [stack(tile(bt, c, p) * wc[u]).astype(BF16) for u, (c, p) in enumerate(units)]
    kh_s = [stack(tile(kt, c, p) * wc[u]).astype(BF16) for u, (c, p) in enumerate(units)]
    nu = range(len(units))

    a_ab = [jnp.where(strict, _dot_nt(a_s[u], b_s[u]), 0.0) for u in nu]
    a_ak = [jnp.where(strict, _dot_nt(a_s[u], k_s[u]), 0.0).astype(BF16) for u in nu]
    a_rb = [jnp.where(incl, _dot_nt(r_s[u], b_s[u]), 0.0).astype(BF16) for u in nu]
    a_rk = [jnp.where(incl, _dot_nt(r_s[u], k_s[u]), 0.0).astype(BF16) for u in nu]

    pw = a_ab
    tinv = [eye_f + a_ab[u] for u in nu]
    for _ in range(5):
        pb = [x.astype(BF16) for x in pw]
        pw = [_dot(pb[u], pb[u]) for u in nu]
        tinv = [tinv[u] + _dot(tinv[u].astype(BF16), pw[u].astype(BF16)) for u in nu]
    tb = [x.astype(BF16) for x in tinv]

    ap = [_dot(tb[u], a_s[u]).astype(BF16) for u in nu]
    akv = [_dot(a_ak[u], v_s[u]).astype(BF16) for u in nu]
    vp = [_dot(tb[u], akv[u]).astype(BF16) for u in nu]
    m_mat = [(jnp.where(eye, wc[u], 0.0) + _dot_tn(bh_s[u], ap[u])).astype(BF16) for u in nu]
    rp = [(rs_f[u] + _dot(a_rb[u], ap[u])).astype(BF16) for u in nu]
    vv = [jnp.concatenate([vp[u], v_s[u]], axis=0) for u in nu]
    g_mat = [_dot_tn(jnp.concatenate([bh_s[u], kh_s[u]], axis=0), vv[u]) for u in nu]
    y0 = [_dot(jnp.concatenate([a_rb[u], a_rk[u]], axis=1), vv[u]) for u in nu]

    for u, (c, p) in enumerate(units):
        zb = state_ref[p].astype(BF16)
        ys = _dot(rp[u], zb) + y0[u]
        state_ref[p] = _dot(m_mat[u], zb) + g_mat[u]
        yacc_ref[c * CHUNK:(c + 1) * CHUNK, p * PAIR:(p + 1) * PAIR] = ys[:CHUNK] + ys[CHUNK:]

    @pl.when(last)
    def _():
        zf_ref[0] = state_ref[...]

    y = yacc_ref[...]
    inv_n = 1.0 / HEAD_DIM
    mean = seg_sum(y, 2) * inv_n
    d = y - mean
    var = seg_sum(d * d, 2) * inv_n
    yn = d * lax.rsqrt(var + GN_EPS) * gnw_ref[...] + gnb_ref[...]
    bonus = seg_sum(xr * kmod * rk_ref[...], 2) * xv
    z = z_ref[0]
    y_ref[0] = ((yn + bonus) * (z * _sigmoid(z))).astype(y_ref.dtype)


def _rwkv(r, k, v, z, wa, prm, z0, prev, tr):
    b, l, _ = r.shape
    wide = pl.BlockSpec((1, tr, D_BRANCH), lambda bi, t: (bi, t, 0))
    narrow = pl.BlockSpec((1, tr, LANES), lambda bi, t: (bi, t, 0))
    row_w = pl.BlockSpec((1, D_BRANCH), lambda bi, t: (0, 0))
    row_n = pl.BlockSpec((1, LANES), lambda bi, t: (0, 0))
    up = pl.BlockSpec((LANES, D_BRANCH), lambda bi, t: (0, 0))
    ones = pl.BlockSpec((D_BRANCH, D_BRANCH), lambda bi, t: (0, 0))
    st_in = pl.BlockSpec((N_PAIRS, PAIR, PAIR), lambda bi, t: (0, 0, 0))
    st_out = pl.BlockSpec((1, N_PAIRS, PAIR, PAIR), lambda bi, t: (bi, 0, 0, 0))
    lrow_w = pl.BlockSpec((1, 1, D_BRANCH), lambda bi, t: (bi, 0, 0))
    lrow_n = pl.BlockSpec((1, 1, LANES), lambda bi, t: (bi, 0, 0))
    return pl.pallas_call(
        functools.partial(_rwkv_kernel, tr=tr),
        grid=(b, l // tr),
        in_specs=[wide, wide, wide, wide, narrow,
                  row_w, row_w, row_w, row_n, row_w, up, up, row_w, up, up,
                  row_w, row_w, row_w, row_w, row_w, ones,
                  st_in, row_w, row_w, row_w, row_n],
        out_specs=[wide, st_out, lrow_w, lrow_w, lrow_w, lrow_n],
        out_shape=[jax.ShapeDtypeStruct((b, l, D_BRANCH), BF16),
                   jax.ShapeDtypeStruct((b, N_PAIRS, PAIR, PAIR), F32),
                   jax.ShapeDtypeStruct((b, 1, D_BRANCH), F32),
                   jax.ShapeDtypeStruct((b, 1, D_BRANCH), F32),
                   jax.ShapeDtypeStruct((b, 1, D_BRANCH), F32),
                   jax.ShapeDtypeStruct((b, 1, LANES), F32)],
        scratch_shapes=[pltpu.VMEM((N_PAIRS, PAIR, PAIR), F32),
                        pltpu.VMEM((1, D_BRANCH), F32), pltpu.VMEM((1, D_BRANCH), F32),
                        pltpu.VMEM((1, D_BRANCH), F32), pltpu.VMEM((1, LANES), F32),
                        pltpu.VMEM((tr, D_BRANCH), F32)],
        compiler_params=_params(("parallel", "arbitrary")),
        name="rwkv",
    )(r, k, v, z, wa, *prm, z0, *prev)


def _out_proj_kernel(o_ref, zf_ref, yr_ref, x_ref, w_ref, fnw_ref, out_ref):
    z = zf_ref[...]
    yf = (o_ref[...] * (z * _sigmoid(z))).astype(BF16)
    mix = _dot(yf, w_ref[:D_BRANCH, :]) + _dot(yr_ref[...], w_ref[D_BRANCH:, :])
    h = x_ref[...] + mix
    out_ref[...] = h * lax.rsqrt(jnp.mean(h * h, axis=-1, keepdims=True) + NORM_EPS) * fnw_ref[...]


def _out_proj(o, zf, yr, x, w_out, fnw, tm):
    n = x.shape[0]
    half = pl.BlockSpec((tm, D_BRANCH), lambda i: (i, 0))
    full = pl.BlockSpec((tm, D_MODEL), lambda i: (i, 0))
    return pl.pallas_call(
        _out_proj_kernel,
        grid=(n // tm,),
        in_specs=[half, half, half, full,
                  pl.BlockSpec((D_MODEL, D_MODEL), lambda i: (0, 0)),
                  pl.BlockSpec((1, D_MODEL), lambda i: (0, 0))],
        out_specs=full,
        out_shape=jax.ShapeDtypeStruct((n, D_MODEL), F32),
        compiler_params=_params(("parallel",)),
        name="out_proj",
    )(o, zf, yr, x, w_out, fnw)


def _tiles(b, l):
    rows = b * l
    tm = 512 if rows % 512 == 0 else 256
    return tm, 256, 512, 128


def kernel(x, meta, norm_w, w_in, b_f, mu_shift, w0, w_up, a0, a_up, k_k, k_a, r_k, gn_w, gn_b,
           w_out, final_norm_w):
    b, l, d = x.shape
    assert d == D_MODEL and norm_w.shape[0] == 1 and l % 256 == 0
    tm, t_prep, t_attn, t_rwkv = _tiles(b, l)

    wi = w_in[0]
    o = 0
    cols = {}
    for name, width in (("q", D_BRANCH), ("k", D_BRANCH), ("v", D_BRANCH), ("fl", N_HEADS), ("zf", D_BRANCH),
                        ("r", D_BRANCH), ("rk", D_BRANCH), ("rv", D_BRANCH), ("wd", RANK), ("ad", RANK),
                        ("zr", D_BRANCH)):
        cols[name] = wi[:, o:o + width]
        o += width
    w_all = jnp.concatenate(
        [cols[n] for n in ("q", "k", "v", "zf", "r", "rk", "rv", "zr", "wd", "ad", "fl")]
        + [jnp.zeros((D_MODEL, LANES - N_HEADS), F32)], axis=1).astype(BF16)

    row = lambda vec: vec.reshape(1, -1).astype(F32)
    mu = mu_shift[0]
    mu_r, mu_k, mu_v = (row(mu[i * D_BRANCH:(i + 1) * D_BRANCH]) for i in range(3))
    mu_wa = row(mu[3 * D_BRANCH:])
    zeros_up = jnp.zeros((RANK, D_BRANCH), F32)
    wup_pad = jnp.concatenate([w_up[0], zeros_up], axis=0)
    aup_pad = jnp.concatenate([zeros_up, a_up[0]], axis=0)
    hid = np.arange(D_BRANCH) // HEAD_DIM
    ones_bd = jnp.asarray(hid[:, None] == hid[None, :], BF16)
    hi_lo = lambda w: (w.astype(BF16), (w - w.astype(BF16).astype(F32)).astype(BF16))
    rwkv_prm = (mu_r, mu_k, mu_v, mu_wa, row(w0[0]), *hi_lo(wup_pad), row(a0[0]), *hi_lo(aup_pad),
                row(k_k[0]), row(k_a[0]), row(r_k[0]), row(gn_w[0]), row(gn_b[0]), ones_bd)
    bf_pad = jnp.concatenate([b_f[0], jnp.zeros((LANES - N_HEADS,), F32)]).reshape(1, LANES)
    pq, pk = _select_matrices()
    nw = row(norm_w[0])

    pre_rows = jnp.concatenate([jnp.zeros((PREFIX_ROWS - N_META, D_MODEL), F32), meta.astype(F32)], axis=0)
    pq_, pk_, pv_, _, pr_, prk_, prv_, pzr_, pwa_, pfl_ = _in_proj(pre_rows, nw, w_all, PREFIX_ROWS)
    lead = lambda a: a[None]
    _, kpre, vpre, c_pre = _fox_prep(lead(pq_), lead(pk_), lead(pv_), lead(pfl_), bf_pad,
                                     jnp.zeros((1, LANES), F32), pq, pk, PREFIX_ROWS, PREFIX_ROWS - N_META)
    zero_w = jnp.zeros((1, D_BRANCH), F32)
    _, z_pre, lr, lk, lv, lwa = _rwkv(
        lead(pr_), lead(prk_), lead(prv_), lead(pzr_), lead(pwa_), rwkv_prm,
        jnp.zeros((N_PAIRS, PAIR, PAIR), F32), (zero_w, zero_w, zero_w, jnp.zeros((1, LANES), F32)), PREFIX_ROWS)

    xf = x.reshape(b * l, D_MODEL)
    q_, k_, v_, zf_, r_, rk_, rv_, zr_, wa_, fl_ = _in_proj(xf, nw, w_all, tm)
    bl = lambda a: a.reshape(b, l, a.shape[-1])
    qp, kp, vp, _ = _fox_prep(bl(q_), bl(k_), bl(v_), bl(fl_), bf_pad, c_pre[0], pq, pk, t_prep, 0)
    o_attn = _fox_attn(qp, kp, vp, kpre, vpre, t_attn)
    y_rwkv, _, _, _, _, _ = _rwkv(bl(r_), bl(rk_), bl(rv_), bl(zr_), bl(wa_), rwkv_prm,
                                  z_pre[0], (lr[0], lk[0], lv[0], lwa[0]), t_rwkv)
    out = _out_proj(o_attn.reshape(b * l, D_BRANCH), zf_, y_rwkv.reshape(b * l, D_BRANCH), xf,
                    w_out[0].astype(BF16), row(final_norm_w), tm)
    return out.reshape(b, l, D_MODEL)
```

```python
import functools

import numpy as np
import jax
import jax.numpy as jnp
from jax import lax
from jax.experimental import pallas as pl
from jax.experimental.pallas import tpu as pltpu

F32 = jnp.float32
BF16 = jnp.bfloat16

D_MODEL = 1024
N_META = 16
HEAD_DIM = 64
N_HEADS = 8
D_BRANCH = N_HEADS * HEAD_DIM
N_PAIRS = N_HEADS // 2
RANK = 64
NORM_EPS = 1e-6
GN_EPS = 64e-5
KK_EPS = 1e-12
NEG = -1e30

LANES = 128
PREFIX_ROWS = 128
CHUNK = 64
SUB = 2 * CHUNK
PAIR = 2 * HEAD_DIM
VMEM_LIMIT = 56 * 1024 * 1024

N_WIDE = 8
COL_WA = N_WIDE * D_BRANCH
COL_FL = COL_WA + LANES
N_COLS = COL_FL + LANES

N_CPARTS = 3
ONES_LANE = (HEAD_DIM, 0)


def _dot(a, b):
    return jnp.dot(a, b, preferred_element_type=F32)


def _pieces(x, n):
    out = []
    for _ in range(n - 1):
        p = x.astype(BF16)
        out.append(p)
        x = x - p.astype(F32)
    out.append(x.astype(BF16))
    return out


def _dot_exact_lhs(a_bf16, x, n):
    acc = None
    for p in _pieces(x, n):
        t = _dot(a_bf16, p)
        acc = t if acc is None else acc + t
    return acc


def _dot_split(x, w_hi, w_lo):
    x_hi, x_lo = _pieces(x, 2)
    return _dot(x_hi, w_hi) + _dot(x_lo, w_hi) + _dot(x_hi, w_lo)


def _dot_nt(a, b):
    return lax.dot_general(a, b, (((1,), (1,)), ((), ())), preferred_element_type=F32)


def _dot_tn(a, b):
    return lax.dot_general(a, b, (((0,), (0,)), ((), ())), preferred_element_type=F32)


def _softplus(x):
    return jnp.maximum(x, 0.0) + jnp.log(1.0 + jnp.exp(-jnp.abs(x)))


def _sigmoid(x):
    return 1.0 / (1.0 + jnp.exp(-x))


def _params(sem):
    return pltpu.CompilerParams(dimension_semantics=sem, vmem_limit_bytes=VMEM_LIMIT)


def _in_proj_kernel(x_ref, nw_ref, w_ref, *out_refs):
    x = x_ref[...]
    u = x * lax.rsqrt(jnp.mean(x * x, axis=-1, keepdims=True) + NORM_EPS) * nw_ref[...]
    ub = u.astype(BF16)
    for g in range(N_WIDE):
        o = out_refs[g]
        o[...] = _dot(ub, w_ref[:, g * D_BRANCH:(g + 1) * D_BRANCH]).astype(o.dtype)
    out_refs[N_WIDE][...] = _dot(ub, w_ref[:, COL_WA:COL_WA + LANES])
    out_refs[N_WIDE + 1][...] = _dot(ub, w_ref[:, COL_FL:COL_FL + LANES])


def _in_proj(rows, norm_w, w_all, tm):
    n = rows.shape[0]
    wide_dtypes = (BF16, BF16, BF16, F32, F32, F32, F32, F32)
    out_shape = [jax.ShapeDtypeStruct((n, D_BRANCH), dt) for dt in wide_dtypes]
    out_shape += [jax.ShapeDtypeStruct((n, LANES), F32)] * 2
    out_specs = [pl.BlockSpec((tm, D_BRANCH), lambda i: (i, 0))] * N_WIDE
    out_specs += [pl.BlockSpec((tm, LANES), lambda i: (i, 0))] * 2
    return pl.pallas_call(
        _in_proj_kernel,
        grid=(n // tm,),
        in_specs=[pl.BlockSpec((tm, D_MODEL), lambda i: (i, 0)),
                  pl.BlockSpec((1, D_MODEL), lambda i: (0, 0)),
                  pl.BlockSpec((D_MODEL, N_COLS), lambda i: (0, 0))],
        out_specs=out_specs,
        out_shape=out_shape,
        compiler_params=_params(("parallel",)),
        name="in_proj",
    )(rows, norm_w, w_all)


def _select_matrices():
    pq = np.zeros((N_HEADS, 2 * LANES, LANES), np.float32)
    pk = np.zeros((N_HEADS, 2 * LANES, LANES), np.float32)
    for h in range(N_HEADS):
        base = (h % 2) * HEAD_DIM
        for d in range(HEAD_DIM):
            pq[h, base + d, d] = 0.125
            pk[h, base + d, d] = 1.0
        for part in range(N_CPARTS):
            pq[h, LANES + part * N_HEADS + h, HEAD_DIM + part] = 1.0
            pq[h, LANES + N_CPARTS * N_HEADS + h, HEAD_DIM + N_CPARTS + part] = 1.0
            pk[h, LANES + N_CPARTS * N_HEADS + h, HEAD_DIM + part] = 1.0
            pk[h, LANES + part * N_HEADS + h, HEAD_DIM + N_CPARTS + part] = -1.0
    return jnp.asarray(pq, BF16), jnp.asarray(pk, BF16)


def _fox_prep_kernel(q_ref, k_ref, v_ref, fl_ref, bf_ref, c0_ref, pq_ref, pk_ref,
                     qo_ref, ko_ref, vo_ref, cl_ref, carry_ref, *, tr, n_pad):
    i = pl.program_id(1)

    @pl.when(i == 0)
    def _():
        carry_ref[...] = c0_ref[...]

    x = fl_ref[0] + bf_ref[...]
    logf = jnp.minimum(x, 0.0) - jnp.log(1.0 + jnp.exp(-jnp.abs(x)))
    lane = lax.broadcasted_iota(jnp.int32, (tr, LANES), 1)
    row = lax.broadcasted_iota(jnp.int32, (tr, LANES), 0) + i * tr
    valid = lane < N_HEADS
    if n_pad:
        valid = valid & (row >= n_pad)
    logf = jnp.where(valid, logf, 0.0)
    r2 = lax.broadcasted_iota(jnp.int32, (tr, tr), 0)
    c2 = lax.broadcasted_iota(jnp.int32, (tr, tr), 1)
    tri = jnp.where(r2 >= c2, 1.0, 0.0).astype(BF16)
    cum = _dot_exact_lhs(tri, logf, 3) + carry_ref[...]
    carry_ref[...] = cum[tr - 1:tr, :]

    @pl.when(i == pl.num_programs(1) - 1)
    def _():
        cl_ref[0] = cum[tr - 1:tr, :]

    p1 = cum.astype(BF16).astype(F32)
    rem = cum - p1
    p2 = rem.astype(BF16).astype(F32)
    p3 = (rem - p2).astype(BF16).astype(F32)
    ones = jnp.where((lane >= N_CPARTS * N_HEADS) & (lane < (N_CPARTS + 1) * N_HEADS), 1.0, 0.0)
    cbits = p1 + pltpu.roll(p2, N_HEADS, 1) + pltpu.roll(p3, 2 * N_HEADS, 1) + ones
    cbits_k = cbits
    if n_pad:
        cbits_k = jnp.where((row < n_pad) & (lane < N_HEADS), -NEG, cbits)
    cq = cbits.astype(BF16)
    ck = cbits_k.astype(BF16)
    lane_lo = lane < HEAD_DIM
    for h in range(N_HEADS):
        sl = slice((h // 2) * PAIR, (h // 2 + 1) * PAIR)
        xq = jnp.concatenate([q_ref[0, :, sl], cq], axis=1)
        xk = jnp.concatenate([k_ref[0, :, sl], ck], axis=1)
        qo_ref[0, h] = _dot(xq, pq_ref[h]).astype(BF16)
        ko_ref[0, h] = _dot(xk, pk_ref[h]).astype(BF16)
        vp = v_ref[0, :, sl]
        keep = lane_lo if h % 2 == 0 else jnp.logical_not(lane_lo)
        one = jnp.where(lane == ONES_LANE[h % 2], 1.0, 0.0).astype(BF16)
        vo_ref[0, h] = jnp.where(keep, vp, one)


def _fox_prep(q, k, v, fl, bf_pad, c0, pq, pk, tr, n_pad):
    b, l, _ = q.shape
    wide = pl.BlockSpec((1, tr, D_BRANCH), lambda bi, i: (bi, i, 0))
    head_out = pl.BlockSpec((1, N_HEADS, tr, LANES), lambda bi, i: (bi, 0, i, 0))
    row128 = pl.BlockSpec((1, LANES), lambda bi, i: (0, 0))
    sel = pl.BlockSpec((N_HEADS, 2 * LANES, LANES), lambda bi, i: (0, 0, 0))
    return pl.pallas_call(
        functools.partial(_fox_prep_kernel, tr=tr, n_pad=n_pad),
        grid=(b, l // tr),
        in_specs=[wide, wide, wide,
                  pl.BlockSpec((1, tr, LANES), lambda bi, i: (bi, i, 0)),
                  row128, row128, sel, sel],
        out_specs=[head_out, head_out, head_out,
                   pl.BlockSpec((1, 1, LANES), lambda bi, i: (bi, 0, 0))],
        out_shape=[jax.ShapeDtypeStruct((b, N_HEADS, l, LANES), BF16)] * 3
                  + [jax.ShapeDtypeStruct((b, 1, LANES), F32)],
        scratch_shapes=[pltpu.VMEM((1, LANES), F32)],
        compiler_params=_params(("parallel", "arbitrary")),
        name="fox_prep",
    )(q, k, v, fl, bf_pad, c0, pq, pk)


def _attn_kernel(q_ref, kpre_ref, vpre_ref, k_ref, v_ref, o_ref, s_ref, *, tq):
    qi = pl.program_id(2)
    heads = range(2)
    qs = [q_ref[0, h] for h in heads]

    def softmax_pv(s, rmax, vc, m, acc):
        m_new = jnp.maximum(m, rmax)
        alpha = jnp.exp(m - m_new)
        p = jnp.exp(s - m_new).astype(BF16)
        return m_new, alpha * acc + _dot(p, vc)

    def rowmax(s):
        return jnp.max(s, axis=-1, keepdims=True)

    def k_chunk(h, idx):
        return k_ref[0, h, pl.ds(pl.multiple_of(idx * tq, tq), tq), :]

    def v_chunk(h, idx):
        return v_ref[0, h, pl.ds(pl.multiple_of(idx * tq, tq), tq), :]

    m = [jnp.full((tq, 1), NEG, F32) for _ in heads]
    acc = [jnp.zeros((tq, LANES), F32) for _ in heads]
    for h in heads:
        s = _dot_nt(qs[h], kpre_ref[0, h])
        m[h], acc[h] = softmax_pv(s, rowmax(s), vpre_ref[0, h], m[h], acc[h])

    row = lax.broadcasted_iota(jnp.int32, (tq, tq), 0)
    col = lax.broadcasted_iota(jnp.int32, (tq, tq), 1)
    rmax = []
    for h in heads:
        s = jnp.where(col <= row, _dot_nt(qs[h], k_chunk(h, qi)), NEG)
        s_ref[h] = s
        rmax.append(rowmax(s))

    def body(j, carry):
        v_idx = jnp.where(j == 0, qi, j - 1)
        s_cur = [s_ref[h] for h in heads]
        s_new = [_dot_nt(qs[h], k_chunk(h, j)) for h in heads]
        out = []
        for h in heads:
            m_h, acc_h, rmax_h = carry[h]
            m_h, acc_h = softmax_pv(s_cur[h], rmax_h, v_chunk(h, v_idx), m_h, acc_h)
            s_ref[h] = s_new[h]
            out.append((m_h, acc_h, rowmax(s_new[h])))
        return tuple(out)

    carry = lax.fori_loop(0, qi, body, tuple((m[h], acc[h], rmax[h]) for h in heads))
    v_idx = jnp.where(qi > 0, qi - 1, qi)
    for h in heads:
        m_h, acc_h, rmax_h = carry[h]
        _, acc[h] = softmax_pv(s_ref[h], rmax_h, v_chunk(h, v_idx), m_h, acc_h)
    l0 = acc[0][:, ONES_LANE[0]:ONES_LANE[0] + 1]
    l1 = acc[1][:, ONES_LANE[1]:ONES_LANE[1] + 1]
    lane = lax.broadcasted_iota(jnp.int32, (tq, LANES), 1)
    o_ref[0] = jnp.where(lane < HEAD_DIM, acc[0] / l0, acc[1] / l1)


def _fox_attn(qp, kp, vp, kpre, vpre, tq):
    b, _, l, _ = qp.shape
    pre = pl.BlockSpec((1, 2, PREFIX_ROWS, LANES), lambda bi, p, qi: (0, p, 0, 0))
    full = pl.BlockSpec((1, 2, l, LANES), lambda bi, p, qi: (bi, p, 0, 0))
    return pl.pallas_call(
        functools.partial(_attn_kernel, tq=tq),
        grid=(b, N_PAIRS, l // tq),
        in_specs=[pl.BlockSpec((1, 2, tq, LANES), lambda bi, p, qi: (bi, p, qi, 0)),
                  pre, pre, full, full],
        out_specs=pl.BlockSpec((1, tq, PAIR), lambda bi, p, qi: (bi, qi, p)),
        out_shape=jax.ShapeDtypeStruct((b, l, D_BRANCH), F32),
        scratch_shapes=[pltpu.VMEM((2, tq, tq), F32)],
        compiler_params=_params(("parallel", "parallel", "parallel")),
        name="fox_attn",
    )(qp, kpre, vpre, kp, vp)


def _rwkv_kernel(r_ref, k_ref, v_ref, z_ref, wa_ref,
                 mu_r_ref, mu_k_ref, mu_v_ref, mu_wa_ref, w0_ref, wuph_ref, wupl_ref,
                 a0_ref, auph_ref, aupl_ref,
                 kk_ref, ka_ref, rk_ref, gnw_ref, gnb_ref, ones_ref,
                 z0_ref, pr_ref, pk_ref, pv_ref, pwa_ref,
                 y_ref, zf_ref, lr_ref, lk_ref, lv_ref, lwa_ref,
                 state_ref, sr_ref, sk_ref, sv_ref, swa_ref, yacc_ref, *, tr):
    t = pl.program_id(1)
    last = t == pl.num_programs(1) - 1

    @pl.when(t == 0)
    def _():
        state_ref[...] = z0_ref[...]
        sr_ref[...] = pr_ref[...]
        sk_ref[...] = pk_ref[...]
        sv_ref[...] = pv_ref[...]
        swa_ref[...] = pwa_ref[...]

    ones_bd = ones_ref[...]

    def seg_sum(x):
        return _dot(x.astype(BF16), ones_bd)

    def shifted(cur, prev_row, mu_ref):
        first = lax.broadcasted_iota(jnp.int32, cur.shape, 0) == 0
        prev = jnp.where(first, prev_row, pltpu.roll(cur, 1, 0))
        return cur + mu_ref[...] * (prev - cur)

    r2 = lax.broadcasted_iota(jnp.int32, (SUB, SUB), 0)
    c2 = lax.broadcasted_iota(jnp.int32, (SUB, SUB), 1)
    tri = jnp.where((r2 >= c2) & (r2 // CHUNK == c2 // CHUNK), 1.0, 0.0).astype(BF16)

    n_sub = tr // SUB
    prev_rows = [sr_ref[...], sk_ref[...], sv_ref[...], swa_ref[...]]
    mus = (mu_r_ref, mu_k_ref, mu_v_ref, mu_wa_ref)
    preps = []
    for sb in range(n_sub):
        rows = slice(sb * SUB, (sb + 1) * SUB)
        cur = [ref[0, rows, :] for ref in (r_ref, k_ref, v_ref, wa_ref)]
        xr, xk, xv, xwa = [shifted(c, p, mu) for c, p, mu in zip(cur, prev_rows, mus)]
        prev_rows = [c[SUB - 1:SUB, :] for c in cur]
        w_lin = w0_ref[...] + _dot_split(jnp.tanh(xwa), wuph_ref[...], wupl_ref[...])
        a_lin = a0_ref[...] + _dot_split(xwa, auph_ref[...], aupl_ref[...])
        w = -_softplus(-w_lin) - 0.5
        ld = -jnp.exp(w)
        a = _sigmoid(a_lin)
        kk = xk * kk_ref[...]
        kk = kk * lax.rsqrt(seg_sum(kk * kk) + KK_EPS)
        kmod = xk * (1.0 + (a - 1.0) * ka_ref[...])
        lw = _dot_exact_lhs(tri, ld, 3)
        w_inv = jnp.exp(-lw)
        preps.append(dict(xr=xr, xv=xv, kmod=kmod, lw=lw, rt=xr * jnp.exp(lw), at=-kk * jnp.exp(lw - ld),
                          bt=kk * a * w_inv, kt=kmod * w_inv))
    for ref, rowv in zip((sr_ref, sk_ref, sv_ref, swa_ref), prev_rows):
        ref[...] = rowv

    @pl.when(last)
    def _():
        for out, rowv in zip((lr_ref, lk_ref, lv_ref, lwa_ref), prev_rows):
            out[0] = rowv

    lane = lax.broadcasted_iota(jnp.int32, (CHUNK, PAIR), 1)
    head_a = lane < HEAD_DIM
    tau_r = lax.broadcasted_iota(jnp.int32, (PAIR, PAIR), 0)
    tau_c = lax.broadcasted_iota(jnp.int32, (PAIR, PAIR), 1)
    strict = (tau_r % CHUNK) > (tau_c % CHUNK)
    incl = (tau_r % CHUNK) >= (tau_c % CHUNK)
    eye = tau_r == tau_c
    eye_f = jnp.where(eye, 1.0, 0.0).astype(F32)

    def stack(x):
        return jnp.concatenate([jnp.where(head_a, x, 0.0), jnp.where(head_a, 0.0, x)], axis=0)

    n_chunks = tr // CHUNK
    per_sub = SUB // CHUNK
    units = [(c, p) for c in range(n_chunks) for p in range(N_PAIRS)]
    w_end = [jnp.exp(preps[c // per_sub]["lw"][(c % per_sub + 1) * CHUNK - 1:(c % per_sub + 1) * CHUNK, :])
             for c in range(n_chunks)]

    def tile(name, c, p):
        lo = (c % per_sub) * CHUNK
        return preps[c // per_sub][name][lo:lo + CHUNK, p * PAIR:(p + 1) * PAIR]

    wc = [w_end[c][:, p * PAIR:(p + 1) * PAIR] for c, p in units]
    rs_f = [stack(tile("rt", c, p)) for c, p in units]
    r_s = [x.astype(BF16) for x in rs_f]
    a_s = [stack(tile("at", c, p)).astype(BF16) for c, p in units]
    b_s = [stack(tile("bt", c, p)).astype(BF16) for c, p in units]
    k_s = [stack(tile("kt", c, p)).astype(BF16) for c, p in units]
    v_s = [stack(tile("xv", c, p)).astype(BF16) for c, p in units]
    bh_s = [stack(tile("bt", c, p) * wc[u]).astype(BF16) for u, (c, p) in enumerate(units)]
    kh_s = [stack(tile("kt", c, p) * wc[u]).astype(BF16) for u, (c, p) in enumerate(units)]
    nu = range(len(units))

    a_ab = [jnp.where(strict, _dot_nt(a_s[u], b_s[u]), 0.0) for u in nu]
    a_ak = [jnp.where(strict, _dot_nt(a_s[u], k_s[u]), 0.0).astype(BF16) for u in nu]
    a_rb = [jnp.where(incl, _dot_nt(r_s[u], b_s[u]), 0.0).astype(BF16) for u in nu]
    a_rk = [jnp.where(incl, _dot_nt(r_s[u], k_s[u]), 0.0).astype(BF16) for u in nu]

    pw = a_ab
    tinv = [eye_f + a_ab[u] for u in nu]
    for _ in range(5):
        pb = [x.astype(BF16) for x in pw]
        pw = [_dot(pb[u], pb[u]) for u in nu]
        tinv = [tinv[u] + _dot(tinv[u].astype(BF16), pw[u].astype(BF16)) for u in nu]
    tb = [x.astype(BF16) for x in tinv]

    ap = [_dot(tb[u], a_s[u]).astype(BF16) for u in nu]
    akv = [_dot(a_ak[u], v_s[u]).astype(BF16) for u in nu]
    vp = [_dot(tb[u], akv[u]).astype(BF16) for u in nu]
    m_mat = [(jnp.where(eye, wc[u], 0.0) + _dot_tn(bh_s[u], ap[u])).astype(BF16) for u in nu]
    rp = [(rs_f[u] + _dot(a_rb[u], ap[u])).astype(BF16) for u in nu]
    vv = [jnp.concatenate([vp[u], v_s[u]], axis=0) for u in nu]
    g_mat = [_dot_tn(jnp.concatenate([bh_s[u], kh_s[u]], axis=0), vv[u]) for u in nu]
    y0 = [_dot(jnp.concatenate([a_rb[u], a_rk[u]], axis=1), vv[u]) for u in nu]

    for u, (c, p) in enumerate(units):
        zb = state_ref[p].astype(BF16)
        ys = _dot(rp[u], zb) + y0[u]
        state_ref[p] = _dot(m_mat[u], zb) + g_mat[u]
        yacc_ref[c * CHUNK:(c + 1) * CHUNK, p * PAIR:(p + 1) * PAIR] = ys[:CHUNK] + ys[CHUNK:]

    @pl.when(last)
    def _():
        zf_ref[0] = state_ref[...]

    inv_n = 1.0 / HEAD_DIM
    for sb in range(n_sub):
        rows = slice(sb * SUB, (sb + 1) * SUB)
        pr = preps[sb]
        y = yacc_ref[rows, :]
        mean = seg_sum(y) * inv_n
        d = y - mean
        var = seg_sum(d * d) * inv_n
        yn = d * lax.rsqrt(var + GN_EPS) * gnw_ref[...] + gnb_ref[...]
        bonus = seg_sum(pr["xr"] * pr["kmod"] * rk_ref[...]) * pr["xv"]
        z = z_ref[0, rows, :]
        y_ref[0, rows, :] = ((yn + bonus) * (z * _sigmoid(z))).astype(y_ref.dtype)


def _rwkv(r, k, v, z, wa, prm, z0, prev, tr):
    b, l, _ = r.shape
    wide = pl.BlockSpec((1, tr, D_BRANCH), lambda bi, t: (bi, t, 0))
    narrow = pl.BlockSpec((1, tr, LANES), lambda bi, t: (bi, t, 0))
    row_w = pl.BlockSpec((1, D_BRANCH), lambda bi, t: (0, 0))
    row_n = pl.BlockSpec((1, LANES), lambda bi, t: (0, 0))
    up = pl.BlockSpec((LANES, D_BRANCH), lambda bi, t: (0, 0))
    ones = pl.BlockSpec((D_BRANCH, D_BRANCH), lambda bi, t: (0, 0))
    st_in = pl.BlockSpec((N_PAIRS, PAIR, PAIR), lambda bi, t: (0, 0, 0))
    st_out = pl.BlockSpec((1, N_PAIRS, PAIR, PAIR), lambda bi, t: (bi, 0, 0, 0))
    lrow_w = pl.BlockSpec((1, 1, D_BRANCH), lambda bi, t: (bi, 0, 0))
    lrow_n = pl.BlockSpec((1, 1, LANES), lambda bi, t: (bi, 0, 0))
    return pl.pallas_call(
        functools.partial(_rwkv_kernel, tr=tr),
        grid=(b, l // tr),
        in_specs=[wide, wide, wide, wide, narrow,
                  row_w, row_w, row_w, row_n, row_w, up, up, row_w, up, up,
                  row_w, row_w, row_w, row_w, row_w, ones,
                  st_in, row_w, row_w, row_w, row_n],
        out_specs=[wide, st_out, lrow_w, lrow_w, lrow_w, lrow_n],
        out_shape=[jax.ShapeDtypeStruct((b, l, D_BRANCH), BF16),
                   jax.ShapeDtypeStruct((b, N_PAIRS, PAIR, PAIR), F32),
                   jax.ShapeDtypeStruct((b, 1, D_BRANCH), F32),
                   jax.ShapeDtypeStruct((b, 1, D_BRANCH), F32),
                   jax.ShapeDtypeStruct((b, 1, D_BRANCH), F32),
                   jax.ShapeDtypeStruct((b, 1, LANES), F32)],
        scratch_shapes=[pltpu.VMEM((N_PAIRS, PAIR, PAIR), F32),
                        pltpu.VMEM((1, D_BRANCH), F32), pltpu.VMEM((1, D_BRANCH), F32),
                        pltpu.VMEM((1, D_BRANCH), F32), pltpu.VMEM((1, LANES), F32),
                        pltpu.VMEM((tr, D_BRANCH), F32)],
        compiler_params=_params(("parallel", "arbitrary")),
        name="rwkv",
    )(r, k, v, z, wa, *prm, z0, *prev)


def _out_proj_kernel(o_ref, zf_ref, yr_ref, x_ref, w_ref, fnw_ref, out_ref):
    z = zf_ref[...]
    yf = (o_ref[...] * (z * _sigmoid(z))).astype(BF16)
    mix = _dot(yf, w_ref[:D_BRANCH, :]) + _dot(yr_ref[...], w_ref[D_BRANCH:, :])
    h = x_ref[...] + mix
    out_ref[...] = h * lax.rsqrt(jnp.mean(h * h, axis=-1, keepdims=True) + NORM_EPS) * fnw_ref[...]


def _out_proj(o, zf, yr, x, w_out, fnw, tm):
    n = x.shape[0]
    half = pl.BlockSpec((tm, D_BRANCH), lambda i: (i, 0))
    full = pl.BlockSpec((tm, D_MODEL), lambda i: (i, 0))
    return pl.pallas_call(
        _out_proj_kernel,
        grid=(n // tm,),
        in_specs=[half, half, half, full,
                  pl.BlockSpec((D_MODEL, D_MODEL), lambda i: (0, 0)),
                  pl.BlockSpec((1, D_MODEL), lambda i: (0, 0))],
        out_specs=full,
        out_shape=jax.ShapeDtypeStruct((n, D_MODEL), F32),
        compiler_params=_params(("parallel",)),
        name="out_proj",
    )(o, zf, yr, x, w_out, fnw)


def _tiles(b, l):
    rows = b * l
    tm = 512 if rows % 512 == 0 else 256
    return tm, 256, 512, 256


def kernel(x, meta, norm_w, w_in, b_f, mu_shift, w0, w_up, a0, a_up, k_k, k_a, r_k, gn_w, gn_b,
           w_out, final_norm_w):
    b, l, d = x.shape
    assert d == D_MODEL and norm_w.shape[0] == 1 and l % 256 == 0
    tm, t_prep, t_attn, t_rwkv = _tiles(b, l)

    wi = w_in[0]
    o = 0
    cols = {}
    for name, width in (("q", D_BRANCH), ("k", D_BRANCH), ("v", D_BRANCH), ("fl", N_HEADS), ("zf", D_BRANCH),
                        ("r", D_BRANCH), ("rk", D_BRANCH), ("rv", D_BRANCH), ("wd", RANK), ("ad", RANK),
                        ("zr", D_BRANCH)):
        cols[name] = wi[:, o:o + width]
        o += width
    w_all = jnp.concatenate(
        [cols[n] for n in ("q", "k", "v", "zf", "r", "rk", "rv", "zr", "wd", "ad", "fl")]
        + [jnp.zeros((D_MODEL, LANES - N_HEADS), F32)], axis=1).astype(BF16)

    row = lambda vec: vec.reshape(1, -1).astype(F32)
    mu = mu_shift[0]
    mu_r, mu_k, mu_v = (row(mu[i * D_BRANCH:(i + 1) * D_BRANCH]) for i in range(3))
    mu_wa = row(mu[3 * D_BRANCH:])
    zeros_up = jnp.zeros((RANK, D_BRANCH), F32)
    wup_pad = jnp.concatenate([w_up[0], zeros_up], axis=0)
    aup_pad = jnp.concatenate([zeros_up, a_up[0]], axis=0)
    hid = np.arange(D_BRANCH) // HEAD_DIM
    ones_bd = jnp.asarray(hid[:, None] == hid[None, :], BF16)
    hi_lo = lambda w: (w.astype(BF16), (w - w.astype(BF16).astype(F32)).astype(BF16))
    rwkv_prm = (mu_r, mu_k, mu_v, mu_wa, row(w0[0]), *hi_lo(wup_pad), row(a0[0]), *hi_lo(aup_pad),
                row(k_k[0]), row(k_a[0]), row(r_k[0]), row(gn_w[0]), row(gn_b[0]), ones_bd)
    bf_pad = jnp.concatenate([b_f[0], jnp.zeros((LANES - N_HEADS,), F32)]).reshape(1, LANES)
    pq, pk = _select_matrices()
    nw = row(norm_w[0])

    pre_rows = jnp.concatenate([jnp.zeros((PREFIX_ROWS - N_META, D_MODEL), F32), meta.astype(F32)], axis=0)
    pq_, pk_, pv_, _, pr_, prk_, prv_, pzr_, pwa_, pfl_ = _in_proj(pre_rows, nw, w_all, PREFIX_ROWS)
    lead = lambda a: a[None]
    _, kpre, vpre, c_pre = _fox_prep(lead(pq_), lead(pk_), lead(pv_), lead(pfl_), bf_pad,
                                     jnp.zeros((1, LANES), F32), pq, pk, PREFIX_ROWS, PREFIX_ROWS - N_META)
    zero_w = jnp.zeros((1, D_BRANCH), F32)
    _, z_pre, lr, lk, lv, lwa = _rwkv(
        lead(pr_), lead(prk_), lead(prv_), lead(pzr_), lead(pwa_), rwkv_prm,
        jnp.zeros((N_PAIRS, PAIR, PAIR), F32), (zero_w, zero_w, zero_w, jnp.zeros((1, LANES), F32)), PREFIX_ROWS)

    xf = x.reshape(b * l, D_MODEL)
    q_, k_, v_, zf_, r_, rk_, rv_, zr_, wa_, fl_ = _in_proj(xf, nw, w_all, tm)
    bl = lambda a: a.reshape(b, l, a.shape[-1])
    qp, kp, vp, _ = _fox_prep(bl(q_), bl(k_), bl(v_), bl(fl_), bf_pad, c_pre[0], pq, pk, t_prep, 0)
    o_attn = _fox_attn(qp, kp, vp, kpre, vpre, t_attn)
    y_rwkv, _, _, _, _, _ = _rwkv(bl(r_), bl(rk_), bl(rv_), bl(zr_), bl(wa_), rwkv_prm,
                                  z_pre[0], (lr[0], lk[0], lv[0], lwa[0]), t_rwkv)
    out = _out_proj(o_attn.reshape(b * l, D_BRANCH), zf_, y_rwkv.reshape(b * l, D_BRANCH), xf,
                    w_out[0].astype(BF16), row(final_norm_w), tm)
    return out.reshape(b, l, D_MODEL)
```

```python
import functools

import numpy as np
import jax
import jax.numpy as jnp
from jax import lax
from jax.experimental import pallas as pl
from jax.experimental.pallas import tpu as pltpu

F32 = jnp.float32
BF16 = jnp.bfloat16

D_MODEL = 1024
N_META = 16
HEAD_DIM = 64
N_HEADS = 8
D_BRANCH = N_HEADS * HEAD_DIM
N_PAIRS = N_HEADS // 2
RANK = 64
NORM_EPS = 1e-6
GN_EPS = 64e-5
KK_EPS = 1e-12
NEG = -1e30

LANES = 128
PREFIX_ROWS = 128
CHUNK = 64
SUB = 2 * CHUNK
PAIR = 2 * HEAD_DIM
VMEM_LIMIT = 56 * 1024 * 1024

N_WIDE = 8
COL_WA = N_WIDE * D_BRANCH
COL_FL = COL_WA + LANES
N_COLS = COL_FL + LANES

N_CPARTS = 3
ONES_LANE = (HEAD_DIM, 0)
ATTN_HEADS = 4


def _dot(a, b):
    return jnp.dot(a, b, preferred_element_type=F32)


def _pieces(x, n):
    out = []
    for _ in range(n - 1):
        p = x.astype(BF16)
        out.append(p)
        x = x - p.astype(F32)
    out.append(x.astype(BF16))
    return out


def _dot_exact_lhs(a_bf16, x, n):
    acc = None
    for p in _pieces(x, n):
        t = _dot(a_bf16, p)
        acc = t if acc is None else acc + t
    return acc


def _dot_split(x, w_hi, w_lo):
    x_hi, x_lo = _pieces(x, 2)
    return _dot(x_hi, w_hi) + _dot(x_lo, w_hi) + _dot(x_hi, w_lo)


def _dot_nt(a, b):
    return lax.dot_general(a, b, (((1,), (1,)), ((), ())), preferred_element_type=F32)


def _dot_tn(a, b):
    return lax.dot_general(a, b, (((0,), (0,)), ((), ())), preferred_element_type=F32)


def _softplus(x):
    return jnp.maximum(x, 0.0) + jnp.log(1.0 + jnp.exp(-jnp.abs(x)))


def _sigmoid(x):
    return 1.0 / (1.0 + jnp.exp(-x))


def _params(sem):
    return pltpu.CompilerParams(dimension_semantics=sem, vmem_limit_bytes=VMEM_LIMIT)


def _in_proj_kernel(x_ref, nw_ref, w_ref, *out_refs):
    x = x_ref[...]
    u = x * lax.rsqrt(jnp.mean(x * x, axis=-1, keepdims=True) + NORM_EPS) * nw_ref[...]
    ub = u.astype(BF16)
    for g in range(N_WIDE):
        o = out_refs[g]
        o[...] = _dot(ub, w_ref[:, g * D_BRANCH:(g + 1) * D_BRANCH]).astype(o.dtype)
    out_refs[N_WIDE][...] = _dot(ub, w_ref[:, COL_WA:COL_WA + LANES])
    out_refs[N_WIDE + 1][...] = _dot(ub, w_ref[:, COL_FL:COL_FL + LANES])


def _in_proj(rows, norm_w, w_all, tm):
    n = rows.shape[0]
    wide_dtypes = (BF16, BF16, BF16, F32, F32, F32, F32, F32)
    out_shape = [jax.ShapeDtypeStruct((n, D_BRANCH), dt) for dt in wide_dtypes]
    out_shape += [jax.ShapeDtypeStruct((n, LANES), F32)] * 2
    out_specs = [pl.BlockSpec((tm, D_BRANCH), lambda i: (i, 0))] * N_WIDE
    out_specs += [pl.BlockSpec((tm, LANES), lambda i: (i, 0))] * 2
    return pl.pallas_call(
        _in_proj_kernel,
        grid=(n // tm,),
        in_specs=[pl.BlockSpec((tm, D_MODEL), lambda i: (i, 0)),
                  pl.BlockSpec((1, D_MODEL), lambda i: (0, 0)),
                  pl.BlockSpec((D_MODEL, N_COLS), lambda i: (0, 0))],
        out_specs=out_specs,
        out_shape=out_shape,
        compiler_params=_params(("parallel",)),
        name="in_proj",
    )(rows, norm_w, w_all)


def _select_matrices():
    pq = np.zeros((N_HEADS, 2 * LANES, LANES), np.float32)
    pk = np.zeros((N_HEADS, 2 * LANES, LANES), np.float32)
    for h in range(N_HEADS):
        base = (h % 2) * HEAD_DIM
        for d in range(HEAD_DIM):
            pq[h, base + d, d] = 0.125
            pk[h, base + d, d] = 1.0
        for part in range(N_CPARTS):
            pq[h, LANES + part * N_HEADS + h, HEAD_DIM + part] = 1.0
            pq[h, LANES + N_CPARTS * N_HEADS + h, HEAD_DIM + N_CPARTS + part] = 1.0
            pk[h, LANES + N_CPARTS * N_HEADS + h, HEAD_DIM + part] = 1.0
            pk[h, LANES + part * N_HEADS + h, HEAD_DIM + N_CPARTS + part] = -1.0
    return jnp.asarray(pq, BF16), jnp.asarray(pk, BF16)


def _fox_prep_kernel(q_ref, k_ref, v_ref, fl_ref, bf_ref, c0_ref, pq_ref, pk_ref,
                     qo_ref, ko_ref, vo_ref, cl_ref, carry_ref, *, tr, n_pad):
    i = pl.program_id(1)

    @pl.when(i == 0)
    def _():
        carry_ref[...] = c0_ref[...]

    x = fl_ref[0] + bf_ref[...]
    logf = jnp.minimum(x, 0.0) - jnp.log(1.0 + jnp.exp(-jnp.abs(x)))
    lane = lax.broadcasted_iota(jnp.int32, (tr, LANES), 1)
    row = lax.broadcasted_iota(jnp.int32, (tr, LANES), 0) + i * tr
    valid = lane < N_HEADS
    if n_pad:
        valid = valid & (row >= n_pad)
    logf = jnp.where(valid, logf, 0.0)
    r2 = lax.broadcasted_iota(jnp.int32, (tr, tr), 0)
    c2 = lax.broadcasted_iota(jnp.int32, (tr, tr), 1)
    tri = jnp.where(r2 >= c2, 1.0, 0.0).astype(BF16)
    cum = _dot_exact_lhs(tri, logf, 3) + carry_ref[...]
    carry_ref[...] = cum[tr - 1:tr, :]

    @pl.when(i == pl.num_programs(1) - 1)
    def _():
        cl_ref[0] = cum[tr - 1:tr, :]

    p1 = cum.astype(BF16).astype(F32)
    rem = cum - p1
    p2 = rem.astype(BF16).astype(F32)
    p3 = (rem - p2).astype(BF16).astype(F32)
    ones = jnp.where((lane >= N_CPARTS * N_HEADS) & (lane < (N_CPARTS + 1) * N_HEADS), 1.0, 0.0)
    cbits = p1 + pltpu.roll(p2, N_HEADS, 1) + pltpu.roll(p3, 2 * N_HEADS, 1) + ones
    cbits_k = cbits
    if n_pad:
        cbits_k = jnp.where((row < n_pad) & (lane < N_HEADS), -NEG, cbits)
    cq = cbits.astype(BF16)
    ck = cbits_k.astype(BF16)
    lane_lo = lane < HEAD_DIM
    for h in range(N_HEADS):
        sl = slice((h // 2) * PAIR, (h // 2 + 1) * PAIR)
        xq = jnp.concatenate([q_ref[0, :, sl], cq], axis=1)
        xk = jnp.concatenate([k_ref[0, :, sl], ck], axis=1)
        qo_ref[0, h] = _dot(xq, pq_ref[h]).astype(BF16)
        ko_ref[0, h] = _dot(xk, pk_ref[h]).astype(BF16)
        vp = v_ref[0, :, sl]
        keep = lane_lo if h % 2 == 0 else jnp.logical_not(lane_lo)
        one = jnp.where(lane == ONES_LANE[h % 2], 1.0, 0.0).astype(BF16)
        vo_ref[0, h] = jnp.where(keep, vp, one)


def _fox_prep(q, k, v, fl, bf_pad, c0, pq, pk, tr, n_pad):
    b, l, _ = q.shape
    wide = pl.BlockSpec((1, tr, D_BRANCH), lambda bi, i: (bi, i, 0))
    head_out = pl.BlockSpec((1, N_HEADS, tr, LANES), lambda bi, i: (bi, 0, i, 0))
    row128 = pl.BlockSpec((1, LANES), lambda bi, i: (0, 0))
    sel = pl.BlockSpec((N_HEADS, 2 * LANES, LANES), lambda bi, i: (0, 0, 0))
    return pl.pallas_call(
        functools.partial(_fox_prep_kernel, tr=tr, n_pad=n_pad),
        grid=(b, l // tr),
        in_specs=[wide, wide, wide,
                  pl.BlockSpec((1, tr, LANES), lambda bi, i: (bi, i, 0)),
                  row128, row128, sel, sel],
        out_specs=[head_out, head_out, head_out,
                   pl.BlockSpec((1, 1, LANES), lambda bi, i: (bi, 0, 0))],
        out_shape=[jax.ShapeDtypeStruct((b, N_HEADS, l, LANES), BF16)] * 3
                  + [jax.ShapeDtypeStruct((b, 1, LANES), F32)],
        scratch_shapes=[pltpu.VMEM((1, LANES), F32)],
        compiler_params=_params(("parallel", "arbitrary")),
        name="fox_prep",
    )(q, k, v, fl, bf_pad, c0, pq, pk)


def _attn_kernel(q_ref, kpre_ref, vpre_ref, k_ref, v_ref, o_ref, s_ref, *, tq):
    qi = pl.program_id(2)
    heads = range(ATTN_HEADS)
    qs = [q_ref[0, h] for h in heads]

    def rowmax(s):
        return jnp.max(s, axis=-1, keepdims=True)

    def k_chunk(h, idx):
        return k_ref[0, h, pl.ds(pl.multiple_of(idx * tq, tq), tq), :]

    def v_chunk(h, idx):
        return v_ref[0, h, pl.ds(pl.multiple_of(idx * tq, tq), tq), :]

    def scores(idx):
        return [_dot_nt(qs[h], k_chunk(h, idx)) for h in heads]

    def softmax_pv(carry, s, rmax, idx):
        out = []
        for h in heads:
            m, acc = carry[h]
            m_new = jnp.maximum(m, rmax[h])
            p = jnp.exp(s[h] - m_new).astype(BF16)
            out.append((m_new, jnp.exp(m - m_new) * acc + _dot(p, v_chunk(h, idx))))
        return out

    row = lax.broadcasted_iota(jnp.int32, (tq, tq), 0)
    col = lax.broadcasted_iota(jnp.int32, (tq, tq), 1)
    s_pre = [_dot_nt(qs[h], kpre_ref[0, h]) for h in heads]
    s_dia = [jnp.where(col <= row, s, NEG) for s in scores(qi)]
    s_nxt = scores(0)
    carry = []
    for h in heads:
        m0 = jnp.maximum(rowmax(s_pre[h]), rowmax(s_dia[h]))
        p_pre = jnp.exp(s_pre[h] - m0).astype(BF16)
        p_dia = jnp.exp(s_dia[h] - m0).astype(BF16)
        carry.append((m0, _dot(p_pre, vpre_ref[0, h]) + _dot(p_dia, v_chunk(h, qi))))
        s_ref[h] = s_nxt[h]
    rmax = [rowmax(s) for s in s_nxt]

    def step2(t, state):
        carry, rmax = state
        k0 = 2 * t
        s_a = [s_ref[h] for h in heads]
        s_b = scores(k0 + 1)
        carry = softmax_pv(carry, s_a, rmax, k0)
        rmax_b = [rowmax(s) for s in s_b]
        s_c = scores(k0 + 2)
        carry = softmax_pv(carry, s_b, rmax_b, k0 + 1)
        for h in heads:
            s_ref[h] = s_c[h]
        return carry, [rowmax(s) for s in s_c]

    def step1(k0, state):
        carry, rmax = state
        s_a = [s_ref[h] for h in heads]
        s_b = scores(k0 + 1)
        carry = softmax_pv(carry, s_a, rmax, k0)
        for h in heads:
            s_ref[h] = s_b[h]
        return carry, [rowmax(s) for s in s_b]

    steps = jnp.maximum(qi - 1, 0)
    pairs = steps // 2
    state = lax.fori_loop(0, pairs, step2, (carry, rmax))
    carry, rmax = lax.fori_loop(2 * pairs, steps, step1, state)

    some = qi > 0
    last = jnp.maximum(qi - 1, 0)
    s_fin = [jnp.where(some, s_ref[h], NEG) for h in heads]
    r_fin = [jnp.where(some, r, NEG) for r in rmax]
    accs = [acc for _, acc in softmax_pv(carry, s_fin, r_fin, last)]
    lane = lax.broadcasted_iota(jnp.int32, (tq, LANES), 1)
    for pr in range(ATTN_HEADS // 2):
        acc0, acc1 = accs[2 * pr], accs[2 * pr + 1]
        l0 = acc0[:, ONES_LANE[0]:ONES_LANE[0] + 1]
        l1 = acc1[:, ONES_LANE[1]:ONES_LANE[1] + 1]
        o_ref[0, :, pr * PAIR:(pr + 1) * PAIR] = jnp.where(lane < HEAD_DIM, acc0 / l0, acc1 / l1)


def _fox_attn(qp, kp, vp, kpre, vpre, tq):
    b, _, l, _ = qp.shape
    g = ATTN_HEADS
    pre = pl.BlockSpec((1, g, PREFIX_ROWS, LANES), lambda bi, p, qi: (0, p, 0, 0))
    full = pl.BlockSpec((1, g, l, LANES), lambda bi, p, qi: (bi, p, 0, 0))
    return pl.pallas_call(
        functools.partial(_attn_kernel, tq=tq),
        grid=(b, N_HEADS // g, l // tq),
        in_specs=[pl.BlockSpec((1, g, tq, LANES), lambda bi, p, qi: (bi, p, qi, 0)),
                  pre, pre, full, full],
        out_specs=pl.BlockSpec((1, tq, g * HEAD_DIM), lambda bi, p, qi: (bi, qi, p)),
        out_shape=jax.ShapeDtypeStruct((b, l, D_BRANCH), F32),
        scratch_shapes=[pltpu.VMEM((g, tq, tq), F32)],
        compiler_params=_params(("parallel", "parallel", "parallel")),
        name="fox_attn",
    )(qp, kpre, vpre, kp, vp)


def _rwkv_kernel(r_ref, k_ref, v_ref, z_ref, wa_ref,
                 mu_r_ref, mu_k_ref, mu_v_ref, mu_wa_ref, w0_ref, wuph_ref, wupl_ref,
                 a0_ref, auph_ref, aupl_ref,
                 kk_ref, ka_ref, rk_ref, gnw_ref, gnb_ref, ones_ref,
                 z0_ref, pr_ref, pk_ref, pv_ref, pwa_ref,
                 y_ref, zf_ref, lr_ref, lk_ref, lv_ref, lwa_ref,
                 state_ref, sr_ref, sk_ref, sv_ref, swa_ref, yacc_ref, *, tr):
    t = pl.program_id(1)
    last = t == pl.num_programs(1) - 1

    @pl.when(t == 0)
    def _():
        state_ref[...] = z0_ref[...]
        sr_ref[...] = pr_ref[...]
        sk_ref[...] = pk_ref[...]
        sv_ref[...] = pv_ref[...]
        swa_ref[...] = pwa_ref[...]

    ones_bd = ones_ref[...]

    def seg_sum(x):
        return _dot(x.astype(BF16), ones_bd)

    def shifted(cur, prev_row, mu_ref):
        first = lax.broadcasted_iota(jnp.int32, cur.shape, 0) == 0
        prev = jnp.where(first, prev_row, pltpu.roll(cur, 1, 0))
        return cur + mu_ref[...] * (prev - cur)

    r2 = lax.broadcasted_iota(jnp.int32, (SUB, SUB), 0)
    c2 = lax.broadcasted_iota(jnp.int32, (SUB, SUB), 1)
    tri = jnp.where((r2 >= c2) & (r2 // CHUNK == c2 // CHUNK), 1.0, 0.0).astype(BF16)

    n_sub = tr // SUB
    prev_rows = [sr_ref[...], sk_ref[...], sv_ref[...], swa_ref[...]]
    mus = (mu_r_ref, mu_k_ref, mu_v_ref, mu_wa_ref)
    preps = []
    for sb in range(n_sub):
        rows = slice(sb * SUB, (sb + 1) * SUB)
        cur = [ref[0, rows, :] for ref in (r_ref, k_ref, v_ref, wa_ref)]
        xr, xk, xv, xwa = [shifted(c, p, mu) for c, p, mu in zip(cur, prev_rows, mus)]
        prev_rows = [c[SUB - 1:SUB, :] for c in cur]
        w_lin = w0_ref[...] + _dot_split(jnp.tanh(xwa), wuph_ref[...], wupl_ref[...])
        a_lin = a0_ref[...] + _dot_split(xwa, auph_ref[...], aupl_ref[...])
        w = -_softplus(-w_lin) - 0.5
        ld = -jnp.exp(w)
        a = _sigmoid(a_lin)
        kk = xk * kk_ref[...]
        kk = kk * lax.rsqrt(seg_sum(kk * kk) + KK_EPS)
        kmod = xk * (1.0 + (a - 1.0) * ka_ref[...])
        lw = _dot_exact_lhs(tri, ld, 3)
        w_inv = jnp.exp(-lw)
        preps.append(dict(xr=xr, xv=xv, kmod=kmod, lw=lw, rt=xr * jnp.exp(lw), at=-kk * jnp.exp(lw - ld),
                          bt=kk * a * w_inv, kt=kmod * w_inv))
    for ref, rowv in zip((sr_ref, sk_ref, sv_ref, swa_ref), prev_rows):
        ref[...] = rowv

    @pl.when(last)
    def _():
        for out, rowv in zip((lr_ref, lk_ref, lv_ref, lwa_ref), prev_rows):
            out[0] = rowv

    lane = lax.broadcasted_iota(jnp.int32, (CHUNK, PAIR), 1)
    head_a = lane < HEAD_DIM
    tau_r = lax.broadcasted_iota(jnp.int32, (PAIR, PAIR), 0)
    tau_c = lax.broadcasted_iota(jnp.int32, (PAIR, PAIR), 1)
    strict = (tau_r % CHUNK) > (tau_c % CHUNK)
    incl = (tau_r % CHUNK) >= (tau_c % CHUNK)
    eye = tau_r == tau_c
    eye_f = jnp.where(eye, 1.0, 0.0).astype(F32)

    def stack(x):
        return jnp.concatenate([jnp.where(head_a, x, 0.0), jnp.where(head_a, 0.0, x)], axis=0)

    n_chunks = tr // CHUNK
    per_sub = SUB // CHUNK
    units = [(c, p) for c in range(n_chunks) for p in range(N_PAIRS)]
    w_end = [jnp.exp(preps[c // per_sub]["lw"][(c % per_sub + 1) * CHUNK - 1:(c % per_sub + 1) * CHUNK, :])
             for c in range(n_chunks)]

    def tile(name, c, p):
        lo = (c % per_sub) * CHUNK
        return preps[c // per_sub][name][lo:lo + CHUNK, p * PAIR:(p + 1) * PAIR]

    wc = [w_end[c][:, p * PAIR:(p + 1) * PAIR] for c, p in units]
    rs_f = [stack(tile("rt", c, p)) for c, p in units]
    r_s = [x.astype(BF16) for x in rs_f]
    a_s = [stack(tile("at", c, p)).astype(BF16) for c, p in units]
    b_s = [stack(tile("bt", c, p)).astype(BF16) for c, p in units]
    k_s = [stack(tile("kt", c, p)).astype(BF16) for c, p in units]
    v_s = [stack(tile("xv", c, p)).astype(BF16) for c, p in units]
    bh_s = [stack(tile("bt", c, p) * wc[u]).astype(BF16) for u, (c, p) in enumerate(units)]
    kh_s = [stack(tile("kt", c, p) * wc[u]).astype(BF16) for u, (c, p) in enumerate(units)]
    nu = range(len(units))

    a_ab = [jnp.where(strict, _dot_nt(a_s[u], b_s[u]), 0.0) for u in nu]
    a_ak = [jnp.where(strict, _dot_nt(a_s[u], k_s[u]), 0.0).astype(BF16) for u in nu]
    a_rb = [jnp.where(incl, _dot_nt(r_s[u], b_s[u]), 0.0).astype(BF16) for u in nu]
    a_rk = [jnp.where(incl, _dot_nt(r_s[u], k_s[u]), 0.0).astype(BF16) for u in nu]

    pw = a_ab
    tinv = [eye_f + a_ab[u] for u in nu]
    for _ in range(5):
        pb = [x.astype(BF16) for x in pw]
        pw = [_dot(pb[u], pb[u]) for u in nu]
        tinv = [tinv[u] + _dot(tinv[u].astype(BF16), pw[u].astype(BF16)) for u in nu]
    tb = [x.astype(BF16) for x in tinv]

    ap = [_dot(tb[u], a_s[u]).astype(BF16) for u in nu]
    akv = [_dot(a_ak[u], v_s[u]).astype(BF16) for u in nu]
    vp = [_dot(tb[u], akv[u]).astype(BF16) for u in nu]
    m_mat = [(jnp.where(eye, wc[u], 0.0) + _dot_tn(bh_s[u], ap[u])).astype(BF16) for u in nu]
    rp = [(rs_f[u] + _dot(a_rb[u], ap[u])).astype(BF16) for u in nu]
    vv = [jnp.concatenate([vp[u], v_s[u]], axis=0) for u in nu]
    g_mat = [_dot_tn(jnp.concatenate([bh_s[u], kh_s[u]], axis=0), vv[u]) for u in nu]
    y0 = [_dot(jnp.concatenate([a_rb[u], a_rk[u]], axis=1), vv[u]) for u in nu]

    for u, (c, p) in enumerate(units):
        zb = state_ref[p].astype(BF16)
        ys = _dot(rp[u], zb) + y0[u]
        state_ref[p] = _dot(m_mat[u], zb) + g_mat[u]
        yacc_ref[c * CHUNK:(c + 1) * CHUNK, p * PAIR:(p + 1) * PAIR] = ys[:CHUNK] + ys[CHUNK:]

    @pl.when(last)
    def _():
        zf_ref[0] = state_ref[...]

    inv_n = 1.0 / HEAD_DIM
    for sb in range(n_sub):
        rows = slice(sb * SUB, (sb + 1) * SUB)
        pr = preps[sb]
        y = yacc_ref[rows, :]
        mean = seg_sum(y) * inv_n
        d = y - mean
        var = seg_sum(d * d) * inv_n
        yn = d * lax.rsqrt(var + GN_EPS) * gnw_ref[...] + gnb_ref[...]
        bonus = seg_sum(pr["xr"] * pr["kmod"] * rk_ref[...]) * pr["xv"]
        z = z_ref[0, rows, :]
        y_ref[0, rows, :] = ((yn + bonus) * (z * _sigmoid(z))).astype(y_ref.dtype)


def _rwkv(r, k, v, z, wa, prm, z0, prev, tr):
    b, l, _ = r.shape
    wide = pl.BlockSpec((1, tr, D_BRANCH), lambda bi, t: (bi, t, 0))
    narrow = pl.BlockSpec((1, tr, LANES), lambda bi, t: (bi, t, 0))
    row_w = pl.BlockSpec((1, D_BRANCH), lambda bi, t: (0, 0))
    row_n = pl.BlockSpec((1, LANES), lambda bi, t: (0, 0))
    up = pl.BlockSpec((LANES, D_BRANCH), lambda bi, t: (0, 0))
    ones = pl.BlockSpec((D_BRANCH, D_BRANCH), lambda bi, t: (0, 0))
    st_in = pl.BlockSpec((N_PAIRS, PAIR, PAIR), lambda bi, t: (0, 0, 0))
    st_out = pl.BlockSpec((1, N_PAIRS, PAIR, PAIR), lambda bi, t: (bi, 0, 0, 0))
    lrow_w = pl.BlockSpec((1, 1, D_BRANCH), lambda bi, t: (bi, 0, 0))
    lrow_n = pl.BlockSpec((1, 1, LANES), lambda bi, t: (bi, 0, 0))
    return pl.pallas_call(
        functools.partial(_rwkv_kernel, tr=tr),
        grid=(b, l // tr),
        in_specs=[wide, wide, wide, wide, narrow,
                  row_w, row_w, row_w, row_n, row_w, up, up, row_w, up, up,
                  row_w, row_w, row_w, row_w, row_w, ones,
                  st_in, row_w, row_w, row_w, row_n],
        out_specs=[wide, st_out, lrow_w, lrow_w, lrow_w, lrow_n],
        out_shape=[jax.ShapeDtypeStruct((b, l, D_BRANCH), BF16),
                   jax.ShapeDtypeStruct((b, N_PAIRS, PAIR, PAIR), F32),
                   jax.ShapeDtypeStruct((b, 1, D_BRANCH), F32),
                   jax.ShapeDtypeStruct((b, 1, D_BRANCH), F32),
                   jax.ShapeDtypeStruct((b, 1, D_BRANCH), F32),
                   jax.ShapeDtypeStruct((b, 1, LANES), F32)],
        scratch_shapes=[pltpu.VMEM((N_PAIRS, PAIR, PAIR), F32),
                        pltpu.VMEM((1, D_BRANCH), F32), pltpu.VMEM((1, D_BRANCH), F32),
                        pltpu.VMEM((1, D_BRANCH), F32), pltpu.VMEM((1, LANES), F32),
                        pltpu.VMEM((tr, D_BRANCH), F32)],
        compiler_params=_params(("parallel", "arbitrary")),
        name="rwkv",
    )(r, k, v, z, wa, *prm, z0, *prev)


def _out_proj_kernel(o_ref, zf_ref, yr_ref, x_ref, w_ref, fnw_ref, out_ref):
    z = zf_ref[...]
    yf = (o_ref[...] * (z * _sigmoid(z))).astype(BF16)
    mix = _dot(yf, w_ref[:D_BRANCH, :]) + _dot(yr_ref[...], w_ref[D_BRANCH:, :])
    h = x_ref[...] + mix
    out_ref[...] = h * lax.rsqrt(jnp.mean(h * h, axis=-1, keepdims=True) + NORM_EPS) * fnw_ref[...]


def _out_proj(o, zf, yr, x, w_out, fnw, tm):
    n = x.shape[0]
    half = pl.BlockSpec((tm, D_BRANCH), lambda i: (i, 0))
    full = pl.BlockSpec((tm, D_MODEL), lambda i: (i, 0))
    return pl.pallas_call(
        _out_proj_kernel,
        grid=(n // tm,),
        in_specs=[half, half, half, full,
                  pl.BlockSpec((D_MODEL, D_MODEL), lambda i: (0, 0)),
                  pl.BlockSpec((1, D_MODEL), lambda i: (0, 0))],
        out_specs=full,
        out_shape=jax.ShapeDtypeStruct((n, D_MODEL), F32),
        compiler_params=_params(("parallel",)),
        name="out_proj",
    )(o, zf, yr, x, w_out, fnw)


def _tiles(b, l):
    rows = b * l
    tm = 512 if rows % 512 == 0 else 256
    return tm, 256, 512, 256


def kernel(x, meta, norm_w, w_in, b_f, mu_shift, w0, w_up, a0, a_up, k_k, k_a, r_k, gn_w, gn_b,
           w_out, final_norm_w):
    b, l, d = x.shape
    assert d == D_MODEL and norm_w.shape[0] == 1 and l % 256 == 0
    tm, t_prep, t_attn, t_rwkv = _tiles(b, l)

    wi = w_in[0]
    o = 0
    cols = {}
    for name, width in (("q", D_BRANCH), ("k", D_BRANCH), ("v", D_BRANCH), ("fl", N_HEADS), ("zf", D_BRANCH),
                        ("r", D_BRANCH), ("rk", D_BRANCH), ("rv", D_BRANCH), ("wd", RANK), ("ad", RANK),
                        ("zr", D_BRANCH)):
        cols[name] = wi[:, o:o + width]
        o += width
    w_all = jnp.concatenate(
        [cols[n] for n in ("q", "k", "v", "zf", "r", "rk", "rv", "zr", "wd", "ad", "fl")]
        + [jnp.zeros((D_MODEL, LANES - N_HEADS), F32)], axis=1).astype(BF16)

    row = lambda vec: vec.reshape(1, -1).astype(F32)
    mu = mu_shift[0]
    mu_r, mu_k, mu_v = (row(mu[i * D_BRANCH:(i + 1) * D_BRANCH]) for i in range(3))
    mu_wa = row(mu[3 * D_BRANCH:])
    zeros_up = jnp.zeros((RANK, D_BRANCH), F32)
    wup_pad = jnp.concatenate([w_up[0], zeros_up], axis=0)
    aup_pad = jnp.concatenate([zeros_up, a_up[0]], axis=0)
    hid = np.arange(D_BRANCH) // HEAD_DIM
    ones_bd = jnp.asarray(hid[:, None] == hid[None, :], BF16)
    hi_lo = lambda w: (w.astype(BF16), (w - w.astype(BF16).astype(F32)).astype(BF16))
    rwkv_prm = (mu_r, mu_k, mu_v, mu_wa, row(w0[0]), *hi_lo(wup_pad), row(a0[0]), *hi_lo(aup_pad),
                row(k_k[0]), row(k_a[0]), row(r_k[0]), row(gn_w[0]), row(gn_b[0]), ones_bd)
    bf_pad = jnp.concatenate([b_f[0], jnp.zeros((LANES - N_HEADS,), F32)]).reshape(1, LANES)
    pq, pk = _select_matrices()
    nw = row(norm_w[0])

    pre_rows = jnp.concatenate([jnp.zeros((PREFIX_ROWS - N_META, D_MODEL), F32), meta.astype(F32)], axis=0)
    pq_, pk_, pv_, _, pr_, prk_, prv_, pzr_, pwa_, pfl_ = _in_proj(pre_rows, nw, w_all, PREFIX_ROWS)
    lead = lambda a: a[None]
    _, kpre, vpre, c_pre = _fox_prep(lead(pq_), lead(pk_), lead(pv_), lead(pfl_), bf_pad,
                                     jnp.zeros((1, LANES), F32), pq, pk, PREFIX_ROWS, PREFIX_ROWS - N_META)
    zero_w = jnp.zeros((1, D_BRANCH), F32)
    _, z_pre, lr, lk, lv, lwa = _rwkv(
        lead(pr_), lead(prk_), lead(prv_), lead(pzr_), lead(pwa_), rwkv_prm,
        jnp.zeros((N_PAIRS, PAIR, PAIR), F32), (zero_w, zero_w, zero_w, jnp.zeros((1, LANES), F32)), PREFIX_ROWS)

    xf = x.reshape(b * l, D_MODEL)
    q_, k_, v_, zf_, r_, rk_, rv_, zr_, wa_, fl_ = _in_proj(xf, nw, w_all, tm)
    bl = lambda a: a.reshape(b, l, a.shape[-1])
    qp, kp, vp, _ = _fox_prep(bl(q_), bl(k_), bl(v_), bl(fl_), bf_pad, c_pre[0], pq, pk, t_prep, 0)
    o_attn = _fox_attn(qp, kp, vp, kpre, vpre, t_attn)
    y_rwkv, _, _, _, _, _ = _rwkv(bl(r_), bl(rk_), bl(rv_), bl(zr_), bl(wa_), rwkv_prm,
                                  z_pre[0], (lr[0], lk[0], lv[0], lwa[0]), t_rwkv)
    out = _out_proj(o_attn.reshape(b * l, D_BRANCH), zf_, y_rwkv.reshape(b * l, D_BRANCH), xf,
                    w_out[0].astype(BF16), row(final_norm_w), tm)
    return out.reshape(b, l, D_MODEL)
```

```python
import functools

import numpy as np
import jax
import jax.numpy as jnp
from jax import lax
from jax.experimental import pallas as pl
from jax.experimental.pallas import tpu as pltpu

F32 = jnp.float32
BF16 = jnp.bfloat16

D_MODEL = 1024
N_META = 16
HEAD_DIM = 64
N_HEADS = 8
D_BRANCH = N_HEADS * HEAD_DIM
N_PAIRS = N_HEADS // 2
RANK = 64
NORM_EPS = 1e-6
GN_EPS = 64e-5
KK_EPS = 1e-12
NEG = -1e30

LANES = 128
PREFIX_ROWS = 128
CHUNK = 64
SUB = 2 * CHUNK
PAIR = 2 * HEAD_DIM
VMEM_LIMIT = 56 * 1024 * 1024

N_WIDE = 8
COL_WA = N_WIDE * D_BRANCH
COL_FL = COL_WA + LANES
N_COLS = COL_FL + LANES

N_CPARTS = 3
ONES_LANE = (HEAD_DIM, 0)
ATTN_HEADS = 4


def _dot(a, b):
    return jnp.dot(a, b, preferred_element_type=F32)


def _pieces(x, n):
    out = []
    for _ in range(n - 1):
        p = x.astype(BF16)
        out.append(p)
        x = x - p.astype(F32)
    out.append(x.astype(BF16))
    return out


def _dot_exact_lhs(a_bf16, x, n):
    acc = None
    for p in _pieces(x, n):
        t = _dot(a_bf16, p)
        acc = t if acc is None else acc + t
    return acc


def _dot_split(x, w_hi, w_lo):
    x_hi, x_lo = _pieces(x, 2)
    return _dot(x_hi, w_hi) + _dot(x_lo, w_hi) + _dot(x_hi, w_lo)


def _dot_nt(a, b):
    return lax.dot_general(a, b, (((1,), (1,)), ((), ())), preferred_element_type=F32)


def _dot_tn(a, b):
    return lax.dot_general(a, b, (((0,), (0,)), ((), ())), preferred_element_type=F32)


def _softplus(x):
    return jnp.maximum(x, 0.0) + jnp.log(1.0 + jnp.exp(-jnp.abs(x)))


def _sigmoid(x):
    return 1.0 / (1.0 + jnp.exp(-x))


def _params(sem):
    return pltpu.CompilerParams(dimension_semantics=sem, vmem_limit_bytes=VMEM_LIMIT)


def _in_proj_kernel(x_ref, nw_ref, w_ref, *out_refs):
    x = x_ref[...]
    u = x * lax.rsqrt(jnp.mean(x * x, axis=-1, keepdims=True) + NORM_EPS) * nw_ref[...]
    ub = u.astype(BF16)
    for g in range(N_WIDE):
        o = out_refs[g]
        o[...] = _dot(ub, w_ref[:, g * D_BRANCH:(g + 1) * D_BRANCH]).astype(o.dtype)
    out_refs[N_WIDE][...] = _dot(ub, w_ref[:, COL_WA:COL_WA + LANES])
    out_refs[N_WIDE + 1][...] = _dot(ub, w_ref[:, COL_FL:COL_FL + LANES])


def _in_proj(rows, norm_w, w_all, tm):
    n = rows.shape[0]
    wide_dtypes = (BF16, BF16, BF16, F32, F32, F32, F32, F32)
    out_shape = [jax.ShapeDtypeStruct((n, D_BRANCH), dt) for dt in wide_dtypes]
    out_shape += [jax.ShapeDtypeStruct((n, LANES), F32)] * 2
    out_specs = [pl.BlockSpec((tm, D_BRANCH), lambda i: (i, 0))] * N_WIDE
    out_specs += [pl.BlockSpec((tm, LANES), lambda i: (i, 0))] * 2
    return pl.pallas_call(
        _in_proj_kernel,
        grid=(n // tm,),
        in_specs=[pl.BlockSpec((tm, D_MODEL), lambda i: (i, 0)),
                  pl.BlockSpec((1, D_MODEL), lambda i: (0, 0)),
                  pl.BlockSpec((D_MODEL, N_COLS), lambda i: (0, 0))],
        out_specs=out_specs,
        out_shape=out_shape,
        compiler_params=_params(("parallel",)),
        name="in_proj",
    )(rows, norm_w, w_all)


def _select_matrices():
    pq = np.zeros((N_HEADS, 2 * LANES, LANES), np.float32)
    pk = np.zeros((N_HEADS, 2 * LANES, LANES), np.float32)
    for h in range(N_HEADS):
        base = (h % 2) * HEAD_DIM
        for d in range(HEAD_DIM):
            pq[h, base + d, d] = 0.125
            pk[h, base + d, d] = 1.0
        for part in range(N_CPARTS):
            pq[h, LANES + part * N_HEADS + h, HEAD_DIM + part] = 1.0
            pq[h, LANES + N_CPARTS * N_HEADS + h, HEAD_DIM + N_CPARTS + part] = 1.0
            pk[h, LANES + N_CPARTS * N_HEADS + h, HEAD_DIM + part] = 1.0
            pk[h, LANES + part * N_HEADS + h, HEAD_DIM + N_CPARTS + part] = -1.0
    return jnp.asarray(pq, BF16), jnp.asarray(pk, BF16)


def _fox_prep_kernel(q_ref, k_ref, v_ref, fl_ref, bf_ref, c0_ref, pq_ref, pk_ref,
                     qo_ref, ko_ref, vo_ref, cl_ref, carry_ref, *, tr, n_pad):
    i = pl.program_id(1)

    @pl.when(i == 0)
    def _():
        carry_ref[...] = c0_ref[...]

    x = fl_ref[0] + bf_ref[...]
    logf = jnp.minimum(x, 0.0) - jnp.log(1.0 + jnp.exp(-jnp.abs(x)))
    lane = lax.broadcasted_iota(jnp.int32, (tr, LANES), 1)
    row = lax.broadcasted_iota(jnp.int32, (tr, LANES), 0) + i * tr
    valid = lane < N_HEADS
    if n_pad:
        valid = valid & (row >= n_pad)
    logf = jnp.where(valid, logf, 0.0)
    r2 = lax.broadcasted_iota(jnp.int32, (tr, tr), 0)
    c2 = lax.broadcasted_iota(jnp.int32, (tr, tr), 1)
    tri = jnp.where(r2 >= c2, 1.0, 0.0).astype(BF16)
    cum = _dot_exact_lhs(tri, logf, 3) + carry_ref[...]
    carry_ref[...] = cum[tr - 1:tr, :]

    @pl.when(i == pl.num_programs(1) - 1)
    def _():
        cl_ref[0] = cum[tr - 1:tr, :]

    p1 = cum.astype(BF16).astype(F32)
    rem = cum - p1
    p2 = rem.astype(BF16).astype(F32)
    p3 = (rem - p2).astype(BF16).astype(F32)
    ones = jnp.where((lane >= N_CPARTS * N_HEADS) & (lane < (N_CPARTS + 1) * N_HEADS), 1.0, 0.0)
    cbits = p1 + pltpu.roll(p2, N_HEADS, 1) + pltpu.roll(p3, 2 * N_HEADS, 1) + ones
    cbits_k = cbits
    if n_pad:
        cbits_k = jnp.where((row < n_pad) & (lane < N_HEADS), -NEG, cbits)
    cq = cbits.astype(BF16)
    ck = cbits_k.astype(BF16)
    sel_r = lax.broadcasted_iota(jnp.int32, (PAIR, PAIR), 0)
    sel_c = lax.broadcasted_iota(jnp.int32, (PAIR, PAIR), 1)
    vrow = lax.broadcasted_iota(jnp.int32, (PAIR, tr), 0)
    for h in range(N_HEADS):
        sl = slice((h // 2) * PAIR, (h // 2 + 1) * PAIR)
        xq = jnp.concatenate([q_ref[0, :, sl], cq], axis=1)
        xk = jnp.concatenate([k_ref[0, :, sl], ck], axis=1)
        qo_ref[0, h] = _dot(xq, pq_ref[h]).astype(BF16)
        ko_ref[0, h] = _dot(xk, pk_ref[h]).astype(BF16)
        own = (sel_r < HEAD_DIM) if h % 2 == 0 else (sel_r >= HEAD_DIM)
        pick = jnp.where((sel_r == sel_c) & own, 1.0, 0.0).astype(BF16)
        v_t = _dot_nt(pick, v_ref[0, :, sl])
        vo_ref[0, h, 0] = jnp.where(vrow == ONES_LANE[h % 2], 1.0, v_t).astype(BF16)


def _fox_prep(q, k, v, fl, bf_pad, c0, pq, pk, tr, n_pad):
    b, l, _ = q.shape
    wide = pl.BlockSpec((1, tr, D_BRANCH), lambda bi, i: (bi, i, 0))
    head_out = pl.BlockSpec((1, N_HEADS, tr, LANES), lambda bi, i: (bi, 0, i, 0))
    row128 = pl.BlockSpec((1, LANES), lambda bi, i: (0, 0))
    sel = pl.BlockSpec((N_HEADS, 2 * LANES, LANES), lambda bi, i: (0, 0, 0))
    return pl.pallas_call(
        functools.partial(_fox_prep_kernel, tr=tr, n_pad=n_pad),
        grid=(b, l // tr),
        in_specs=[wide, wide, wide,
                  pl.BlockSpec((1, tr, LANES), lambda bi, i: (bi, i, 0)),
                  row128, row128, sel, sel],
        out_specs=[head_out, head_out,
                   pl.BlockSpec((1, N_HEADS, 1, PAIR, tr), lambda bi, i: (bi, 0, i, 0, 0)),
                   pl.BlockSpec((1, 1, LANES), lambda bi, i: (bi, 0, 0))],
        out_shape=[jax.ShapeDtypeStruct((b, N_HEADS, l, LANES), BF16)] * 2
                  + [jax.ShapeDtypeStruct((b, N_HEADS, l // tr, PAIR, tr), BF16),
                     jax.ShapeDtypeStruct((b, 1, LANES), F32)],
        scratch_shapes=[pltpu.VMEM((1, LANES), F32)],
        compiler_params=_params(("parallel", "arbitrary")),
        name="fox_prep",
    )(q, k, v, fl, bf_pad, c0, pq, pk)


def _attn_kernel(q_ref, kpre_ref, vpre_ref, k_ref, v_ref, o_ref, s_ref, *, tq):
    qi = pl.program_id(2)
    heads = range(ATTN_HEADS)
    qs = [q_ref[0, h] for h in heads]

    def rowmax(s):
        return jnp.max(s, axis=0, keepdims=True)

    def k_chunk(h, idx):
        return k_ref[0, h, pl.ds(pl.multiple_of(idx * tq, tq), tq), :]

    def v_chunk(h, idx):
        return v_ref[0, h, idx]

    def scores(idx):
        return [_dot_nt(k_chunk(h, idx), qs[h]) for h in heads]

    def softmax_pv(carry, s, rmax, idx):
        out = []
        for h in heads:
            m, acc = carry[h]
            m_new = jnp.maximum(m, rmax[h])
            p = jnp.exp(s[h] - m_new).astype(BF16)
            out.append((m_new, jnp.exp(m - m_new) * acc + _dot(v_chunk(h, idx), p)))
        return out

    key = lax.broadcasted_iota(jnp.int32, (tq, tq), 0)
    qry = lax.broadcasted_iota(jnp.int32, (tq, tq), 1)
    s_pre = [_dot_nt(kpre_ref[0, h], qs[h]) for h in heads]
    s_dia = [jnp.where(key <= qry, s, NEG) for s in scores(qi)]
    s_nxt = scores(0)
    carry = []
    for h in heads:
        m0 = jnp.maximum(rowmax(s_pre[h]), rowmax(s_dia[h]))
        p_pre = jnp.exp(s_pre[h] - m0).astype(BF16)
        p_dia = jnp.exp(s_dia[h] - m0).astype(BF16)
        carry.append((m0, _dot(vpre_ref[0, h, 0], p_pre) + _dot(v_chunk(h, qi), p_dia)))
        s_ref[h] = s_nxt[h]
    rmax = [rowmax(s) for s in s_nxt]

    def step2(t, state):
        carry, rmax = state
        k0 = 2 * t
        s_a = [s_ref[h] for h in heads]
        s_b = scores(k0 + 1)
        carry = softmax_pv(carry, s_a, rmax, k0)
        rmax_b = [rowmax(s) for s in s_b]
        s_c = scores(k0 + 2)
        carry = softmax_pv(carry, s_b, rmax_b, k0 + 1)
        for h in heads:
            s_ref[h] = s_c[h]
        return carry, [rowmax(s) for s in s_c]

    def step1(k0, state):
        carry, rmax = state
        s_a = [s_ref[h] for h in heads]
        s_b = scores(k0 + 1)
        carry = softmax_pv(carry, s_a, rmax, k0)
        for h in heads:
            s_ref[h] = s_b[h]
        return carry, [rowmax(s) for s in s_b]

    steps = jnp.maximum(qi - 1, 0)
    pairs = steps // 2
    state = lax.fori_loop(0, pairs, step2, (carry, rmax))
    carry, rmax = lax.fori_loop(2 * pairs, steps, step1, state)

    some = qi > 0
    last = jnp.maximum(qi - 1, 0)
    s_fin = [jnp.where(some, s_ref[h], NEG) for h in heads]
    r_fin = [jnp.where(some, r, NEG) for r in rmax]
    accs = [acc for _, acc in softmax_pv(carry, s_fin, r_fin, last)]
    vrow = lax.broadcasted_iota(jnp.int32, (PAIR, tq), 0)
    for pr in range(ATTN_HEADS // 2):
        acc0, acc1 = accs[2 * pr], accs[2 * pr + 1]
        l0 = acc0[ONES_LANE[0]:ONES_LANE[0] + 1, :]
        l1 = acc1[ONES_LANE[1]:ONES_LANE[1] + 1, :]
        o_t = jnp.where(vrow < HEAD_DIM, acc0 / l0, acc1 / l1)
        o_ref[0, :, pr * PAIR:(pr + 1) * PAIR] = o_t.T


def _fox_attn(qp, kp, vp, kpre, vpre, tq):
    b, _, l, _ = qp.shape
    g = ATTN_HEADS
    assert vp.shape[-1] == tq and vpre.shape[-1] == PREFIX_ROWS
    pre = pl.BlockSpec((1, g, PREFIX_ROWS, LANES), lambda bi, p, qi: (0, p, 0, 0))
    pre_v = pl.BlockSpec((1, g, 1, PAIR, PREFIX_ROWS), lambda bi, p, qi: (0, p, 0, 0, 0))
    full = pl.BlockSpec((1, g, l, LANES), lambda bi, p, qi: (bi, p, 0, 0))
    full_v = pl.BlockSpec((1, g, l // tq, PAIR, tq), lambda bi, p, qi: (bi, p, 0, 0, 0))
    return pl.pallas_call(
        functools.partial(_attn_kernel, tq=tq),
        grid=(b, N_HEADS // g, l // tq),
        in_specs=[pl.BlockSpec((1, g, tq, LANES), lambda bi, p, qi: (bi, p, qi, 0)),
                  pre, pre_v, full, full_v],
        out_specs=pl.BlockSpec((1, tq, g * HEAD_DIM), lambda bi, p, qi: (bi, qi, p)),
        out_shape=jax.ShapeDtypeStruct((b, l, D_BRANCH), F32),
        scratch_shapes=[pltpu.VMEM((g, tq, tq), F32)],
        compiler_params=_params(("parallel", "parallel", "parallel")),
        name="fox_attn",
    )(qp, kpre, vpre, kp, vp)


def _rwkv_kernel(r_ref, k_ref, v_ref, z_ref, wa_ref,
                 mu_r_ref, mu_k_ref, mu_v_ref, mu_wa_ref, w0_ref, wuph_ref, wupl_ref,
                 a0_ref, auph_ref, aupl_ref,
                 kk_ref, ka_ref, rk_ref, gnw_ref, gnb_ref, ones_ref,
                 z0_ref, pr_ref, pk_ref, pv_ref, pwa_ref,
                 y_ref, zf_ref, lr_ref, lk_ref, lv_ref, lwa_ref,
                 state_ref, sr_ref, sk_ref, sv_ref, swa_ref, yacc_ref, *, tr):
    t = pl.program_id(1)
    last = t == pl.num_programs(1) - 1

    @pl.when(t == 0)
    def _():
        state_ref[...] = z0_ref[...]
        sr_ref[...] = pr_ref[...]
        sk_ref[...] = pk_ref[...]
        sv_ref[...] = pv_ref[...]
        swa_ref[...] = pwa_ref[...]

    ones_bd = ones_ref[...]

    def seg_sum(x):
        return _dot(x.astype(BF16), ones_bd)

    def shifted(cur, prev_row, mu_ref):
        first = lax.broadcasted_iota(jnp.int32, cur.shape, 0) == 0
        prev = jnp.where(first, prev_row, pltpu.roll(cur, 1, 0))
        return cur + mu_ref[...] * (prev - cur)

    r2 = lax.broadcasted_iota(jnp.int32, (SUB, SUB), 0)
    c2 = lax.broadcasted_iota(jnp.int32, (SUB, SUB), 1)
    tri = jnp.where((r2 >= c2) & (r2 // CHUNK == c2 // CHUNK), 1.0, 0.0).astype(BF16)

    n_sub = tr // SUB
    prev_rows = [sr_ref[...], sk_ref[...], sv_ref[...], swa_ref[...]]
    mus = (mu_r_ref, mu_k_ref, mu_v_ref, mu_wa_ref)
    preps = []
    for sb in range(n_sub):
        rows = slice(sb * SUB, (sb + 1) * SUB)
        cur = [ref[0, rows, :] for ref in (r_ref, k_ref, v_ref, wa_ref)]
        xr, xk, xv, xwa = [shifted(c, p, mu) for c, p, mu in zip(cur, prev_rows, mus)]
        prev_rows = [c[SUB - 1:SUB, :] for c in cur]
        w_lin = w0_ref[...] + _dot_split(jnp.tanh(xwa), wuph_ref[...], wupl_ref[...])
        a_lin = a0_ref[...] + _dot_split(xwa, auph_ref[...], aupl_ref[...])
        w = -_softplus(-w_lin) - 0.5
        ld = -jnp.exp(w)
        a = _sigmoid(a_lin)
        kk = xk * kk_ref[...]
        kk = kk * lax.rsqrt(seg_sum(kk * kk) + KK_EPS)
        kmod = xk * (1.0 + (a - 1.0) * ka_ref[...])
        lw = _dot_exact_lhs(tri, ld, 3)
        w_inv = jnp.exp(-lw)
        preps.append(dict(xr=xr, xv=xv, kmod=kmod, lw=lw, rt=xr * jnp.exp(lw), at=-kk * jnp.exp(lw - ld),
                          bt=kk * a * w_inv, kt=kmod * w_inv))
    for ref, rowv in zip((sr_ref, sk_ref, sv_ref, swa_ref), prev_rows):
        ref[...] = rowv

    @pl.when(last)
    def _():
        for out, rowv in zip((lr_ref, lk_ref, lv_ref, lwa_ref), prev_rows):
            out[0] = rowv

    lane = lax.broadcasted_iota(jnp.int32, (CHUNK, PAIR), 1)
    head_a = lane < HEAD_DIM
    tau_r = lax.broadcasted_iota(jnp.int32, (PAIR, PAIR), 0)
    tau_c = lax.broadcasted_iota(jnp.int32, (PAIR, PAIR), 1)
    strict = (tau_r % CHUNK) > (tau_c % CHUNK)
    incl = (tau_r % CHUNK) >= (tau_c % CHUNK)
    eye = tau_r == tau_c
    eye_f = jnp.where(eye, 1.0, 0.0).astype(F32)

    def stack(x):
        return jnp.concatenate([jnp.where(head_a, x, 0.0), jnp.where(head_a, 0.0, x)], axis=0)

    n_chunks = tr // CHUNK
    per_sub = SUB // CHUNK
    units = [(c, p) for c in range(n_chunks) for p in range(N_PAIRS)]
    w_end = [jnp.exp(preps[c // per_sub]["lw"][(c % per_sub + 1) * CHUNK - 1:(c % per_sub + 1) * CHUNK, :])
             for c in range(n_chunks)]

    def tile(name, c, p):
        lo = (c % per_sub) * CHUNK
        return preps[c // per_sub][name][lo:lo + CHUNK, p * PAIR:(p + 1) * PAIR]

    wc = [w_end[c][:, p * PAIR:(p + 1) * PAIR] for c, p in units]
    rs_f = [stack(tile("rt", c, p)) for c, p in units]
    r_s = [x.astype(BF16) for x in rs_f]
    a_s = [stack(tile("at", c, p)).astype(BF16) for c, p in units]
    b_s = [stack(tile("bt", c, p)).astype(BF16) for c, p in units]
    k_s = [stack(tile("kt", c, p)).astype(BF16) for c, p in units]
    v_s = [stack(tile("xv", c, p)).astype(BF16) for c, p in units]
    bh_s = [stack(tile("bt", c, p) * wc[u]).astype(BF16) for u, (c, p) in enumerate(units)]
    kh_s = [stack(tile("kt", c, p) * wc[u]).astype(BF16) for u, (c, p) in enumerate(units)]
    nu = range(len(units))

    a_ab = [jnp.where(strict, _dot_nt(a_s[u], b_s[u]), 0.0) for u in nu]
    a_ak = [jnp.where(strict, _dot_nt(a_s[u], k_s[u]), 0.0).astype(BF16) for u in nu]
    a_rb = [jnp.where(incl, _dot_nt(r_s[u], b_s[u]), 0.0).astype(BF16) for u in nu]
    a_rk = [jnp.where(incl, _dot_nt(r_s[u], k_s[u]), 0.0).astype(BF16) for u in nu]

    pw = a_ab
    tinv = [eye_f + a_ab[u] for u in nu]
    for _ in range(5):
        pb = [x.astype(BF16) for x in pw]
        pw = [_dot(pb[u], pb[u]) for u in nu]
        tinv = [tinv[u] + _dot(tinv[u].astype(BF16), pw[u].astype(BF16)) for u in nu]
    tb = [x.astype(BF16) for x in tinv]

    ap = [_dot(tb[u], a_s[u]).astype(BF16) for u in nu]
    akv = [_dot(a_ak[u], v_s[u]).astype(BF16) for u in nu]
    vp = [_dot(tb[u], akv[u]).astype(BF16) for u in nu]
    m_mat = [(jnp.where(eye, wc[u], 0.0) + _dot_tn(bh_s[u], ap[u])).astype(BF16) for u in nu]
    rp = [(rs_f[u] + _dot(a_rb[u], ap[u])).astype(BF16) for u in nu]
    vv = [jnp.concatenate([vp[u], v_s[u]], axis=0) for u in nu]
    g_mat = [_dot_tn(jnp.concatenate([bh_s[u], kh_s[u]], axis=0), vv[u]) for u in nu]
    y0 = [_dot(jnp.concatenate([a_rb[u], a_rk[u]], axis=1), vv[u]) for u in nu]

    for u, (c, p) in enumerate(units):
        zb = state_ref[p].astype(BF16)
        ys = _dot(rp[u], zb) + y0[u]
        state_ref[p] = _dot(m_mat[u], zb) + g_mat[u]
        yacc_ref[c * CHUNK:(c + 1) * CHUNK, p * PAIR:(p + 1) * PAIR] = ys[:CHUNK] + ys[CHUNK:]

    @pl.when(last)
    def _():
        zf_ref[0] = state_ref[...]

    inv_n = 1.0 / HEAD_DIM
    for sb in range(n_sub):
        rows = slice(sb * SUB, (sb + 1) * SUB)
        pr = preps[sb]
        y = yacc_ref[rows, :]
        mean = seg_sum(y) * inv_n
        d = y - mean
        var = seg_sum(d * d) * inv_n
        yn = d * lax.rsqrt(var + GN_EPS) * gnw_ref[...] + gnb_ref[...]
        bonus = seg_sum(pr["xr"] * pr["kmod"] * rk_ref[...]) * pr["xv"]
        z = z_ref[0, rows, :]
        y_ref[0, rows, :] = ((yn + bonus) * (z * _sigmoid(z))).astype(y_ref.dtype)


def _rwkv(r, k, v, z, wa, prm, z0, prev, tr):
    b, l, _ = r.shape
    wide = pl.BlockSpec((1, tr, D_BRANCH), lambda bi, t: (bi, t, 0))
    narrow = pl.BlockSpec((1, tr, LANES), lambda bi, t: (bi, t, 0))
    row_w = pl.BlockSpec((1, D_BRANCH), lambda bi, t: (0, 0))
    row_n = pl.BlockSpec((1, LANES), lambda bi, t: (0, 0))
    up = pl.BlockSpec((LANES, D_BRANCH), lambda bi, t: (0, 0))
    ones = pl.BlockSpec((D_BRANCH, D_BRANCH), lambda bi, t: (0, 0))
    st_in = pl.BlockSpec((N_PAIRS, PAIR, PAIR), lambda bi, t: (0, 0, 0))
    st_out = pl.BlockSpec((1, N_PAIRS, PAIR, PAIR), lambda bi, t: (bi, 0, 0, 0))
    lrow_w = pl.BlockSpec((1, 1, D_BRANCH), lambda bi, t: (bi, 0, 0))
    lrow_n = pl.BlockSpec((1, 1, LANES), lambda bi, t: (bi, 0, 0))
    return pl.pallas_call(
        functools.partial(_rwkv_kernel, tr=tr),
        grid=(b, l // tr),
        in_specs=[wide, wide, wide, wide, narrow,
                  row_w, row_w, row_w, row_n, row_w, up, up, row_w, up, up,
                  row_w, row_w, row_w, row_w, row_w, ones,
                  st_in, row_w, row_w, row_w, row_n],
        out_specs=[wide, st_out, lrow_w, lrow_w, lrow_w, lrow_n],
        out_shape=[jax.ShapeDtypeStruct((b, l, D_BRANCH), BF16),
                   jax.ShapeDtypeStruct((b, N_PAIRS, PAIR, PAIR), F32),
                   jax.ShapeDtypeStruct((b, 1, D_BRANCH), F32),
                   jax.ShapeDtypeStruct((b, 1, D_BRANCH), F32),
                   jax.ShapeDtypeStruct((b, 1, D_BRANCH), F32),
                   jax.ShapeDtypeStruct((b, 1, LANES), F32)],
        scratch_shapes=[pltpu.VMEM((N_PAIRS, PAIR, PAIR), F32),
                        pltpu.VMEM((1, D_BRANCH), F32), pltpu.VMEM((1, D_BRANCH), F32),
                        pltpu.VMEM((1, D_BRANCH), F32), pltpu.VMEM((1, LANES), F32),
                        pltpu.VMEM((tr, D_BRANCH), F32)],
        compiler_params=_params(("parallel", "arbitrary")),
        name="rwkv",
    )(r, k, v, z, wa, *prm, z0, *prev)


def _out_proj_kernel(o_ref, zf_ref, yr_ref, x_ref, w_ref, fnw_ref, out_ref):
    z = zf_ref[...]
    yf = (o_ref[...] * (z * _sigmoid(z))).astype(BF16)
    mix = _dot(yf, w_ref[:D_BRANCH, :]) + _dot(yr_ref[...], w_ref[D_BRANCH:, :])
    h = x_ref[...] + mix
    out_ref[...] = h * lax.rsqrt(jnp.mean(h * h, axis=-1, keepdims=True) + NORM_EPS) * fnw_ref[...]


def _out_proj(o, zf, yr, x, w_out, fnw, tm):
    n = x.shape[0]
    half = pl.BlockSpec((tm, D_BRANCH), lambda i: (i, 0))
    full = pl.BlockSpec((tm, D_MODEL), lambda i: (i, 0))
    return pl.pallas_call(
        _out_proj_kernel,
        grid=(n // tm,),
        in_specs=[half, half, half, full,
                  pl.BlockSpec((D_MODEL, D_MODEL), lambda i: (0, 0)),
                  pl.BlockSpec((1, D_MODEL), lambda i: (0, 0))],
        out_specs=full,
        out_shape=jax.ShapeDtypeStruct((n, D_MODEL), F32),
        compiler_params=_params(("parallel",)),
        name="out_proj",
    )(o, zf, yr, x, w_out, fnw)


def _tiles(b, l):
    rows = b * l
    tm = 512 if rows % 512 == 0 else 256
    return tm, 512, 512, 256


def kernel(x, meta, norm_w, w_in, b_f, mu_shift, w0, w_up, a0, a_up, k_k, k_a, r_k, gn_w, gn_b,
           w_out, final_norm_w):
    b, l, d = x.shape
    assert d == D_MODEL and norm_w.shape[0] == 1 and l % 256 == 0
    tm, t_prep, t_attn, t_rwkv = _tiles(b, l)

    wi = w_in[0]
    o = 0
    cols = {}
    for name, width in (("q", D_BRANCH), ("k", D_BRANCH), ("v", D_BRANCH), ("fl", N_HEADS), ("zf", D_BRANCH),
                        ("r", D_BRANCH), ("rk", D_BRANCH), ("rv", D_BRANCH), ("wd", RANK), ("ad", RANK),
                        ("zr", D_BRANCH)):
        cols[name] = wi[:, o:o + width]
        o += width
    w_all = jnp.concatenate(
        [cols[n] for n in ("q", "k", "v", "zf", "r", "rk", "rv", "zr", "wd", "ad", "fl")]
        + [jnp.zeros((D_MODEL, LANES - N_HEADS), F32)], axis=1).astype(BF16)

    row = lambda vec: vec.reshape(1, -1).astype(F32)
    mu = mu_shift[0]
    mu_r, mu_k, mu_v = (row(mu[i * D_BRANCH:(i + 1) * D_BRANCH]) for i in range(3))
    mu_wa = row(mu[3 * D_BRANCH:])
    zeros_up = jnp.zeros((RANK, D_BRANCH), F32)
    wup_pad = jnp.concatenate([w_up[0], zeros_up], axis=0)
    aup_pad = jnp.concatenate([zeros_up, a_up[0]], axis=0)
    hid = np.arange(D_BRANCH) // HEAD_DIM
    ones_bd = jnp.asarray(hid[:, None] == hid[None, :], BF16)
    hi_lo = lambda w: (w.astype(BF16), (w - w.astype(BF16).astype(F32)).astype(BF16))
    rwkv_prm = (mu_r, mu_k, mu_v, mu_wa, row(w0[0]), *hi_lo(wup_pad), row(a0[0]), *hi_lo(aup_pad),
                row(k_k[0]), row(k_a[0]), row(r_k[0]), row(gn_w[0]), row(gn_b[0]), ones_bd)
    bf_pad = jnp.concatenate([b_f[0], jnp.zeros((LANES - N_HEADS,), F32)]).reshape(1, LANES)
    pq, pk = _select_matrices()
    nw = row(norm_w[0])

    pre_rows = jnp.concatenate([jnp.zeros((PREFIX_ROWS - N_META, D_MODEL), F32), meta.astype(F32)], axis=0)
    pq_, pk_, pv_, _, pr_, prk_, prv_, pzr_, pwa_, pfl_ = _in_proj(pre_rows, nw, w_all, PREFIX_ROWS)
    lead = lambda a: a[None]
    _, kpre, vpre, c_pre = _fox_prep(lead(pq_), lead(pk_), lead(pv_), lead(pfl_), bf_pad,
                                     jnp.zeros((1, LANES), F32), pq, pk, PREFIX_ROWS, PREFIX_ROWS - N_META)
    zero_w = jnp.zeros((1, D_BRANCH), F32)
    _, z_pre, lr, lk, lv, lwa = _rwkv(
        lead(pr_), lead(prk_), lead(prv_), lead(pzr_), lead(pwa_), rwkv_prm,
        jnp.zeros((N_PAIRS, PAIR, PAIR), F32), (zero_w, zero_w, zero_w, jnp.zeros((1, LANES), F32)), PREFIX_ROWS)

    xf = x.reshape(b * l, D_MODEL)
    q_, k_, v_, zf_, r_, rk_, rv_, zr_, wa_, fl_ = _in_proj(xf, nw, w_all, tm)
    bl = lambda a: a.reshape(b, l, a.shape[-1])
    qp, kp, vp, _ = _fox_prep(bl(q_), bl(k_), bl(v_), bl(fl_), bf_pad, c_pre[0], pq, pk, t_prep, 0)
    o_attn = _fox_attn(qp, kp, vp, kpre, vpre, t_attn)
    y_rwkv, _, _, _, _, _ = _rwkv(bl(r_), bl(rk_), bl(rv_), bl(zr_), bl(wa_), rwkv_prm,
                                  z_pre[0], (lr[0], lk[0], lv[0], lwa[0]), t_rwkv)
    out = _out_proj(o_attn.reshape(b * l, D_BRANCH), zf_, y_rwkv.reshape(b * l, D_BRANCH), xf,
                    w_out[0].astype(BF16), row(final_norm_w), tm)
    return out.reshape(b, l, D_MODEL)
```

```python
import functools

import numpy as np
import jax
import jax.numpy as jnp
from jax import lax
from jax.experimental import pallas as pl
from jax.experimental.pallas import tpu as pltpu

F32 = jnp.float32
BF16 = jnp.bfloat16

D_MODEL = 1024
N_META = 16
HEAD_DIM = 64
N_HEADS = 8
D_BRANCH = N_HEADS * HEAD_DIM
N_PAIRS = N_HEADS // 2
RANK = 64
NORM_EPS = 1e-6
GN_EPS = 64e-5
KK_EPS = 1e-12
NEG = -1e30

LANES = 128
PREFIX_ROWS = 128
CHUNK = 64
SUB = 2 * CHUNK
PAIR = 2 * HEAD_DIM
VMEM_LIMIT = 56 * 1024 * 1024

N_WIDE = 8
COL_WA = N_WIDE * D_BRANCH
COL_FL = COL_WA + LANES
N_COLS = COL_FL + LANES

N_CPARTS = 3
ONES_LANE = (HEAD_DIM, 0)
ATTN_HEADS = 8


def _dot(a, b):
    return jnp.dot(a, b, preferred_element_type=F32)


def _pieces(x, n):
    out = []
    for _ in range(n - 1):
        p = x.astype(BF16)
        out.append(p)
        x = x - p.astype(F32)
    out.append(x.astype(BF16))
    return out


def _dot_exact_lhs(a_bf16, x, n):
    acc = None
    for p in _pieces(x, n):
        t = _dot(a_bf16, p)
        acc = t if acc is None else acc + t
    return acc


def _dot_split(x, w_hi, w_lo):
    x_hi, x_lo = _pieces(x, 2)
    return _dot(x_hi, w_hi) + _dot(x_lo, w_hi) + _dot(x_hi, w_lo)


def _dot_nt(a, b):
    return lax.dot_general(a, b, (((1,), (1,)), ((), ())), preferred_element_type=F32)


def _dot_tn(a, b):
    return lax.dot_general(a, b, (((0,), (0,)), ((), ())), preferred_element_type=F32)


def _softplus(x):
    return jnp.maximum(x, 0.0) + jnp.log(1.0 + jnp.exp(-jnp.abs(x)))


def _sigmoid(x):
    return 1.0 / (1.0 + jnp.exp(-x))


def _params(sem):
    return pltpu.CompilerParams(dimension_semantics=sem, vmem_limit_bytes=VMEM_LIMIT)


def _in_proj_kernel(x_ref, nw_ref, w_ref, *out_refs):
    x = x_ref[...]
    u = x * lax.rsqrt(jnp.mean(x * x, axis=-1, keepdims=True) + NORM_EPS) * nw_ref[...]
    ub = u.astype(BF16)
    for g in range(N_WIDE):
        o = out_refs[g]
        o[...] = _dot(ub, w_ref[:, g * D_BRANCH:(g + 1) * D_BRANCH]).astype(o.dtype)
    out_refs[N_WIDE][...] = _dot(ub, w_ref[:, COL_WA:COL_WA + LANES])
    out_refs[N_WIDE + 1][...] = _dot(ub, w_ref[:, COL_FL:COL_FL + LANES])


def _in_proj(rows, norm_w, w_all, tm):
    n = rows.shape[0]
    wide_dtypes = (BF16, BF16, BF16, F32, F32, F32, F32, F32)
    out_shape = [jax.ShapeDtypeStruct((n, D_BRANCH), dt) for dt in wide_dtypes]
    out_shape += [jax.ShapeDtypeStruct((n, LANES), F32)] * 2
    out_specs = [pl.BlockSpec((tm, D_BRANCH), lambda i: (i, 0))] * N_WIDE
    out_specs += [pl.BlockSpec((tm, LANES), lambda i: (i, 0))] * 2
    return pl.pallas_call(
        _in_proj_kernel,
        grid=(n // tm,),
        in_specs=[pl.BlockSpec((tm, D_MODEL), lambda i: (i, 0)),
                  pl.BlockSpec((1, D_MODEL), lambda i: (0, 0)),
                  pl.BlockSpec((D_MODEL, N_COLS), lambda i: (0, 0))],
        out_specs=out_specs,
        out_shape=out_shape,
        compiler_params=_params(("parallel",)),
        name="in_proj",
    )(rows, norm_w, w_all)


def _select_matrices():
    pq = np.zeros((N_HEADS, 2 * LANES, LANES), np.float32)
    pk = np.zeros((N_HEADS, 2 * LANES, LANES), np.float32)
    for h in range(N_HEADS):
        base = (h % 2) * HEAD_DIM
        for d in range(HEAD_DIM):
            pq[h, base + d, d] = 0.125
            pk[h, base + d, d] = 1.0
        for part in range(N_CPARTS):
            pq[h, LANES + part * N_HEADS + h, HEAD_DIM + part] = 1.0
            pq[h, LANES + N_CPARTS * N_HEADS + h, HEAD_DIM + N_CPARTS + part] = 1.0
            pk[h, LANES + N_CPARTS * N_HEADS + h, HEAD_DIM + part] = 1.0
            pk[h, LANES + part * N_HEADS + h, HEAD_DIM + N_CPARTS + part] = -1.0
    return jnp.asarray(pq, BF16), jnp.asarray(pk, BF16)


def _fox_prep_kernel(q_ref, k_ref, v_ref, fl_ref, bf_ref, c0_ref, pq_ref, pk_ref,
                     qo_ref, ko_ref, vo_ref, cl_ref, carry_ref, *, tr, n_pad):
    i = pl.program_id(1)

    @pl.when(i == 0)
    def _():
        carry_ref[...] = c0_ref[...]

    x = fl_ref[0] + bf_ref[...]
    logf = jnp.minimum(x, 0.0) - jnp.log(1.0 + jnp.exp(-jnp.abs(x)))
    lane = lax.broadcasted_iota(jnp.int32, (tr, LANES), 1)
    row = lax.broadcasted_iota(jnp.int32, (tr, LANES), 0) + i * tr
    valid = lane < N_HEADS
    if n_pad:
        valid = valid & (row >= n_pad)
    logf = jnp.where(valid, logf, 0.0)
    r2 = lax.broadcasted_iota(jnp.int32, (tr, tr), 0)
    c2 = lax.broadcasted_iota(jnp.int32, (tr, tr), 1)
    tri = jnp.where(r2 >= c2, 1.0, 0.0).astype(BF16)
    cum = _dot_exact_lhs(tri, logf, 3) + carry_ref[...]
    carry_ref[...] = cum[tr - 1:tr, :]

    @pl.when(i == pl.num_programs(1) - 1)
    def _():
        cl_ref[0] = cum[tr - 1:tr, :]

    p1 = cum.astype(BF16).astype(F32)
    rem = cum - p1
    p2 = rem.astype(BF16).astype(F32)
    p3 = (rem - p2).astype(BF16).astype(F32)
    ones = jnp.where((lane >= N_CPARTS * N_HEADS) & (lane < (N_CPARTS + 1) * N_HEADS), 1.0, 0.0)
    cbits = p1 + pltpu.roll(p2, N_HEADS, 1) + pltpu.roll(p3, 2 * N_HEADS, 1) + ones
    cbits_k = cbits
    if n_pad:
        cbits_k = jnp.where((row < n_pad) & (lane < N_HEADS), -NEG, cbits)
    cq = cbits.astype(BF16)
    ck = cbits_k.astype(BF16)
    sel_r = lax.broadcasted_iota(jnp.int32, (PAIR, PAIR), 0)
    sel_c = lax.broadcasted_iota(jnp.int32, (PAIR, PAIR), 1)
    vrow = lax.broadcasted_iota(jnp.int32, (PAIR, tr), 0)
    for h in range(N_HEADS):
        sl = slice((h // 2) * PAIR, (h // 2 + 1) * PAIR)
        xq = jnp.concatenate([q_ref[0, :, sl], cq], axis=1)
        xk = jnp.concatenate([k_ref[0, :, sl], ck], axis=1)
        qo_ref[0, h] = _dot(xq, pq_ref[h]).astype(BF16)
        ko_ref[0, h] = _dot(xk, pk_ref[h]).astype(BF16)
        own = (sel_r < HEAD_DIM) if h % 2 == 0 else (sel_r >= HEAD_DIM)
        pick = jnp.where((sel_r == sel_c) & own, 1.0, 0.0).astype(BF16)
        v_t = _dot_nt(pick, v_ref[0, :, sl])
        vo_ref[0, h, 0] = jnp.where(vrow == ONES_LANE[h % 2], 1.0, v_t).astype(BF16)


def _fox_prep(q, k, v, fl, bf_pad, c0, pq, pk, tr, n_pad):
    b, l, _ = q.shape
    wide = pl.BlockSpec((1, tr, D_BRANCH), lambda bi, i: (bi, i, 0))
    head_out = pl.BlockSpec((1, N_HEADS, tr, LANES), lambda bi, i: (bi, 0, i, 0))
    row128 = pl.BlockSpec((1, LANES), lambda bi, i: (0, 0))
    sel = pl.BlockSpec((N_HEADS, 2 * LANES, LANES), lambda bi, i: (0, 0, 0))
    return pl.pallas_call(
        functools.partial(_fox_prep_kernel, tr=tr, n_pad=n_pad),
        grid=(b, l // tr),
        in_specs=[wide, wide, wide,
                  pl.BlockSpec((1, tr, LANES), lambda bi, i: (bi, i, 0)),
                  row128, row128, sel, sel],
        out_specs=[head_out, head_out,
                   pl.BlockSpec((1, N_HEADS, 1, PAIR, tr), lambda bi, i: (bi, 0, i, 0, 0)),
                   pl.BlockSpec((1, 1, LANES), lambda bi, i: (bi, 0, 0))],
        out_shape=[jax.ShapeDtypeStruct((b, N_HEADS, l, LANES), BF16)] * 2
                  + [jax.ShapeDtypeStruct((b, N_HEADS, l // tr, PAIR, tr), BF16),
                     jax.ShapeDtypeStruct((b, 1, LANES), F32)],
        scratch_shapes=[pltpu.VMEM((1, LANES), F32)],
        compiler_params=_params(("parallel", "arbitrary")),
        name="fox_prep",
    )(q, k, v, fl, bf_pad, c0, pq, pk)


def _attn_kernel(q_ref, kpre_ref, vpre_ref, k_ref, v_ref, o_ref, s_ref, *, tq):
    qi = pl.program_id(2)
    heads = range(ATTN_HEADS)
    qs = [q_ref[0, h] for h in heads]

    def rowmax(s):
        return jnp.max(s, axis=0, keepdims=True)

    def k_chunk(h, idx):
        return k_ref[0, h, pl.ds(pl.multiple_of(idx * tq, tq), tq), :]

    def v_chunk(h, idx):
        return v_ref[0, h, idx]

    def scores(idx):
        return [_dot_nt(k_chunk(h, idx), qs[h]) for h in heads]

    def softmax_pv(carry, s, rmax, idx):
        out = []
        for h in heads:
            m, acc = carry[h]
            m_new = jnp.maximum(m, rmax[h])
            p = jnp.exp(s[h] - m_new).astype(BF16)
            out.append((m_new, jnp.exp(m - m_new) * acc + _dot(v_chunk(h, idx), p)))
        return out

    key = lax.broadcasted_iota(jnp.int32, (tq, tq), 0)
    qry = lax.broadcasted_iota(jnp.int32, (tq, tq), 1)
    s_pre = [_dot_nt(kpre_ref[0, h], qs[h]) for h in heads]
    s_dia = [jnp.where(key <= qry, s, NEG) for s in scores(qi)]
    s_nxt = scores(0)
    carry = []
    for h in heads:
        m0 = jnp.maximum(rowmax(s_pre[h]), rowmax(s_dia[h]))
        p_pre = jnp.exp(s_pre[h] - m0).astype(BF16)
        p_dia = jnp.exp(s_dia[h] - m0).astype(BF16)
        carry.append((m0, _dot(vpre_ref[0, h, 0], p_pre) + _dot(v_chunk(h, qi), p_dia)))
        s_ref[h] = s_nxt[h]
    rmax = [rowmax(s) for s in s_nxt]

    def step2(t, state):
        carry, rmax = state
        k0 = 2 * t
        s_a = [s_ref[h] for h in heads]
        s_b = scores(k0 + 1)
        carry = softmax_pv(carry, s_a, rmax, k0)
        rmax_b = [rowmax(s) for s in s_b]
        s_c = scores(k0 + 2)
        carry = softmax_pv(carry, s_b, rmax_b, k0 + 1)
        for h in heads:
            s_ref[h] = s_c[h]
        return carry, [rowmax(s) for s in s_c]

    def step1(k0, state):
        carry, rmax = state
        s_a = [s_ref[h] for h in heads]
        s_b = scores(k0 + 1)
        carry = softmax_pv(carry, s_a, rmax, k0)
        for h in heads:
            s_ref[h] = s_b[h]
        return carry, [rowmax(s) for s in s_b]

    steps = jnp.maximum(qi - 1, 0)
    pairs = steps // 2
    state = lax.fori_loop(0, pairs, step2, (carry, rmax))
    carry, rmax = lax.fori_loop(2 * pairs, steps, step1, state)

    some = qi > 0
    last = jnp.maximum(qi - 1, 0)
    s_fin = [jnp.where(some, s_ref[h], NEG) for h in heads]
    r_fin = [jnp.where(some, r, NEG) for r in rmax]
    accs = [acc for _, acc in softmax_pv(carry, s_fin, r_fin, last)]
    vrow = lax.broadcasted_iota(jnp.int32, (PAIR, tq), 0)
    for pr in range(ATTN_HEADS // 2):
        acc0, acc1 = accs[2 * pr], accs[2 * pr + 1]
        l0 = acc0[ONES_LANE[0]:ONES_LANE[0] + 1, :]
        l1 = acc1[ONES_LANE[1]:ONES_LANE[1] + 1, :]
        o_t = jnp.where(vrow < HEAD_DIM, acc0 / l0, acc1 / l1)
        o_ref[0, :, pr * PAIR:(pr + 1) * PAIR] = o_t.T


def _fox_attn(qp, kp, vp, kpre, vpre, tq):
    b, _, l, _ = qp.shape
    g = ATTN_HEADS
    assert vp.shape[-1] == tq and vpre.shape[-1] == PREFIX_ROWS
    pre = pl.BlockSpec((1, g, PREFIX_ROWS, LANES), lambda bi, p, qi: (0, p, 0, 0))
    pre_v = pl.BlockSpec((1, g, 1, PAIR, PREFIX_ROWS), lambda bi, p, qi: (0, p, 0, 0, 0))
    once = pl.Buffered(1)
    full = pl.BlockSpec((1, g, l, LANES), lambda bi, p, qi: (bi, p, 0, 0), pipeline_mode=once)
    full_v = pl.BlockSpec((1, g, l // tq, PAIR, tq), lambda bi, p, qi: (bi, p, 0, 0, 0), pipeline_mode=once)
    return pl.pallas_call(
        functools.partial(_attn_kernel, tq=tq),
        grid=(b, N_HEADS // g, l // tq),
        in_specs=[pl.BlockSpec((1, g, tq, LANES), lambda bi, p, qi: (bi, p, qi, 0)),
                  pre, pre_v, full, full_v],
        out_specs=pl.BlockSpec((1, tq, g * HEAD_DIM), lambda bi, p, qi: (bi, qi, p)),
        out_shape=jax.ShapeDtypeStruct((b, l, D_BRANCH), F32),
        scratch_shapes=[pltpu.VMEM((g, tq, tq), F32)],
        compiler_params=_params(("parallel", "parallel", "parallel")),
        name="fox_attn",
    )(qp, kpre, vpre, kp, vp)


def _rwkv_kernel(r_ref, k_ref, v_ref, z_ref, wa_ref,
                 mu_r_ref, mu_k_ref, mu_v_ref, mu_wa_ref, w0_ref, wuph_ref, wupl_ref,
                 a0_ref, auph_ref, aupl_ref,
                 kk_ref, ka_ref, rk_ref, gnw_ref, gnb_ref, ones_ref,
                 z0_ref, pr_ref, pk_ref, pv_ref, pwa_ref,
                 y_ref, zf_ref, lr_ref, lk_ref, lv_ref, lwa_ref,
                 state_ref, sr_ref, sk_ref, sv_ref, swa_ref, yacc_ref, *, tr):
    t = pl.program_id(1)
    last = t == pl.num_programs(1) - 1

    @pl.when(t == 0)
    def _():
        state_ref[...] = z0_ref[...]
        sr_ref[...] = pr_ref[...]
        sk_ref[...] = pk_ref[...]
        sv_ref[...] = pv_ref[...]
        swa_ref[...] = pwa_ref[...]

    ones_bd = ones_ref[...]

    def seg_sum(x):
        return _dot(x.astype(BF16), ones_bd)

    def shifted(cur, prev_row, mu_ref):
        first = lax.broadcasted_iota(jnp.int32, cur.shape, 0) == 0
        prev = jnp.where(first, prev_row, pltpu.roll(cur, 1, 0))
        return cur + mu_ref[...] * (prev - cur)

    r2 = lax.broadcasted_iota(jnp.int32, (SUB, SUB), 0)
    c2 = lax.broadcasted_iota(jnp.int32, (SUB, SUB), 1)
    tri = jnp.where((r2 >= c2) & (r2 // CHUNK == c2 // CHUNK), 1.0, 0.0).astype(BF16)

    n_sub = tr // SUB
    prev_rows = [sr_ref[...], sk_ref[...], sv_ref[...], swa_ref[...]]
    mus = (mu_r_ref, mu_k_ref, mu_v_ref, mu_wa_ref)
    preps = []
    for sb in range(n_sub):
        rows = slice(sb * SUB, (sb + 1) * SUB)
        cur = [ref[0, rows, :] for ref in (r_ref, k_ref, v_ref, wa_ref)]
        xr, xk, xv, xwa = [shifted(c, p, mu) for c, p, mu in zip(cur, prev_rows, mus)]
        prev_rows = [c[SUB - 1:SUB, :] for c in cur]
        w_lin = w0_ref[...] + _dot_split(jnp.tanh(xwa), wuph_ref[...], wupl_ref[...])
        a_lin = a0_ref[...] + _dot_split(xwa, auph_ref[...], aupl_ref[...])
        w = -_softplus(-w_lin) - 0.5
        ld = -jnp.exp(w)
        a = _sigmoid(a_lin)
        kk = xk * kk_ref[...]
        kk = kk * lax.rsqrt(seg_sum(kk * kk) + KK_EPS)
        kmod = xk * (1.0 + (a - 1.0) * ka_ref[...])
        lw = _dot_exact_lhs(tri, ld, 3)
        w_inv = jnp.exp(-lw)
        preps.append(dict(xr=xr, xv=xv, kmod=kmod, lw=lw, rt=xr * jnp.exp(lw), at=-kk * jnp.exp(lw - ld),
                          bt=kk * a * w_inv, kt=kmod * w_inv))
    for ref, rowv in zip((sr_ref, sk_ref, sv_ref, swa_ref), prev_rows):
        ref[...] = rowv

    @pl.when(last)
    def _():
        for out, rowv in zip((lr_ref, lk_ref, lv_ref, lwa_ref), prev_rows):
            out[0] = rowv

    lane = lax.broadcasted_iota(jnp.int32, (CHUNK, PAIR), 1)
    head_a = lane < HEAD_DIM
    tau_r = lax.broadcasted_iota(jnp.int32, (PAIR, PAIR), 0)
    tau_c = lax.broadcasted_iota(jnp.int32, (PAIR, PAIR), 1)
    strict = (tau_r % CHUNK) > (tau_c % CHUNK)
    incl = (tau_r % CHUNK) >= (tau_c % CHUNK)
    eye = tau_r == tau_c
    eye_f = jnp.where(eye, 1.0, 0.0).astype(F32)

    def stack(x):
        return jnp.concatenate([jnp.where(head_a, x, 0.0), jnp.where(head_a, 0.0, x)], axis=0)

    n_chunks = tr // CHUNK
    per_sub = SUB // CHUNK
    units = [(c, p) for c in range(n_chunks) for p in range(N_PAIRS)]
    w_end = [jnp.exp(preps[c // per_sub]["lw"][(c % per_sub + 1) * CHUNK - 1:(c % per_sub + 1) * CHUNK, :])
             for c in range(n_chunks)]

    def tile(name, c, p):
        lo = (c % per_sub) * CHUNK
        return preps[c // per_sub][name][lo:lo + CHUNK, p * PAIR:(p + 1) * PAIR]

    wc = [w_end[c][:, p * PAIR:(p + 1) * PAIR] for c, p in units]
    rs_f = [stack(tile("rt", c, p)) for c, p in units]
    r_s = [x.astype(BF16) for x in rs_f]
    a_s = [stack(tile("at", c, p)).astype(BF16) for c, p in units]
    b_s = [stack(tile("bt", c, p)).astype(BF16) for c, p in units]
    k_s = [stack(tile("kt", c, p)).astype(BF16) for c, p in units]
    v_s = [stack(tile("xv", c, p)).astype(BF16) for c, p in units]
    bh_s = [stack(tile("bt", c, p) * wc[u]).astype(BF16) for u, (c, p) in enumerate(units)]
    kh_s = [stack(tile("kt", c, p) * wc[u]).astype(BF16) for u, (c, p) in enumerate(units)]
    nu = range(len(units))

    a_ab = [jnp.where(strict, _dot_nt(a_s[u], b_s[u]), 0.0) for u in nu]
    a_ak = [jnp.where(strict, _dot_nt(a_s[u], k_s[u]), 0.0).astype(BF16) for u in nu]
    a_rb = [jnp.where(incl, _dot_nt(r_s[u], b_s[u]), 0.0).astype(BF16) for u in nu]
    a_rk = [jnp.where(incl, _dot_nt(r_s[u], k_s[u]), 0.0).astype(BF16) for u in nu]

    pw = [_dot(a.astype(BF16), a.astype(BF16)) for a in a_ab]
    tinv = [eye_f + a_ab[u] for u in nu]
    for level in range(1, 6):
        pb = [x.astype(BF16) for x in pw]
        if level < 5:
            both = [_dot(jnp.concatenate([tinv[u].astype(BF16), pb[u]], axis=0), pb[u]) for u in nu]
            tinv = [tinv[u] + both[u][:PAIR] for u in nu]
            pw = [both[u][PAIR:] for u in nu]
        else:
            tinv = [tinv[u] + _dot(tinv[u].astype(BF16), pb[u]) for u in nu]
    tb = [x.astype(BF16) for x in tinv]

    ap = [_dot(tb[u], a_s[u]).astype(BF16) for u in nu]
    akv = [_dot(a_ak[u], v_s[u]).astype(BF16) for u in nu]
    vp = [_dot(tb[u], akv[u]).astype(BF16) for u in nu]
    m_mat = [(jnp.where(eye, wc[u], 0.0) + _dot_tn(bh_s[u], ap[u])).astype(BF16) for u in nu]
    rp = [(rs_f[u] + _dot(a_rb[u], ap[u])).astype(BF16) for u in nu]
    vv = [jnp.concatenate([vp[u], v_s[u]], axis=0) for u in nu]
    g_mat = [_dot_tn(jnp.concatenate([bh_s[u], kh_s[u]], axis=0), vv[u]) for u in nu]
    y0 = [_dot(jnp.concatenate([a_rb[u], a_rk[u]], axis=1), vv[u]) for u in nu]

    for u, (c, p) in enumerate(units):
        zb = state_ref[p].astype(BF16)
        ys = _dot(rp[u], zb) + y0[u]
        state_ref[p] = _dot(m_mat[u], zb) + g_mat[u]
        yacc_ref[c * CHUNK:(c + 1) * CHUNK, p * PAIR:(p + 1) * PAIR] = ys[:CHUNK] + ys[CHUNK:]

    @pl.when(last)
    def _():
        zf_ref[0] = state_ref[...]

    inv_n = 1.0 / HEAD_DIM
    for sb in range(n_sub):
        rows = slice(sb * SUB, (sb + 1) * SUB)
        pr = preps[sb]
        y = yacc_ref[rows, :]
        mean = seg_sum(y) * inv_n
        d = y - mean
        var = seg_sum(d * d) * inv_n
        yn = d * lax.rsqrt(var + GN_EPS) * gnw_ref[...] + gnb_ref[...]
        bonus = seg_sum(pr["xr"] * pr["kmod"] * rk_ref[...]) * pr["xv"]
        z = z_ref[0, rows, :]
        y_ref[0, rows, :] = ((yn + bonus) * (z * _sigmoid(z))).astype(y_ref.dtype)

def _rwkv(r, k, v, z, wa, prm, z0, prev, tr):
    b, l, _ = r.shape
    wide = pl.BlockSpec((1, tr, D_BRANCH), lambda bi, t: (bi, t, 0))
    narrow = pl.BlockSpec((1, tr, LANES), lambda bi, t: (bi, t, 0))
    row_w = pl.BlockSpec((1, D_BRANCH), lambda bi, t: (0, 0))
    row_n = pl.BlockSpec((1, LANES), lambda bi, t: (0, 0))
    up = pl.BlockSpec((LANES, D_BRANCH), lambda bi, t: (0, 0))
    ones = pl.BlockSpec((D_BRANCH, D_BRANCH), lambda bi, t: (0, 0))
    st_in = pl.BlockSpec((N_PAIRS, PAIR, PAIR), lambda bi, t: (0, 0, 0))
    st_out = pl.BlockSpec((1, N_PAIRS, PAIR, PAIR), lambda bi, t: (bi, 0, 0, 0))
    lrow_w = pl.BlockSpec((1, 1, D_BRANCH), lambda bi, t: (bi, 0, 0))
    lrow_n = pl.BlockSpec((1, 1, LANES), lambda bi, t: (bi, 0, 0))
    return pl.pallas_call(
        functools.partial(_rwkv_kernel, tr=tr),
        grid=(b, l // tr),
        in_specs=[wide, wide, wide, wide, narrow,
                  row_w, row_w, row_w, row_n, row_w, up, up, row_w, up, up,
                  row_w, row_w, row_w, row_w, row_w, ones,
                  st_in, row_w, row_w, row_w, row_n],
        out_specs=[wide, st_out, lrow_w, lrow_w, lrow_w, lrow_n],
        out_shape=[jax.ShapeDtypeStruct((b, l, D_BRANCH), BF16),
                   jax.ShapeDtypeStruct((b, N_PAIRS, PAIR, PAIR), F32),
                   jax.ShapeDtypeStruct((b, 1, D_BRANCH), F32),
                   jax.ShapeDtypeStruct((b, 1, D_BRANCH), F32),
                   jax.ShapeDtypeStruct((b, 1, D_BRANCH), F32),
                   jax.ShapeDtypeStruct((b, 1, LANES), F32)],
        scratch_shapes=[pltpu.VMEM((N_PAIRS, PAIR, PAIR), F32),
                        pltpu.VMEM((1, D_BRANCH), F32), pltpu.VMEM((1, D_BRANCH), F32),
                        pltpu.VMEM((1, D_BRANCH), F32), pltpu.VMEM((1, LANES), F32),
                        pltpu.VMEM((tr, D_BRANCH), F32)],
        compiler_params=_params(("parallel", "arbitrary")),
        name="rwkv",
    )(r, k, v, z, wa, *prm, z0, *prev)


def _out_proj_kernel(o_ref, zf_ref, yr_ref, x_ref, w_ref, fnw_ref, out_ref):
    z = zf_ref[...]
    yf = (o_ref[...] * (z * _sigmoid(z))).astype(BF16)
    mix = _dot(yf, w_ref[:D_BRANCH, :]) + _dot(yr_ref[...], w_ref[D_BRANCH:, :])
    h = x_ref[...] + mix
    out_ref[...] = h * lax.rsqrt(jnp.mean(h * h, axis=-1, keepdims=True) + NORM_EPS) * fnw_ref[...]


def _out_proj(o, zf, yr, x, w_out, fnw, tm):
    n = x.shape[0]
    half = pl.BlockSpec((tm, D_BRANCH), lambda i: (i, 0))
    full = pl.BlockSpec((tm, D_MODEL), lambda i: (i, 0))
    return pl.pallas_call(
        _out_proj_kernel,
        grid=(n // tm,),
        in_specs=[half, half, half, full,
                  pl.BlockSpec((D_MODEL, D_MODEL), lambda i: (0, 0)),
                  pl.BlockSpec((1, D_MODEL), lambda i: (0, 0))],
        out_specs=full,
        out_shape=jax.ShapeDtypeStruct((n, D_MODEL), F32),
        compiler_params=_params(("parallel",)),
        name="out_proj",
    )(o, zf, yr, x, w_out, fnw)


def _tiles(b, l):
    rows = b * l
    tm = 512 if rows % 512 == 0 else 256
    return tm, 512, 512, 512


def kernel(x, meta, norm_w, w_in, b_f, mu_shift, w0, w_up, a0, a_up, k_k, k_a, r_k, gn_w, gn_b,
           w_out, final_norm_w):
    b, l, d = x.shape
    assert d == D_MODEL and norm_w.shape[0] == 1 and l % 256 == 0
    tm, t_prep, t_attn, t_rwkv = _tiles(b, l)

    wi = w_in[0]
    o = 0
    cols = {}
    for name, width in (("q", D_BRANCH), ("k", D_BRANCH), ("v", D_BRANCH), ("fl", N_HEADS), ("zf", D_BRANCH),
                        ("r", D_BRANCH), ("rk", D_BRANCH), ("rv", D_BRANCH), ("wd", RANK), ("ad", RANK),
                        ("zr", D_BRANCH)):
        cols[name] = wi[:, o:o + width]
        o += width
    w_all = jnp.concatenate(
        [cols[n] for n in ("q", "k", "v", "zf", "r", "rk", "rv", "zr", "wd", "ad", "fl")]
        + [jnp.zeros((D_MODEL, LANES - N_HEADS), F32)], axis=1).astype(BF16)

    row = lambda vec: vec.reshape(1, -1).astype(F32)
    mu = mu_shift[0]
    mu_r, mu_k, mu_v = (row(mu[i * D_BRANCH:(i + 1) * D_BRANCH]) for i in range(3))
    mu_wa = row(mu[3 * D_BRANCH:])
    zeros_up = jnp.zeros((RANK, D_BRANCH), F32)
    wup_pad = jnp.concatenate([w_up[0], zeros_up], axis=0)
    aup_pad = jnp.concatenate([zeros_up, a_up[0]], axis=0)
    hid = np.arange(D_BRANCH) // HEAD_DIM
    ones_bd = jnp.asarray(hid[:, None] == hid[None, :], BF16)
    hi_lo = lambda w: (w.astype(BF16), (w - w.astype(BF16).astype(F32)).astype(BF16))
    rwkv_prm = (mu_r, mu_k, mu_v, mu_wa, row(w0[0]), *hi_lo(wup_pad), row(a0[0]), *hi_lo(aup_pad),
                row(k_k[0]), row(k_a[0]), row(r_k[0]), row(gn_w[0]), row(gn_b[0]), ones_bd)
    bf_pad = jnp.concatenate([b_f[0], jnp.zeros((LANES - N_HEADS,), F32)]).reshape(1, LANES)
    pq, pk = _select_matrices()
    nw = row(norm_w[0])

    pre_rows = jnp.concatenate([jnp.zeros((PREFIX_ROWS - N_META, D_MODEL), F32), meta.astype(F32)], axis=0)
    pq_, pk_, pv_, _, pr_, prk_, prv_, pzr_, pwa_, pfl_ = _in_proj(pre_rows, nw, w_all, PREFIX_ROWS)
    lead = lambda a: a[None]
    _, kpre, vpre, c_pre = _fox_prep(lead(pq_), lead(pk_), lead(pv_), lead(pfl_), bf_pad,
                                     jnp.zeros((1, LANES), F32), pq, pk, PREFIX_ROWS, PREFIX_ROWS - N_META)
    zero_w = jnp.zeros((1, D_BRANCH), F32)
    _, z_pre, lr, lk, lv, lwa = _rwkv(
        lead(pr_), lead(prk_), lead(prv_), lead(pzr_), lead(pwa_), rwkv_prm,
        jnp.zeros((N_PAIRS, PAIR, PAIR), F32), (zero_w, zero_w, zero_w, jnp.zeros((1, LANES), F32)), PREFIX_ROWS)

    xf = x.reshape(b * l, D_MODEL)
    q_, k_, v_, zf_, r_, rk_, rv_, zr_, wa_, fl_ = _in_proj(xf, nw, w_all, tm)
    bl = lambda a: a.reshape(b, l, a.shape[-1])
    qp, kp, vp, _ = _fox_prep(bl(q_), bl(k_), bl(v_), bl(fl_), bf_pad, c_pre[0], pq, pk, t_prep, 0)
    o_attn = _fox_attn(qp, kp, vp, kpre, vpre, t_attn)
    y_rwkv, _, _, _, _, _ = _rwkv(bl(r_), bl(rk_), bl(rv_), bl(zr_), bl(wa_), rwkv_prm,
                                  z_pre[0], (lr[0], lk[0], lv[0], lwa[0]), t_rwkv)
    out = _out_proj(o_attn.reshape(b * l, D_BRANCH), zf_, y_rwkv.reshape(b * l, D_BRANCH), xf,
                    w_out[0].astype(BF16), row(final_norm_w), tm)
    return out.reshape(b, l, D_MODEL)
```

```python
import functools

import numpy as np
import jax
import jax.numpy as jnp
from jax import lax
from jax.experimental import pallas as pl
from jax.experimental.pallas import tpu as pltpu

F32 = jnp.float32
BF16 = jnp.bfloat16

D_MODEL = 1024
N_META = 16
HEAD_DIM = 64
N_HEADS = 8
D_BRANCH = N_HEADS * HEAD_DIM
N_PAIRS = N_HEADS // 2
RANK = 64
NORM_EPS = 1e-6
GN_EPS = 64e-5
KK_EPS = 1e-12
NEG = -1e30

LANES = 128
PREFIX_ROWS = 128
CHUNK = 64
SUB = 2 * CHUNK
PAIR = 2 * HEAD_DIM
VMEM_LIMIT = 56 * 1024 * 1024

N_WIDE = 8
COL_WA = N_WIDE * D_BRANCH
COL_FL = COL_WA + LANES
N_COLS = COL_FL + LANES

N_CPARTS = 3
ONES_LANE = (HEAD_DIM, 0)
ATTN_HEADS = 4


def _dot(a, b):
    return jnp.dot(a, b, preferred_element_type=F32)


def _pieces(x, n):
    out = []
    for _ in range(n - 1):
        p = x.astype(BF16)
        out.append(p)
        x = x - p.astype(F32)
    out.append(x.astype(BF16))
    return out


def _dot_exact_lhs(a_bf16, x, n):
    acc = None
    for p in _pieces(x, n):
        t = _dot(a_bf16, p)
        acc = t if acc is None else acc + t
    return acc


def _dot_x2(x, w_bf16):
    x_hi, x_lo = _pieces(x, 2)
    return _dot(x_hi, w_bf16) + _dot(x_lo, w_bf16)


def _dot_nt(a, b):
    return lax.dot_general(a, b, (((1,), (1,)), ((), ())), preferred_element_type=F32)


def _dot_tn(a, b):
    return lax.dot_general(a, b, (((0,), (0,)), ((), ())), preferred_element_type=F32)


def _softplus(x):
    return jnp.maximum(x, 0.0) + jnp.log(1.0 + jnp.exp(-jnp.abs(x)))


def _sigmoid(x):
    return 1.0 / (1.0 + jnp.exp(-x))


def _params(sem):
    return pltpu.CompilerParams(dimension_semantics=sem, vmem_limit_bytes=VMEM_LIMIT)


G_Q, G_K, G_V, G_ZF, G_R, G_RK, G_RV, G_ZR = range(N_WIDE)
RWKV_OPERANDS = ("rt", "at", "bt", "kt", "xv", "bh", "kh", "bonus")


def _in_proj_kernel(x_ref, nw_ref, w_ref,
                    mu_r_ref, mu_k_ref, mu_v_ref, mu_wa_ref, w0_ref, wup_ref,
                    a0_ref, aup_ref, kk_ref, ka_ref, rk_ref, ones_ref,
                    pr_ref, pk_ref, pv_ref, pwa_ref,
                    q_ref, k_ref, v_ref, zf_ref, zr_ref, fl_ref,
                    rt_ref, at_ref, bt_ref, kt_ref, xv_ref, bh_ref, kh_ref, bonus_ref, wend_ref,
                    lr_ref, lk_ref, lv_ref, lwa_ref,
                    sr_ref, sk_ref, sv_ref, swa_ref, *, tm, tiles_per_seq):
    i = pl.program_id(0)

    @pl.when(lax.rem(i, tiles_per_seq) == 0)
    def _():
        sr_ref[...] = pr_ref[...]
        sk_ref[...] = pk_ref[...]
        sv_ref[...] = pv_ref[...]
        swa_ref[...] = pwa_ref[...]

    x = x_ref[...]
    u = x * lax.rsqrt(jnp.mean(x * x, axis=-1, keepdims=True) + NORM_EPS) * nw_ref[...]
    ub = u.astype(BF16)

    def group(g):
        return _dot(ub, w_ref[:, g * D_BRANCH:(g + 1) * D_BRANCH])

    raw = [group(G_R), group(G_RK), group(G_RV), _dot(ub, w_ref[:, COL_WA:COL_WA + LANES])]
    ones_bd = ones_ref[...]

    def seg_sum(v):
        return _dot(v.astype(BF16), ones_bd)

    def shifted(cur, prev_row, mu_ref):
        first = lax.broadcasted_iota(jnp.int32, cur.shape, 0) == 0
        prev = jnp.where(first, prev_row, pltpu.roll(cur, 1, 0))
        return cur + mu_ref[...] * (prev - cur)

    r2 = lax.broadcasted_iota(jnp.int32, (SUB, SUB), 0)
    c2 = lax.broadcasted_iota(jnp.int32, (SUB, SUB), 1)
    tri = jnp.where((r2 >= c2) & (r2 // CHUNK == c2 // CHUNK), 1.0, 0.0).astype(BF16)
    outs = dict(zip(RWKV_OPERANDS, (rt_ref, at_ref, bt_ref, kt_ref, xv_ref, bh_ref, kh_ref, bonus_ref)))

    def rwkv_prepare():
        prev_rows = [sr_ref[...], sk_ref[...], sv_ref[...], swa_ref[...]]
        mus = (mu_r_ref, mu_k_ref, mu_v_ref, mu_wa_ref)
        part = []
        for sb in range(tm // SUB):
            rows = slice(sb * SUB, (sb + 1) * SUB)
            cur = [a[rows, :] for a in raw]
            xr, xk, xv, xwa = [shifted(c, p, mu) for c, p, mu in zip(cur, prev_rows, mus)]
            prev_rows = [c[SUB - 1:SUB, :] for c in cur]
            w_lin = w0_ref[...] + _dot_x2(jnp.tanh(xwa), wup_ref[...])
            a_lin = a0_ref[...] + _dot(xwa.astype(BF16), aup_ref[...])
            kk = xk * kk_ref[...]
            part.append((sb, xr, xk, xv, w_lin, a_lin, kk, seg_sum(kk * kk)))
        yield prev_rows
        part2 = []
        for sb, xr, xk, xv, w_lin, a_lin, kk, kk_ss in part:
            w = -_softplus(-w_lin) - 0.5
            ld = -jnp.exp(w)
            a = _sigmoid(a_lin)
            kk = kk * lax.rsqrt(kk_ss + KK_EPS)
            kmod = xk * (1.0 + (a - 1.0) * ka_ref[...])
            lw = _dot_exact_lhs(tri, ld, 2)
            part2.append((sb, xr, xv, kmod, ld, a, kk, lw, seg_sum(xr * kmod * rk_ref[...])))
        yield None
        for sb, xr, xv, kmod, ld, a, kk, lw, rk_sum in part2:
            rows = slice(sb * SUB, (sb + 1) * SUB)
            w_inv = jnp.exp(-lw)
            bt = kk * a * w_inv
            kt = kmod * w_inv
            for name, val in (("rt", xr * jnp.exp(lw)), ("at", -kk * jnp.exp(lw - ld)), ("bt", bt), ("kt", kt),
                              ("xv", xv), ("bonus", rk_sum * xv)):
                outs[name][rows, :] = val.astype(BF16)
            for c in range(SUB // CHUNK):
                crow = sb * SUB + c * CHUNK
                w_c = jnp.exp(lw[(c + 1) * CHUNK - 1:(c + 1) * CHUNK, :])
                wend_ref[crow // CHUNK:crow // CHUNK + 1, :] = w_c
                bh_ref[crow:crow + CHUNK, :] = (bt[c * CHUNK:(c + 1) * CHUNK, :] * w_c).astype(BF16)
                kh_ref[crow:crow + CHUNK, :] = (kt[c * CHUNK:(c + 1) * CHUNK, :] * w_c).astype(BF16)
        yield None

    parts = rwkv_prepare()
    last_rows = next(parts)
    q_ref[...] = group(G_Q).astype(BF16)
    k_ref[...] = group(G_K).astype(BF16)
    next(parts)
    v_ref[...] = group(G_V).astype(BF16)
    zf_ref[...] = group(G_ZF)
    next(parts)
    zr_ref[...] = group(G_ZR)
    fl_ref[...] = _dot(ub, w_ref[:, COL_FL:COL_FL + LANES])

    for ref, rowv in zip((sr_ref, sk_ref, sv_ref, swa_ref), last_rows):
        ref[...] = rowv

    @pl.when(i == pl.num_programs(0) - 1)
    def _():
        for out, rowv in zip((lr_ref, lk_ref, lv_ref, lwa_ref), last_rows):
            out[...] = rowv


def _in_proj(rows, norm_w, w_all, prm, prev, tm, tiles_per_seq):
    n = rows.shape[0]
    tile = lambda width: pl.BlockSpec((tm, width), lambda i: (i, 0))
    const = lambda shape: pl.BlockSpec(shape, lambda i: (0,) * len(shape))
    row_w, row_n, up = const((1, D_BRANCH)), const((1, LANES)), const((LANES, D_BRANCH))
    wide = lambda dt: jax.ShapeDtypeStruct((n, D_BRANCH), dt)
    return pl.pallas_call(
        functools.partial(_in_proj_kernel, tm=tm, tiles_per_seq=tiles_per_seq),
        grid=(n // tm,),
        in_specs=[tile(D_MODEL), const((1, D_MODEL)),
                  pl.BlockSpec((D_MODEL, N_COLS), lambda i: (0, 0), pipeline_mode=pl.Buffered(1)),
                  row_w, row_w, row_w, row_n, row_w, up, row_w, up,
                  row_w, row_w, row_w, const((D_BRANCH, D_BRANCH)),
                  row_w, row_w, row_w, row_n],
        out_specs=[tile(D_BRANCH)] * 5 + [tile(LANES)] + [tile(D_BRANCH)] * len(RWKV_OPERANDS)
                  + [pl.BlockSpec((tm // CHUNK, D_BRANCH), lambda i: (i, 0)),
                     row_w, row_w, row_w, row_n],
        out_shape=[wide(BF16), wide(BF16), wide(BF16), wide(F32), wide(F32),
                   jax.ShapeDtypeStruct((n, LANES), F32)]
                  + [wide(BF16)] * len(RWKV_OPERANDS)
                  + [jax.ShapeDtypeStruct((n // CHUNK, D_BRANCH), F32)]
                  + [jax.ShapeDtypeStruct((1, D_BRANCH), F32)] * 3 + [jax.ShapeDtypeStruct((1, LANES), F32)],
        scratch_shapes=[pltpu.VMEM((1, D_BRANCH), F32), pltpu.VMEM((1, D_BRANCH), F32),
                        pltpu.VMEM((1, D_BRANCH), F32), pltpu.VMEM((1, LANES), F32)],
        compiler_params=_params(("arbitrary",)),
        name="in_proj",
    )(rows, norm_w, w_all, *prm, *prev)


def _select_matrices():
    pq = np.zeros((N_HEADS, 2 * LANES, LANES), np.float32)
    pk = np.zeros((N_HEADS, 2 * LANES, LANES), np.float32)
    for h in range(N_HEADS):
        base = (h % 2) * HEAD_DIM
        for d in range(HEAD_DIM):
            pq[h, base + d, d] = 0.125
            pk[h, base + d, d] = 1.0
        for part in range(N_CPARTS):
            pq[h, LANES + part * N_HEADS + h, HEAD_DIM + part] = 1.0
            pq[h, LANES + N_CPARTS * N_HEADS + h, HEAD_DIM + N_CPARTS + part] = 1.0
            pk[h, LANES + N_CPARTS * N_HEADS + h, HEAD_DIM + part] = 1.0
            pk[h, LANES + part * N_HEADS + h, HEAD_DIM + N_CPARTS + part] = -1.0
    return jnp.asarray(pq, BF16), jnp.asarray(pk, BF16)


def _fox_prep_kernel(q_ref, k_ref, v_ref, fl_ref, bf_ref, c0_ref, pq_ref, pk_ref,
                     qo_ref, ko_ref, vo_ref, cl_ref, carry_ref, *, tr, n_pad):
    i = pl.program_id(1)

    @pl.when(i == 0)
    def _():
        carry_ref[...] = c0_ref[...]

    x = fl_ref[0] + bf_ref[...]
    logf = jnp.minimum(x, 0.0) - jnp.log(1.0 + jnp.exp(-jnp.abs(x)))
    lane = lax.broadcasted_iota(jnp.int32, (tr, LANES), 1)
    row = lax.broadcasted_iota(jnp.int32, (tr, LANES), 0) + i * tr
    valid = lane < N_HEADS
    if n_pad:
        valid = valid & (row >= n_pad)
    logf = jnp.where(valid, logf, 0.0)
    r2 = lax.broadcasted_iota(jnp.int32, (tr, tr), 0)
    c2 = lax.broadcasted_iota(jnp.int32, (tr, tr), 1)
    tri = jnp.where(r2 >= c2, 1.0, 0.0).astype(BF16)
    cum = _dot_exact_lhs(tri, logf, 3) + carry_ref[...]
    carry_ref[...] = cum[tr - 1:tr, :]

    @pl.when(i == pl.num_programs(1) - 1)
    def _():
        cl_ref[0] = cum[tr - 1:tr, :]

    p1 = cum.astype(BF16).astype(F32)
    rem = cum - p1
    p2 = rem.astype(BF16).astype(F32)
    p3 = (rem - p2).astype(BF16).astype(F32)
    ones = jnp.where((lane >= N_CPARTS * N_HEADS) & (lane < (N_CPARTS + 1) * N_HEADS), 1.0, 0.0)
    cbits = p1 + pltpu.roll(p2, N_HEADS, 1) + pltpu.roll(p3, 2 * N_HEADS, 1) + ones
    cbits_k = cbits
    if n_pad:
        cbits_k = jnp.where((row < n_pad) & (lane < N_HEADS), -NEG, cbits)
    cq = cbits.astype(BF16)
    ck = cbits_k.astype(BF16)
    sel_r = lax.broadcasted_iota(jnp.int32, (PAIR, PAIR), 0)
    sel_c = lax.broadcasted_iota(jnp.int32, (PAIR, PAIR), 1)
    vrow = lax.broadcasted_iota(jnp.int32, (PAIR, tr), 0)
    for h in range(N_HEADS):
        sl = slice((h // 2) * PAIR, (h // 2 + 1) * PAIR)
        xq = jnp.concatenate([q_ref[0, :, sl], cq], axis=1)
        xk = jnp.concatenate([k_ref[0, :, sl], ck], axis=1)
        qo_ref[0, h] = _dot(xq, pq_ref[h]).astype(BF16)
        ko_ref[0, h] = _dot(xk, pk_ref[h]).astype(BF16)
        own = (sel_r < HEAD_DIM) if h % 2 == 0 else (sel_r >= HEAD_DIM)
        pick = jnp.where((sel_r == sel_c) & own, 1.0, 0.0).astype(BF16)
        v_t = _dot_nt(pick, v_ref[0, :, sl])
        vo_ref[0, h, 0] = jnp.where(vrow == ONES_LANE[h % 2], 1.0, v_t).astype(BF16)


def _fox_prep(q, k, v, fl, bf_pad, c0, pq, pk, tr, n_pad):
    b, l, _ = q.shape
    wide = pl.BlockSpec((1, tr, D_BRANCH), lambda bi, i: (bi, i, 0))
    head_out = pl.BlockSpec((1, N_HEADS, tr, LANES), lambda bi, i: (bi, 0, i, 0))
    row128 = pl.BlockSpec((1, LANES), lambda bi, i: (0, 0))
    sel = pl.BlockSpec((N_HEADS, 2 * LANES, LANES), lambda bi, i: (0, 0, 0))
    return pl.pallas_call(
        functools.partial(_fox_prep_kernel, tr=tr, n_pad=n_pad),
        grid=(b, l // tr),
        in_specs=[wide, wide, wide,
                  pl.BlockSpec((1, tr, LANES), lambda bi, i: (bi, i, 0)),
                  row128, row128, sel, sel],
        out_specs=[head_out, head_out,
                   pl.BlockSpec((1, N_HEADS, 1, PAIR, tr), lambda bi, i: (bi, 0, i, 0, 0)),
                   pl.BlockSpec((1, 1, LANES), lambda bi, i: (bi, 0, 0))],
        out_shape=[jax.ShapeDtypeStruct((b, N_HEADS, l, LANES), BF16)] * 2
                  + [jax.ShapeDtypeStruct((b, N_HEADS, l // tr, PAIR, tr), BF16),
                     jax.ShapeDtypeStruct((b, 1, LANES), F32)],
        scratch_shapes=[pltpu.VMEM((1, LANES), F32)],
        compiler_params=_params(("parallel", "arbitrary")),
        name="fox_prep",
    )(q, k, v, fl, bf_pad, c0, pq, pk)


def _attn_kernel(q_ref, kpre_ref, vpre_ref, k_ref, v_ref, o_ref, s_ref, *, tq):
    qi = pl.program_id(2)
    heads = range(ATTN_HEADS)
    qs = [q_ref[0, h] for h in heads]

    def rowmax(s):
        return jnp.max(s, axis=0, keepdims=True)

    def k_chunk(h, idx):
        return k_ref[0, h, pl.ds(pl.multiple_of(idx * tq, tq), tq), :]

    def v_chunk(h, idx):
        return v_ref[0, h, idx]

    def scores(idx):
        return [_dot_nt(k_chunk(h, idx), qs[h]) for h in heads]

    def softmax_pv(carry, s, rmax, idx):
        out = []
        for h in heads:
            m, acc = carry[h]
            m_new = jnp.maximum(m, rmax[h])
            p = jnp.exp(s[h] - m_new).astype(BF16)
            out.append((m_new, jnp.exp(m - m_new) * acc + _dot(v_chunk(h, idx), p)))
        return out

    key = lax.broadcasted_iota(jnp.int32, (tq, tq), 0)
    qry = lax.broadcasted_iota(jnp.int32, (tq, tq), 1)
    s_pre = [_dot_nt(kpre_ref[0, h], qs[h]) for h in heads]
    s_dia = [jnp.where(key <= qry, s, NEG) for s in scores(qi)]
    s_nxt = scores(0)
    carry = []
    for h in heads:
        m0 = jnp.maximum(rowmax(s_pre[h]), rowmax(s_dia[h]))
        p_pre = jnp.exp(s_pre[h] - m0).astype(BF16)
        p_dia = jnp.exp(s_dia[h] - m0).astype(BF16)
        carry.append((m0, _dot(vpre_ref[0, h, 0], p_pre) + _dot(v_chunk(h, qi), p_dia)))
        s_ref[h] = s_nxt[h]
    rmax = [rowmax(s) for s in s_nxt]

    def step2(t, state):
        carry, rmax = state
        k0 = 2 * t
        s_a = [s_ref[h] for h in heads]
        s_b = scores(k0 + 1)
        carry = softmax_pv(carry, s_a, rmax, k0)
        rmax_b = [rowmax(s) for s in s_b]
        s_c = scores(k0 + 2)
        carry = softmax_pv(carry, s_b, rmax_b, k0 + 1)
        for h in heads:
            s_ref[h] = s_c[h]
        return carry, [rowmax(s) for s in s_c]

    def step1(k0, state):
        carry, rmax = state
        s_a = [s_ref[h] for h in heads]
        s_b = scores(k0 + 1)
        carry = softmax_pv(carry, s_a, rmax, k0)
        for h in heads:
            s_ref[h] = s_b[h]
        return carry, [rowmax(s) for s in s_b]

    steps = jnp.maximum(qi - 1, 0)
    pairs = steps // 2
    state = lax.fori_loop(0, pairs, step2, (carry, rmax))
    carry, rmax = lax.fori_loop(2 * pairs, steps, step1, state)

    some = qi > 0
    last = jnp.maximum(qi - 1, 0)
    s_fin = [jnp.where(some, s_ref[h], NEG) for h in heads]
    r_fin = [jnp.where(some, r, NEG) for r in rmax]
    accs = [acc for _, acc in softmax_pv(carry, s_fin, r_fin, last)]
    vrow = lax.broadcasted_iota(jnp.int32, (PAIR, tq), 0)
    for pr in range(ATTN_HEADS // 2):
        acc0, acc1 = accs[2 * pr], accs[2 * pr + 1]
        l0 = acc0[ONES_LANE[0]:ONES_LANE[0] + 1, :]
        l1 = acc1[ONES_LANE[1]:ONES_LANE[1] + 1, :]
        o_t = jnp.where(vrow < HEAD_DIM, acc0 / l0, acc1 / l1)
        o_ref[0, :, pr * PAIR:(pr + 1) * PAIR] = o_t.T


def _fox_attn(qp, kp, vp, kpre, vpre, tq):
    b, _, l, _ = qp.shape
    g = ATTN_HEADS
    assert vp.shape[-1] == tq and vpre.shape[-1] == PREFIX_ROWS
    pre = pl.BlockSpec((1, g, PREFIX_ROWS, LANES), lambda bi, p, qi: (0, p, 0, 0))
    pre_v = pl.BlockSpec((1, g, 1, PAIR, PREFIX_ROWS), lambda bi, p, qi: (0, p, 0, 0, 0))
    once = pl.Buffered(1)
    full = pl.BlockSpec((1, g, l, LANES), lambda bi, p, qi: (bi, p, 0, 0), pipeline_mode=once)
    full_v = pl.BlockSpec((1, g, l // tq, PAIR, tq), lambda bi, p, qi: (bi, p, 0, 0, 0), pipeline_mode=once)
    return pl.pallas_call(
        functools.partial(_attn_kernel, tq=tq),
        grid=(b, N_HEADS // g, l // tq),
        in_specs=[pl.BlockSpec((1, g, tq, LANES), lambda bi, p, qi: (bi, p, qi, 0)),
                  pre, pre_v, full, full_v],
        out_specs=pl.BlockSpec((1, tq, g * HEAD_DIM), lambda bi, p, qi: (bi, qi, p)),
        out_shape=jax.ShapeDtypeStruct((b, l, D_BRANCH), F32),
        scratch_shapes=[pltpu.VMEM((g, tq, tq), F32)],
        compiler_params=_params(("parallel", "parallel", "parallel")),
        name="fox_attn",
    )(qp, kpre, vpre, kp, vp)


def _rwkv_kernel(rt_ref, at_ref, bt_ref, kt_ref, xv_ref, bh_ref, kh_ref, bonus_ref, z_ref, wend_ref,
                 gnw_ref, gnb_ref, ones_ref, z0_ref,
                 y_ref, zf_ref, state_ref, yacc_ref, *, tr):
    t = pl.program_id(1)
    last = t == pl.num_programs(1) - 1

    @pl.when(t == 0)
    def _():
        state_ref[...] = z0_ref[...]

    ones_bd = ones_ref[...]

    def seg_sum(x):
        return _dot(x.astype(BF16), ones_bd)

    operand = dict(rt=rt_ref, at=at_ref, bt=bt_ref, kt=kt_ref, xv=xv_ref, bh=bh_ref, kh=kh_ref)

    lane = lax.broadcasted_iota(jnp.int32, (CHUNK, PAIR), 1)
    head_a = lane < HEAD_DIM
    tau_r = lax.broadcasted_iota(jnp.int32, (PAIR, PAIR), 0)
    tau_c = lax.broadcasted_iota(jnp.int32, (PAIR, PAIR), 1)
    strict = (tau_r % CHUNK) > (tau_c % CHUNK)
    incl = (tau_r % CHUNK) >= (tau_c % CHUNK)
    eye = tau_r == tau_c
    eye_f = jnp.where(eye, 1.0, 0.0).astype(F32)

    def stacked(name, c, p):
        x = operand[name][0, c * CHUNK:(c + 1) * CHUNK, p * PAIR:(p + 1) * PAIR]
        zero = jnp.zeros_like(x)
        return jnp.concatenate([jnp.where(head_a, x, zero), jnp.where(head_a, zero, x)], axis=0)

    n_chunks = tr // CHUNK
    n_sub = tr // SUB
    units = [(c, p) for c in range(n_chunks) for p in range(N_PAIRS)]
    wc = [wend_ref[0, c:c + 1, p * PAIR:(p + 1) * PAIR] for c, p in units]
    r_s = [stacked("rt", c, p) for c, p in units]
    rs_f = [x.astype(F32) for x in r_s]
    a_s = [stacked("at", c, p) for c, p in units]
    b_s = [stacked("bt", c, p) for c, p in units]
    k_s = [stacked("kt", c, p) for c, p in units]
    v_s = [stacked("xv", c, p) for c, p in units]
    bh_s = [stacked("bh", c, p) for c, p in units]
    kh_s = [stacked("kh", c, p) for c, p in units]
    nu = range(len(units))

    a_ab = [jnp.where(strict, _dot_nt(a_s[u], b_s[u]), 0.0) for u in nu]
    a_ak = [jnp.where(strict, _dot_nt(a_s[u], k_s[u]), 0.0).astype(BF16) for u in nu]
    a_rb = [jnp.where(incl, _dot_nt(r_s[u], b_s[u]), 0.0).astype(BF16) for u in nu]
    a_rk = [jnp.where(incl, _dot_nt(r_s[u], k_s[u]), 0.0).astype(BF16) for u in nu]

    pw = [_dot(a.astype(BF16), a.astype(BF16)) for a in a_ab]
    tinv = [eye_f + a_ab[u] for u in nu]
    for level in range(1, 6):
        pb = [x.astype(BF16) for x in pw]
        if level < 5:
            both = [_dot(jnp.concatenate([tinv[u].astype(BF16), pb[u]], axis=0), pb[u]) for u in nu]
            tinv = [tinv[u] + both[u][:PAIR] for u in nu]
            pw = [both[u][PAIR:] for u in nu]
        else:
            tinv = [tinv[u] + _dot(tinv[u].astype(BF16), pb[u]) for u in nu]
    tb = [x.astype(BF16) for x in tinv]

    ap = [_dot(tb[u], a_s[u]).astype(BF16) for u in nu]
    akv = [_dot(a_ak[u], v_s[u]).astype(BF16) for u in nu]
    vp = [_dot(tb[u], akv[u]).astype(BF16) for u in nu]
    m_mat = [(jnp.where(eye, wc[u], 0.0) + _dot_tn(bh_s[u], ap[u])).astype(BF16) for u in nu]
    rp = [(rs_f[u] + _dot(a_rb[u], ap[u])).astype(BF16) for u in nu]
    vv = [jnp.concatenate([vp[u], v_s[u]], axis=0) for u in nu]
    g_mat = [_dot_tn(jnp.concatenate([bh_s[u], kh_s[u]], axis=0), vv[u]) for u in nu]
    y0 = [_dot(jnp.concatenate([a_rb[u], a_rk[u]], axis=1), vv[u]) for u in nu]

    for u, (c, p) in enumerate(units):
        zb = state_ref[p].astype(BF16)
        ys = _dot(rp[u], zb) + y0[u]
        state_ref[p] = _dot(m_mat[u], zb) + g_mat[u]
        yacc_ref[c * CHUNK:(c + 1) * CHUNK, p * PAIR:(p + 1) * PAIR] = ys[:CHUNK] + ys[CHUNK:]

    @pl.when(last)
    def _():
        zf_ref[0] = state_ref[...]

    inv_n = 1.0 / HEAD_DIM
    for sb in range(n_sub):
        rows = slice(sb * SUB, (sb + 1) * SUB)
        y = yacc_ref[rows, :]
        mean = seg_sum(y) * inv_n
        d = y - mean
        var = seg_sum(d * d) * inv_n
        yn = d * lax.rsqrt(var + GN_EPS) * gnw_ref[...] + gnb_ref[...]
        z = z_ref[0, rows, :]
        y_ref[0, rows, :] = ((yn + bonus_ref[0, rows, :].astype(F32)) * (z * _sigmoid(z))).astype(y_ref.dtype)


def _rwkv(ops, z, wend, gn_w, gn_b, ones_bd, z0, tr):
    b, l, _ = z.shape
    wide = pl.BlockSpec((1, tr, D_BRANCH), lambda bi, t: (bi, t, 0))
    row_w = pl.BlockSpec((1, D_BRANCH), lambda bi, t: (0, 0))
    ones = pl.BlockSpec((D_BRANCH, D_BRANCH), lambda bi, t: (0, 0))
    st_in = pl.BlockSpec((N_PAIRS, PAIR, PAIR), lambda bi, t: (0, 0, 0))
    st_out = pl.BlockSpec((1, N_PAIRS, PAIR, PAIR), lambda bi, t: (bi, 0, 0, 0))
    return pl.pallas_call(
        functools.partial(_rwkv_kernel, tr=tr),
        grid=(b, l // tr),
        in_specs=[wide] * (len(RWKV_OPERANDS) + 1)
                 + [pl.BlockSpec((1, tr // CHUNK, D_BRANCH), lambda bi, t: (bi, t, 0)),
                    row_w, row_w, ones, st_in],
        out_specs=[wide, st_out],
        out_shape=[jax.ShapeDtypeStruct((b, l, D_BRANCH), BF16),
                   jax.ShapeDtypeStruct((b, N_PAIRS, PAIR, PAIR), F32)],
        scratch_shapes=[pltpu.VMEM((N_PAIRS, PAIR, PAIR), F32), pltpu.VMEM((tr, D_BRANCH), F32)],
        compiler_params=_params(("parallel", "arbitrary")),
        name="rwkv",
    )(*ops, z, wend, gn_w, gn_b, ones_bd, z0)


def _out_proj_kernel(o_ref, zf_ref, yr_ref, x_ref, w_ref, fnw_ref, out_ref):
    z = zf_ref[...]
    yf = (o_ref[...] * (z * _sigmoid(z))).astype(BF16)
    mix = _dot(yf, w_ref[:D_BRANCH, :]) + _dot(yr_ref[...], w_ref[D_BRANCH:, :])
    h = x_ref[...] + mix
    out_ref[...] = h * lax.rsqrt(jnp.mean(h * h, axis=-1, keepdims=True) + NORM_EPS) * fnw_ref[...]


def _out_proj(o, zf, yr, x, w_out, fnw, tm):
    n = x.shape[0]
    half = pl.BlockSpec((tm, D_BRANCH), lambda i: (i, 0))
    full = pl.BlockSpec((tm, D_MODEL), lambda i: (i, 0))
    return pl.pallas_call(
        _out_proj_kernel,
        grid=(n // tm,),
        in_specs=[half, half, half, full,
                  pl.BlockSpec((D_MODEL, D_MODEL), lambda i: (0, 0)),
                  pl.BlockSpec((1, D_MODEL), lambda i: (0, 0))],
        out_specs=full,
        out_shape=jax.ShapeDtypeStruct((n, D_MODEL), F32),
        compiler_params=_params(("parallel",)),
        name="out_proj",
    )(o, zf, yr, x, w_out, fnw)


def _tiles(b, l):
    rows = b * l
    tm = 512 if rows % 512 == 0 else 256
    return tm, 512, 512, 512


def kernel(x, meta, norm_w, w_in, b_f, mu_shift, w0, w_up, a0, a_up, k_k, k_a, r_k, gn_w, gn_b,
           w_out, final_norm_w):
    b, l, d = x.shape
    assert d == D_MODEL and norm_w.shape[0] == 1 and l % 256 == 0
    tm, t_prep, t_attn, t_rwkv = _tiles(b, l)

    wi = w_in[0]
    o = 0
    cols = {}
    for name, width in (("q", D_BRANCH), ("k", D_BRANCH), ("v", D_BRANCH), ("fl", N_HEADS), ("zf", D_BRANCH),
                        ("r", D_BRANCH), ("rk", D_BRANCH), ("rv", D_BRANCH), ("wd", RANK), ("ad", RANK),
                        ("zr", D_BRANCH)):
        cols[name] = wi[:, o:o + width]
        o += width
    w_all = jnp.concatenate(
        [cols[n] for n in ("q", "k", "v", "zf", "r", "rk", "rv", "zr", "wd", "ad", "fl")]
        + [jnp.zeros((D_MODEL, LANES - N_HEADS), F32)], axis=1).astype(BF16)

    row = lambda vec: vec.reshape(1, -1).astype(F32)
    mu = mu_shift[0]
    mu_r, mu_k, mu_v = (row(mu[i * D_BRANCH:(i + 1) * D_BRANCH]) for i in range(3))
    mu_wa = row(mu[3 * D_BRANCH:])
    zeros_up = jnp.zeros((RANK, D_BRANCH), F32)
    wup_pad = jnp.concatenate([w_up[0], zeros_up], axis=0)
    aup_pad = jnp.concatenate([zeros_up, a_up[0]], axis=0)
    hid = np.arange(D_BRANCH) // HEAD_DIM
    ones_bd = jnp.asarray(hid[:, None] == hid[None, :], BF16)
    shift_prm = (mu_r, mu_k, mu_v, mu_wa, row(w0[0]), wup_pad.astype(BF16), row(a0[0]), aup_pad.astype(BF16),
                 row(k_k[0]), row(k_a[0]), row(r_k[0]), ones_bd)
    gnw, gnb = row(gn_w[0]), row(gn_b[0])
    bf_pad = jnp.concatenate([b_f[0], jnp.zeros((LANES - N_HEADS,), F32)]).reshape(1, LANES)
    pq, pk = _select_matrices()
    nw = row(norm_w[0])
    n_ops = len(RWKV_OPERANDS)

    pre_rows = jnp.concatenate([jnp.zeros((PREFIX_ROWS - N_META, D_MODEL), F32), meta.astype(F32)], axis=0)
    zero_w = jnp.zeros((1, D_BRANCH), F32)
    pre = _in_proj(pre_rows, nw, w_all, shift_prm, (zero_w, zero_w, zero_w, jnp.zeros((1, LANES), F32)),
                   PREFIX_ROWS, 1)
    pq_, pk_, pv_, _, pzr_, pfl_ = pre[:6]
    pre_ops, pre_wend, last_raw = pre[6:6 + n_ops], pre[6 + n_ops], pre[7 + n_ops:]
    lead = lambda a: a[None]
    _, kpre, vpre, c_pre = _fox_prep(lead(pq_), lead(pk_), lead(pv_), lead(pfl_), bf_pad,
                                     jnp.zeros((1, LANES), F32), pq, pk, PREFIX_ROWS, PREFIX_ROWS - N_META)
    _, z_pre = _rwkv([lead(a) for a in pre_ops], lead(pzr_), lead(pre_wend), gnw, gnb, ones_bd,
                     jnp.zeros((N_PAIRS, PAIR, PAIR), F32), PREFIX_ROWS)

    xf = x.reshape(b * l, D_MODEL)
    main = _in_proj(xf, nw, w_all, shift_prm, last_raw, tm, l // tm)
    q_, k_, v_, zf_, zr_, fl_ = main[:6]
    bl = lambda a: a.reshape(b, l, a.shape[-1])
    qp, kp, vp, _ = _fox_prep(bl(q_), bl(k_), bl(v_), bl(fl_), bf_pad, c_pre[0], pq, pk, t_prep, 0)
    o_attn = _fox_attn(qp, kp, vp, kpre, vpre, t_attn)
    y_rwkv, _ = _rwkv([bl(a) for a in main[6:6 + n_ops]], bl(zr_),
                      main[6 + n_ops].reshape(b, l // CHUNK, D_BRANCH), gnw, gnb, ones_bd, z_pre[0], t_rwkv)
    out = _out_proj(o_attn.reshape(b * l, D_BRANCH), zf_, y_rwkv.reshape(b * l, D_BRANCH), xf,
                    w_out[0].astype(BF16), row(final_norm_w), tm)
    return out.reshape(b, l, D_MODEL)
```

```python
import functools

import numpy as np
import jax
import jax.numpy as jnp
from jax import lax
from jax.experimental import pallas as pl
from jax.experimental.pallas import tpu as pltpu

F32 = jnp.float32
BF16 = jnp.bfloat16

D_MODEL = 1024
N_META = 16
HEAD_DIM = 64
N_HEADS = 8
D_BRANCH = N_HEADS * HEAD_DIM
N_PAIRS = N_HEADS // 2
RANK = 64
NORM_EPS = 1e-6
GN_EPS = 64e-5
KK_EPS = 1e-12
NEG = -1e30

LANES = 128
PREFIX_ROWS = 128
CHUNK = 64
SUB = 2 * CHUNK
PAIR = 2 * HEAD_DIM
VMEM_LIMIT = 56 * 1024 * 1024

N_WIDE = 8
COL_WA = N_WIDE * D_BRANCH
COL_FL = COL_WA + LANES
N_COLS = COL_FL + LANES

N_CPARTS = 3
ONES_LANE = (HEAD_DIM, 0)
ATTN_HEADS = 4


def _dot(a, b):
    return jnp.dot(a, b, preferred_element_type=F32)


def _pieces(x, n):
    out = []
    for _ in range(n - 1):
        p = x.astype(BF16)
        out.append(p)
        x = x - p.astype(F32)
    out.append(x.astype(BF16))
    return out


def _dot_exact_lhs(a_bf16, x, n):
    acc = None
    for p in _pieces(x, n):
        t = _dot(a_bf16, p)
        acc = t if acc is None else acc + t
    return acc


def _dot_x2(x, w_bf16):
    x_hi, x_lo = _pieces(x, 2)
    return _dot(x_hi, w_bf16) + _dot(x_lo, w_bf16)


def _dot_nt(a, b):
    return lax.dot_general(a, b, (((1,), (1,)), ((), ())), preferred_element_type=F32)


def _dot_tn(a, b):
    return lax.dot_general(a, b, (((0,), (0,)), ((), ())), preferred_element_type=F32)


def _softplus(x):
    return jnp.maximum(x, 0.0) + jnp.log(1.0 + jnp.exp(-jnp.abs(x)))


def _sigmoid(x):
    return 1.0 / (1.0 + jnp.exp(-x))


def _params(sem):
    return pltpu.CompilerParams(dimension_semantics=sem, vmem_limit_bytes=VMEM_LIMIT)


G_Q, G_K, G_V, G_ZF, G_R, G_RK, G_RV, G_ZR = range(N_WIDE)
RWKV_OPERANDS = ("rt", "at", "bt", "kt", "xv", "bh", "kh", "bonus")


def _in_proj_kernel(x_ref, nw_ref, w_ref,
                    mu_r_ref, mu_k_ref, mu_v_ref, mu_wa_ref, w0_ref, wup_ref,
                    a0_ref, aup_ref, kk_ref, ka_ref, rk_ref, ones_ref,
                    pr_ref, pk_ref, pv_ref, pwa_ref,
                    q_ref, k_ref, v_ref, zf_ref, zr_ref, fl_ref,
                    rt_ref, at_ref, bt_ref, kt_ref, xv_ref, bh_ref, kh_ref, bonus_ref, wend_ref,
                    lr_ref, lk_ref, lv_ref, lwa_ref,
                    sr_ref, sk_ref, sv_ref, swa_ref, *, tm, tiles_per_seq):
    i = pl.program_id(0)

    @pl.when(lax.rem(i, tiles_per_seq) == 0)
    def _():
        sr_ref[...] = pr_ref[...]
        sk_ref[...] = pk_ref[...]
        sv_ref[...] = pv_ref[...]
        swa_ref[...] = pwa_ref[...]

    x = x_ref[...]
    u = x * lax.rsqrt(jnp.mean(x * x, axis=-1, keepdims=True) + NORM_EPS) * nw_ref[...]
    ub = u.astype(BF16)

    def group(g):
        return _dot(ub, w_ref[:, g * D_BRANCH:(g + 1) * D_BRANCH])

    raw = [group(G_R), group(G_RK), group(G_RV), _dot(ub, w_ref[:, COL_WA:COL_WA + LANES])]
    ones_bd = ones_ref[...]

    def seg_sum(v):
        return _dot(v.astype(BF16), ones_bd)

    def shifted(cur, prev_row, mu_ref):
        first = lax.broadcasted_iota(jnp.int32, cur.shape, 0) == 0
        prev = jnp.where(first, prev_row, pltpu.roll(cur, 1, 0))
        return cur + mu_ref[...] * (prev - cur)

    r2 = lax.broadcasted_iota(jnp.int32, (SUB, SUB), 0)
    c2 = lax.broadcasted_iota(jnp.int32, (SUB, SUB), 1)
    tri = jnp.where((r2 >= c2) & (r2 // CHUNK == c2 // CHUNK), 1.0, 0.0).astype(BF16)
    outs = dict(zip(RWKV_OPERANDS, (rt_ref, at_ref, bt_ref, kt_ref, xv_ref, bh_ref, kh_ref, bonus_ref)))

    def rwkv_prepare():
        prev_rows = [sr_ref[...], sk_ref[...], sv_ref[...], swa_ref[...]]
        mus = (mu_r_ref, mu_k_ref, mu_v_ref, mu_wa_ref)
        part = []
        for sb in range(tm // SUB):
            rows = slice(sb * SUB, (sb + 1) * SUB)
            cur = [a[rows, :] for a in raw]
            xr, xk, xv, xwa = [shifted(c, p, mu) for c, p, mu in zip(cur, prev_rows, mus)]
            prev_rows = [c[SUB - 1:SUB, :] for c in cur]
            w_lin = w0_ref[...] + _dot_x2(jnp.tanh(xwa), wup_ref[...])
            a_lin = a0_ref[...] + _dot(xwa.astype(BF16), aup_ref[...])
            kk = xk * kk_ref[...]
            part.append((sb, xr, xk, xv, w_lin, a_lin, kk, seg_sum(kk * kk)))
        yield prev_rows
        part2 = []
        for sb, xr, xk, xv, w_lin, a_lin, kk, kk_ss in part:
            w = -_softplus(-w_lin) - 0.5
            ld = -jnp.exp(w)
            a = _sigmoid(a_lin)
            kk = kk * lax.rsqrt(kk_ss + KK_EPS)
            kmod = xk * (1.0 + (a - 1.0) * ka_ref[...])
            lw = _dot_exact_lhs(tri, ld, 2)
            part2.append((sb, xr, xv, kmod, ld, a, kk, lw, seg_sum(xr * kmod * rk_ref[...])))
        yield None
        for sb, xr, xv, kmod, ld, a, kk, lw, rk_sum in part2:
            rows = slice(sb * SUB, (sb + 1) * SUB)
            w_inv = jnp.exp(-lw)
            bt = kk * a * w_inv
            kt = kmod * w_inv
            for name, val in (("rt", xr * jnp.exp(lw)), ("at", -kk * jnp.exp(lw - ld)), ("bt", bt), ("kt", kt),
                              ("xv", xv), ("bonus", rk_sum * xv)):
                outs[name][rows, :] = val.astype(BF16)
            for c in range(SUB // CHUNK):
                crow = sb * SUB + c * CHUNK
                w_c = jnp.exp(lw[(c + 1) * CHUNK - 1:(c + 1) * CHUNK, :])
                wend_ref[crow // CHUNK:crow // CHUNK + 1, :] = w_c
                bh_ref[crow:crow + CHUNK, :] = (bt[c * CHUNK:(c + 1) * CHUNK, :] * w_c).astype(BF16)
                kh_ref[crow:crow + CHUNK, :] = (kt[c * CHUNK:(c + 1) * CHUNK, :] * w_c).astype(BF16)
        yield None

    parts = rwkv_prepare()
    last_rows = next(parts)
    q_ref[...] = group(G_Q).astype(BF16)
    k_ref[...] = group(G_K).astype(BF16)
    next(parts)
    v_ref[...] = group(G_V).astype(BF16)
    zf_ref[...] = group(G_ZF)
    next(parts)
    zr_ref[...] = group(G_ZR)
    fl_ref[...] = _dot(ub, w_ref[:, COL_FL:COL_FL + LANES])

    for ref, rowv in zip((sr_ref, sk_ref, sv_ref, swa_ref), last_rows):
        ref[...] = rowv

    @pl.when(i == pl.num_programs(0) - 1)
    def _():
        for out, rowv in zip((lr_ref, lk_ref, lv_ref, lwa_ref), last_rows):
            out[...] = rowv


def _in_proj(rows, norm_w, w_all, prm, prev, tm, tiles_per_seq):
    n = rows.shape[0]
    tile = lambda width: pl.BlockSpec((tm, width), lambda i: (i, 0))
    const = lambda shape: pl.BlockSpec(shape, lambda i: (0,) * len(shape))
    row_w, row_n, up = const((1, D_BRANCH)), const((1, LANES)), const((LANES, D_BRANCH))
    wide = lambda dt: jax.ShapeDtypeStruct((n, D_BRANCH), dt)
    return pl.pallas_call(
        functools.partial(_in_proj_kernel, tm=tm, tiles_per_seq=tiles_per_seq),
        grid=(n // tm,),
        in_specs=[tile(D_MODEL), const((1, D_MODEL)),
                  pl.BlockSpec((D_MODEL, N_COLS), lambda i: (0, 0), pipeline_mode=pl.Buffered(1)),
                  row_w, row_w, row_w, row_n, row_w, up, row_w, up,
                  row_w, row_w, row_w, const((D_BRANCH, D_BRANCH)),
                  row_w, row_w, row_w, row_n],
        out_specs=[tile(D_BRANCH)] * 5 + [tile(LANES)] + [tile(D_BRANCH)] * len(RWKV_OPERANDS)
                  + [pl.BlockSpec((tm // CHUNK, D_BRANCH), lambda i: (i, 0)),
                     row_w, row_w, row_w, row_n],
        out_shape=[wide(BF16), wide(BF16), wide(BF16), wide(F32), wide(F32),
                   jax.ShapeDtypeStruct((n, LANES), F32)]
                  + [wide(BF16)] * len(RWKV_OPERANDS)
                  + [jax.ShapeDtypeStruct((n // CHUNK, D_BRANCH), F32)]
                  + [jax.ShapeDtypeStruct((1, D_BRANCH), F32)] * 3 + [jax.ShapeDtypeStruct((1, LANES), F32)],
        scratch_shapes=[pltpu.VMEM((1, D_BRANCH), F32), pltpu.VMEM((1, D_BRANCH), F32),
                        pltpu.VMEM((1, D_BRANCH), F32), pltpu.VMEM((1, LANES), F32)],
        compiler_params=_params(("arbitrary",)),
        name="in_proj",
    )(rows, norm_w, w_all, *prm, *prev)


def _select_matrices():
    pq = np.zeros((N_HEADS, 2 * LANES, LANES), np.float32)
    pk = np.zeros((N_HEADS, 2 * LANES, LANES), np.float32)
    for h in range(N_HEADS):
        base = (h % 2) * HEAD_DIM
        for d in range(HEAD_DIM):
            pq[h, base + d, d] = 0.125
            pk[h, base + d, d] = 1.0
        for part in range(N_CPARTS):
            pq[h, LANES + part * N_HEADS + h, HEAD_DIM + part] = 1.0
            pq[h, LANES + N_CPARTS * N_HEADS + h, HEAD_DIM + N_CPARTS + part] = 1.0
            pk[h, LANES + N_CPARTS * N_HEADS + h, HEAD_DIM + part] = 1.0
            pk[h, LANES + part * N_HEADS + h, HEAD_DIM + N_CPARTS + part] = -1.0
    return jnp.asarray(pq, BF16), jnp.asarray(pk, BF16)


def _fox_prep_kernel(q_ref, k_ref, v_ref, fl_ref, bf_ref, c0_ref, pq_ref, pk_ref,
                     qo_ref, ko_ref, vo_ref, cl_ref, carry_ref, *, tr, n_pad):
    i = pl.program_id(1)

    @pl.when(i == 0)
    def _():
        carry_ref[...] = c0_ref[...]

    x = fl_ref[0] + bf_ref[...]
    logf = jnp.minimum(x, 0.0) - jnp.log(1.0 + jnp.exp(-jnp.abs(x)))
    lane = lax.broadcasted_iota(jnp.int32, (tr, LANES), 1)
    row = lax.broadcasted_iota(jnp.int32, (tr, LANES), 0) + i * tr
    valid = lane < N_HEADS
    if n_pad:
        valid = valid & (row >= n_pad)
    logf = jnp.where(valid, logf, 0.0)
    r2 = lax.broadcasted_iota(jnp.int32, (tr, tr), 0)
    c2 = lax.broadcasted_iota(jnp.int32, (tr, tr), 1)
    tri = jnp.where(r2 >= c2, 1.0, 0.0).astype(BF16)
    cum = _dot_exact_lhs(tri, logf, 3) + carry_ref[...]
    carry_ref[...] = cum[tr - 1:tr, :]

    @pl.when(i == pl.num_programs(1) - 1)
    def _():
        cl_ref[0] = cum[tr - 1:tr, :]

    p1 = cum.astype(BF16).astype(F32)
    rem = cum - p1
    p2 = rem.astype(BF16).astype(F32)
    p3 = (rem - p2).astype(BF16).astype(F32)
    ones = jnp.where((lane >= N_CPARTS * N_HEADS) & (lane < (N_CPARTS + 1) * N_HEADS), 1.0, 0.0)
    cbits = p1 + pltpu.roll(p2, N_HEADS, 1) + pltpu.roll(p3, 2 * N_HEADS, 1) + ones
    cbits_k = cbits
    if n_pad:
        cbits_k = jnp.where((row < n_pad) & (lane < N_HEADS), -NEG, cbits)
    cq = cbits.astype(BF16)
    ck = cbits_k.astype(BF16)
    sel_r = lax.broadcasted_iota(jnp.int32, (PAIR, PAIR), 0)
    sel_c = lax.broadcasted_iota(jnp.int32, (PAIR, PAIR), 1)
    vrow = lax.broadcasted_iota(jnp.int32, (PAIR, tr), 0)
    for h in range(N_HEADS):
        sl = slice((h // 2) * PAIR, (h // 2 + 1) * PAIR)
        xq = jnp.concatenate([q_ref[0, :, sl], cq], axis=1)
        xk = jnp.concatenate([k_ref[0, :, sl], ck], axis=1)
        qo_ref[0, h] = _dot(xq, pq_ref[h]).astype(BF16)
        ko_ref[0, h] = _dot(xk, pk_ref[h]).astype(BF16)
        own = (sel_r < HEAD_DIM) if h % 2 == 0 else (sel_r >= HEAD_DIM)
        pick = jnp.where((sel_r == sel_c) & own, 1.0, 0.0).astype(BF16)
        v_t = _dot_nt(pick, v_ref[0, :, sl])
        vo_ref[0, h, 0] = jnp.where(vrow == ONES_LANE[h % 2], 1.0, v_t).astype(BF16)


def _fox_prep(q, k, v, fl, bf_pad, c0, pq, pk, tr, n_pad):
    b, l, _ = q.shape
    wide = pl.BlockSpec((1, tr, D_BRANCH), lambda bi, i: (bi, i, 0))
    head_out = pl.BlockSpec((1, N_HEADS, tr, LANES), lambda bi, i: (bi, 0, i, 0))
    row128 = pl.BlockSpec((1, LANES), lambda bi, i: (0, 0))
    sel = pl.BlockSpec((N_HEADS, 2 * LANES, LANES), lambda bi, i: (0, 0, 0))
    return pl.pallas_call(
        functools.partial(_fox_prep_kernel, tr=tr, n_pad=n_pad),
        grid=(b, l // tr),
        in_specs=[wide, wide, wide,
                  pl.BlockSpec((1, tr, LANES), lambda bi, i: (bi, i, 0)),
                  row128, row128, sel, sel],
        out_specs=[head_out, head_out,
                   pl.BlockSpec((1, N_HEADS, 1, PAIR, tr), lambda bi, i: (bi, 0, i, 0, 0)),
                   pl.BlockSpec((1, 1, LANES), lambda bi, i: (bi, 0, 0))],
        out_shape=[jax.ShapeDtypeStruct((b, N_HEADS, l, LANES), BF16)] * 2
                  + [jax.ShapeDtypeStruct((b, N_HEADS, l // tr, PAIR, tr), BF16),
                     jax.ShapeDtypeStruct((b, 1, LANES), F32)],
        scratch_shapes=[pltpu.VMEM((1, LANES), F32)],
        compiler_params=_params(("parallel", "arbitrary")),
        name="fox_prep",
    )(q, k, v, fl, bf_pad, c0, pq, pk)


def _attn_kernel(q_ref, kpre_ref, vpre_ref, k_ref, v_ref, o_ref, s_ref, *, tq):
    qi = pl.program_id(2)
    heads = range(ATTN_HEADS)
    qs = [q_ref[0, h] for h in heads]

    def rowmax(s):
        return jnp.max(s, axis=0, keepdims=True)

    def k_chunk(h, idx):
        return k_ref[0, h, pl.ds(pl.multiple_of(idx * tq, tq), tq), :]

    def v_chunk(h, idx):
        return v_ref[0, h, idx]

    def scores(idx):
        return [_dot_nt(k_chunk(h, idx), qs[h]) for h in heads]

    def softmax_pv(carry, s, rmax, idx):
        out = []
        for h in heads:
            m, acc = carry[h]
            m_new = jnp.maximum(m, rmax[h])
            p = jnp.exp(s[h] - m_new).astype(BF16)
            out.append((m_new, jnp.exp(m - m_new) * acc + _dot(v_chunk(h, idx), p)))
        return out

    key = lax.broadcasted_iota(jnp.int32, (tq, tq), 0)
    qry = lax.broadcasted_iota(jnp.int32, (tq, tq), 1)
    s_pre = [_dot_nt(kpre_ref[0, h], qs[h]) for h in heads]
    s_dia = [jnp.where(key <= qry, s, NEG) for s in scores(qi)]
    s_nxt = scores(0)
    carry = []
    for h in heads:
        m0 = jnp.maximum(rowmax(s_pre[h]), rowmax(s_dia[h]))
        p_pre = jnp.exp(s_pre[h] - m0).astype(BF16)
        p_dia = jnp.exp(s_dia[h] - m0).astype(BF16)
        carry.append((m0, _dot(vpre_ref[0, h, 0], p_pre) + _dot(v_chunk(h, qi), p_dia)))
        s_ref[h] = s_nxt[h]
    rmax = [rowmax(s) for s in s_nxt]

    def step2(t, state):
        carry, rmax = state
        k0 = 2 * t
        s_a = [s_ref[h] for h in heads]
        s_b = scores(k0 + 1)
        carry = softmax_pv(carry, s_a, rmax, k0)
        rmax_b = [rowmax(s) for s in s_b]
        s_c = scores(k0 + 2)
        carry = softmax_pv(carry, s_b, rmax_b, k0 + 1)
        for h in heads:
            s_ref[h] = s_c[h]
        return carry, [rowmax(s) for s in s_c]

    def step1(k0, state):
        carry, rmax = state
        s_a = [s_ref[h] for h in heads]
        s_b = scores(k0 + 1)
        carry = softmax_pv(carry, s_a, rmax, k0)
        for h in heads:
            s_ref[h] = s_b[h]
        return carry, [rowmax(s) for s in s_b]

    steps = jnp.maximum(qi - 1, 0)
    pairs = steps // 2
    state = lax.fori_loop(0, pairs, step2, (carry, rmax))
    carry, rmax = lax.fori_loop(2 * pairs, steps, step1, state)

    some = qi > 0
    last = jnp.maximum(qi - 1, 0)
    s_fin = [jnp.where(some, s_ref[h], NEG) for h in heads]
    r_fin = [jnp.where(some, r, NEG) for r in rmax]
    accs = [acc for _, acc in softmax_pv(carry, s_fin, r_fin, last)]
    vrow = lax.broadcasted_iota(jnp.int32, (PAIR, tq), 0)
    for pr in range(ATTN_HEADS // 2):
        acc0, acc1 = accs[2 * pr], accs[2 * pr + 1]
        l0 = acc0[ONES_LANE[0]:ONES_LANE[0] + 1, :]
        l1 = acc1[ONES_LANE[1]:ONES_LANE[1] + 1, :]
        o_t = jnp.where(vrow < HEAD_DIM, acc0 / l0, acc1 / l1)
        o_ref[0, :, pr * PAIR:(pr + 1) * PAIR] = o_t.T


def _fox_attn(qp, kp, vp, kpre, vpre, tq):
    b, _, l, _ = qp.shape
    g = ATTN_HEADS
    assert vp.shape[-1] == tq and vpre.shape[-1] == PREFIX_ROWS
    pre = pl.BlockSpec((1, g, PREFIX_ROWS, LANES), lambda bi, p, qi: (0, p, 0, 0))
    pre_v = pl.BlockSpec((1, g, 1, PAIR, PREFIX_ROWS), lambda bi, p, qi: (0, p, 0, 0, 0))
    full = pl.BlockSpec((1, g, l, LANES), lambda bi, p, qi: (bi, p, 0, 0))
    full_v = pl.BlockSpec((1, g, l // tq, PAIR, tq), lambda bi, p, qi: (bi, p, 0, 0, 0))
    return pl.pallas_call(
        functools.partial(_attn_kernel, tq=tq),
        grid=(b, N_HEADS // g, l // tq),
        in_specs=[pl.BlockSpec((1, g, tq, LANES), lambda bi, p, qi: (bi, p, qi, 0)),
                  pre, pre_v, full, full_v],
        out_specs=pl.BlockSpec((1, tq, g * HEAD_DIM), lambda bi, p, qi: (bi, qi, p)),
        out_shape=jax.ShapeDtypeStruct((b, l, D_BRANCH), F32),
        scratch_shapes=[pltpu.VMEM((g, tq, tq), F32)],
        compiler_params=_params(("parallel", "parallel", "parallel")),
        name="fox_attn",
    )(qp, kpre, vpre, kp, vp)


def _rwkv_kernel(rt_ref, at_ref, bt_ref, kt_ref, xv_ref, bh_ref, kh_ref, bonus_ref, z_ref, wend_ref,
                 gnw_ref, gnb_ref, ones_ref, z0_ref,
                 y_ref, zf_ref, state_ref, yacc_ref, *, tr):
    t = pl.program_id(1)
    last = t == pl.num_programs(1) - 1

    @pl.when(t == 0)
    def _():
        state_ref[...] = z0_ref[...]

    ones_bd = ones_ref[...]

    def seg_sum(x):
        return _dot(x.astype(BF16), ones_bd)

    operand = dict(rt=rt_ref, at=at_ref, bt=bt_ref, kt=kt_ref, xv=xv_ref, bh=bh_ref, kh=kh_ref)

    lane = lax.broadcasted_iota(jnp.int32, (CHUNK, PAIR), 1)
    head_a = lane < HEAD_DIM
    tau_r = lax.broadcasted_iota(jnp.int32, (PAIR, PAIR), 0)
    tau_c = lax.broadcasted_iota(jnp.int32, (PAIR, PAIR), 1)
    strict = (tau_r % CHUNK) > (tau_c % CHUNK)
    incl = (tau_r % CHUNK) >= (tau_c % CHUNK)
    eye = tau_r == tau_c
    eye_f = jnp.where(eye, 1.0, 0.0).astype(F32)

    def stacked(name, c, p):
        x = operand[name][0, c * CHUNK:(c + 1) * CHUNK, p * PAIR:(p + 1) * PAIR]
        zero = jnp.zeros_like(x)
        return jnp.concatenate([jnp.where(head_a, x, zero), jnp.where(head_a, zero, x)], axis=0)

    n_chunks = tr // CHUNK
    n_sub = tr // SUB
    units = [(c, p) for c in range(n_chunks) for p in range(N_PAIRS)]
    wc = [wend_ref[0, c:c + 1, p * PAIR:(p + 1) * PAIR] for c, p in units]
    r_s = [stacked("rt", c, p) for c, p in units]
    rs_f = [x.astype(F32) for x in r_s]
    a_s = [stacked("at", c, p) for c, p in units]
    b_s = [stacked("bt", c, p) for c, p in units]
    k_s = [stacked("kt", c, p) for c, p in units]
    v_s = [stacked("xv", c, p) for c, p in units]
    bh_s = [stacked("bh", c, p) for c, p in units]
    kh_s = [stacked("kh", c, p) for c, p in units]
    nu = range(len(units))

    a_ab = [jnp.where(strict, _dot_nt(a_s[u], b_s[u]), 0.0) for u in nu]
    a_ak = [jnp.where(strict, _dot_nt(a_s[u], k_s[u]), 0.0).astype(BF16) for u in nu]
    a_rb = [jnp.where(incl, _dot_nt(r_s[u], b_s[u]), 0.0).astype(BF16) for u in nu]
    a_rk = [jnp.where(incl, _dot_nt(r_s[u], k_s[u]), 0.0).astype(BF16) for u in nu]

    pw = [_dot(a.astype(BF16), a.astype(BF16)) for a in a_ab]
    tinv = [eye_f + a_ab[u] for u in nu]
    for level in range(1, 6):
        pb = [x.astype(BF16) for x in pw]
        if level < 5:
            both = [_dot(jnp.concatenate([tinv[u].astype(BF16), pb[u]], axis=0), pb[u]) for u in nu]
            tinv = [tinv[u] + both[u][:PAIR] for u in nu]
            pw = [both[u][PAIR:] for u in nu]
        else:
            tinv = [tinv[u] + _dot(tinv[u].astype(BF16), pb[u]) for u in nu]
    tb = [x.astype(BF16) for x in tinv]

    ap = [_dot(tb[u], a_s[u]).astype(BF16) for u in nu]
    akv = [_dot(a_ak[u], v_s[u]).astype(BF16) for u in nu]
    vp = [_dot(tb[u], akv[u]).astype(BF16) for u in nu]
    m_mat = [(jnp.where(eye, wc[u], 0.0) + _dot_tn(bh_s[u], ap[u])).astype(BF16) for u in nu]
    rp = [(rs_f[u] + _dot(a_rb[u], ap[u])).astype(BF16) for u in nu]
    vv = [jnp.concatenate([vp[u], v_s[u]], axis=0) for u in nu]
    g_mat = [_dot_tn(jnp.concatenate([bh_s[u], kh_s[u]], axis=0), vv[u]) for u in nu]
    y0 = [_dot(jnp.concatenate([a_rb[u], a_rk[u]], axis=1), vv[u]) for u in nu]

    for u, (c, p) in enumerate(units):
        zb = state_ref[p].astype(BF16)
        ys = _dot(rp[u], zb) + y0[u]
        state_ref[p] = _dot(m_mat[u], zb) + g_mat[u]
        yacc_ref[c * CHUNK:(c + 1) * CHUNK, p * PAIR:(p + 1) * PAIR] = ys[:CHUNK] + ys[CHUNK:]

    @pl.when(last)
    def _():
        zf_ref[0] = state_ref[...]

    inv_n = 1.0 / HEAD_DIM
    for sb in range(n_sub):
        rows = slice(sb * SUB, (sb + 1) * SUB)
        y = yacc_ref[rows, :]
        mean = seg_sum(y) * inv_n
        d = y - mean
        var = seg_sum(d * d) * inv_n
        yn = d * lax.rsqrt(var + GN_EPS) * gnw_ref[...] + gnb_ref[...]
        z = z_ref[0, rows, :]
        y_ref[0, rows, :] = ((yn + bonus_ref[0, rows, :].astype(F32)) * (z * _sigmoid(z))).astype(y_ref.dtype)


def _rwkv(ops, z, wend, gn_w, gn_b, ones_bd, z0, tr):
    b, l, _ = z.shape
    wide = pl.BlockSpec((1, tr, D_BRANCH), lambda bi, t: (bi, t, 0))
    row_w = pl.BlockSpec((1, D_BRANCH), lambda bi, t: (0, 0))
    ones = pl.BlockSpec((D_BRANCH, D_BRANCH), lambda bi, t: (0, 0))
    st_in = pl.BlockSpec((N_PAIRS, PAIR, PAIR), lambda bi, t: (0, 0, 0))
    st_out = pl.BlockSpec((1, N_PAIRS, PAIR, PAIR), lambda bi, t: (bi, 0, 0, 0))
    return pl.pallas_call(
        functools.partial(_rwkv_kernel, tr=tr),
        grid=(b, l // tr),
        in_specs=[wide] * (len(RWKV_OPERANDS) + 1)
                 + [pl.BlockSpec((1, tr // CHUNK, D_BRANCH), lambda bi, t: (bi, t, 0)),
                    row_w, row_w, ones, st_in],
        out_specs=[wide, st_out],
        out_shape=[jax.ShapeDtypeStruct((b, l, D_BRANCH), BF16),
                   jax.ShapeDtypeStruct((b, N_PAIRS, PAIR, PAIR), F32)],
        scratch_shapes=[pltpu.VMEM((N_PAIRS, PAIR, PAIR), F32), pltpu.VMEM((tr, D_BRANCH), F32)],
        compiler_params=_params(("parallel", "arbitrary")),
        name="rwkv",
    )(*ops, z, wend, gn_w, gn_b, ones_bd, z0)


def _out_proj_kernel(o_ref, zf_ref, yr_ref, x_ref, w_ref, fnw_ref, out_ref):
    z = zf_ref[...]
    yf = (o_ref[...] * (z * _sigmoid(z))).astype(BF16)
    mix = _dot(yf, w_ref[:D_BRANCH, :]) + _dot(yr_ref[...], w_ref[D_BRANCH:, :])
    h = x_ref[...] + mix
    out_ref[...] = h * lax.rsqrt(jnp.mean(h * h, axis=-1, keepdims=True) + NORM_EPS) * fnw_ref[...]


def _out_proj(o, zf, yr, x, w_out, fnw, tm):
    n = x.shape[0]
    half = pl.BlockSpec((tm, D_BRANCH), lambda i: (i, 0))
    full = pl.BlockSpec((tm, D_MODEL), lambda i: (i, 0))
    return pl.pallas_call(
        _out_proj_kernel,
        grid=(n // tm,),
        in_specs=[half, half, half, full,
                  pl.BlockSpec((D_MODEL, D_MODEL), lambda i: (0, 0)),
                  pl.BlockSpec((1, D_MODEL), lambda i: (0, 0))],
        out_specs=full,
        out_shape=jax.ShapeDtypeStruct((n, D_MODEL), F32),
        compiler_params=_params(("parallel",)),
        name="out_proj",
    )(o, zf, yr, x, w_out, fnw)


def _tiles(b, l):
    rows = b * l
    tm = 512 if rows % 512 == 0 else 256
    return tm, 512, 512, 512


def kernel(x, meta, norm_w, w_in, b_f, mu_shift, w0, w_up, a0, a_up, k_k, k_a, r_k, gn_w, gn_b,
           w_out, final_norm_w):
    b, l, d = x.shape
    assert d == D_MODEL and norm_w.shape[0] == 1 and l % 256 == 0
    tm, t_prep, t_attn, t_rwkv = _tiles(b, l)

    wi = w_in[0]
    o = 0
    cols = {}
    for name, width in (("q", D_BRANCH), ("k", D_BRANCH), ("v", D_BRANCH), ("fl", N_HEADS), ("zf", D_BRANCH),
                        ("r", D_BRANCH), ("rk", D_BRANCH), ("rv", D_BRANCH), ("wd", RANK), ("ad", RANK),
                        ("zr", D_BRANCH)):
        cols[name] = wi[:, o:o + width]
        o += width
    w_all = jnp.concatenate(
        [cols[n] for n in ("q", "k", "v", "zf", "r", "rk", "rv", "zr", "wd", "ad", "fl")]
        + [jnp.zeros((D_MODEL, LANES - N_HEADS), F32)], axis=1).astype(BF16)

    row = lambda vec: vec.reshape(1, -1).astype(F32)
    mu = mu_shift[0]
    mu_r, mu_k, mu_v = (row(mu[i * D_BRANCH:(i + 1) * D_BRANCH]) for i in range(3))
    mu_wa = row(mu[3 * D_BRANCH:])
    zeros_up = jnp.zeros((RANK, D_BRANCH), F32)
    wup_pad = jnp.concatenate([w_up[0], zeros_up], axis=0)
    aup_pad = jnp.concatenate([zeros_up, a_up[0]], axis=0)
    hid = np.arange(D_BRANCH) // HEAD_DIM
    ones_bd = jnp.asarray(hid[:, None] == hid[None, :], BF16)
    shift_prm = (mu_r, mu_k, mu_v, mu_wa, row(w0[0]), wup_pad.astype(BF16), row(a0[0]), aup_pad.astype(BF16),
                 row(k_k[0]), row(k_a[0]), row(r_k[0]), ones_bd)
    gnw, gnb = row(gn_w[0]), row(gn_b[0])
    bf_pad = jnp.concatenate([b_f[0], jnp.zeros((LANES - N_HEADS,), F32)]).reshape(1, LANES)
    pq, pk = _select_matrices()
    nw = row(norm_w[0])
    n_ops = len(RWKV_OPERANDS)

    pre_rows = jnp.concatenate([jnp.zeros((PREFIX_ROWS - N_META, D_MODEL), F32), meta.astype(F32)], axis=0)
    zero_w = jnp.zeros((1, D_BRANCH), F32)
    pre = _in_proj(pre_rows, nw, w_all, shift_prm, (zero_w, zero_w, zero_w, jnp.zeros((1, LANES), F32)),
                   PREFIX_ROWS, 1)
    pq_, pk_, pv_, _, pzr_, pfl_ = pre[:6]
    pre_ops, pre_wend, last_raw = pre[6:6 + n_ops], pre[6 + n_ops], pre[7 + n_ops:]
    lead = lambda a: a[None]
    _, kpre, vpre, c_pre = _fox_prep(lead(pq_), lead(pk_), lead(pv_), lead(pfl_), bf_pad,
                                     jnp.zeros((1, LANES), F32), pq, pk, PREFIX_ROWS, PREFIX_ROWS - N_META)
    _, z_pre = _rwkv([lead(a) for a in pre_ops], lead(pzr_), lead(pre_wend), gnw, gnb, ones_bd,
                     jnp.zeros((N_PAIRS, PAIR, PAIR), F32), PREFIX_ROWS)

    xf = x.reshape(b * l, D_MODEL)
    main = _in_proj(xf, nw, w_all, shift_prm, last_raw, tm, l // tm)
    q_, k_, v_, zf_, zr_, fl_ = main[:6]
    bl = lambda a: a.reshape(b, l, a.shape[-1])
    qp, kp, vp, _ = _fox_prep(bl(q_), bl(k_), bl(v_), bl(fl_), bf_pad, c_pre[0], pq, pk, t_prep, 0)
    o_attn = _fox_attn(qp, kp, vp, kpre, vpre, t_attn)
    y_rwkv, _ = _rwkv([bl(a) for a in main[6:6 + n_ops]], bl(zr_),
                      main[6 + n_ops].reshape(b, l // CHUNK, D_BRANCH), gnw, gnb, ones_bd, z_pre[0], t_rwkv)
    out = _out_proj(o_attn.reshape(b * l, D_BRANCH), zf_, y_rwkv.reshape(b * l, D_BRANCH), xf,
                    w_out[0].astype(BF16), row(final_norm_w), tm)
    return out.reshape(b, l, D_MODEL)
```

```python
import functools

import numpy as np
import jax
import jax.numpy as jnp
from jax import lax
from jax.experimental import pallas as pl
from jax.experimental.pallas import tpu as pltpu

F32 = jnp.float32
BF16 = jnp.bfloat16

D_MODEL = 1024
N_META = 16
HEAD_DIM = 64
N_HEADS = 8
D_BRANCH = N_HEADS * HEAD_DIM
N_PAIRS = N_HEADS // 2
RANK = 64
NORM_EPS = 1e-6
GN_EPS = 64e-5
KK_EPS = 1e-12
NEG = -1e30

LANES = 128
PREFIX_ROWS = 128
CHUNK = 64
SUB = 2 * CHUNK
PAIR = 2 * HEAD_DIM
VMEM_LIMIT = 56 * 1024 * 1024

N_WIDE = 8
COL_WA = N_WIDE * D_BRANCH
COL_FL = COL_WA + LANES
N_COLS = COL_FL + LANES

N_CPARTS = 3
ONES_LANE = (HEAD_DIM, 0)
ATTN_HEADS = 4


def _dot(a, b):
    return jnp.dot(a, b, preferred_element_type=F32)


def _pieces(x, n):
    out = []
    for _ in range(n - 1):
        p = x.astype(BF16)
        out.append(p)
        x = x - p.astype(F32)
    out.append(x.astype(BF16))
    return out


def _dot_exact_lhs(a_bf16, x, n):
    acc = None
    for p in _pieces(x, n):
        t = _dot(a_bf16, p)
        acc = t if acc is None else acc + t
    return acc


def _dot_x2(x, w_bf16):
    x_hi, x_lo = _pieces(x, 2)
    return _dot(x_hi, w_bf16) + _dot(x_lo, w_bf16)


def _dot_nt(a, b):
    return lax.dot_general(a, b, (((1,), (1,)), ((), ())), preferred_element_type=F32)


def _dot_tn(a, b):
    return lax.dot_general(a, b, (((0,), (0,)), ((), ())), preferred_element_type=F32)


def _softplus(x):
    return jnp.maximum(x, 0.0) + jnp.log(1.0 + jnp.exp(-jnp.abs(x)))


def _sigmoid(x):
    return 1.0 / (1.0 + jnp.exp(-x))


def _params(sem):
    return pltpu.CompilerParams(dimension_semantics=sem, vmem_limit_bytes=VMEM_LIMIT)


G_Q, G_K, G_V, G_ZF, G_R, G_RK, G_RV, G_ZR = range(N_WIDE)
RWKV_OPERANDS = ("rt", "at", "bt", "kt", "xv", "bh", "kh", "bonus")


def _in_proj_kernel(x_ref, nw_ref, w_ref,
                    mu_r_ref, mu_k_ref, mu_v_ref, mu_wa_ref, w0_ref, wup_ref,
                    a0_ref, aup_ref, kk_ref, ka_ref, rk_ref, ones_ref,
                    pr_ref, pk_ref, pv_ref, pwa_ref,
                    q_ref, k_ref, v_ref, zf_ref, zr_ref, fl_ref,
                    rt_ref, at_ref, bt_ref, kt_ref, xv_ref, bh_ref, kh_ref, bonus_ref, wend_ref,
                    lr_ref, lk_ref, lv_ref, lwa_ref,
                    sr_ref, sk_ref, sv_ref, swa_ref, *, tm, tiles_per_seq):
    i = pl.program_id(0)

    @pl.when(lax.rem(i, tiles_per_seq) == 0)
    def _():
        sr_ref[...] = pr_ref[...]
        sk_ref[...] = pk_ref[...]
        sv_ref[...] = pv_ref[...]
        swa_ref[...] = pwa_ref[...]

    x = x_ref[...]
    u = x * lax.rsqrt(jnp.mean(x * x, axis=-1, keepdims=True) + NORM_EPS) * nw_ref[...]
    ub = u.astype(BF16)

    def group(g):
        return _dot(ub, w_ref[:, g * D_BRANCH:(g + 1) * D_BRANCH])

    raw = [group(G_R), group(G_RK), group(G_RV), _dot(ub, w_ref[:, COL_WA:COL_WA + LANES])]
    ones_bd = ones_ref[...]

    def seg_sum(v):
        return _dot(v.astype(BF16), ones_bd)

    def shifted(cur, prev_row, mu_ref):
        first = lax.broadcasted_iota(jnp.int32, cur.shape, 0) == 0
        prev = jnp.where(first, prev_row, pltpu.roll(cur, 1, 0))
        return cur + mu_ref[...] * (prev - cur)

    r2 = lax.broadcasted_iota(jnp.int32, (SUB, SUB), 0)
    c2 = lax.broadcasted_iota(jnp.int32, (SUB, SUB), 1)
    tri = jnp.where((r2 >= c2) & (r2 // CHUNK == c2 // CHUNK), 1.0, 0.0).astype(BF16)
    outs = dict(zip(RWKV_OPERANDS, (rt_ref, at_ref, bt_ref, kt_ref, xv_ref, bh_ref, kh_ref, bonus_ref)))

    def rwkv_prepare():
        prev_rows = [sr_ref[...], sk_ref[...], sv_ref[...], swa_ref[...]]
        mus = (mu_r_ref, mu_k_ref, mu_v_ref, mu_wa_ref)
        xr, xk, xv, xwa = [shifted(c, p, mu) for c, p, mu in zip(raw, prev_rows, mus)]
        w_lin = w0_ref[...] + _dot_x2(jnp.tanh(xwa), wup_ref[...])
        a_lin = a0_ref[...] + _dot(xwa.astype(BF16), aup_ref[...])
        kk = xk * kk_ref[...]
        kk_ss = seg_sum(kk * kk)
        yield [c[tm - 1:tm, :] for c in raw]
        w = -_softplus(-w_lin) - 0.5
        ld = -jnp.exp(w)
        a = _sigmoid(a_lin)
        kk = kk * lax.rsqrt(kk_ss + KK_EPS)
        kmod = xk * (1.0 + (a - 1.0) * ka_ref[...])
        lw = jnp.concatenate([_dot_exact_lhs(tri, ld[sb * SUB:(sb + 1) * SUB, :], 2)
                              for sb in range(tm // SUB)], axis=0)
        rk_sum = seg_sum(xr * kmod * rk_ref[...])
        yield None
        w_inv = jnp.exp(-lw)
        bt = kk * a * w_inv
        kt = kmod * w_inv
        for name, val in (("rt", xr * jnp.exp(lw)), ("at", -kk * jnp.exp(lw - ld)), ("bt", bt), ("kt", kt),
                          ("xv", xv), ("bonus", rk_sum * xv)):
            outs[name][...] = val.astype(BF16)
        for c in range(tm // CHUNK):
            rows = slice(c * CHUNK, (c + 1) * CHUNK)
            w_c = jnp.exp(lw[(c + 1) * CHUNK - 1:(c + 1) * CHUNK, :])
            wend_ref[c:c + 1, :] = w_c
            bh_ref[rows, :] = (bt[rows, :] * w_c).astype(BF16)
            kh_ref[rows, :] = (kt[rows, :] * w_c).astype(BF16)
        yield None

    parts = rwkv_prepare()
    last_rows = next(parts)
    q_ref[...] = group(G_Q).astype(BF16)
    k_ref[...] = group(G_K).astype(BF16)
    next(parts)
    v_ref[...] = group(G_V).astype(BF16)
    zf_ref[...] = group(G_ZF)
    next(parts)
    zr_ref[...] = group(G_ZR)
    fl_ref[...] = _dot(ub, w_ref[:, COL_FL:COL_FL + LANES])

    for ref, rowv in zip((sr_ref, sk_ref, sv_ref, swa_ref), last_rows):
        ref[...] = rowv

    @pl.when(i == pl.num_programs(0) - 1)
    def _():
        for out, rowv in zip((lr_ref, lk_ref, lv_ref, lwa_ref), last_rows):
            out[...] = rowv


def _in_proj(rows, norm_w, w_all, prm, prev, tm, tiles_per_seq):
    n = rows.shape[0]
    tile = lambda width: pl.BlockSpec((tm, width), lambda i: (i, 0))
    const = lambda shape: pl.BlockSpec(shape, lambda i: (0,) * len(shape))
    row_w, row_n, up = const((1, D_BRANCH)), const((1, LANES)), const((LANES, D_BRANCH))
    wide = lambda dt: jax.ShapeDtypeStruct((n, D_BRANCH), dt)
    return pl.pallas_call(
        functools.partial(_in_proj_kernel, tm=tm, tiles_per_seq=tiles_per_seq),
        grid=(n // tm,),
        in_specs=[tile(D_MODEL), const((1, D_MODEL)),
                  pl.BlockSpec((D_MODEL, N_COLS), lambda i: (0, 0), pipeline_mode=pl.Buffered(1)),
                  row_w, row_w, row_w, row_n, row_w, up, row_w, up,
                  row_w, row_w, row_w, const((D_BRANCH, D_BRANCH)),
                  row_w, row_w, row_w, row_n],
        out_specs=[tile(D_BRANCH)] * 5 + [tile(LANES)] + [tile(D_BRANCH)] * len(RWKV_OPERANDS)
                  + [pl.BlockSpec((tm // CHUNK, D_BRANCH), lambda i: (i, 0)),
                     row_w, row_w, row_w, row_n],
        out_shape=[wide(BF16), wide(BF16), wide(BF16), wide(F32), wide(F32),
                   jax.ShapeDtypeStruct((n, LANES), F32)]
                  + [wide(BF16)] * len(RWKV_OPERANDS)
                  + [jax.ShapeDtypeStruct((n // CHUNK, D_BRANCH), F32)]
                  + [jax.ShapeDtypeStruct((1, D_BRANCH), F32)] * 3 + [jax.ShapeDtypeStruct((1, LANES), F32)],
        scratch_shapes=[pltpu.VMEM((1, D_BRANCH), F32), pltpu.VMEM((1, D_BRANCH), F32),
                        pltpu.VMEM((1, D_BRANCH), F32), pltpu.VMEM((1, LANES), F32)],
        compiler_params=_params(("arbitrary",)),
        name="in_proj",
    )(rows, norm_w, w_all, *prm, *prev)


def _select_matrices():
    pq = np.zeros((N_HEADS, 2 * LANES, LANES), np.float32)
    pk = np.zeros((N_HEADS, 2 * LANES, LANES), np.float32)
    for h in range(N_HEADS):
        base = (h % 2) * HEAD_DIM
        for d in range(HEAD_DIM):
            pq[h, base + d, d] = 0.125
            pk[h, base + d, d] = 1.0
        for part in range(N_CPARTS):
            pq[h, LANES + part * N_HEADS + h, HEAD_DIM + part] = 1.0
            pq[h, LANES + N_CPARTS * N_HEADS + h, HEAD_DIM + N_CPARTS + part] = 1.0
            pk[h, LANES + N_CPARTS * N_HEADS + h, HEAD_DIM + part] = 1.0
            pk[h, LANES + part * N_HEADS + h, HEAD_DIM + N_CPARTS + part] = -1.0
    return jnp.asarray(pq, BF16), jnp.asarray(pk, BF16)


def _fox_prep_kernel(q_ref, k_ref, v_ref, fl_ref, bf_ref, c0_ref, pq_ref, pk_ref,
                     qo_ref, ko_ref, vo_ref, cl_ref, carry_ref, *, tr, n_pad):
    i = pl.program_id(1)

    @pl.when(i == 0)
    def _():
        carry_ref[...] = c0_ref[...]

    x = fl_ref[0] + bf_ref[...]
    logf = jnp.minimum(x, 0.0) - jnp.log(1.0 + jnp.exp(-jnp.abs(x)))
    lane = lax.broadcasted_iota(jnp.int32, (tr, LANES), 1)
    row = lax.broadcasted_iota(jnp.int32, (tr, LANES), 0) + i * tr
    valid = lane < N_HEADS
    if n_pad:
        valid = valid & (row >= n_pad)
    logf = jnp.where(valid, logf, 0.0)
    r2 = lax.broadcasted_iota(jnp.int32, (tr, tr), 0)
    c2 = lax.broadcasted_iota(jnp.int32, (tr, tr), 1)
    tri = jnp.where(r2 >= c2, 1.0, 0.0).astype(BF16)
    cum = _dot_exact_lhs(tri, logf, 3) + carry_ref[...]
    carry_ref[...] = cum[tr - 1:tr, :]

    @pl.when(i == pl.num_programs(1) - 1)
    def _():
        cl_ref[0] = cum[tr - 1:tr, :]

    p1 = cum.astype(BF16).astype(F32)
    rem = cum - p1
    p2 = rem.astype(BF16).astype(F32)
    p3 = (rem - p2).astype(BF16).astype(F32)
    ones = jnp.where((lane >= N_CPARTS * N_HEADS) & (lane < (N_CPARTS + 1) * N_HEADS), 1.0, 0.0)
    cbits = p1 + pltpu.roll(p2, N_HEADS, 1) + pltpu.roll(p3, 2 * N_HEADS, 1) + ones
    cbits_k = cbits
    if n_pad:
        cbits_k = jnp.where((row < n_pad) & (lane < N_HEADS), -NEG, cbits)
    cq = cbits.astype(BF16)
    ck = cbits_k.astype(BF16)
    sel_r = lax.broadcasted_iota(jnp.int32, (PAIR, PAIR), 0)
    sel_c = lax.broadcasted_iota(jnp.int32, (PAIR, PAIR), 1)
    vrow = lax.broadcasted_iota(jnp.int32, (PAIR, tr), 0)
    for h in range(N_HEADS):
        sl = slice((h // 2) * PAIR, (h // 2 + 1) * PAIR)
        xq = jnp.concatenate([q_ref[0, :, sl], cq], axis=1)
        xk = jnp.concatenate([k_ref[0, :, sl], ck], axis=1)
        qo_ref[0, h] = _dot(xq, pq_ref[h]).astype(BF16)
        ko_ref[0, h] = _dot(xk, pk_ref[h]).astype(BF16)
        own = (sel_r < HEAD_DIM) if h % 2 == 0 else (sel_r >= HEAD_DIM)
        pick = jnp.where((sel_r == sel_c) & own, 1.0, 0.0).astype(BF16)
        v_t = _dot_nt(pick, v_ref[0, :, sl])
        vo_ref[0, h, 0] = jnp.where(vrow == ONES_LANE[h % 2], 1.0, v_t).astype(BF16)


def _fox_prep(q, k, v, fl, bf_pad, c0, pq, pk, tr, n_pad):
    b, l, _ = q.shape
    wide = pl.BlockSpec((1, tr, D_BRANCH), lambda bi, i: (bi, i, 0))
    head_out = pl.BlockSpec((1, N_HEADS, tr, LANES), lambda bi, i: (bi, 0, i, 0))
    row128 = pl.BlockSpec((1, LANES), lambda bi, i: (0, 0))
    sel = pl.BlockSpec((N_HEADS, 2 * LANES, LANES), lambda bi, i: (0, 0, 0))
    return pl.pallas_call(
        functools.partial(_fox_prep_kernel, tr=tr, n_pad=n_pad),
        grid=(b, l // tr),
        in_specs=[wide, wide, wide,
                  pl.BlockSpec((1, tr, LANES), lambda bi, i: (bi, i, 0)),
                  row128, row128, sel, sel],
        out_specs=[head_out, head_out,
                   pl.BlockSpec((1, N_HEADS, 1, PAIR, tr), lambda bi, i: (bi, 0, i, 0, 0)),
                   pl.BlockSpec((1, 1, LANES), lambda bi, i: (bi, 0, 0))],
        out_shape=[jax.ShapeDtypeStruct((b, N_HEADS, l, LANES), BF16)] * 2
                  + [jax.ShapeDtypeStruct((b, N_HEADS, l // tr, PAIR, tr), BF16),
                     jax.ShapeDtypeStruct((b, 1, LANES), F32)],
        scratch_shapes=[pltpu.VMEM((1, LANES), F32)],
        compiler_params=_params(("parallel", "arbitrary")),
        name="fox_prep",
    )(q, k, v, fl, bf_pad, c0, pq, pk)


def _attn_kernel(q_ref, kpre_ref, vpre_ref, k_ref, v_ref, o_ref, s_ref, *, tq):
    qi = pl.program_id(2)
    heads = range(ATTN_HEADS)
    qs = [q_ref[0, h] for h in heads]

    def rowmax(s):
        return jnp.max(s, axis=0, keepdims=True)

    def k_chunk(h, idx):
        return k_ref[0, h, pl.ds(pl.multiple_of(idx * tq, tq), tq), :]

    def v_chunk(h, idx):
        return v_ref[0, h, idx]

    def scores(idx):
        return [_dot_nt(k_chunk(h, idx), qs[h]) for h in heads]

    def softmax_pv(carry, s, rmax, idx):
        out = []
        for h in heads:
            m, acc = carry[h]
            m_new = jnp.maximum(m, rmax[h])
            p = jnp.exp(s[h] - m_new).astype(BF16)
            out.append((m_new, jnp.exp(m - m_new) * acc + _dot(v_chunk(h, idx), p)))
        return out

    key = lax.broadcasted_iota(jnp.int32, (tq, tq), 0)
    qry = lax.broadcasted_iota(jnp.int32, (tq, tq), 1)
    s_pre = [_dot_nt(kpre_ref[0, h], qs[h]) for h in heads]
    s_dia = [jnp.where(key <= qry, s, NEG) for s in scores(qi)]
    s_nxt = scores(0)
    carry = []
    for h in heads:
        m0 = jnp.maximum(rowmax(s_pre[h]), rowmax(s_dia[h]))
        p_pre = jnp.exp(s_pre[h] - m0).astype(BF16)
        p_dia = jnp.exp(s_dia[h] - m0).astype(BF16)
        carry.append((m0, _dot(vpre_ref[0, h, 0], p_pre) + _dot(v_chunk(h, qi), p_dia)))
        s_ref[h] = s_nxt[h]
    rmax = [rowmax(s) for s in s_nxt]

    def step2(t, state):
        carry, rmax = state
        k0 = 2 * t
        s_a = [s_ref[h] for h in heads]
        s_b = scores(k0 + 1)
        carry = softmax_pv(carry, s_a, rmax, k0)
        rmax_b = [rowmax(s) for s in s_b]
        s_c = scores(k0 + 2)
        carry = softmax_pv(carry, s_b, rmax_b, k0 + 1)
        for h in heads:
            s_ref[h] = s_c[h]
        return carry, [rowmax(s) for s in s_c]

    def step1(k0, state):
        carry, rmax = state
        s_a = [s_ref[h] for h in heads]
        s_b = scores(k0 + 1)
        carry = softmax_pv(carry, s_a, rmax, k0)
        for h in heads:
            s_ref[h] = s_b[h]
        return carry, [rowmax(s) for s in s_b]

    steps = jnp.maximum(qi - 1, 0)
    pairs = steps // 2
    state = lax.fori_loop(0, pairs, step2, (carry, rmax))
    carry, rmax = lax.fori_loop(2 * pairs, steps, step1, state)

    some = qi > 0
    last = jnp.maximum(qi - 1, 0)
    s_fin = [jnp.where(some, s_ref[h], NEG) for h in heads]
    r_fin = [jnp.where(some, r, NEG) for r in rmax]
    accs = [acc for _, acc in softmax_pv(carry, s_fin, r_fin, last)]
    vrow = lax.broadcasted_iota(jnp.int32, (PAIR, tq), 0)
    for pr in range(ATTN_HEADS // 2):
        acc0, acc1 = accs[2 * pr], accs[2 * pr + 1]
        l0 = acc0[ONES_LANE[0]:ONES_LANE[0] + 1, :]
        l1 = acc1[ONES_LANE[1]:ONES_LANE[1] + 1, :]
        o_t = jnp.where(vrow < HEAD_DIM, acc0 / l0, acc1 / l1)
        o_ref[0, :, pr * PAIR:(pr + 1) * PAIR] = o_t.T


def _fox_attn(qp, kp, vp, kpre, vpre, tq):
    b, _, l, _ = qp.shape
    g = ATTN_HEADS
    assert vp.shape[-1] == tq and vpre.shape[-1] == PREFIX_ROWS
    pre = pl.BlockSpec((1, g, PREFIX_ROWS, LANES), lambda bi, p, qi: (0, p, 0, 0))
    pre_v = pl.BlockSpec((1, g, 1, PAIR, PREFIX_ROWS), lambda bi, p, qi: (0, p, 0, 0, 0))
    full = pl.BlockSpec((1, g, l, LANES), lambda bi, p, qi: (bi, p, 0, 0))
    full_v = pl.BlockSpec((1, g, l // tq, PAIR, tq), lambda bi, p, qi: (bi, p, 0, 0, 0))
    return pl.pallas_call(
        functools.partial(_attn_kernel, tq=tq),
        grid=(b, N_HEADS // g, l // tq),
        in_specs=[pl.BlockSpec((1, g, tq, LANES), lambda bi, p, qi: (bi, p, qi, 0)),
                  pre, pre_v, full, full_v],
        out_specs=pl.BlockSpec((1, tq, g * HEAD_DIM), lambda bi, p, qi: (bi, qi, p)),
        out_shape=jax.ShapeDtypeStruct((b, l, D_BRANCH), F32),
        scratch_shapes=[pltpu.VMEM((g, tq, tq), F32)],
        compiler_params=_params(("parallel", "parallel", "parallel")),
        name="fox_attn",
    )(qp, kpre, vpre, kp, vp)


def _rwkv_kernel(rt_ref, at_ref, bt_ref, kt_ref, xv_ref, bh_ref, kh_ref, bonus_ref, z_ref, wend_ref,
                 gnw_ref, gnb_ref, ones_ref, z0_ref,
                 y_ref, zf_ref, state_ref, yacc_ref, *, tr):
    t = pl.program_id(1)
    last = t == pl.num_programs(1) - 1

    @pl.when(t == 0)
    def _():
        state_ref[...] = z0_ref[...]

    ones_bd = ones_ref[...]

    def seg_sum(x):
        return _dot(x.astype(BF16), ones_bd)

    operand = dict(rt=rt_ref, at=at_ref, bt=bt_ref, kt=kt_ref, xv=xv_ref, bh=bh_ref, kh=kh_ref)

    col = lax.broadcasted_iota(jnp.int32, (CHUNK, PAIR), 1)
    trow = lax.broadcasted_iota(jnp.int32, (CHUNK, PAIR), 0)
    tcol = col % CHUNK
    head_a = col < HEAD_DIM
    strict = trow > tcol
    incl = trow >= tcol
    eye_sbs = jnp.where(trow == tcol, 1.0, 0.0).astype(F32)
    sq_r = lax.broadcasted_iota(jnp.int32, (PAIR, PAIR), 0)
    sq_c = lax.broadcasted_iota(jnp.int32, (PAIR, PAIR), 1)
    same_head = (sq_r // HEAD_DIM) == (sq_c // HEAD_DIM)
    eye_sq = sq_r == sq_c

    def stack(x):
        zero = jnp.zeros_like(x)
        return jnp.concatenate([jnp.where(head_a, x, zero), jnp.where(head_a, zero, x)], axis=0)

    def fold(sq):
        sq = jnp.where(same_head, sq, 0.0)
        return sq[:CHUNK] + sq[CHUNK:]

    def tile(name, c, p):
        return operand[name][0, c * CHUNK:(c + 1) * CHUNK, p * PAIR:(p + 1) * PAIR]

    n_chunks = tr // CHUNK
    units =[(c, p) for c in range(n_chunks) for p in range(N_PAIRS)]
    nu = range(len(units))
    wc = [wend_ref[0, c:c + 1, p * PAIR:(p + 1) * PAIR] for c, p in units]
    rt, at, bt, kt, xv, bh, kh = ([tile(name, c, p) for c, p in units]
                                  for name in ("rt", "at", "bt", "kt", "xv", "bh", "kh"))
    bt_s = [stack(x) for x in bt]
    kt_s = [stack(x) for x in kt]
    xv_s = [stack(x) for x in xv]

    ar = [jnp.concatenate([at[u], rt[u]], axis=0) for u in nu]
    x_b = [_dot_nt(ar[u], bt_s[u]) for u in nu]
    x_k = [_dot_nt(ar[u], kt_s[u]) for u in nu]
    a_ab = [jnp.where(strict, x[:CHUNK], 0.0) for x in x_b]
    a_rb = [jnp.where(incl, x[CHUNK:], 0.0).astype(BF16) for x in x_b]
    a_ak = [jnp.where(strict, x[:CHUNK], 0.0).astype(BF16) for x in x_k]
    a_rk = [jnp.where(incl, x[CHUNK:], 0.0).astype(BF16) for x in x_k]

    ab = [a.astype(BF16) for a in a_ab]
    pw = [_dot(ab[u], stack(ab[u])) for u in nu]
    tinv = [eye_sbs + a_ab[u] for u in nu]
    for level in range(1, 6):
        pb = [x.astype(BF16) for x in pw]
        if level < 5:
            both = [_dot(jnp.concatenate([tinv[u].astype(BF16), pb[u]], axis=0), stack(pb[u])) for u in nu]
            tinv = [tinv[u] + both[u][:CHUNK] for u in nu]
            pw = [both[u][CHUNK:] for u in nu]
        else:
            tinv = [tinv[u] + _dot(tinv[u].astype(BF16), stack(pb[u])) for u in nu]
    tb = [x.astype(BF16) for x in tinv]

    ap = [_dot(tb[u], stack(at[u])).astype(BF16) for u in nu]
    akv = [_dot(a_ak[u], xv_s[u]).astype(BF16) for u in nu]
    vp = [_dot(tb[u], stack(akv[u])).astype(BF16) for u in nu]
    ap_s = [stack(x) for x in ap]
    vp_s = [stack(x) for x in vp]
    m_sbs = [fold(jnp.where(eye_sq, wc[u], 0.0) + _dot_tn(bh[u], ap[u])).astype(BF16) for u in nu]
    g_sbs = [fold(_dot_tn(jnp.concatenate([bh[u], kh[u]], axis=0), jnp.concatenate([vp[u], xv[u]], axis=0)))
             for u in nu]
    rp = [(rt[u].astype(F32) + _dot(a_rb[u], ap_s[u])).astype(BF16) for u in nu]
    y0 = [_dot(jnp.concatenate([a_rb[u], a_rk[u]], axis=1), jnp.concatenate([vp_s[u], xv_s[u]], axis=0))
          for u in nu]

    for u, (c, p) in enumerate(units):
        z_bd = stack(state_ref[p].astype(BF16))
        both = _dot(jnp.concatenate([rp[u], m_sbs[u]], axis=0), z_bd)
        yacc_ref[c * CHUNK:(c + 1) * CHUNK, p * PAIR:(p + 1) * PAIR] = both[:CHUNK] + y0[u]
        state_ref[p] = both[CHUNK:] + g_sbs[u]

    @pl.when(last)
    def _():
        zf_ref[0] = state_ref[...]

    inv_n = 1.0 / HEAD_DIM
    y = yacc_ref[...]
    mean = seg_sum(y) * inv_n
    d = y - mean
    var = seg_sum(d * d) * inv_n
    yn = d * lax.rsqrt(var + GN_EPS) * gnw_ref[...] + gnb_ref[...]
    z = z_ref[0]
    y_ref[0] = ((yn + bonus_ref[0].astype(F32)) * (z * _sigmoid(z))).astype(y_ref.dtype)


def _rwkv(ops, z, wend, gn_w, gn_b, ones_bd, z0, tr):
    b, l, _ = z.shape
    wide = pl.BlockSpec((1, tr, D_BRANCH), lambda bi, t: (bi, t, 0))
    row_w = pl.BlockSpec((1, D_BRANCH), lambda bi, t: (0, 0))
    ones = pl.BlockSpec((D_BRANCH, D_BRANCH), lambda bi, t: (0, 0))
    st_in = pl.BlockSpec((N_PAIRS, CHUNK, PAIR), lambda bi, t: (0, 0, 0))
    st_out = pl.BlockSpec((1, N_PAIRS, CHUNK, PAIR), lambda bi, t: (bi, 0, 0, 0))
    return pl.pallas_call(
        functools.partial(_rwkv_kernel, tr=tr),
        grid=(b, l // tr),
        in_specs=[wide] * (len(RWKV_OPERANDS) + 1)
                 + [pl.BlockSpec((1, tr // CHUNK, D_BRANCH), lambda bi, t: (bi, t, 0)),
                    row_w, row_w, ones, st_in],
        out_specs=[wide, st_out],
        out_shape=[jax.ShapeDtypeStruct((b, l, D_BRANCH), BF16),
                   jax.ShapeDtypeStruct((b, N_PAIRS, CHUNK, PAIR), F32)],
        scratch_shapes=[pltpu.VMEM((N_PAIRS, CHUNK, PAIR), F32), pltpu.VMEM((tr, D_BRANCH), F32)],
        compiler_params=_params(("parallel", "arbitrary")),
        name="rwkv",
    )(*ops, z, wend, gn_w, gn_b, ones_bd, z0)


def _out_proj_kernel(o_ref, zf_ref, yr_ref, x_ref, w_ref, fnw_ref, out_ref):
    z = zf_ref[...]
    yf = (o_ref[...] * (z * _sigmoid(z))).astype(BF16)
    mix = _dot(yf, w_ref[:D_BRANCH, :]) + _dot(yr_ref[...], w_ref[D_BRANCH:, :])
    h = x_ref[...] + mix
    out_ref[...] = h * lax.rsqrt(jnp.mean(h * h, axis=-1, keepdims=True) + NORM_EPS) * fnw_ref[...]


def _out_proj(o, zf, yr, x, w_out, fnw, tm):
    n = x.shape[0]
    half = pl.BlockSpec((tm, D_BRANCH), lambda i: (i, 0))
    full = pl.BlockSpec((tm, D_MODEL), lambda i: (i, 0))
    return pl.pallas_call(
        _out_proj_kernel,
        grid=(n // tm,),
        in_specs=[half, half, half, full,
                  pl.BlockSpec((D_MODEL, D_MODEL), lambda i: (0, 0)),
                  pl.BlockSpec((1, D_MODEL), lambda i: (0, 0))],
        out_specs=full,
        out_shape=jax.ShapeDtypeStruct((n, D_MODEL), F32),
        compiler_params=_params(("parallel",)),
        name="out_proj",
    )(o, zf, yr, x, w_out, fnw)


def _tiles(b, l):
    rows = b * l
    tm = 512 if rows % 512 == 0 else 256
    return tm, 512, 512, 512


def kernel(x, meta, norm_w, w_in, b_f, mu_shift, w0, w_up, a0, a_up, k_k, k_a, r_k, gn_w, gn_b,
           w_out, final_norm_w):
    b, l, d = x.shape
    assert d == D_MODEL and norm_w.shape[0] == 1 and l % 256 == 0
    tm, t_prep, t_attn, t_rwkv = _tiles(b, l)

    wi = w_in[0]
    o = 0
    cols = {}
    for name, width in (("q", D_BRANCH), ("k", D_BRANCH), ("v", D_BRANCH), ("fl", N_HEADS), ("zf", D_BRANCH),
                        ("r", D_BRANCH), ("rk", D_BRANCH), ("rv", D_BRANCH), ("wd", RANK), ("ad", RANK),
                        ("zr", D_BRANCH)):
        cols[name] = wi[:, o:o + width]
        o += width
    w_all = jnp.concatenate(
        [cols[n] for n in ("q", "k", "v", "zf", "r", "rk", "rv", "zr", "wd", "ad", "fl")]
        + [jnp.zeros((D_MODEL, LANES - N_HEADS), F32)], axis=1).astype(BF16)

    row = lambda vec: vec.reshape(1, -1).astype(F32)
    mu = mu_shift[0]
    mu_r, mu_k, mu_v = (row(mu[i * D_BRANCH:(i + 1) * D_BRANCH]) for i in range(3))
    mu_wa = row(mu[3 * D_BRANCH:])
    zeros_up = jnp.zeros((RANK, D_BRANCH), F32)
    wup_pad = jnp.concatenate([w_up[0], zeros_up], axis=0)
    aup_pad = jnp.concatenate([zeros_up, a_up[0]], axis=0)
    hid = np.arange(D_BRANCH) // HEAD_DIM
    ones_bd = jnp.asarray(hid[:, None] == hid[None, :], BF16)
    shift_prm = (mu_r, mu_k, mu_v, mu_wa, row(w0[0]), wup_pad.astype(BF16), row(a0[0]), aup_pad.astype(BF16),
                 row(k_k[0]), row(k_a[0]), row(r_k[0]), ones_bd)
    gnw, gnb = row(gn_w[0]), row(gn_b[0])
    bf_pad = jnp.concatenate([b_f[0], jnp.zeros((LANES - N_HEADS,), F32)]).reshape(1, LANES)
    pq, pk = _select_matrices()
    nw = row(norm_w[0])
    n_ops = len(RWKV_OPERANDS)

    pre_rows = jnp.concatenate([jnp.zeros((PREFIX_ROWS - N_META, D_MODEL), F32), meta.astype(F32)], axis=0)
    zero_w = jnp.zeros((1, D_BRANCH), F32)
    pre = _in_proj(pre_rows, nw, w_all, shift_prm, (zero_w, zero_w, zero_w, jnp.zeros((1, LANES), F32)),
                   PREFIX_ROWS, 1)
    pq_, pk_, pv_, _, pzr_, pfl_ = pre[:6]
    pre_ops, pre_wend, last_raw = pre[6:6 + n_ops], pre[6 + n_ops], pre[7 + n_ops:]
    lead = lambda a: a[None]
    _, kpre, vpre, c_pre = _fox_prep(lead(pq_), lead(pk_), lead(pv_), lead(pfl_), bf_pad,
                                     jnp.zeros((1, LANES), F32), pq, pk, PREFIX_ROWS, PREFIX_ROWS - N_META)
    _, z_pre = _rwkv([lead(a) for a in pre_ops], lead(pzr_), lead(pre_wend), gnw, gnb, ones_bd,
                     jnp.zeros((N_PAIRS, CHUNK, PAIR), F32), PREFIX_ROWS)

    xf = x.reshape(b * l, D_MODEL)
    main = _in_proj(xf, nw, w_all, shift_prm, last_raw, tm, l // tm)
    q_, k_, v_, zf_, zr_, fl_ = main[:6]
    bl = lambda a: a.reshape(b, l, a.shape[-1])
    qp, kp, vp, _ = _fox_prep(bl(q_), bl(k_), bl(v_), bl(fl_), bf_pad, c_pre[0], pq, pk, t_prep, 0)
    o_attn = _fox_attn(qp, kp, vp, kpre, vpre, t_attn)
    y_rwkv, _ = _rwkv([bl(a) for a in main[6:6 + n_ops]], bl(zr_),
                      main[6 + n_ops].reshape(b, l // CHUNK, D_BRANCH), gnw, gnb, ones_bd, z_pre[0], t_rwkv)
    out = _out_proj(o_attn.reshape(b * l, D_BRANCH), zf_, y_rwkv.reshape(b * l, D_BRANCH), xf,
                    w_out[0].astype(BF16), row(final_norm_w), tm)
    return out.reshape(b, l, D_MODEL)
```

```python
import functools

import numpy as np
import jax
import jax.numpy as jnp
from jax import lax
from jax.experimental import pallas as pl
from jax.experimental.pallas import tpu as pltpu

F32 = jnp.float32
BF16 = jnp.bfloat16

D_MODEL = 1024
N_META = 16
HEAD_DIM = 64
N_HEADS = 8
D_BRANCH = N_HEADS * HEAD_DIM
N_PAIRS = N_HEADS // 2
RANK = 64
NORM_EPS = 1e-6
GN_EPS = 64e-5
KK_EPS = 1e-12
NEG = -1e30

LANES = 128
PREFIX_ROWS = 128
CHUNK = 64
SUB = 2 * CHUNK
PAIR = 2 * HEAD_DIM
VMEM_LIMIT = 56 * 1024 * 1024

N_WIDE = 8
COL_WA = N_WIDE * D_BRANCH
COL_FL = COL_WA + LANES
N_COLS = COL_FL + LANES

N_CPARTS = 3
ONES_LANE = (HEAD_DIM, 0)
ATTN_HEADS = 4


def _dot(a, b):
    return jnp.dot(a, b, preferred_element_type=F32)


def _pieces(x, n):
    out = []
    for _ in range(n - 1):
        p = x.astype(BF16)
        out.append(p)
        x = x - p.astype(F32)
    out.append(x.astype(BF16))
    return out


def _dot_exact_lhs(a_bf16, x, n):
    acc = None
    for p in _pieces(x, n):
        t = _dot(a_bf16, p)
        acc = t if acc is None else acc + t
    return acc


def _dot_x2(x, w_bf16):
    x_hi, x_lo = _pieces(x, 2)
    return _dot(x_hi, w_bf16) + _dot(x_lo, w_bf16)


def _dot_nt(a, b):
    return lax.dot_general(a, b, (((1,), (1,)), ((), ())), preferred_element_type=F32)


def _dot_tn(a, b):
    return lax.dot_general(a, b, (((0,), (0,)), ((), ())), preferred_element_type=F32)


def _softplus(x):
    return jnp.maximum(x, 0.0) + jnp.log(1.0 + jnp.exp(-jnp.abs(x)))


def _sigmoid(x):
    return 1.0 / (1.0 + jnp.exp(-x))


def _params(sem):
    return pltpu.CompilerParams(dimension_semantics=sem, vmem_limit_bytes=VMEM_LIMIT)


G_Q, G_K, G_V, G_ZF, G_R, G_RK, G_RV, G_ZR = range(N_WIDE)
RWKV_OPERANDS = ("rt", "at", "bt", "kt", "xv", "bh", "kh", "bonus")


def _in_proj_kernel(x_ref, nw_ref, w_ref,
                    mu_r_ref, mu_k_ref, mu_v_ref, mu_wa_ref, w0_ref, wup_ref,
                    a0_ref, aup_ref, kk_ref, ka_ref, rk_ref, ones_ref,
                    pr_ref, pk_ref, pv_ref, pwa_ref,
                    q_ref, k_ref, v_ref, zf_ref, zr_ref, fl_ref,
                    rt_ref, at_ref, bt_ref, kt_ref, xv_ref, bh_ref, kh_ref, bonus_ref, wend_ref,
                    lr_ref, lk_ref, lv_ref, lwa_ref,
                    sr_ref, sk_ref, sv_ref, swa_ref, *, tm, tiles_per_seq):
    i = pl.program_id(0)

    @pl.when(lax.rem(i, tiles_per_seq) == 0)
    def _():
        sr_ref[...] = pr_ref[...]
        sk_ref[...] = pk_ref[...]
        sv_ref[...] = pv_ref[...]
        swa_ref[...] = pwa_ref[...]

    x = x_ref[...]
    u = x * lax.rsqrt(jnp.mean(x * x, axis=-1, keepdims=True) + NORM_EPS) * nw_ref[...]
    ub = u.astype(BF16)

    def group(g):
        return _dot(ub, w_ref[:, g * D_BRANCH:(g + 1) * D_BRANCH])

    raw = [group(G_R), group(G_RK), group(G_RV), _dot(ub, w_ref[:, COL_WA:COL_WA + LANES])]
    ones_bd = ones_ref[...]

    def seg_sum(v):
        return _dot(v.astype(BF16), ones_bd)

    def shifted(cur, prev_row, mu_ref):
        first = lax.broadcasted_iota(jnp.int32, cur.shape, 0) == 0
        prev = jnp.where(first, prev_row, pltpu.roll(cur, 1, 0))
        return cur + mu_ref[...] * (prev - cur)

    r2 = lax.broadcasted_iota(jnp.int32, (SUB, SUB), 0)
    c2 = lax.broadcasted_iota(jnp.int32, (SUB, SUB), 1)
    tri = jnp.where((r2 >= c2) & (r2 // CHUNK == c2 // CHUNK), 1.0, 0.0).astype(BF16)
    outs = dict(zip(RWKV_OPERANDS, (rt_ref, at_ref, bt_ref, kt_ref, xv_ref, bh_ref, kh_ref, bonus_ref)))

    def rwkv_prepare():
        prev_rows = [sr_ref[...], sk_ref[...], sv_ref[...], swa_ref[...]]
        mus = (mu_r_ref, mu_k_ref, mu_v_ref, mu_wa_ref)
        xr, xk, xv, xwa = [shifted(c, p, mu) for c, p, mu in zip(raw, prev_rows, mus)]
        w_lin = w0_ref[...] + _dot_x2(jnp.tanh(xwa), wup_ref[...])
        a_lin = a0_ref[...] + _dot(xwa.astype(BF16), aup_ref[...])
        kk = xk * kk_ref[...]
        kk_ss = seg_sum(kk * kk)
        yield [c[tm - 1:tm, :] for c in raw]
        w = -_softplus(-w_lin) - 0.5
        ld = -jnp.exp(w)
        a = _sigmoid(a_lin)
        kk = kk * lax.rsqrt(kk_ss + KK_EPS)
        kmod = xk * (1.0 + (a - 1.0) * ka_ref[...])
        lw = jnp.concatenate([_dot_exact_lhs(tri, ld[sb * SUB:(sb + 1) * SUB, :], 2)
                              for sb in range(tm // SUB)], axis=0)
        rk_sum = seg_sum(xr * kmod * rk_ref[...])
        yield None
        w_inv = jnp.exp(-lw)
        bt = kk * a * w_inv
        kt = kmod * w_inv
        for name, val in (("rt", xr * jnp.exp(lw)), ("at", -kk * jnp.exp(lw - ld)), ("bt", bt), ("kt", kt),
                          ("xv", xv), ("bonus", rk_sum * xv)):
            outs[name][...] = val.astype(BF16)
        for c in range(tm // CHUNK):
            rows = slice(c * CHUNK, (c + 1) * CHUNK)
            w_c = jnp.exp(lw[(c + 1) * CHUNK - 1:(c + 1) * CHUNK, :])
            wend_ref[c:c + 1, :] = w_c
            bh_ref[rows, :] = (bt[rows, :] * w_c).astype(BF16)
            kh_ref[rows, :] = (kt[rows, :] * w_c).astype(BF16)
        yield None

    parts = rwkv_prepare()
    last_rows = next(parts)
    q_ref[...] = group(G_Q).astype(BF16)
    k_ref[...] = group(G_K).astype(BF16)
    next(parts)
    v_ref[...] = group(G_V).astype(BF16)
    zf_ref[...] = group(G_ZF).astype(BF16)
    next(parts)
    zr_ref[...] = group(G_ZR).astype(BF16)
    fl_ref[...] = _dot(ub, w_ref[:, COL_FL:COL_FL + LANES])

    for ref, rowv in zip((sr_ref, sk_ref, sv_ref, swa_ref), last_rows):
        ref[...] = rowv

    @pl.when(i == pl.num_programs(0) - 1)
    def _():
        for out, rowv in zip((lr_ref, lk_ref, lv_ref, lwa_ref), last_rows):
            out[...] = rowv


def _in_proj(rows, norm_w, w_all, prm, prev, tm, tiles_per_seq):
    n = rows.shape[0]
    tile = lambda width: pl.BlockSpec((tm, width), lambda i: (i, 0))
    const = lambda shape: pl.BlockSpec(shape, lambda i: (0,) * len(shape))
    row_w, row_n, up = const((1, D_BRANCH)), const((1, LANES)), const((LANES, D_BRANCH))
    wide = lambda dt: jax.ShapeDtypeStruct((n, D_BRANCH), dt)
    return pl.pallas_call(
        functools.partial(_in_proj_kernel, tm=tm, tiles_per_seq=tiles_per_seq),
        grid=(n // tm,),
        in_specs=[tile(D_MODEL), const((1, D_MODEL)),
                  pl.BlockSpec((D_MODEL, N_COLS), lambda i: (0, 0), pipeline_mode=pl.Buffered(1)),
                  row_w, row_w, row_w, row_n, row_w, up, row_w, up,
                  row_w, row_w, row_w, const((D_BRANCH, D_BRANCH)),
                  row_w, row_w, row_w, row_n],
        out_specs=[tile(D_BRANCH)] * 5 + [tile(LANES)] + [tile(D_BRANCH)] * len(RWKV_OPERANDS)
                  + [pl.BlockSpec((tm // CHUNK, D_BRANCH), lambda i: (i, 0)),
                     row_w, row_w, row_w, row_n],
        out_shape=[wide(BF16), wide(BF16), wide(BF16), wide(BF16), wide(BF16),
                   jax.ShapeDtypeStruct((n, LANES), F32)]
                  + [wide(BF16)] * len(RWKV_OPERANDS)
                  + [jax.ShapeDtypeStruct((n // CHUNK, D_BRANCH), F32)]
                  + [jax.ShapeDtypeStruct((1, D_BRANCH), F32)] * 3 + [jax.ShapeDtypeStruct((1, LANES), F32)],
        scratch_shapes=[pltpu.VMEM((1, D_BRANCH), F32), pltpu.VMEM((1, D_BRANCH), F32),
                        pltpu.VMEM((1, D_BRANCH), F32), pltpu.VMEM((1, LANES), F32)],
        compiler_params=_params(("arbitrary",)),
        name="in_proj",
    )(rows, norm_w, w_all, *prm, *prev)


def _select_matrices():
    pq = np.zeros((LANES, D_BRANCH), np.float32)
    pk = np.zeros((LANES, D_BRANCH), np.float32)
    for h in range(N_HEADS):
        base = (h // 2) * PAIR + (HEAD_DIM if h % 2 == 0 else 0)
        for part in range(N_CPARTS):
            pq[part * N_HEADS + h, base + part] = 1.0
            pq[N_CPARTS * N_HEADS + h, base + N_CPARTS + part] = 1.0
            pk[N_CPARTS * N_HEADS + h, base + part] = 1.0
            pk[part * N_HEADS + h, base + N_CPARTS + part] = -1.0
    return jnp.asarray(pq, BF16), jnp.asarray(pk, BF16)


def _fox_prep_kernel(q_ref, k_ref, v_ref, fl_ref, bf_ref, c0_ref, pq_ref, pk_ref,
                     qo_ref, ko_ref, vo_ref, cl_ref, carry_ref, *, tr, n_pad):
    i = pl.program_id(1)

    @pl.when(i == 0)
    def _():
        carry_ref[...] = c0_ref[...]

    x = fl_ref[0] + bf_ref[...]
    logf = jnp.minimum(x, 0.0) - jnp.log(1.0 + jnp.exp(-jnp.abs(x)))
    lane = lax.broadcasted_iota(jnp.int32, (tr, LANES), 1)
    row = lax.broadcasted_iota(jnp.int32, (tr, LANES), 0) + i * tr
    valid = lane < N_HEADS
    if n_pad:
        valid = valid & (row >= n_pad)
    logf = jnp.where(valid, logf, 0.0)
    r2 = lax.broadcasted_iota(jnp.int32, (tr, tr), 0)
    c2 = lax.broadcasted_iota(jnp.int32, (tr, tr), 1)
    tri = jnp.where(r2 >= c2, 1.0, 0.0).astype(BF16)
    cum = _dot_exact_lhs(tri, logf, 2) + carry_ref[...]
    carry_ref[...] = cum[tr - 1:tr, :]

    @pl.when(i == pl.num_programs(1) - 1)
    def _():
        cl_ref[0] = cum[tr - 1:tr, :]

    p1 = cum.astype(BF16).astype(F32)
    rem = cum - p1
    p2 = rem.astype(BF16).astype(F32)
    p3 = (rem - p2).astype(BF16).astype(F32)
    ones = jnp.where((lane >= N_CPARTS * N_HEADS) & (lane < (N_CPARTS + 1) * N_HEADS), 1.0, 0.0)
    cbits = p1 + pltpu.roll(p2, N_HEADS, 1) + pltpu.roll(p3, 2 * N_HEADS, 1) + ones
    cbits_k = cbits
    if n_pad:
        cbits_k = jnp.where((row < n_pad) & (lane < N_HEADS), -NEG, cbits)
    bias_q = _dot(cbits.astype(BF16), pq_ref[...]).astype(BF16)
    bias_k = _dot(cbits_k.astype(BF16), pk_ref[...]).astype(BF16)
    low = lane < HEAD_DIM
    sel_r = lax.broadcasted_iota(jnp.int32, (PAIR, PAIR), 0)
    sel_c = lax.broadcasted_iota(jnp.int32, (PAIR, PAIR), 1)
    eye = jnp.where(sel_r == sel_c, 1.0, 0.0).astype(BF16)
    vrow = lax.broadcasted_iota(jnp.int32, (PAIR, tr), 0)
    for pr in range(N_PAIRS):
        sl = slice(pr * PAIR, (pr + 1) * PAIR)
        q_sc = q_ref[0, :, sl] * jnp.asarray(0.125, BF16)
        k_pr = k_ref[0, :, sl]
        v_t = _dot_nt(eye, v_ref[0, :, sl])
        for half in range(2):
            h = 2 * pr + half
            own = low if half == 0 else jnp.logical_not(low)
            qo_ref[0, h] = jnp.where(own, q_sc, bias_q[:, sl])
            ko_ref[0, h] = jnp.where(own, k_pr, bias_k[:, sl])
            own_v = (vrow < HEAD_DIM) if half == 0 else (vrow >= HEAD_DIM)
            vo_ref[0, h, 0] = jnp.where(vrow == ONES_LANE[half], 1.0, jnp.where(own_v, v_t, 0.0)).astype(BF16)


def _fox_prep(q, k, v, fl, bf_pad, c0, pq, pk, tr, n_pad):
    b, l, _ = q.shape
    wide = pl.BlockSpec((1, tr, D_BRANCH), lambda bi, i: (bi, i, 0))
    head_out = pl.BlockSpec((1, N_HEADS, tr, LANES), lambda bi, i: (bi, 0, i, 0))
    row128 = pl.BlockSpec((1, LANES), lambda bi, i: (0, 0))
    sel = pl.BlockSpec((LANES, D_BRANCH), lambda bi, i: (0, 0))
    return pl.pallas_call(
        functools.partial(_fox_prep_kernel, tr=tr, n_pad=n_pad),
        grid=(b, l // tr),
        in_specs=[wide, wide, wide,
                  pl.BlockSpec((1, tr, LANES), lambda bi, i: (bi, i, 0)),
                  row128, row128, sel, sel],
        out_specs=[head_out, head_out,
                   pl.BlockSpec((1, N_HEADS, 1, PAIR, tr), lambda bi, i: (bi, 0, i, 0, 0)),
                   pl.BlockSpec((1, 1, LANES), lambda bi, i: (bi, 0, 0))],
        out_shape=[jax.ShapeDtypeStruct((b, N_HEADS, l, LANES), BF16)] * 2
                  + [jax.ShapeDtypeStruct((b, N_HEADS, l // tr, PAIR, tr), BF16),
                     jax.ShapeDtypeStruct((b, 1, LANES), F32)],
        scratch_shapes=[pltpu.VMEM((1, LANES), F32)],
        compiler_params=_params(("parallel", "arbitrary")),
        name="fox_prep",
    )(q, k, v, fl, bf_pad, c0, pq, pk)


def _attn_kernel(q_ref, kpre_ref, vpre_ref, k_ref, v_ref, z_ref, o_ref, s_ref, *, tq):
    qi = pl.program_id(2)
    heads = range(ATTN_HEADS)
    qs = [q_ref[0, h] for h in heads]

    def rowmax(s):
        return jnp.max(s, axis=0, keepdims=True)

    def k_chunk(h, idx):
        return k_ref[0, h, pl.ds(pl.multiple_of(idx * tq, tq), tq), :]

    def v_chunk(h, idx):
        return v_ref[0, h, idx]

    def scores(idx):
        return [_dot_nt(k_chunk(h, idx), qs[h]) for h in heads]

    def softmax_pv(carry, s, rmax, idx):
        out = []
        for h in heads:
            m, acc = carry[h]
            m_new = jnp.maximum(m, rmax[h])
            p = jnp.exp(s[h] - m_new).astype(BF16)
            out.append((m_new, jnp.exp(m - m_new) * acc + _dot(v_chunk(h, idx), p)))
        return out

    key = lax.broadcasted_iota(jnp.int32, (tq, tq), 0)
    qry = lax.broadcasted_iota(jnp.int32, (tq, tq), 1)
    s_pre = [_dot_nt(kpre_ref[0, h], qs[h]) for h in heads]
    s_dia = [jnp.where(key <= qry, s, NEG) for s in scores(qi)]
    s_nxt = scores(0)
    carry = []
    for h in heads:
        m0 = jnp.maximum(rowmax(s_pre[h]), rowmax(s_dia[h]))
        p_pre = jnp.exp(s_pre[h] - m0).astype(BF16)
        p_dia = jnp.exp(s_dia[h] - m0).astype(BF16)
        carry.append((m0, _dot(vpre_ref[0, h, 0], p_pre) + _dot(v_chunk(h, qi), p_dia)))
        s_ref[h] = s_nxt[h]
    rmax = [rowmax(s) for s in s_nxt]

    def step2(t, state):
        carry, rmax = state
        k0 = 2 * t
        s_a = [s_ref[h] for h in heads]
        s_b = scores(k0 + 1)
        carry = softmax_pv(carry, s_a, rmax, k0)
        rmax_b = [rowmax(s) for s in s_b]
        s_c = scores(k0 + 2)
        carry = softmax_pv(carry, s_b, rmax_b, k0 + 1)
        for h in heads:
            s_ref[h] = s_c[h]
        return carry, [rowmax(s) for s in s_c]

    def step1(k0, state):
        carry, rmax = state
        s_a = [s_ref[h] for h in heads]
        s_b = scores(k0 + 1)
        carry = softmax_pv(carry, s_a, rmax, k0)
        for h in heads:
            s_ref[h] = s_b[h]
        return carry, [rowmax(s) for s in s_b]

    steps = jnp.maximum(qi - 1, 0)
    pairs = steps // 2
    state = lax.fori_loop(0, pairs, step2, (carry, rmax))
    carry, rmax = lax.fori_loop(2 * pairs, steps, step1, state)

    some = qi > 0
    last = jnp.maximum(qi - 1, 0)
    s_fin = [jnp.where(some, s_ref[h], NEG) for h in heads]
    r_fin = [jnp.where(some, r, NEG) for r in rmax]
    accs = [acc for _, acc in softmax_pv(carry, s_fin, r_fin, last)]
    vrow = lax.broadcasted_iota(jnp.int32, (PAIR, tq), 0)
    for pr in range(ATTN_HEADS // 2):
        acc0, acc1 = accs[2 * pr], accs[2 * pr + 1]
        l0 = acc0[ONES_LANE[0]:ONES_LANE[0] + 1, :]
        l1 = acc1[ONES_LANE[1]:ONES_LANE[1] + 1, :]
        o_t = jnp.where(vrow < HEAD_DIM, acc0 / l0, acc1 / l1)
        z = z_ref[0, :, pr * PAIR:(pr + 1) * PAIR].astype(F32)
        o_ref[0, :, pr * PAIR:(pr + 1) * PAIR] = (o_t.T * (z * _sigmoid(z))).astype(o_ref.dtype)


def _fox_attn(qp, kp, vp, kpre, vpre, zf, tq):
    b, _, l, _ = qp.shape
    g = ATTN_HEADS
    out_block = pl.BlockSpec((1, tq, g * HEAD_DIM), lambda bi, p, qi: (bi, qi, p))
    assert vp.shape[-1] == tq and vpre.shape[-1] == PREFIX_ROWS
    pre = pl.BlockSpec((1, g, PREFIX_ROWS, LANES), lambda bi, p, qi: (0, p, 0, 0))
    pre_v = pl.BlockSpec((1, g, 1, PAIR, PREFIX_ROWS), lambda bi, p, qi: (0, p, 0, 0, 0))
    full = pl.BlockSpec((1, g, l, LANES), lambda bi, p, qi: (bi, p, 0, 0))
    full_v = pl.BlockSpec((1, g, l // tq, PAIR, tq), lambda bi, p, qi: (bi, p, 0, 0, 0))
    return pl.pallas_call(
        functools.partial(_attn_kernel, tq=tq),
        grid=(b, N_HEADS // g, l // tq),
        in_specs=[pl.BlockSpec((1, g, tq, LANES), lambda bi, p, qi: (bi, p, qi, 0)),
                  pre, pre_v, full, full_v, out_block],
        out_specs=out_block,
        out_shape=jax.ShapeDtypeStruct((b, l, D_BRANCH), BF16),
        scratch_shapes=[pltpu.VMEM((g, tq, tq), F32)],
        compiler_params=_params(("parallel", "parallel", "parallel")),
        name="fox_attn",
    )(qp, kpre, vpre, kp, vp, zf)


def _rwkv_kernel(rt_ref, at_ref, bt_ref, kt_ref, xv_ref, bh_ref, kh_ref, bonus_ref, z_ref, wend_ref,
                 gnw_ref, gnb_ref, ones_ref, z0_ref,
                 y_ref, zf_ref, state_ref, yacc_ref, *, tr):
    t = pl.program_id(1)
    last = t == pl.num_programs(1) - 1

    @pl.when(t == 0)
    def _():
        state_ref[...] = z0_ref[...]

    ones_bd = ones_ref[...]

    def seg_sum(x):
        return _dot(x.astype(BF16), ones_bd)

    operand = dict(rt=rt_ref, at=at_ref, bt=bt_ref, kt=kt_ref, xv=xv_ref, bh=bh_ref, kh=kh_ref)

    col = lax.broadcasted_iota(jnp.int32, (CHUNK, PAIR), 1)
    trow = lax.broadcasted_iota(jnp.int32, (CHUNK, PAIR), 0)
    tcol = col % CHUNK
    head_a = col < HEAD_DIM
    strict = trow > tcol
    incl = trow >= tcol
    eye_sbs = jnp.where(trow == tcol, 1.0, 0.0).astype(F32)
    sq_r = lax.broadcasted_iota(jnp.int32, (PAIR, PAIR), 0)
    sq_c = lax.broadcasted_iota(jnp.int32, (PAIR, PAIR), 1)
    same_head = (sq_r // HEAD_DIM) == (sq_c // HEAD_DIM)
    eye_sq = sq_r == sq_c

    def stack(x):
        zero = jnp.zeros_like(x)
        return jnp.concatenate([jnp.where(head_a, x, zero), jnp.where(head_a, zero, x)], axis=0)

    def fold(sq):
        sq = jnp.where(same_head, sq, 0.0)
        return sq[:CHUNK] + sq[CHUNK:]

    def tile(name, c, p):
        return operand[name][0, c * CHUNK:(c + 1) * CHUNK, p * PAIR:(p + 1) * PAIR]

    n_chunks = tr // CHUNK
    units =[(c, p) for c in range(n_chunks) for p in range(N_PAIRS)]
    nu = range(len(units))
    wc = [wend_ref[0, c:c + 1, p * PAIR:(p + 1) * PAIR] for c, p in units]
    rt, at, bt, kt, xv, bh, kh = ([tile(name, c, p) for c, p in units]
                                  for name in ("rt", "at", "bt", "kt", "xv", "bh", "kh"))
    bt_s = [stack(x) for x in bt]
    kt_s = [stack(x) for x in kt]
    xv_s = [stack(x) for x in xv]

    ar = [jnp.concatenate([at[u], rt[u]], axis=0) for u in nu]
    x_b = [_dot_nt(ar[u], bt_s[u]) for u in nu]
    x_k = [_dot_nt(ar[u], kt_s[u]) for u in nu]
    a_ab = [jnp.where(strict, x[:CHUNK], 0.0) for x in x_b]
    a_rb = [jnp.where(incl, x[CHUNK:], 0.0).astype(BF16) for x in x_b]
    a_ak = [jnp.where(strict, x[:CHUNK], 0.0).astype(BF16) for x in x_k]
    a_rk = [jnp.where(incl, x[CHUNK:], 0.0).astype(BF16) for x in x_k]

    ab = [a.astype(BF16) for a in a_ab]
    pw = [_dot(ab[u], stack(ab[u])) for u in nu]
    tinv = [eye_sbs + a_ab[u] for u in nu]
    for level in range(1, 6):
        pb = [x.astype(BF16) for x in pw]
        if level < 5:
            both = [_dot(jnp.concatenate([tinv[u].astype(BF16), pb[u]], axis=0), stack(pb[u])) for u in nu]
            tinv = [tinv[u] + both[u][:CHUNK] for u in nu]
            pw = [both[u][CHUNK:] for u in nu]
        else:
            tinv = [tinv[u] + _dot(tinv[u].astype(BF16), stack(pb[u])) for u in nu]
    tb = [x.astype(BF16) for x in tinv]

    ap = [_dot(tb[u], stack(at[u])).astype(BF16) for u in nu]
    akv = [_dot(a_ak[u], xv_s[u]).astype(BF16) for u in nu]
    vp = [_dot(tb[u], stack(akv[u])).astype(BF16) for u in nu]
    ap_s = [stack(x) for x in ap]
    vp_s = [stack(x) for x in vp]
    m_sbs = [fold(jnp.where(eye_sq, wc[u], 0.0) + _dot_tn(bh[u], ap[u])).astype(BF16) for u in nu]
    g_sbs = [fold(_dot_tn(jnp.concatenate([bh[u], kh[u]], axis=0), jnp.concatenate([vp[u], xv[u]], axis=0)))
             for u in nu]
    rp = [(rt[u].astype(F32) + _dot(a_rb[u], ap_s[u])).astype(BF16) for u in nu]
    y0 = [_dot(jnp.concatenate([a_rb[u], a_rk[u]], axis=1), jnp.concatenate([vp_s[u], xv_s[u]], axis=0))
          for u in nu]

    for u, (c, p) in enumerate(units):
        z_bd = stack(state_ref[p].astype(BF16))
        both = _dot(jnp.concatenate([rp[u], m_sbs[u]], axis=0), z_bd)
        yacc_ref[c * CHUNK:(c + 1) * CHUNK, p * PAIR:(p + 1) * PAIR] = both[:CHUNK] + y0[u]
        state_ref[p] = both[CHUNK:] + g_sbs[u]

    @pl.when(last)
    def _():
        zf_ref[0] = state_ref[...]

    inv_n = 1.0 / HEAD_DIM
    y = yacc_ref[...]
    mean = seg_sum(y) * inv_n
    d = y - mean
    var = seg_sum(d * d) * inv_n
    yn = d * lax.rsqrt(var + GN_EPS) * gnw_ref[...] + gnb_ref[...]
    z = z_ref[0].astype(F32)
    y_ref[0] = ((yn + bonus_ref[0].astype(F32)) * (z * _sigmoid(z))).astype(y_ref.dtype)


def _rwkv(ops, z, wend, gn_w, gn_b, ones_bd, z0, tr):
    b, l, _ = z.shape
    wide = pl.BlockSpec((1, tr, D_BRANCH), lambda bi, t: (bi, t, 0))
    row_w = pl.BlockSpec((1, D_BRANCH), lambda bi, t: (0, 0))
    ones = pl.BlockSpec((D_BRANCH, D_BRANCH), lambda bi, t: (0, 0))
    st_in = pl.BlockSpec((N_PAIRS, CHUNK, PAIR), lambda bi, t: (0, 0, 0))
    st_out = pl.BlockSpec((1, N_PAIRS, CHUNK, PAIR), lambda bi, t: (bi, 0, 0, 0))
    return pl.pallas_call(
        functools.partial(_rwkv_kernel, tr=tr),
        grid=(b, l // tr),
        in_specs=[wide] * (len(RWKV_OPERANDS) + 1)
                 + [pl.BlockSpec((1, tr // CHUNK, D_BRANCH), lambda bi, t: (bi, t, 0)),
                    row_w, row_w, ones, st_in],
        out_specs=[wide, st_out],
        out_shape=[jax.ShapeDtypeStruct((b, l, D_BRANCH), BF16),
                   jax.ShapeDtypeStruct((b, N_PAIRS, CHUNK, PAIR), F32)],
        scratch_shapes=[pltpu.VMEM((N_PAIRS, CHUNK, PAIR), F32), pltpu.VMEM((tr, D_BRANCH), F32)],
        compiler_params=_params(("parallel", "arbitrary")),
        name="rwkv",
    )(*ops, z, wend, gn_w, gn_b, ones_bd, z0)


def _out_proj_kernel(yf_ref, yr_ref, x_ref, w_ref, fnw_ref, out_ref):
    mix = _dot(yf_ref[...], w_ref[:D_BRANCH, :]) + _dot(yr_ref[...], w_ref[D_BRANCH:, :])
    h = x_ref[...] + mix
    out_ref[...] = h * lax.rsqrt(jnp.mean(h * h, axis=-1, keepdims=True) + NORM_EPS) * fnw_ref[...]


def _out_proj(yf, yr, x, w_out, fnw, tm):
    n = x.shape[0]
    half = pl.BlockSpec((tm, D_BRANCH), lambda i: (i, 0))
    full = pl.BlockSpec((tm, D_MODEL), lambda i: (i, 0))
    return pl.pallas_call(
        _out_proj_kernel,
        grid=(n // tm,),
        in_specs=[half, half, full,
                  pl.BlockSpec((D_MODEL, D_MODEL), lambda i: (0, 0)),
                  pl.BlockSpec((1, D_MODEL), lambda i: (0, 0))],
        out_specs=full,
        out_shape=jax.ShapeDtypeStruct((n, D_MODEL), F32),
        compiler_params=_params(("parallel",)),
        name="out_proj",
    )(yf, yr, x, w_out, fnw)


def _tiles(b, l):
    rows = b * l
    tm = 512 if rows % 512 == 0 else 256
    return tm, 512, 512, 512


def kernel(x, meta, norm_w, w_in, b_f, mu_shift, w0, w_up, a0, a_up, k_k, k_a, r_k, gn_w, gn_b,
           w_out, final_norm_w):
    b, l, d = x.shape
    assert d == D_MODEL and norm_w.shape[0] == 1 and l % 256 == 0
    tm, t_prep, t_attn, t_rwkv = _tiles(b, l)

    wi = w_in[0]
    o = 0
    cols = {}
    for name, width in (("q", D_BRANCH), ("k", D_BRANCH), ("v", D_BRANCH), ("fl", N_HEADS), ("zf", D_BRANCH),
                        ("r", D_BRANCH), ("rk", D_BRANCH), ("rv", D_BRANCH), ("wd", RANK), ("ad", RANK),
                        ("zr", D_BRANCH)):
        cols[name] = wi[:, o:o + width]
        o += width
    w_all = jnp.concatenate(
        [cols[n] for n in ("q", "k", "v", "zf", "r", "rk", "rv", "zr", "wd", "ad", "fl")]
        + [jnp.zeros((D_MODEL, LANES - N_HEADS), F32)], axis=1).astype(BF16)

    row = lambda vec: vec.reshape(1, -1).astype(F32)
    mu = mu_shift[0]
    mu_r, mu_k, mu_v = (row(mu[i * D_BRANCH:(i + 1) * D_BRANCH]) for i in range(3))
    mu_wa = row(mu[3 * D_BRANCH:])
    zeros_up = jnp.zeros((RANK, D_BRANCH), F32)
    wup_pad = jnp.concatenate([w_up[0], zeros_up], axis=0)
    aup_pad = jnp.concatenate([zeros_up, a_up[0]], axis=0)
    hid = np.arange(D_BRANCH) // HEAD_DIM
    ones_bd = jnp.asarray(hid[:, None] == hid[None, :], BF16)
    shift_prm = (mu_r, mu_k, mu_v, mu_wa, row(w0[0]), wup_pad.astype(BF16), row(a0[0]), aup_pad.astype(BF16),
                 row(k_k[0]), row(k_a[0]), row(r_k[0]), ones_bd)
    gnw, gnb = row(gn_w[0]), row(gn_b[0])
    bf_pad = jnp.concatenate([b_f[0], jnp.zeros((LANES - N_HEADS,), F32)]).reshape(1, LANES)
    pq, pk = _select_matrices()
    nw = row(norm_w[0])
    n_ops = len(RWKV_OPERANDS)

    pre_rows = jnp.concatenate([jnp.zeros((PREFIX_ROWS - N_META, D_MODEL), F32), meta.astype(F32)], axis=0)
    zero_w = jnp.zeros((1, D_BRANCH), F32)
    pre = _in_proj(pre_rows, nw, w_all, shift_prm, (zero_w, zero_w, zero_w, jnp.zeros((1, LANES), F32)),
                   PREFIX_ROWS, 1)
    pq_, pk_, pv_, _, pzr_, pfl_ = pre[:6]
    pre_ops, pre_wend, last_raw = pre[6:6 + n_ops], pre[6 + n_ops], pre[7 + n_ops:]
    lead = lambda a: a[None]
    _, kpre, vpre, c_pre = _fox_prep(lead(pq_), lead(pk_), lead(pv_), lead(pfl_), bf_pad,
                                     jnp.zeros((1, LANES), F32), pq, pk, PREFIX_ROWS, PREFIX_ROWS - N_META)
    _, z_pre = _rwkv([lead(a) for a in pre_ops], lead(pzr_), lead(pre_wend), gnw, gnb, ones_bd,
                     jnp.zeros((N_PAIRS, CHUNK, PAIR), F32), PREFIX_ROWS)

    xf = x.reshape(b * l, D_MODEL)
    main = _in_proj(xf, nw, w_all, shift_prm, last_raw, tm, l // tm)
    q_, k_, v_, zf_, zr_, fl_ = main[:6]
    bl = lambda a: a.reshape(b, l, a.shape[-1])
    qp, kp, vp, _ = _fox_prep(bl(q_), bl(k_), bl(v_), bl(fl_), bf_pad, c_pre[0], pq, pk, t_prep, 0)
    y_fox = _fox_attn(qp, kp, vp, kpre, vpre, bl(zf_), t_attn)
    y_rwkv, _ = _rwkv([bl(a) for a in main[6:6 + n_ops]], bl(zr_),
                      main[6 + n_ops].reshape(b, l // CHUNK, D_BRANCH), gnw, gnb, ones_bd, z_pre[0], t_rwkv)
    out = _out_proj(y_fox.reshape(b * l, D_BRANCH), y_rwkv.reshape(b * l, D_BRANCH), xf,
                    w_out[0].astype(BF16), row(final_norm_w), tm)
    return out.reshape(b, l, D_MODEL)
```

```python
import functools

import numpy as np
import jax
import jax.numpy as jnp
from jax import lax
from jax.experimental import pallas as pl
from jax.experimental.pallas import tpu as pltpu

F32 = jnp.float32
BF16 = jnp.bfloat16

D_MODEL = 1024
N_META = 16
HEAD_DIM = 64
N_HEADS = 8
D_BRANCH = N_HEADS * HEAD_DIM
N_PAIRS = N_HEADS // 2
RANK = 64
NORM_EPS = 1e-6
GN_EPS = 64e-5
KK_EPS = 1e-12
NEG = -1e30

LANES = 128
PREFIX_ROWS = 128
CHUNK = 64
SUB = 2 * CHUNK
PAIR = 2 * HEAD_DIM
VMEM_LIMIT = 56 * 1024 * 1024

N_WIDE = 8
COL_WA = N_WIDE * D_BRANCH
COL_FL = COL_WA + LANES
N_COLS = COL_FL + LANES

N_CPARTS = 3
ONES_LANE = (HEAD_DIM, 0)
ATTN_HEADS = 4


def _dot(a, b):
    return jnp.dot(a, b, preferred_element_type=F32)


def _pieces(x, n):
    out = []
    for _ in range(n - 1):
        p = x.astype(BF16)
        out.append(p)
        x = x - p.astype(F32)
    out.append(x.astype(BF16))
    return out


def _dot_exact_lhs(a_bf16, x, n):
    acc = None
    for p in _pieces(x, n):
        t = _dot(a_bf16, p)
        acc = t if acc is None else acc + t
    return acc


def _dot_x2(x, w_bf16):
    x_hi, x_lo = _pieces(x, 2)
    return _dot(x_hi, w_bf16) + _dot(x_lo, w_bf16)


def _dot_nt(a, b):
    return lax.dot_general(a, b, (((1,), (1,)), ((), ())), preferred_element_type=F32)


def _dot_tn(a, b):
    return lax.dot_general(a, b, (((0,), (0,)), ((), ())), preferred_element_type=F32)


def _softplus(x):
    return jnp.maximum(x, 0.0) + jnp.log(1.0 + jnp.exp(-jnp.abs(x)))


def _sigmoid(x):
    return 1.0 / (1.0 + jnp.exp(-x))


def _params(sem):
    return pltpu.CompilerParams(dimension_semantics=sem, vmem_limit_bytes=VMEM_LIMIT)


G_Q, G_K, G_V, G_ZF, G_R, G_RK, G_RV, G_ZR = range(N_WIDE)
RWKV_OPERANDS = ("rt", "at", "bt", "kt", "xv", "bh", "kh", "bonus")


def _in_proj_kernel(x_ref, nw_ref, w_ref,
                    mu_r_ref, mu_k_ref, mu_v_ref, mu_wa_ref, w0_ref, wup_ref,
                    a0_ref, aup_ref, kk_ref, ka_ref, rk_ref, ones_ref,
                    pr_ref, pk_ref, pv_ref, pwa_ref,
                    q_ref, k_ref, v_ref, zf_ref, zr_ref, fl_ref,
                    rt_ref, at_ref, bt_ref, kt_ref, xv_ref, bh_ref, kh_ref, bonus_ref, wend_ref,
                    lr_ref, lk_ref, lv_ref, lwa_ref,
                    sr_ref, sk_ref, sv_ref, swa_ref, *, tm, tiles_per_seq):
    i = pl.program_id(0)

    @pl.when(lax.rem(i, tiles_per_seq) == 0)
    def _():
        sr_ref[...] = pr_ref[...]
        sk_ref[...] = pk_ref[...]
        sv_ref[...] = pv_ref[...]
        swa_ref[...] = pwa_ref[...]

    x = x_ref[...]
    u = x * lax.rsqrt(jnp.mean(x * x, axis=-1, keepdims=True) + NORM_EPS) * nw_ref[...]
    ub = u.astype(BF16)

    def group(g):
        return _dot(ub, w_ref[:, g * D_BRANCH:(g + 1) * D_BRANCH])

    raw = [group(G_R), group(G_RK), group(G_RV), _dot(ub, w_ref[:, COL_WA:COL_WA + LANES])]
    ones_bd = ones_ref[...]

    def seg_sum(v):
        return _dot(v.astype(BF16), ones_bd)

    def shifted(cur, prev_row, mu_ref):
        first = lax.broadcasted_iota(jnp.int32, cur.shape, 0) == 0
        prev = jnp.where(first, prev_row, pltpu.roll(cur, 1, 0))
        return cur + mu_ref[...] * (prev - cur)

    r2 = lax.broadcasted_iota(jnp.int32, (SUB, SUB), 0)
    c2 = lax.broadcasted_iota(jnp.int32, (SUB, SUB), 1)
    tri = jnp.where((r2 >= c2) & (r2 // CHUNK == c2 // CHUNK), 1.0, 0.0).astype(BF16)
    outs = dict(zip(RWKV_OPERANDS, (rt_ref, at_ref, bt_ref, kt_ref, xv_ref, bh_ref, kh_ref, bonus_ref)))

    def rwkv_prepare():
        prev_rows = [sr_ref[...], sk_ref[...], sv_ref[...], swa_ref[...]]
        mus = (mu_r_ref, mu_k_ref, mu_v_ref, mu_wa_ref)
        xr, xk, xv, xwa = [shifted(c, p, mu) for c, p, mu in zip(raw, prev_rows, mus)]
        w_lin = w0_ref[...] + _dot_x2(jnp.tanh(xwa), wup_ref[...])
        a_lin = a0_ref[...] + _dot(xwa.astype(BF16), aup_ref[...])
        kk = xk * kk_ref[...]
        kk_ss = seg_sum(kk * kk)
        yield [c[tm - 1:tm, :] for c in raw]
        w = -_softplus(-w_lin) - 0.5
        ld = -jnp.exp(w)
        a = _sigmoid(a_lin)
        kk = kk * lax.rsqrt(kk_ss + KK_EPS)
        kmod = xk * (1.0 + (a - 1.0) * ka_ref[...])
        lw = jnp.concatenate([_dot_exact_lhs(tri, ld[sb * SUB:(sb + 1) * SUB, :], 2)
                              for sb in range(tm // SUB)], axis=0)
        rk_sum = seg_sum(xr * kmod * rk_ref[...])
        yield None
        w_inv = jnp.exp(-lw)
        bt = kk * a * w_inv
        kt = kmod * w_inv
        for name, val in (("rt", xr * jnp.exp(lw)), ("at", -kk * jnp.exp(lw - ld)), ("bt", bt), ("kt", kt),
                          ("xv", xv), ("bonus", rk_sum * xv)):
            outs[name][...] = val.astype(BF16)
        for c in range(tm // CHUNK):
            rows = slice(c * CHUNK, (c + 1) * CHUNK)
            w_c = jnp.exp(lw[(c + 1) * CHUNK - 1:(c + 1) * CHUNK, :])
            wend_ref[c:c + 1, :] = w_c
            bh_ref[rows, :] = (bt[rows, :] * w_c).astype(BF16)
            kh_ref[rows, :] = (kt[rows, :] * w_c).astype(BF16)
        yield None

    parts = rwkv_prepare()
    last_rows = next(parts)
    q_ref[...] = group(G_Q).astype(BF16)
    k_ref[...] = group(G_K).astype(BF16)
    next(parts)
    v_ref[...] = group(G_V).astype(BF16)
    zf_ref[...] = group(G_ZF).astype(BF16)
    next(parts)
    zr_ref[...] = group(G_ZR).astype(BF16)
    fl_ref[...] = _dot(ub, w_ref[:, COL_FL:COL_FL + LANES])

    for ref, rowv in zip((sr_ref, sk_ref, sv_ref, swa_ref), last_rows):
        ref[...] = rowv

    @pl.when(i == pl.num_programs(0) - 1)
    def _():
        for out, rowv in zip((lr_ref, lk_ref, lv_ref, lwa_ref), last_rows):
            out[...] = rowv


def _in_proj(rows, norm_w, w_all, prm, prev, tm, tiles_per_seq):
    n = rows.shape[0]
    tile = lambda width: pl.BlockSpec((tm, width), lambda i: (i, 0))
    const = lambda shape: pl.BlockSpec(shape, lambda i: (0,) * len(shape))
    row_w, row_n, up = const((1, D_BRANCH)), const((1, LANES)), const((LANES, D_BRANCH))
    wide = lambda dt: jax.ShapeDtypeStruct((n, D_BRANCH), dt)
    return pl.pallas_call(
        functools.partial(_in_proj_kernel, tm=tm, tiles_per_seq=tiles_per_seq),
        grid=(n // tm,),
        in_specs=[tile(D_MODEL), const((1, D_MODEL)),
                  pl.BlockSpec((D_MODEL, N_COLS), lambda i: (0, 0), pipeline_mode=pl.Buffered(1)),
                  row_w, row_w, row_w, row_n, row_w, up, row_w, up,
                  row_w, row_w, row_w, const((D_BRANCH, D_BRANCH)),
                  row_w, row_w, row_w, row_n],
        out_specs=[tile(D_BRANCH)] * 5 + [tile(LANES)] + [tile(D_BRANCH)] * len(RWKV_OPERANDS)
                  + [pl.BlockSpec((tm // CHUNK, D_BRANCH), lambda i: (i, 0)),
                     row_w, row_w, row_w, row_n],
        out_shape=[wide(BF16), wide(BF16), wide(BF16), wide(BF16), wide(BF16),
                   jax.ShapeDtypeStruct((n, LANES), F32)]
                  + [wide(BF16)] * len(RWKV_OPERANDS)
                  + [jax.ShapeDtypeStruct((n // CHUNK, D_BRANCH), F32)]
                  + [jax.ShapeDtypeStruct((1, D_BRANCH), F32)] * 3 + [jax.ShapeDtypeStruct((1, LANES), F32)],
        scratch_shapes=[pltpu.VMEM((1, D_BRANCH), F32), pltpu.VMEM((1, D_BRANCH), F32),
                        pltpu.VMEM((1, D_BRANCH), F32), pltpu.VMEM((1, LANES), F32)],
        compiler_params=_params(("arbitrary",)),
        name="in_proj",
    )(rows, norm_w, w_all, *prm, *prev)


def _select_matrices():
    pq = np.zeros((LANES, D_BRANCH), np.float32)
    pk = np.zeros((LANES, D_BRANCH), np.float32)
    for h in range(N_HEADS):
        base = (h // 2) * PAIR + (HEAD_DIM if h % 2 == 0 else 0)
        for part in range(N_CPARTS):
            pq[part * N_HEADS + h, base + part] = 1.0
            pq[N_CPARTS * N_HEADS + h, base + N_CPARTS + part] = 1.0
            pk[N_CPARTS * N_HEADS + h, base + part] = 1.0
            pk[part * N_HEADS + h, base + N_CPARTS + part] = -1.0
    return jnp.asarray(pq, BF16), jnp.asarray(pk, BF16)


def _fox_prep_kernel(q_ref, k_ref, v_ref, fl_ref, bf_ref, c0_ref, pq_ref, pk_ref,
                     qo_ref, ko_ref, vo_ref, cl_ref, carry_ref, *, tr, n_pad):
    i = pl.program_id(1)

    @pl.when(i == 0)
    def _():
        carry_ref[...] = c0_ref[...]

    x = fl_ref[0] + bf_ref[...]
    logf = jnp.minimum(x, 0.0) - jnp.log(1.0 + jnp.exp(-jnp.abs(x)))
    lane = lax.broadcasted_iota(jnp.int32, (tr, LANES), 1)
    row = lax.broadcasted_iota(jnp.int32, (tr, LANES), 0) + i * tr
    valid = lane < N_HEADS
    if n_pad:
        valid = valid & (row >= n_pad)
    logf = jnp.where(valid, logf, 0.0)
    r2 = lax.broadcasted_iota(jnp.int32, (tr, tr), 0)
    c2 = lax.broadcasted_iota(jnp.int32, (tr, tr), 1)
    tri = jnp.where(r2 >= c2, 1.0, 0.0).astype(BF16)
    cum = _dot_exact_lhs(tri, logf, 2) + carry_ref[...]
    carry_ref[...] = cum[tr - 1:tr, :]

    @pl.when(i == pl.num_programs(1) - 1)
    def _():
        cl_ref[0] = cum[tr - 1:tr, :]

    p1 = cum.astype(BF16).astype(F32)
    rem = cum - p1
    p2 = rem.astype(BF16).astype(F32)
    p3 = (rem - p2).astype(BF16).astype(F32)
    ones = jnp.where((lane >= N_CPARTS * N_HEADS) & (lane < (N_CPARTS + 1) * N_HEADS), 1.0, 0.0)
    cbits = p1 + pltpu.roll(p2, N_HEADS, 1) + pltpu.roll(p3, 2 * N_HEADS, 1) + ones
    cbits_k = cbits
    if n_pad:
        cbits_k = jnp.where((row < n_pad) & (lane < N_HEADS), -NEG, cbits)
    bias_q = _dot(cbits.astype(BF16), pq_ref[...]).astype(BF16)
    bias_k = _dot(cbits_k.astype(BF16), pk_ref[...]).astype(BF16)
    low = lane < HEAD_DIM
    sel_r = lax.broadcasted_iota(jnp.int32, (PAIR, PAIR), 0)
    sel_c = lax.broadcasted_iota(jnp.int32, (PAIR, PAIR), 1)
    eye = jnp.where(sel_r == sel_c, 1.0, 0.0).astype(BF16)
    vrow = lax.broadcasted_iota(jnp.int32, (PAIR, tr), 0)
    for pr in range(N_PAIRS):
        sl = slice(pr * PAIR, (pr + 1) * PAIR)
        q_sc = q_ref[0, :, sl] * jnp.asarray(0.125, BF16)
        k_pr = k_ref[0, :, sl]
        v_t = _dot_nt(eye, v_ref[0, :, sl])
        for half in range(2):
            h = 2 * pr + half
            own = low if half == 0 else jnp.logical_not(low)
            qo_ref[0, h] = jnp.where(own, q_sc, bias_q[:, sl])
            ko_ref[0, h] = jnp.where(own, k_pr, bias_k[:, sl])
            own_v = (vrow < HEAD_DIM) if half == 0 else (vrow >= HEAD_DIM)
            vo_ref[0, h, 0] = jnp.where(vrow == ONES_LANE[half], 1.0, jnp.where(own_v, v_t, 0.0)).astype(BF16)


def _fox_prep(q, k, v, fl, bf_pad, c0, pq, pk, tr, n_pad):
    b, l, _ = q.shape
    wide = pl.BlockSpec((1, tr, D_BRANCH), lambda bi, i: (bi, i, 0))
    head_out = pl.BlockSpec((1, N_HEADS, tr, LANES), lambda bi, i: (bi, 0, i, 0))
    row128 = pl.BlockSpec((1, LANES), lambda bi, i: (0, 0))
    sel = pl.BlockSpec((LANES, D_BRANCH), lambda bi, i: (0, 0))
    return pl.pallas_call(
        functools.partial(_fox_prep_kernel, tr=tr, n_pad=n_pad),
        grid=(b, l // tr),
        in_specs=[wide, wide, wide,
                  pl.BlockSpec((1, tr, LANES), lambda bi, i: (bi, i, 0)),
                  row128, row128, sel, sel],
        out_specs=[head_out, head_out,
                   pl.BlockSpec((1, N_HEADS, 1, PAIR, tr), lambda bi, i: (bi, 0, i, 0, 0)),
                   pl.BlockSpec((1, 1, LANES), lambda bi, i: (bi, 0, 0))],
        out_shape=[jax.ShapeDtypeStruct((b, N_HEADS, l, LANES), BF16)] * 2
                  + [jax.ShapeDtypeStruct((b, N_HEADS, l // tr, PAIR, tr), BF16),
                     jax.ShapeDtypeStruct((b, 1, LANES), F32)],
        scratch_shapes=[pltpu.VMEM((1, LANES), F32)],
        compiler_params=_params(("parallel", "arbitrary")),
        name="fox_prep",
    )(q, k, v, fl, bf_pad, c0, pq, pk)


def _attn_kernel(q_ref, kpre_ref, vpre_ref, k_ref, v_ref, z_ref, o_ref, s_ref, *, tq):
    qi = pl.program_id(2)
    heads = range(ATTN_HEADS)
    qs = [q_ref[0, h] for h in heads]

    def rowmax(s):
        return jnp.max(s, axis=0, keepdims=True)

    def k_chunk(h, idx):
        return k_ref[0, h, pl.ds(pl.multiple_of(idx * tq, tq), tq), :]

    def v_chunk(h, idx):
        return v_ref[0, h, idx]

    def scores(idx):
        return [_dot_nt(k_chunk(h, idx), qs[h]) for h in heads]

    def softmax_pv(carry, s, rmax, idx):
        out = []
        for h in heads:
            m, acc = carry[h]
            m_new = jnp.maximum(m, rmax[h])
            p = jnp.exp(s[h] - m_new).astype(BF16)
            out.append((m_new, jnp.exp(m - m_new) * acc + _dot(v_chunk(h, idx), p)))
        return out

    key = lax.broadcasted_iota(jnp.int32, (tq, tq), 0)
    qry = lax.broadcasted_iota(jnp.int32, (tq, tq), 1)
    s_pre = [_dot_nt(kpre_ref[0, h], qs[h]) for h in heads]
    s_dia = [jnp.where(key <= qry, s, NEG) for s in scores(qi)]
    s_nxt = scores(0)
    carry = []
    for h in heads:
        m0 = jnp.maximum(rowmax(s_pre[h]), rowmax(s_dia[h]))
        p_pre = jnp.exp(s_pre[h] - m0).astype(BF16)
        p_dia = jnp.exp(s_dia[h] - m0).astype(BF16)
        carry.append((m0, _dot(vpre_ref[0, h, 0], p_pre) + _dot(v_chunk(h, qi), p_dia)))
        s_ref[h] = s_nxt[h]
    rmax = [rowmax(s) for s in s_nxt]

    def step2(t, state):
        carry, rmax = state
        k0 = 2 * t
        s_a = [s_ref[h] for h in heads]
        s_b = scores(k0 + 1)
        carry = softmax_pv(carry, s_a, rmax, k0)
        rmax_b = [rowmax(s) for s in s_b]
        s_c = scores(k0 + 2)
        carry = softmax_pv(carry, s_b, rmax_b, k0 + 1)
        for h in heads:
            s_ref[h] = s_c[h]
        return carry, [rowmax(s) for s in s_c]

    def step1(k0, state):
        carry, rmax = state
        s_a = [s_ref[h] for h in heads]
        s_b = scores(k0 + 1)
        carry = softmax_pv(carry, s_a, rmax, k0)
        for h in heads:
            s_ref[h] = s_b[h]
        return carry, [rowmax(s) for s in s_b]

    steps = jnp.maximum(qi - 1, 0)
    pairs = steps // 2
    state = lax.fori_loop(0, pairs, step2, (carry, rmax))
    carry, rmax = lax.fori_loop(2 * pairs, steps, step1, state)

    some = qi > 0
    last = jnp.maximum(qi - 1, 0)
    s_fin = [jnp.where(some, s_ref[h], NEG) for h in heads]
    r_fin = [jnp.where(some, r, NEG) for r in rmax]
    accs = [acc for _, acc in softmax_pv(carry, s_fin, r_fin, last)]
    vrow = lax.broadcasted_iota(jnp.int32, (PAIR, tq), 0)
    for pr in range(ATTN_HEADS // 2):
        acc0, acc1 = accs[2 * pr], accs[2 * pr + 1]
        l0 = acc0[ONES_LANE[0]:ONES_LANE[0] + 1, :]
        l1 = acc1[ONES_LANE[1]:ONES_LANE[1] + 1, :]
        o_t = jnp.where(vrow < HEAD_DIM, acc0 / l0, acc1 / l1)
        z = z_ref[0, :, pr * PAIR:(pr + 1) * PAIR].astype(F32)
        o_ref[0, :, pr * PAIR:(pr + 1) * PAIR] = (o_t.T * (z * _sigmoid(z))).astype(o_ref.dtype)


def _fox_attn(qp, kp, vp, kpre, vpre, zf, tq):
    b, _, l, _ = qp.shape
    g = ATTN_HEADS
    out_block = pl.BlockSpec((1, tq, g * HEAD_DIM), lambda bi, p, qi: (bi, qi, p))
    assert vp.shape[-1] == tq and vpre.shape[-1] == PREFIX_ROWS
    pre = pl.BlockSpec((1, g, PREFIX_ROWS, LANES), lambda bi, p, qi: (0, p, 0, 0))
    pre_v = pl.BlockSpec((1, g, 1, PAIR, PREFIX_ROWS), lambda bi, p, qi: (0, p, 0, 0, 0))
    full = pl.BlockSpec((1, g, l, LANES), lambda bi, p, qi: (bi, p, 0, 0))
    full_v = pl.BlockSpec((1, g, l // tq, PAIR, tq), lambda bi, p, qi: (bi, p, 0, 0, 0))
    return pl.pallas_call(
        functools.partial(_attn_kernel, tq=tq),
        grid=(b, N_HEADS // g, l // tq),
        in_specs=[pl.BlockSpec((1, g, tq, LANES), lambda bi, p, qi: (bi, p, qi, 0)),
                  pre, pre_v, full, full_v, out_block],
        out_specs=out_block,
        out_shape=jax.ShapeDtypeStruct((b, l, D_BRANCH), BF16),
        scratch_shapes=[pltpu.VMEM((g, tq, tq), F32)],
        compiler_params=_params(("parallel", "parallel", "parallel")),
        name="fox_attn",
    )(qp, kpre, vpre, kp, vp, zf)


def _rwkv_kernel(rt_ref, at_ref, bt_ref, kt_ref, xv_ref, bh_ref, kh_ref, bonus_ref, z_ref, wend_ref,
                 gnw_ref, gnb_ref, ones_ref, z0_ref,
                 y_ref, zf_ref, state_ref, yacc_ref, *, tr):
    t = pl.program_id(1)
    last = t == pl.num_programs(1) - 1

    @pl.when(t == 0)
    def _():
        state_ref[...] = z0_ref[...]

    ones_bd = ones_ref[...]

    def seg_sum(x):
        return _dot(x.astype(BF16), ones_bd)

    operand = dict(rt=rt_ref, at=at_ref, bt=bt_ref, kt=kt_ref, xv=xv_ref, bh=bh_ref, kh=kh_ref)

    col = lax.broadcasted_iota(jnp.int32, (CHUNK, PAIR), 1)
    trow = lax.broadcasted_iota(jnp.int32, (CHUNK, PAIR), 0)
    tcol = col % CHUNK
    head_a = col < HEAD_DIM
    strict = trow > tcol
    incl = trow >= tcol
    eye_sbs = jnp.where(trow == tcol, 1.0, 0.0).astype(F32)
    sq_r = lax.broadcasted_iota(jnp.int32, (PAIR, PAIR), 0)
    sq_c = lax.broadcasted_iota(jnp.int32, (PAIR, PAIR), 1)
    same_head = (sq_r // HEAD_DIM) == (sq_c // HEAD_DIM)
    eye_sq = sq_r == sq_c

    def stack(x):
        zero = jnp.zeros_like(x)
        return jnp.concatenate([jnp.where(head_a, x, zero), jnp.where(head_a, zero, x)], axis=0)

    def fold(sq):
        sq = jnp.where(same_head, sq, 0.0)
        return sq[:CHUNK] + sq[CHUNK:]

    def tile(name, c, p):
        return operand[name][0, c * CHUNK:(c + 1) * CHUNK, p * PAIR:(p + 1) * PAIR]

    n_chunks = tr // CHUNK
    units =[(c, p) for c in range(n_chunks) for p in range(N_PAIRS)]
    nu = range(len(units))
    wc = [wend_ref[0, c:c + 1, p * PAIR:(p + 1) * PAIR] for c, p in units]
    rt, at, bt, kt, xv, bh, kh = ([tile(name, c, p) for c, p in units]
                                  for name in ("rt", "at", "bt", "kt", "xv", "bh", "kh"))
    bt_s = [stack(x) for x in bt]
    kt_s = [stack(x) for x in kt]
    xv_s = [stack(x) for x in xv]

    ar = [jnp.concatenate([at[u], rt[u]], axis=0) for u in nu]
    x_b = [_dot_nt(ar[u], bt_s[u]) for u in nu]
    x_k = [_dot_nt(ar[u], kt_s[u]) for u in nu]
    a_ab = [jnp.where(strict, x[:CHUNK], 0.0) for x in x_b]
    a_rb = [jnp.where(incl, x[CHUNK:], 0.0).astype(BF16) for x in x_b]
    a_ak = [jnp.where(strict, x[:CHUNK], 0.0).astype(BF16) for x in x_k]
    a_rk = [jnp.where(incl, x[CHUNK:], 0.0).astype(BF16) for x in x_k]

    ab = [a.astype(BF16) for a in a_ab]
    pw = [_dot(ab[u], stack(ab[u])) for u in nu]
    tinv = [eye_sbs + a_ab[u] for u in nu]
    for level in range(1, 6):
        pb = [x.astype(BF16) for x in pw]
        if level < 5:
            both = [_dot(jnp.concatenate([tinv[u].astype(BF16), pb[u]], axis=0), stack(pb[u])) for u in nu]
            tinv = [tinv[u] + both[u][:CHUNK] for u in nu]
            pw = [both[u][CHUNK:] for u in nu]
        else:
            tinv = [tinv[u] + _dot(tinv[u].astype(BF16), stack(pb[u])) for u in nu]
    tb = [x.astype(BF16) for x in tinv]

    ap = [_dot(tb[u], stack(at[u])).astype(BF16) for u in nu]
    akv = [_dot(a_ak[u], xv_s[u]).astype(BF16) for u in nu]
    vp = [_dot(tb[u], stack(akv[u])).astype(BF16) for u in nu]
    ap_s = [stack(x) for x in ap]
    vp_s = [stack(x) for x in vp]
    m_sbs = [fold(jnp.where(eye_sq, wc[u], 0.0) + _dot_tn(bh[u], ap[u])).astype(BF16) for u in nu]
    g_sbs = [fold(_dot_tn(jnp.concatenate([bh[u], kh[u]], axis=0), jnp.concatenate([vp[u], xv[u]], axis=0)))
             for u in nu]
    rp = [(rt[u].astype(F32) + _dot(a_rb[u], ap_s[u])).astype(BF16) for u in nu]
    y0 = [_dot(jnp.concatenate([a_rb[u], a_rk[u]], axis=1), jnp.concatenate([vp_s[u], xv_s[u]], axis=0))
          for u in nu]

    for u, (c, p) in enumerate(units):
        z_bd = stack(state_ref[p].astype(BF16))
        both = _dot(jnp.concatenate([rp[u], m_sbs[u]], axis=0), z_bd)
        yacc_ref[c * CHUNK:(c + 1) * CHUNK, p * PAIR:(p + 1) * PAIR] = both[:CHUNK] + y0[u]
        state_ref[p] = both[CHUNK:] + g_sbs[u]

    @pl.when(last)
    def _():
        zf_ref[0] = state_ref[...]

    inv_n = 1.0 / HEAD_DIM
    y = yacc_ref[...]
    mean = seg_sum(y) * inv_n
    d = y - mean
    var = seg_sum(d * d) * inv_n
    yn = d * lax.rsqrt(var + GN_EPS) * gnw_ref[...] + gnb_ref[...]
    z = z_ref[0].astype(F32)
    y_ref[0] = ((yn + bonus_ref[0].astype(F32)) * (z * _sigmoid(z))).astype(y_ref.dtype)


def _rwkv(ops, z, wend, gn_w, gn_b, ones_bd, z0, tr):
    b, l, _ = z.shape
    wide = pl.BlockSpec((1, tr, D_BRANCH), lambda bi, t: (bi, t, 0))
    row_w = pl.BlockSpec((1, D_BRANCH), lambda bi, t: (0, 0))
    ones = pl.BlockSpec((D_BRANCH, D_BRANCH), lambda bi, t: (0, 0))
    st_in = pl.BlockSpec((N_PAIRS, CHUNK, PAIR), lambda bi, t: (0, 0, 0))
    st_out = pl.BlockSpec((1, N_PAIRS, CHUNK, PAIR), lambda bi, t: (bi, 0, 0, 0))
    return pl.pallas_call(
        functools.partial(_rwkv_kernel, tr=tr),
        grid=(b, l // tr),
        in_specs=[wide] * (len(RWKV_OPERANDS) + 1)
                 + [pl.BlockSpec((1, tr // CHUNK, D_BRANCH), lambda bi, t: (bi, t, 0)),
                    row_w, row_w, ones, st_in],
        out_specs=[wide, st_out],
        out_shape=[jax.ShapeDtypeStruct((b, l, D_BRANCH), BF16),
                   jax.ShapeDtypeStruct((b, N_PAIRS, CHUNK, PAIR), F32)],
        scratch_shapes=[pltpu.VMEM((N_PAIRS, CHUNK, PAIR), F32), pltpu.VMEM((tr, D_BRANCH), F32)],
        compiler_params=_params(("parallel", "arbitrary")),
        name="rwkv",
    )(*ops, z, wend, gn_w, gn_b, ones_bd, z0)


def _out_proj_kernel(yf_ref, yr_ref, x_ref, w_ref, fnw_ref, out_ref):
    mix = _dot(yf_ref[...], w_ref[:D_BRANCH, :]) + _dot(yr_ref[...], w_ref[D_BRANCH:, :])
    h = x_ref[...] + mix
    out_ref[...] = h * lax.rsqrt(jnp.mean(h * h, axis=-1, keepdims=True) + NORM_EPS) * fnw_ref[...]


def _out_proj(yf, yr, x, w_out, fnw, tm):
    n = x.shape[0]
    half = pl.BlockSpec((tm, D_BRANCH), lambda i: (i, 0))
    full = pl.BlockSpec((tm, D_MODEL), lambda i: (i, 0))
    return pl.pallas_call(
        _out_proj_kernel,
        grid=(n // tm,),
        in_specs=[half, half, full,
                  pl.BlockSpec((D_MODEL, D_MODEL), lambda i: (0, 0)),
                  pl.BlockSpec((1, D_MODEL), lambda i: (0, 0))],
        out_specs=full,
        out_shape=jax.ShapeDtypeStruct((n, D_MODEL), F32),
        compiler_params=_params(("parallel",)),
        name="out_proj",
    )(yf, yr, x, w_out, fnw)


def _tiles(b, l):
    rows = b * l
    tm = 512 if rows % 512 == 0 else 256
    return tm, 512, 512, 512


def kernel(x, meta, norm_w, w_in, b_f, mu_shift, w0, w_up, a0, a_up, k_k, k_a, r_k, gn_w, gn_b,
           w_out, final_norm_w):
    b, l, d = x.shape
    assert d == D_MODEL and norm_w.shape[0] == 1 and l % 256 == 0
    tm, t_prep, t_attn, t_rwkv = _tiles(b, l)

    wi = w_in[0].astype(BF16)
    o = 0
    cols = {}
    for name, width in (("q", D_BRANCH), ("k", D_BRANCH), ("v", D_BRANCH), ("fl", N_HEADS), ("zf", D_BRANCH),
                        ("r", D_BRANCH), ("rk", D_BRANCH), ("rv", D_BRANCH), ("wd", RANK), ("ad", RANK),
                        ("zr", D_BRANCH)):
        cols[name] = wi[:, o:o + width]
        o += width
    w_all = jnp.concatenate(
        [cols[n] for n in ("q", "k", "v", "zf", "r", "rk", "rv", "zr", "wd", "ad", "fl")]
        + [jnp.zeros((D_MODEL, LANES - N_HEADS), BF16)], axis=1)

    row = lambda vec: vec.reshape(1, -1).astype(F32)
    mu = mu_shift[0]
    mu_r, mu_k, mu_v = (row(mu[i * D_BRANCH:(i + 1) * D_BRANCH]) for i in range(3))
    mu_wa = row(mu[3 * D_BRANCH:])
    zeros_up = jnp.zeros((RANK, D_BRANCH), F32)
    wup_pad = jnp.concatenate([w_up[0], zeros_up], axis=0)
    aup_pad = jnp.concatenate([zeros_up, a_up[0]], axis=0)
    hid = np.arange(D_BRANCH) // HEAD_DIM
    ones_bd = jnp.asarray(hid[:, None] == hid[None, :], BF16)
    shift_prm = (mu_r, mu_k, mu_v, mu_wa, row(w0[0]), wup_pad.astype(BF16), row(a0[0]), aup_pad.astype(BF16),
                 row(k_k[0]), row(k_a[0]), row(r_k[0]), ones_bd)
    gnw, gnb = row(gn_w[0]), row(gn_b[0])
    bf_pad = jnp.concatenate([b_f[0], jnp.zeros((LANES - N_HEADS,), F32)]).reshape(1, LANES)
    pq, pk = _select_matrices()
    nw = row(norm_w[0])
    n_ops = len(RWKV_OPERANDS)

    pre_rows = jnp.concatenate([jnp.zeros((PREFIX_ROWS - N_META, D_MODEL), F32), meta.astype(F32)], axis=0)
    zero_w = jnp.zeros((1, D_BRANCH), F32)
    pre = _in_proj(pre_rows, nw, w_all, shift_prm, (zero_w, zero_w, zero_w, jnp.zeros((1, LANES), F32)),
                   PREFIX_ROWS, 1)
    pq_, pk_, pv_, _, pzr_, pfl_ = pre[:6]
    pre_ops, pre_wend, last_raw = pre[6:6 + n_ops], pre[6 + n_ops], pre[7 + n_ops:]
    lead = lambda a: a[None]
    _, kpre, vpre, c_pre = _fox_prep(lead(pq_), lead(pk_), lead(pv_), lead(pfl_), bf_pad,
                                     jnp.zeros((1, LANES), F32), pq, pk, PREFIX_ROWS, PREFIX_ROWS - N_META)
    _, z_pre = _rwkv([lead(a) for a in pre_ops], lead(pzr_), lead(pre_wend), gnw, gnb, ones_bd,
                     jnp.zeros((N_PAIRS, CHUNK, PAIR), F32), PREFIX_ROWS)

    xf = x.reshape(b * l, D_MODEL)
    main = _in_proj(xf, nw, w_all, shift_prm, last_raw, tm, l // tm)
    q_, k_, v_, zf_, zr_, fl_ = main[:6]
    bl = lambda a: a.reshape(b, l, a.shape[-1])
    qp, kp, vp, _ = _fox_prep(bl(q_), bl(k_), bl(v_), bl(fl_), bf_pad, c_pre[0], pq, pk, t_prep, 0)
    y_fox = _fox_attn(qp, kp, vp, kpre, vpre, bl(zf_), t_attn)
    y_rwkv, _ = _rwkv([bl(a) for a in main[6:6 + n_ops]], bl(zr_),
                      main[6 + n_ops].reshape(b, l // CHUNK, D_BRANCH), gnw, gnb, ones_bd, z_pre[0], t_rwkv)
    out = _out_proj(y_fox.reshape(b * l, D_BRANCH), y_rwkv.reshape(b * l, D_BRANCH), xf,
                    w_out[0].astype(BF16), row(final_norm_w), tm)
    return out.reshape(b, l, D_MODEL)
```

```python
import functools

import numpy as np
import jax
import jax.numpy as jnp
from jax import lax
from jax.experimental import pallas as pl
from jax.experimental.pallas import tpu as pltpu

F32 = jnp.float32
BF16 = jnp.bfloat16

D_MODEL = 1024
N_META = 16
HEAD_DIM = 64
N_HEADS = 8
D_BRANCH = N_HEADS * HEAD_DIM
N_PAIRS = N_HEADS // 2
RANK = 64
NORM_EPS = 1e-6
GN_EPS = 64e-5
KK_EPS = 1e-12
NEG = -1e30

LANES = 128
PREFIX_ROWS = 128
CHUNK = 64
SUB = 2 * CHUNK
PAIR = 2 * HEAD_DIM
VMEM_LIMIT = 56 * 1024 * 1024

N_WIDE = 8
COL_WA = N_WIDE * D_BRANCH
COL_FL = COL_WA + LANES
N_COLS = COL_FL + LANES

N_CPARTS = 3
ONES_LANE = (HEAD_DIM, 0)
ATTN_HEADS = 4


def _dot(a, b):
    return jnp.dot(a, b, preferred_element_type=F32)


def _pieces(x, n):
    out = []
    for _ in range(n - 1):
        p = x.astype(BF16)
        out.append(p)
        x = x - p.astype(F32)
    out.append(x.astype(BF16))
    return out


def _dot_exact_lhs(a_bf16, x, n):
    acc = None
    for p in _pieces(x, n):
        t = _dot(a_bf16, p)
        acc = t if acc is None else acc + t
    return acc


def _dot_x2(x, w_bf16):
    x_hi, x_lo = _pieces(x, 2)
    return _dot(x_hi, w_bf16) + _dot(x_lo, w_bf16)


def _dot_nt(a, b):
    return lax.dot_general(a, b, (((1,), (1,)), ((), ())), preferred_element_type=F32)


def _dot_tn(a, b):
    return lax.dot_general(a, b, (((0,), (0,)), ((), ())), preferred_element_type=F32)


def _softplus(x):
    return jnp.maximum(x, 0.0) + jnp.log(1.0 + jnp.exp(-jnp.abs(x)))


def _sigmoid(x):
    return 1.0 / (1.0 + jnp.exp(-x))


def _params(sem):
    return pltpu.CompilerParams(dimension_semantics=sem, vmem_limit_bytes=VMEM_LIMIT)


G_Q, G_K, G_V, G_ZF, G_R, G_RK, G_RV, G_ZR = range(N_WIDE)
RWKV_OPERANDS = ("rt", "at", "bt", "kt", "xv", "bh", "kh", "bonus")


def _in_proj_kernel(x_ref, nw_ref, w_ref,
                    mu_r_ref, mu_k_ref, mu_v_ref, mu_wa_ref, w0_ref, wup_ref,
                    a0_ref, aup_ref, kk_ref, ka_ref, rk_ref, ones_ref,
                    pr_ref, pk_ref, pv_ref, pwa_ref,
                    q_ref, k_ref, v_ref, zf_ref, zr_ref, fl_ref,
                    rt_ref, at_ref, bt_ref, kt_ref, xv_ref, bh_ref, kh_ref, bonus_ref, wend_ref,
                    lr_ref, lk_ref, lv_ref, lwa_ref,
                    sr_ref, sk_ref, sv_ref, swa_ref, *, tm, tiles_per_seq):
    i = pl.program_id(0)

    @pl.when(lax.rem(i, tiles_per_seq) == 0)
    def _():
        sr_ref[...] = pr_ref[...]
        sk_ref[...] = pk_ref[...]
        sv_ref[...] = pv_ref[...]
        swa_ref[...] = pwa_ref[...]

    x = x_ref[...]
    u = x * lax.rsqrt(jnp.mean(x * x, axis=-1, keepdims=True) + NORM_EPS) * nw_ref[...]
    ub = u.astype(BF16)

    def group(g):
        return _dot(ub, w_ref[:, g * D_BRANCH:(g + 1) * D_BRANCH])

    raw = [group(G_R), group(G_RK), group(G_RV), _dot(ub, w_ref[:, COL_WA:COL_WA + LANES])]
    ones_bd = ones_ref[...]

    def seg_sum(v):
        return _dot(v.astype(BF16), ones_bd)

    def shifted(cur, prev_row, mu_ref):
        first = lax.broadcasted_iota(jnp.int32, cur.shape, 0) == 0
        prev = jnp.where(first, prev_row, pltpu.roll(cur, 1, 0))
        return cur + mu_ref[...] * (prev - cur)

    r2 = lax.broadcasted_iota(jnp.int32, (SUB, SUB), 0)
    c2 = lax.broadcasted_iota(jnp.int32, (SUB, SUB), 1)
    tri = jnp.where((r2 >= c2) & (r2 // CHUNK == c2 // CHUNK), 1.0, 0.0).astype(BF16)
    outs = dict(zip(RWKV_OPERANDS, (rt_ref, at_ref, bt_ref, kt_ref, xv_ref, bh_ref, kh_ref, bonus_ref)))

    def rwkv_prepare():
        prev_rows = [sr_ref[...], sk_ref[...], sv_ref[...], swa_ref[...]]
        mus = (mu_r_ref, mu_k_ref, mu_v_ref, mu_wa_ref)
        xr, xk, xv, xwa = [shifted(c, p, mu) for c, p, mu in zip(raw, prev_rows, mus)]
        w_lin = w0_ref[...] + _dot_x2(jnp.tanh(xwa), wup_ref[...])
        a_lin = a0_ref[...] + _dot(xwa.astype(BF16), aup_ref[...])
        kk = xk * kk_ref[...]
        kk_ss = seg_sum(kk * kk)
        yield [c[tm - 1:tm, :] for c in raw]
        w = -_softplus(-w_lin) - 0.5
        ld = -jnp.exp(w)
        a = _sigmoid(a_lin)
        kk = kk * lax.rsqrt(kk_ss + KK_EPS)
        kmod = xk * (1.0 + (a - 1.0) * ka_ref[...])
        lw = jnp.concatenate([_dot_exact_lhs(tri, ld[sb * SUB:(sb + 1) * SUB, :], 2)
                              for sb in range(tm // SUB)], axis=0)
        rk_sum = seg_sum(xr * kmod * rk_ref[...])
        yield None
        w_inv = jnp.exp(-lw)
        bt = kk * a * w_inv
        kt = kmod * w_inv
        for name, val in (("rt", xr * jnp.exp(lw)), ("at", -kk * jnp.exp(lw - ld)), ("bt", bt), ("kt", kt),
                          ("xv", xv), ("bonus", rk_sum * xv)):
            outs[name][...] = val.astype(BF16)
        for c in range(tm // CHUNK):
            rows = slice(c * CHUNK, (c + 1) * CHUNK)
            w_c = jnp.exp(lw[(c + 1) * CHUNK - 1:(c + 1) * CHUNK, :])
            wend_ref[c:c + 1, :] = w_c
            bh_ref[rows, :] = (bt[rows, :] * w_c).astype(BF16)
            kh_ref[rows, :] = (kt[rows, :] * w_c).astype(BF16)
        yield None

    parts = rwkv_prepare()
    last_rows = next(parts)
    q_ref[...] = group(G_Q).astype(BF16)
    k_ref[...] = group(G_K).astype(BF16)
    next(parts)
    v_ref[...] = group(G_V).astype(BF16)
    zf_ref[...] = group(G_ZF).astype(BF16)
    next(parts)
    zr_ref[...] = group(G_ZR).astype(BF16)
    fl_ref[...] = _dot(ub, w_ref[:, COL_FL:COL_FL + LANES])

    for ref, rowv in zip((sr_ref, sk_ref, sv_ref, swa_ref), last_rows):
        ref[...] = rowv

    @pl.when(i == pl.num_programs(0) - 1)
    def _():
        for out, rowv in zip((lr_ref, lk_ref, lv_ref, lwa_ref), last_rows):
            out[...] = rowv


def _in_proj(rows, norm_w, w_all, prm, prev, tm, tiles_per_seq):
    n = rows.shape[0]
    tile = lambda width: pl.BlockSpec((tm, width), lambda i: (i, 0))
    const = lambda shape: pl.BlockSpec(shape, lambda i: (0,) * len(shape))
    row_w, row_n, up = const((1, D_BRANCH)), const((1, LANES)), const((LANES, D_BRANCH))
    wide = lambda dt: jax.ShapeDtypeStruct((n, D_BRANCH), dt)
    return pl.pallas_call(
        functools.partial(_in_proj_kernel, tm=tm, tiles_per_seq=tiles_per_seq),
        grid=(n // tm,),
        in_specs=[tile(D_MODEL), const((1, D_MODEL)),
                  pl.BlockSpec((D_MODEL, N_COLS), lambda i: (0, 0), pipeline_mode=pl.Buffered(1)),
                  row_w, row_w, row_w, row_n, row_w, up, row_w, up,
                  row_w, row_w, row_w, const((D_BRANCH, D_BRANCH)),
                  row_w, row_w, row_w, row_n],
        out_specs=[tile(D_BRANCH)] * 5 + [tile(LANES)] + [tile(D_BRANCH)] * len(RWKV_OPERANDS)
                  + [pl.BlockSpec((tm // CHUNK, D_BRANCH), lambda i: (i, 0)),
                     row_w, row_w, row_w, row_n],
        out_shape=[wide(BF16), wide(BF16), wide(BF16), wide(BF16), wide(BF16),
                   jax.ShapeDtypeStruct((n, LANES), F32)]
                  + [wide(BF16)] * len(RWKV_OPERANDS)
                  + [jax.ShapeDtypeStruct((n // CHUNK, D_BRANCH), F32)]
                  + [jax.ShapeDtypeStruct((1, D_BRANCH), F32)] * 3 + [jax.ShapeDtypeStruct((1, LANES), F32)],
        scratch_shapes=[pltpu.VMEM((1, D_BRANCH), F32), pltpu.VMEM((1, D_BRANCH), F32),
                        pltpu.VMEM((1, D_BRANCH), F32), pltpu.VMEM((1, LANES), F32)],
        compiler_params=_params(("arbitrary",)),
        name="in_proj",
    )(rows, norm_w, w_all, *prm, *prev)


def _select_matrices():
    pq = np.zeros((LANES, D_BRANCH), np.float32)
    pk = np.zeros((LANES, D_BRANCH), np.float32)
    for h in range(N_HEADS):
        base = (h // 2) * PAIR + (HEAD_DIM if h % 2 == 0 else 0)
        for part in range(N_CPARTS):
            pq[part * N_HEADS + h, base + part] = 1.0
            pq[N_CPARTS * N_HEADS + h, base + N_CPARTS + part] = 1.0
            pk[N_CPARTS * N_HEADS + h, base + part] = 1.0
            pk[part * N_HEADS + h, base + N_CPARTS + part] = -1.0
    return jnp.asarray(pq, BF16), jnp.asarray(pk, BF16)


def _fox_prep_kernel(q_ref, k_ref, v_ref, fl_ref, bf_ref, c0_ref, pq_ref, pk_ref,
                     qo_ref, ko_ref, vo_ref, cl_ref, carry_ref, *, tr, n_pad):
    i = pl.program_id(1)

    @pl.when(i == 0)
    def _():
        carry_ref[...] = c0_ref[...]

    x = fl_ref[0] + bf_ref[...]
    logf = jnp.minimum(x, 0.0) - jnp.log(1.0 + jnp.exp(-jnp.abs(x)))
    lane = lax.broadcasted_iota(jnp.int32, (tr, LANES), 1)
    row = lax.broadcasted_iota(jnp.int32, (tr, LANES), 0) + i * tr
    valid = lane < N_HEADS
    if n_pad:
        valid = valid & (row >= n_pad)
    logf = jnp.where(valid, logf, 0.0)
    r2 = lax.broadcasted_iota(jnp.int32, (tr, tr), 0)
    c2 = lax.broadcasted_iota(jnp.int32, (tr, tr), 1)
    tri = jnp.where(r2 >= c2, 1.0, 0.0).astype(BF16)
    cum = _dot_exact_lhs(tri, logf, 2) + carry_ref[...]
    carry_ref[...] = cum[tr - 1:tr, :]

    @pl.when(i == pl.num_programs(1) - 1)
    def _():
        cl_ref[0] = cum[tr - 1:tr, :]

    p1 = cum.astype(BF16).astype(F32)
    rem = cum - p1
    p2 = rem.astype(BF16).astype(F32)
    p3 = (rem - p2).astype(BF16).astype(F32)
    ones = jnp.where((lane >= N_CPARTS * N_HEADS) & (lane < (N_CPARTS + 1) * N_HEADS), 1.0, 0.0)
    cbits = p1 + pltpu.roll(p2, N_HEADS, 1) + pltpu.roll(p3, 2 * N_HEADS, 1) + ones
    cbits_k = cbits
    if n_pad:
        cbits_k = jnp.where((row < n_pad) & (lane < N_HEADS), -NEG, cbits)
    bias_q = _dot(cbits.astype(BF16), pq_ref[...]).astype(BF16)
    bias_k = _dot(cbits_k.astype(BF16), pk_ref[...]).astype(BF16)
    low = lane < HEAD_DIM
    sel_r = lax.broadcasted_iota(jnp.int32, (PAIR, PAIR), 0)
    sel_c = lax.broadcasted_iota(jnp.int32, (PAIR, PAIR), 1)
    eye = jnp.where(sel_r == sel_c, 1.0, 0.0).astype(BF16)
    vrow = lax.broadcasted_iota(jnp.int32, (PAIR, tr), 0)
    for pr in range(N_PAIRS):
        sl = slice(pr * PAIR, (pr + 1) * PAIR)
        q_sc = q_ref[0, :, sl] * jnp.asarray(0.125, BF16)
        k_pr = k_ref[0, :, sl]
        v_t = _dot_nt(eye, v_ref[0, :, sl])
        for half in range(2):
            h = 2 * pr + half
            own = low if half == 0 else jnp.logical_not(low)
            qo_ref[0, h] = jnp.where(own, q_sc, bias_q[:, sl])
            ko_ref[0, h] = jnp.where(own, k_pr, bias_k[:, sl])
            own_v = (vrow < HEAD_DIM) if half == 0 else (vrow >= HEAD_DIM)
            vo_ref[0, h, 0] = jnp.where(vrow == ONES_LANE[half], 1.0, jnp.where(own_v, v_t, 0.0)).astype(BF16)


def _fox_prep(q, k, v, fl, bf_pad, c0, pq, pk, tr, n_pad):
    b, l, _ = q.shape
    wide = pl.BlockSpec((1, tr, D_BRANCH), lambda bi, i: (bi, i, 0))
    head_out = pl.BlockSpec((1, N_HEADS, tr, LANES), lambda bi, i: (bi, 0, i, 0))
    row128 = pl.BlockSpec((1, LANES), lambda bi, i: (0, 0))
    sel = pl.BlockSpec((LANES, D_BRANCH), lambda bi, i: (0, 0))
    return pl.pallas_call(
        functools.partial(_fox_prep_kernel, tr=tr, n_pad=n_pad),
        grid=(b, l // tr),
        in_specs=[wide, wide, wide,
                  pl.BlockSpec((1, tr, LANES), lambda bi, i: (bi, i, 0)),
                  row128, row128, sel, sel],
        out_specs=[head_out, head_out,
                   pl.BlockSpec((1, N_HEADS, 1, PAIR, tr), lambda bi, i: (bi, 0, i, 0, 0)),
                   pl.BlockSpec((1, 1, LANES), lambda bi, i: (bi, 0, 0))],
        out_shape=[jax.ShapeDtypeStruct((b, N_HEADS, l, LANES), BF16)] * 2
                  + [jax.ShapeDtypeStruct((b, N_HEADS, l // tr, PAIR, tr), BF16),
                     jax.ShapeDtypeStruct((b, 1, LANES), F32)],
        scratch_shapes=[pltpu.VMEM((1, LANES), F32)],
        compiler_params=_params(("parallel", "arbitrary")),
        name="fox_prep",
    )(q, k, v, fl, bf_pad, c0, pq, pk)


def _attn_kernel(q_ref, kpre_ref, vpre_ref, k_ref, v_ref, z_ref, o_ref, s_ref, *, tq):
    qi = pl.program_id(2)
    heads = range(ATTN_HEADS)
    qs = [q_ref[0, h] for h in heads]

    def rowmax(s):
        return jnp.max(s, axis=0, keepdims=True)

    def k_chunk(h, idx):
        return k_ref[0, h, pl.ds(pl.multiple_of(idx * tq, tq), tq), :]

    def v_chunk(h, idx):
        return v_ref[0, h, idx]

    def scores(idx):
        return [_dot_nt(k_chunk(h, idx), qs[h]) for h in heads]

    def softmax_pv(carry, s, rmax, idx):
        out = []
        for h in heads:
            m, acc = carry[h]
            m_new = jnp.maximum(m, rmax[h])
            p = jnp.exp(s[h] - m_new).astype(BF16)
            out.append((m_new, jnp.exp(m - m_new) * acc + _dot(v_chunk(h, idx), p)))
        return out

    key = lax.broadcasted_iota(jnp.int32, (tq, tq), 0)
    qry = lax.broadcasted_iota(jnp.int32, (tq, tq), 1)
    n_zero = PREFIX_ROWS - N_META
    s_pre = [_dot_nt(kpre_ref[0, h, n_zero:, :], qs[h]) for h in heads]
    s_dia = [jnp.where(key <= qry, s, NEG) for s in scores(qi)]
    s_nxt = scores(0)
    carry = []
    for h in heads:
        m0 = jnp.maximum(rowmax(s_pre[h]), rowmax(s_dia[h]))
        p_pre = jnp.concatenate([jnp.zeros((n_zero, tq), BF16), jnp.exp(s_pre[h] - m0).astype(BF16)], axis=0)
        p_dia = jnp.exp(s_dia[h] - m0).astype(BF16)
        carry.append((m0, _dot(vpre_ref[0, h, 0], p_pre) + _dot(v_chunk(h, qi), p_dia)))
        s_ref[h] = s_nxt[h]
    rmax = [rowmax(s) for s in s_nxt]

    def step2(t, state):
        carry, rmax = state
        k0 = 2 * t
        s_a = [s_ref[h] for h in heads]
        s_b = scores(k0 + 1)
        carry = softmax_pv(carry, s_a, rmax, k0)
        rmax_b = [rowmax(s) for s in s_b]
        s_c = scores(k0 + 2)
        carry = softmax_pv(carry, s_b, rmax_b, k0 + 1)
        for h in heads:
            s_ref[h] = s_c[h]
        return carry, [rowmax(s) for s in s_c]

    def step1(k0, state):
        carry, rmax = state
        s_a = [s_ref[h] for h in heads]
        s_b = scores(k0 + 1)
        carry = softmax_pv(carry, s_a, rmax, k0)
        for h in heads:
            s_ref[h] = s_b[h]
        return carry, [rowmax(s) for s in s_b]

    steps = jnp.maximum(qi - 1, 0)
    pairs = steps // 2
    state = lax.fori_loop(0, pairs, step2, (carry, rmax))
    carry, rmax = lax.fori_loop(2 * pairs, steps, step1, state)

    some = qi > 0
    last = jnp.maximum(qi - 1, 0)
    s_fin = [jnp.where(some, s_ref[h], NEG) for h in heads]
    r_fin = [jnp.where(some, r, NEG) for r in rmax]
    accs = [acc for _, acc in softmax_pv(carry, s_fin, r_fin, last)]
    vrow = lax.broadcasted_iota(jnp.int32, (PAIR, tq), 0)
    for pr in range(ATTN_HEADS // 2):
        acc0, acc1 = accs[2 * pr], accs[2 * pr + 1]
        l0 = acc0[ONES_LANE[0]:ONES_LANE[0] + 1, :]
        l1 = acc1[ONES_LANE[1]:ONES_LANE[1] + 1, :]
        o_t = jnp.where(vrow < HEAD_DIM, acc0 / l0, acc1 / l1)
        z = z_ref[0, :, pr * PAIR:(pr + 1) * PAIR].astype(F32)
        o_ref[0, :, pr * PAIR:(pr + 1) * PAIR] = (o_t.T * (z * _sigmoid(z))).astype(o_ref.dtype)


def _fox_attn(qp, kp, vp, kpre, vpre, zf, tq):
    b, _, l, _ = qp.shape
    g = ATTN_HEADS
    out_block = pl.BlockSpec((1, tq, g * HEAD_DIM), lambda bi, p, qi: (bi, qi, p))
    assert vp.shape[-1] == tq and vpre.shape[-1] == PREFIX_ROWS
    pre = pl.BlockSpec((1, g, PREFIX_ROWS, LANES), lambda bi, p, qi: (0, p, 0, 0))
    pre_v = pl.BlockSpec((1, g, 1, PAIR, PREFIX_ROWS), lambda bi, p, qi: (0, p, 0, 0, 0))
    full = pl.BlockSpec((1, g, l, LANES), lambda bi, p, qi: (bi, p, 0, 0))
    full_v = pl.BlockSpec((1, g, l // tq, PAIR, tq), lambda bi, p, qi: (bi, p, 0, 0, 0))
    return pl.pallas_call(
        functools.partial(_attn_kernel, tq=tq),
        grid=(b, N_HEADS // g, l // tq),
        in_specs=[pl.BlockSpec((1, g, tq, LANES), lambda bi, p, qi: (bi, p, qi, 0)),
                  pre, pre_v, full, full_v, out_block],
        out_specs=out_block,
        out_shape=jax.ShapeDtypeStruct((b, l, D_BRANCH), BF16),
        scratch_shapes=[pltpu.VMEM((g, tq, tq), F32)],
        compiler_params=_params(("parallel", "parallel", "parallel")),
        name="fox_attn",
    )(qp, kpre, vpre, kp, vp, zf)


def _rwkv_kernel(rt_ref, at_ref, bt_ref, kt_ref, xv_ref, bh_ref, kh_ref, bonus_ref, z_ref, wend_ref,
                 gnw_ref, gnb_ref, ones_ref, z0_ref,
                 y_ref, zf_ref, state_ref, yacc_ref, *, tr):
    t = pl.program_id(1)
    last = t == pl.num_programs(1) - 1

    @pl.when(t == 0)
    def _():
        state_ref[...] = z0_ref[...]

    ones_bd = ones_ref[...]

    def seg_sum(x):
        return _dot(x.astype(BF16), ones_bd)

    operand = dict(rt=rt_ref, at=at_ref, bt=bt_ref, kt=kt_ref, xv=xv_ref, bh=bh_ref, kh=kh_ref)

    col = lax.broadcasted_iota(jnp.int32, (CHUNK, PAIR), 1)
    trow = lax.broadcasted_iota(jnp.int32, (CHUNK, PAIR), 0)
    tcol = col % CHUNK
    head_a = col < HEAD_DIM
    strict = trow > tcol
    incl = trow >= tcol
    eye_sbs = jnp.where(trow == tcol, 1.0, 0.0).astype(F32)
    sq_r = lax.broadcasted_iota(jnp.int32, (PAIR, PAIR), 0)
    sq_c = lax.broadcasted_iota(jnp.int32, (PAIR, PAIR), 1)
    same_head = (sq_r // HEAD_DIM) == (sq_c // HEAD_DIM)
    eye_sq = sq_r == sq_c

    def stack(x):
        zero = jnp.zeros_like(x)
        return jnp.concatenate([jnp.where(head_a, x, zero), jnp.where(head_a, zero, x)], axis=0)

    def fold(sq):
        sq = jnp.where(same_head, sq, 0.0)
        return sq[:CHUNK] + sq[CHUNK:]

    def tile(name, c, p):
        return operand[name][0, c * CHUNK:(c + 1) * CHUNK, p * PAIR:(p + 1) * PAIR]

    n_chunks = tr // CHUNK
    units =[(c, p) for c in range(n_chunks) for p in range(N_PAIRS)]
    nu = range(len(units))
    wc = [wend_ref[0, c:c + 1, p * PAIR:(p + 1) * PAIR] for c, p in units]
    rt, at, bt, kt, xv, bh, kh = ([tile(name, c, p) for c, p in units]
                                  for name in ("rt", "at", "bt", "kt", "xv", "bh", "kh"))
    bt_s = [stack(x) for x in bt]
    kt_s = [stack(x) for x in kt]
    xv_s = [stack(x) for x in xv]

    ar = [jnp.concatenate([at[u], rt[u]], axis=0) for u in nu]
    x_b = [_dot_nt(ar[u], bt_s[u]) for u in nu]
    x_k = [_dot_nt(ar[u], kt_s[u]) for u in nu]
    a_ab = [jnp.where(strict, x[:CHUNK], 0.0) for x in x_b]
    a_rb = [jnp.where(incl, x[CHUNK:], 0.0).astype(BF16) for x in x_b]
    a_ak = [jnp.where(strict, x[:CHUNK], 0.0).astype(BF16) for x in x_k]
    a_rk = [jnp.where(incl, x[CHUNK:], 0.0).astype(BF16) for x in x_k]

    ab = [a.astype(BF16) for a in a_ab]
    pw = [_dot(ab[u], stack(ab[u])) for u in nu]
    tinv = [eye_sbs + a_ab[u] for u in nu]
    for level in range(1, 6):
        pb = [x.astype(BF16) for x in pw]
        if level < 5:
            both = [_dot(jnp.concatenate([tinv[u].astype(BF16), pb[u]], axis=0), stack(pb[u])) for u in nu]
            tinv = [tinv[u] + both[u][:CHUNK] for u in nu]
            pw = [both[u][CHUNK:] for u in nu]
        else:
            tinv = [tinv[u] + _dot(tinv[u].astype(BF16), stack(pb[u])) for u in nu]
    tb = [x.astype(BF16) for x in tinv]

    ap = [_dot(tb[u], stack(at[u])).astype(BF16) for u in nu]
    akv = [_dot(a_ak[u], xv_s[u]).astype(BF16) for u in nu]
    vp = [_dot(tb[u], stack(akv[u])).astype(BF16) for u in nu]
    ap_s = [stack(x) for x in ap]
    vp_s = [stack(x) for x in vp]
    m_sbs = [fold(jnp.where(eye_sq, wc[u], 0.0) + _dot_tn(bh[u], ap[u])).astype(BF16) for u in nu]
    g_sbs = [fold(_dot_tn(jnp.concatenate([bh[u], kh[u]], axis=0), jnp.concatenate([vp[u], xv[u]], axis=0)))
             for u in nu]
    rp = [(rt[u].astype(F32) + _dot(a_rb[u], ap_s[u])).astype(BF16) for u in nu]
    y0 = [_dot(jnp.concatenate([a_rb[u], a_rk[u]], axis=1), jnp.concatenate([vp_s[u], xv_s[u]], axis=0))
          for u in nu]

    for u, (c, p) in enumerate(units):
        z_bd = stack(state_ref[p].astype(BF16))
        both = _dot(jnp.concatenate([rp[u], m_sbs[u]], axis=0), z_bd)
        yacc_ref[c * CHUNK:(c + 1) * CHUNK, p * PAIR:(p + 1) * PAIR] = both[:CHUNK] + y0[u]
        state_ref[p] = both[CHUNK:] + g_sbs[u]

    @pl.when(last)
    def _():
        zf_ref[0] = state_ref[...]

    inv_n = 1.0 / HEAD_DIM
    y = yacc_ref[...]
    mean = seg_sum(y) * inv_n
    d = y - mean
    var = seg_sum(d * d) * inv_n
    yn = d * lax.rsqrt(var + GN_EPS) * gnw_ref[...] + gnb_ref[...]
    z = z_ref[0].astype(F32)
    y_ref[0] = ((yn + bonus_ref[0].astype(F32)) * (z * _sigmoid(z))).astype(y_ref.dtype)


def _rwkv(ops, z, wend, gn_w, gn_b, ones_bd, z0, tr):
    b, l, _ = z.shape
    wide = pl.BlockSpec((1, tr, D_BRANCH), lambda bi, t: (bi, t, 0))
    row_w = pl.BlockSpec((1, D_BRANCH), lambda bi, t: (0, 0))
    ones = pl.BlockSpec((D_BRANCH, D_BRANCH), lambda bi, t: (0, 0))
    st_in = pl.BlockSpec((N_PAIRS, CHUNK, PAIR), lambda bi, t: (0, 0, 0))
    st_out = pl.BlockSpec((1, N_PAIRS, CHUNK, PAIR), lambda bi, t: (bi, 0, 0, 0))
    return pl.pallas_call(
        functools.partial(_rwkv_kernel, tr=tr),
        grid=(b, l // tr),
        in_specs=[wide] * (len(RWKV_OPERANDS) + 1)
                 + [pl.BlockSpec((1, tr // CHUNK, D_BRANCH), lambda bi, t: (bi, t, 0)),
                    row_w, row_w, ones, st_in],
        out_specs=[wide, st_out],
        out_shape=[jax.ShapeDtypeStruct((b, l, D_BRANCH), BF16),
                   jax.ShapeDtypeStruct((b, N_PAIRS, CHUNK, PAIR), F32)],
        scratch_shapes=[pltpu.VMEM((N_PAIRS, CHUNK, PAIR), F32), pltpu.VMEM((tr, D_BRANCH), F32)],
        compiler_params=_params(("parallel", "arbitrary")),
        name="rwkv",
    )(*ops, z, wend, gn_w, gn_b, ones_bd, z0)


def _out_proj_kernel(yf_ref, yr_ref, x_ref, w_ref, fnw_ref, out_ref):
    mix = _dot(yf_ref[...], w_ref[:D_BRANCH, :]) + _dot(yr_ref[...], w_ref[D_BRANCH:, :])
    h = x_ref[...] + mix
    out_ref[...] = h * lax.rsqrt(jnp.mean(h * h, axis=-1, keepdims=True) + NORM_EPS) * fnw_ref[...]


def _out_proj(yf, yr, x, w_out, fnw, tm):
    n = x.shape[0]
    half = pl.BlockSpec((tm, D_BRANCH), lambda i: (i, 0))
    full = pl.BlockSpec((tm, D_MODEL), lambda i: (i, 0))
    return pl.pallas_call(
        _out_proj_kernel,
        grid=(n // tm,),
        in_specs=[half, half, full,
                  pl.BlockSpec((D_MODEL, D_MODEL), lambda i: (0, 0)),
                  pl.BlockSpec((1, D_MODEL), lambda i: (0, 0))],
        out_specs=full,
        out_shape=jax.ShapeDtypeStruct((n, D_MODEL), F32),
        compiler_params=_params(("parallel",)),
        name="out_proj",
    )(yf, yr, x, w_out, fnw)


def _tiles(b, l):
    rows = b * l
    tm = 512 if rows % 512 == 0 else 256
    return tm, 512, 512, 512


def kernel(x, meta, norm_w, w_in, b_f, mu_shift, w0, w_up, a0, a_up, k_k, k_a, r_k, gn_w, gn_b,
           w_out, final_norm_w):
    b, l, d = x.shape
    assert d == D_MODEL and norm_w.shape[0] == 1 and l % 256 == 0
    tm, t_prep, t_attn, t_rwkv = _tiles(b, l)

    wi = w_in[0].astype(BF16)
    o = 0
    cols = {}
    for name, width in (("q", D_BRANCH), ("k", D_BRANCH), ("v", D_BRANCH), ("fl", N_HEADS), ("zf", D_BRANCH),
                        ("r", D_BRANCH), ("rk", D_BRANCH), ("rv", D_BRANCH), ("wd", RANK), ("ad", RANK),
                        ("zr", D_BRANCH)):
        cols[name] = wi[:, o:o + width]
        o += width
    w_all = jnp.concatenate(
        [cols[n] for n in ("q", "k", "v", "zf", "r", "rk", "rv", "zr", "wd", "ad", "fl")]
        + [jnp.zeros((D_MODEL, LANES - N_HEADS), BF16)], axis=1)

    row = lambda vec: vec.reshape(1, -1).astype(F32)
    mu = mu_shift[0]
    mu_r, mu_k, mu_v = (row(mu[i * D_BRANCH:(i + 1) * D_BRANCH]) for i in range(3))
    mu_wa = row(mu[3 * D_BRANCH:])
    zeros_up = jnp.zeros((RANK, D_BRANCH), F32)
    wup_pad = jnp.concatenate([w_up[0], zeros_up], axis=0)
    aup_pad = jnp.concatenate([zeros_up, a_up[0]], axis=0)
    hid = np.arange(D_BRANCH) // HEAD_DIM
    ones_bd = jnp.asarray(hid[:, None] == hid[None, :], BF16)
    shift_prm = (mu_r, mu_k, mu_v, mu_wa, row(w0[0]), wup_pad.astype(BF16), row(a0[0]), aup_pad.astype(BF16),
                 row(k_k[0]), row(k_a[0]), row(r_k[0]), ones_bd)
    gnw, gnb = row(gn_w[0]), row(gn_b[0])
    bf_pad = jnp.concatenate([b_f[0], jnp.zeros((LANES - N_HEADS,), F32)]).reshape(1, LANES)
    pq, pk = _select_matrices()
    nw = row(norm_w[0])
    n_ops = len(RWKV_OPERANDS)

    pre_rows = jnp.concatenate([jnp.zeros((PREFIX_ROWS - N_META, D_MODEL), F32), meta.astype(F32)], axis=0)
    zero_w = jnp.zeros((1, D_BRANCH), F32)
    pre = _in_proj(pre_rows, nw, w_all, shift_prm, (zero_w, zero_w, zero_w, jnp.zeros((1, LANES), F32)),
                   PREFIX_ROWS, 1)
    pq_, pk_, pv_, _, pzr_, pfl_ = pre[:6]
    pre_ops, pre_wend, last_raw = pre[6:6 + n_ops], pre[6 + n_ops], pre[7 + n_ops:]
    lead = lambda a: a[None]
    _, kpre, vpre, c_pre = _fox_prep(lead(pq_), lead(pk_), lead(pv_), lead(pfl_), bf_pad,
                                     jnp.zeros((1, LANES), F32), pq, pk, PREFIX_ROWS, PREFIX_ROWS - N_META)
    _, z_pre = _rwkv([lead(a) for a in pre_ops], lead(pzr_), lead(pre_wend), gnw, gnb, ones_bd,
                     jnp.zeros((N_PAIRS, CHUNK, PAIR), F32), PREFIX_ROWS)

    xf = x.reshape(b * l, D_MODEL)
    main = _in_proj(xf, nw, w_all, shift_prm, last_raw, tm, l // tm)
    q_, k_, v_, zf_, zr_, fl_ = main[:6]
    bl = lambda a: a.reshape(b, l, a.shape[-1])
    qp, kp, vp, _ = _fox_prep(bl(q_), bl(k_), bl(v_), bl(fl_), bf_pad, c_pre[0], pq, pk, t_prep, 0)
    y_fox = _fox_attn(qp, kp, vp, kpre, vpre, bl(zf_), t_attn)
    y_rwkv, _ = _rwkv([bl(a) for a in main[6:6 + n_ops]], bl(zr_),
                      main[6 + n_ops].reshape(b, l // CHUNK, D_BRANCH), gnw, gnb, ones_bd, z_pre[0], t_rwkv)
    out = _out_proj(y_fox.reshape(b * l, D_BRANCH), y_rwkv.reshape(b * l, D_BRANCH), xf,
                    w_out[0].astype(BF16), row(final_norm_w), tm)
    return out.reshape(b, l, D_MODEL)
```

```python
import functools

import numpy as np
import jax
import jax.numpy as jnp
from jax import lax
from jax.experimental import pallas as pl
from jax.experimental.pallas import tpu as pltpu

F32 = jnp.float32
BF16 = jnp.bfloat16

D_MODEL = 1024
N_META = 16
HEAD_DIM = 64
N_HEADS = 8
D_BRANCH = N_HEADS * HEAD_DIM
N_PAIRS = N_HEADS // 2
RANK = 64
NORM_EPS = 1e-6
GN_EPS = 64e-5
KK_EPS = 1e-12
NEG = -1e30

LANES = 128
PREFIX_ROWS = 128
CHUNK = 64
SUB = 2 * CHUNK
PAIR = 2 * HEAD_DIM
VMEM_LIMIT = 56 * 1024 * 1024

N_WIDE = 8
COL_WA = N_WIDE * D_BRANCH
COL_FL = COL_WA + LANES
N_COLS = COL_FL + LANES

N_CPARTS = 3
ONES_LANE = (HEAD_DIM, 0)
ATTN_HEADS = 4


def _dot(a, b):
    return jnp.dot(a, b, preferred_element_type=F32)


def _pieces(x, n):
    out = []
    for _ in range(n - 1):
        p = x.astype(BF16)
        out.append(p)
        x = x - p.astype(F32)
    out.append(x.astype(BF16))
    return out


def _dot_exact_lhs(a_bf16, x, n):
    acc = None
    for p in _pieces(x, n):
        t = _dot(a_bf16, p)
        acc = t if acc is None else acc + t
    return acc


def _dot_x2(x, w_bf16):
    x_hi, x_lo = _pieces(x, 2)
    return _dot(x_hi, w_bf16) + _dot(x_lo, w_bf16)


def _dot_nt(a, b):
    return lax.dot_general(a, b, (((1,), (1,)), ((), ())), preferred_element_type=F32)


def _dot_tn(a, b):
    return lax.dot_general(a, b, (((0,), (0,)), ((), ())), preferred_element_type=F32)


def _softplus(x):
    return jnp.maximum(x, 0.0) + jnp.log(1.0 + jnp.exp(-jnp.abs(x)))


def _sigmoid(x):
    return 1.0 / (1.0 + jnp.exp(-x))


def _params(sem):
    return pltpu.CompilerParams(dimension_semantics=sem, vmem_limit_bytes=VMEM_LIMIT)


G_Q, G_K, G_V, G_ZF, G_R, G_RK, G_RV, G_ZR = range(N_WIDE)
RWKV_OPERANDS = ("rt", "at", "bt", "kt", "xv", "bh", "kh", "bonus")


def _in_proj_kernel(x_ref, nw_ref, w_ref,
                    mu_r_ref, mu_k_ref, mu_v_ref, mu_wa_ref, w0_ref, wup_ref,
                    a0_ref, aup_ref, kk_ref, ka_ref, rk_ref, ones_ref,
                    pr_ref, pk_ref, pv_ref, pwa_ref,
                    q_ref, k_ref, v_ref, zf_ref, zr_ref, fl_ref,
                    rt_ref, at_ref, bt_ref, kt_ref, xv_ref, bh_ref, kh_ref, bonus_ref, wend_ref,
                    lr_ref, lk_ref, lv_ref, lwa_ref,
                    sr_ref, sk_ref, sv_ref, swa_ref, *, tm, tiles_per_seq):
    i = pl.program_id(0)

    @pl.when(lax.rem(i, tiles_per_seq) == 0)
    def _():
        sr_ref[...] = pr_ref[...]
        sk_ref[...] = pk_ref[...]
        sv_ref[...] = pv_ref[...]
        swa_ref[...] = pwa_ref[...]

    x = x_ref[...]
    u = x * lax.rsqrt(jnp.mean(x * x, axis=-1, keepdims=True) + NORM_EPS) * nw_ref[...]
    ub = u.astype(BF16)

    def group(g):
        return _dot(ub, w_ref[:, g * D_BRANCH:(g + 1) * D_BRANCH])

    raw = [group(G_R), group(G_RK), group(G_RV), _dot(ub, w_ref[:, COL_WA:COL_WA + LANES])]
    ones_bd = ones_ref[...]

    def seg_sum(v):
        return _dot(v.astype(BF16), ones_bd)

    def shifted(cur, prev_row, mu_ref):
        first = lax.broadcasted_iota(jnp.int32, cur.shape, 0) == 0
        prev = jnp.where(first, prev_row, pltpu.roll(cur, 1, 0))
        return cur + mu_ref[...] * (prev - cur)

    r2 = lax.broadcasted_iota(jnp.int32, (SUB, SUB), 0)
    c2 = lax.broadcasted_iota(jnp.int32, (SUB, SUB), 1)
    tri = jnp.where((r2 >= c2) & (r2 // CHUNK == c2 // CHUNK), 1.0, 0.0).astype(BF16)
    outs = dict(zip(RWKV_OPERANDS, (rt_ref, at_ref, bt_ref, kt_ref, xv_ref, bh_ref, kh_ref, bonus_ref)))

    def rwkv_prepare():
        prev_rows = [sr_ref[...], sk_ref[...], sv_ref[...], swa_ref[...]]
        mus = (mu_r_ref, mu_k_ref, mu_v_ref, mu_wa_ref)
        xr, xk, xv, xwa = [shifted(c, p, mu) for c, p, mu in zip(raw, prev_rows, mus)]
        w_lin = w0_ref[...] + _dot_x2(jnp.tanh(xwa), wup_ref[...])
        a_lin = a0_ref[...] + _dot(xwa.astype(BF16), aup_ref[...])
        kk = xk * kk_ref[...]
        kk_ss = seg_sum(kk * kk)
        yield [c[tm - 1:tm, :] for c in raw]
        w = -_softplus(-w_lin) - 0.5
        ld = -jnp.exp(w)
        a = _sigmoid(a_lin)
        kk = kk * lax.rsqrt(kk_ss + KK_EPS)
        kmod = xk * (1.0 + (a - 1.0) * ka_ref[...])
        lw = jnp.concatenate([_dot_exact_lhs(tri, ld[sb * SUB:(sb + 1) * SUB, :], 2)
                              for sb in range(tm // SUB)], axis=0)
        rk_sum = seg_sum(xr * kmod * rk_ref[...])
        yield None
        w_inv = jnp.exp(-lw)
        bt = kk * a * w_inv
        kt = kmod * w_inv
        for name, val in (("rt", xr * jnp.exp(lw)), ("at", -kk * jnp.exp(lw - ld)), ("bt", bt), ("kt", kt),
                          ("xv", xv), ("bonus", rk_sum * xv)):
            outs[name][...] = val.astype(BF16)
        for c in range(tm // CHUNK):
            rows = slice(c * CHUNK, (c + 1) * CHUNK)
            w_c = jnp.exp(lw[(c + 1) * CHUNK - 1:(c + 1) * CHUNK, :])
            wend_ref[c:c + 1, :] = w_c
            bh_ref[rows, :] = (bt[rows, :] * w_c).astype(BF16)
            kh_ref[rows, :] = (kt[rows, :] * w_c).astype(BF16)
        yield None

    parts = rwkv_prepare()
    last_rows = next(parts)
    q_ref[...] = group(G_Q).astype(BF16)
    k_ref[...] = group(G_K).astype(BF16)
    next(parts)
    v_ref[...] = group(G_V).astype(BF16)
    zf_ref[...] = group(G_ZF).astype(BF16)
    next(parts)
    zr_ref[...] = group(G_ZR).astype(BF16)
    fl_ref[...] = _dot(ub, w_ref[:, COL_FL:COL_FL + LANES])

    for ref, rowv in zip((sr_ref, sk_ref, sv_ref, swa_ref), last_rows):
        ref[...] = rowv

    @pl.when(i == pl.num_programs(0) - 1)
    def _():
        for out, rowv in zip((lr_ref, lk_ref, lv_ref, lwa_ref), last_rows):
            out[...] = rowv


def _in_proj(rows, norm_w, w_all, prm, prev, tm, tiles_per_seq):
    n = rows.shape[0]
    tile = lambda width: pl.BlockSpec((tm, width), lambda i: (i, 0))
    const = lambda shape: pl.BlockSpec(shape, lambda i: (0,) * len(shape))
    row_w, row_n, up = const((1, D_BRANCH)), const((1, LANES)), const((LANES, D_BRANCH))
    wide = lambda dt: jax.ShapeDtypeStruct((n, D_BRANCH), dt)
    return pl.pallas_call(
        functools.partial(_in_proj_kernel, tm=tm, tiles_per_seq=tiles_per_seq),
        grid=(n // tm,),
        in_specs=[tile(D_MODEL), const((1, D_MODEL)),
                  pl.BlockSpec((D_MODEL, N_COLS), lambda i: (0, 0), pipeline_mode=pl.Buffered(1)),
                  row_w, row_w, row_w, row_n, row_w, up, row_w, up,
                  row_w, row_w, row_w, const((D_BRANCH, D_BRANCH)),
                  row_w, row_w, row_w, row_n],
        out_specs=[tile(D_BRANCH)] * 5 + [tile(LANES)] + [tile(D_BRANCH)] * len(RWKV_OPERANDS)
                  + [pl.BlockSpec((tm // CHUNK, D_BRANCH), lambda i: (i, 0)),
                     row_w, row_w, row_w, row_n],
        out_shape=[wide(BF16), wide(BF16), wide(BF16), wide(BF16), wide(BF16),
                   jax.ShapeDtypeStruct((n, LANES), F32)]
                  + [wide(BF16)] * len(RWKV_OPERANDS)
                  + [jax.ShapeDtypeStruct((n // CHUNK, D_BRANCH), F32)]
                  + [jax.ShapeDtypeStruct((1, D_BRANCH), F32)] * 3 + [jax.ShapeDtypeStruct((1, LANES), F32)],
        scratch_shapes=[pltpu.VMEM((1, D_BRANCH), F32), pltpu.VMEM((1, D_BRANCH), F32),
                        pltpu.VMEM((1, D_BRANCH), F32), pltpu.VMEM((1, LANES), F32)],
        compiler_params=_params(("arbitrary",)),
        name="in_proj",
    )(rows, norm_w, w_all, *prm, *prev)


def _select_matrices():
    pq = np.zeros((LANES, D_BRANCH), np.float32)
    pk = np.zeros((LANES, D_BRANCH), np.float32)
    for h in range(N_HEADS):
        base = (h // 2) * PAIR + (HEAD_DIM if h % 2 == 0 else 0)
        for part in range(N_CPARTS):
            pq[part * N_HEADS + h, base + part] = 1.0
            pq[N_CPARTS * N_HEADS + h, base + N_CPARTS + part] = 1.0
            pk[N_CPARTS * N_HEADS + h, base + part] = 1.0
            pk[part * N_HEADS + h, base + N_CPARTS + part] = -1.0
    return jnp.asarray(pq, BF16), jnp.asarray(pk, BF16)


def _fox_prep_kernel(q_ref, k_ref, v_ref, fl_ref, bf_ref, c0_ref, pq_ref, pk_ref,
                     qo_ref, ko_ref, vo_ref, cl_ref, carry_ref, *, tr, n_pad):
    i = pl.program_id(1)

    @pl.when(i == 0)
    def _():
        carry_ref[...] = c0_ref[...]

    x = fl_ref[0] + bf_ref[...]
    logf = jnp.minimum(x, 0.0) - jnp.log(1.0 + jnp.exp(-jnp.abs(x)))
    lane = lax.broadcasted_iota(jnp.int32, (tr, LANES), 1)
    row = lax.broadcasted_iota(jnp.int32, (tr, LANES), 0) + i * tr
    valid = lane < N_HEADS
    if n_pad:
        valid = valid & (row >= n_pad)
    logf = jnp.where(valid, logf, 0.0)
    r2 = lax.broadcasted_iota(jnp.int32, (tr, tr), 0)
    c2 = lax.broadcasted_iota(jnp.int32, (tr, tr), 1)
    tri = jnp.where(r2 >= c2, 1.0, 0.0).astype(BF16)
    cum = _dot_exact_lhs(tri, logf, 2) + carry_ref[...]
    carry_ref[...] = cum[tr - 1:tr, :]

    @pl.when(i == pl.num_programs(1) - 1)
    def _():
        cl_ref[0] = cum[tr - 1:tr, :]

    p1 = cum.astype(BF16).astype(F32)
    rem = cum - p1
    p2 = rem.astype(BF16).astype(F32)
    p3 = (rem - p2).astype(BF16).astype(F32)
    ones = jnp.where((lane >= N_CPARTS * N_HEADS) & (lane < (N_CPARTS + 1) * N_HEADS), 1.0, 0.0)
    cbits = p1 + pltpu.roll(p2, N_HEADS, 1) + pltpu.roll(p3, 2 * N_HEADS, 1) + ones
    cbits_k = cbits
    if n_pad:
        cbits_k = jnp.where((row < n_pad) & (lane < N_HEADS), -NEG, cbits)
    bias_q = _dot(cbits.astype(BF16), pq_ref[...]).astype(BF16)
    bias_k = _dot(cbits_k.astype(BF16), pk_ref[...]).astype(BF16)
    low = lane < HEAD_DIM
    sel_r = lax.broadcasted_iota(jnp.int32, (PAIR, PAIR), 0)
    sel_c = lax.broadcasted_iota(jnp.int32, (PAIR, PAIR), 1)
    eye = jnp.where(sel_r == sel_c, 1.0, 0.0).astype(BF16)
    vrow = lax.broadcasted_iota(jnp.int32, (PAIR, tr), 0)
    for pr in range(N_PAIRS):
        sl = slice(pr * PAIR, (pr + 1) * PAIR)
        q_sc = q_ref[0, :, sl] * jnp.asarray(0.125, BF16)
        k_pr = k_ref[0, :, sl]
        v_t = _dot_nt(eye, v_ref[0, :, sl])
        for half in range(2):
            h = 2 * pr + half
            own = low if half == 0 else jnp.logical_not(low)
            qo_ref[0, h] = jnp.where(own, q_sc, bias_q[:, sl])
            ko_ref[0, h] = jnp.where(own, k_pr, bias_k[:, sl])
            own_v = (vrow < HEAD_DIM) if half == 0 else (vrow >= HEAD_DIM)
            vo_ref[0, h, 0] = jnp.where(vrow == ONES_LANE[half], 1.0, jnp.where(own_v, v_t, 0.0)).astype(BF16)


def _fox_prep(q, k, v, fl, bf_pad, c0, pq, pk, tr, n_pad):
    b, l, _ = q.shape
    wide = pl.BlockSpec((1, tr, D_BRANCH), lambda bi, i: (bi, i, 0))
    head_out = pl.BlockSpec((1, N_HEADS, tr, LANES), lambda bi, i: (bi, 0, i, 0))
    row128 = pl.BlockSpec((1, LANES), lambda bi, i: (0, 0))
    sel = pl.BlockSpec((LANES, D_BRANCH), lambda bi, i: (0, 0))
    return pl.pallas_call(
        functools.partial(_fox_prep_kernel, tr=tr, n_pad=n_pad),
        grid=(b, l // tr),
        in_specs=[wide, wide, wide,
                  pl.BlockSpec((1, tr, LANES), lambda bi, i: (bi, i, 0)),
                  row128, row128, sel, sel],
        out_specs=[head_out, head_out,
                   pl.BlockSpec((1, N_HEADS, 1, PAIR, tr), lambda bi, i: (bi, 0, i, 0, 0)),
                   pl.BlockSpec((1, 1, LANES), lambda bi, i: (bi, 0, 0))],
        out_shape=[jax.ShapeDtypeStruct((b, N_HEADS, l, LANES), BF16)] * 2
                  + [jax.ShapeDtypeStruct((b, N_HEADS, l // tr, PAIR, tr), BF16),
                     jax.ShapeDtypeStruct((b, 1, LANES), F32)],
        scratch_shapes=[pltpu.VMEM((1, LANES), F32)],
        compiler_params=_params(("parallel", "arbitrary")),
        name="fox_prep",
    )(q, k, v, fl, bf_pad, c0, pq, pk)


def _attn_kernel(q_ref, kpre_ref, vpre_ref, k_ref, v_ref, z_ref, o_ref, s_ref, *, tq):
    qi = pl.program_id(2)
    heads = range(ATTN_HEADS)
    qs = [q_ref[0, h] for h in heads]

    def rowmax(s):
        return jnp.max(s, axis=0, keepdims=True)

    def k_chunk(h, idx):
        return k_ref[0, h, pl.ds(pl.multiple_of(idx * tq, tq), tq), :]

    def v_chunk(h, idx):
        return v_ref[0, h, idx]

    def scores(idx):
        return [_dot_nt(k_chunk(h, idx), qs[h]) for h in heads]

    def softmax_pv(carry, s, rmax, idx):
        out = []
        for h in heads:
            m, acc = carry[h]
            m_new = jnp.maximum(m, rmax[h])
            p = jnp.exp(s[h] - m_new).astype(BF16)
            out.append((m_new, jnp.exp(m - m_new) * acc + _dot(v_chunk(h, idx), p)))
        return out

    n_zero = PREFIX_ROWS - N_META
    s_pre = [_dot_nt(kpre_ref[0, h, n_zero:, :], qs[h]) for h in heads]
    key = lax.broadcasted_iota(jnp.int32, (tq, tq), 0)
    qry = lax.broadcasted_iota(jnp.int32, (tq, tq), 1)
    s_dia = [jnp.where(key <= qry, s, NEG) for s in scores(qi)]
    s_nxt = scores(0)
    carry = []
    for h in heads:
        m0 = jnp.maximum(rowmax(s_pre[h]), rowmax(s_dia[h]))
        p_pre = jnp.concatenate([jnp.zeros((n_zero, tq), BF16), jnp.exp(s_pre[h] - m0).astype(BF16)], axis=0)
        p_dia = jnp.exp(s_dia[h] - m0).astype(BF16)
        carry.append((m0, _dot(vpre_ref[0, h, 0], p_pre) + _dot(v_chunk(h, qi), p_dia)))
        s_ref[h] = s_nxt[h]
    rmax = [rowmax(s) for s in s_nxt]

    def step2(t, state):
        carry, rmax = state
        k0 = 2 * t
        s_a = [s_ref[h] for h in heads]
        s_b = scores(k0 + 1)
        carry = softmax_pv(carry, s_a, rmax, k0)
        rmax_b = [rowmax(s) for s in s_b]
        s_c = scores(k0 + 2)
        carry = softmax_pv(carry, s_b, rmax_b, k0 + 1)
        for h in heads:
            s_ref[h] = s_c[h]
        return carry, [rowmax(s) for s in s_c]

    def step1(k0, state):
        carry, rmax = state
        s_a = [s_ref[h] for h in heads]
        s_b = scores(k0 + 1)
        carry = softmax_pv(carry, s_a, rmax, k0)
        for h in heads:
            s_ref[h] = s_b[h]
        return carry, [rowmax(s) for s in s_b]

    steps = jnp.maximum(qi - 1, 0)
    pairs = steps // 2
    state = lax.fori_loop(0, pairs, step2, (carry, rmax))
    carry, rmax = lax.fori_loop(2 * pairs, steps, step1, state)

    some = qi > 0
    last = jnp.maximum(qi - 1, 0)
    s_fin = [jnp.where(some, s_ref[h], NEG) for h in heads]
    r_fin = [jnp.where(some, r, NEG) for r in rmax]
    accs = [acc for _, acc in softmax_pv(carry, s_fin, r_fin, last)]
    vrow = lax.broadcasted_iota(jnp.int32, (PAIR, tq), 0)
    for pr in range(ATTN_HEADS // 2):
        acc0, acc1 = accs[2 * pr], accs[2 * pr + 1]
        l0 = acc0[ONES_LANE[0]:ONES_LANE[0] + 1, :]
        l1 = acc1[ONES_LANE[1]:ONES_LANE[1] + 1, :]
        o_t = jnp.where(vrow < HEAD_DIM, acc0 / l0, acc1 / l1)
        z = z_ref[0, :, pr * PAIR:(pr + 1) * PAIR].astype(F32)
        o_ref[0, :, pr * PAIR:(pr + 1) * PAIR] = (o_t.T * (z * _sigmoid(z))).astype(o_ref.dtype)


def _fox_attn(qp, kp, vp, kpre, vpre, zf, tq):
    b, _, l, _ = qp.shape
    g = ATTN_HEADS
    out_block = pl.BlockSpec((1, tq, g * HEAD_DIM), lambda bi, p, qi: (bi, qi, p))
    assert vp.shape[-1] == tq and vpre.shape[-1] == PREFIX_ROWS
    pre = pl.BlockSpec((1, g, PREFIX_ROWS, LANES), lambda bi, p, qi: (0, p, 0, 0))
    pre_v = pl.BlockSpec((1, g, 1, PAIR, PREFIX_ROWS), lambda bi, p, qi: (0, p, 0, 0, 0))
    full = pl.BlockSpec((1, g, l, LANES), lambda bi, p, qi: (bi, p, 0, 0))
    full_v = pl.BlockSpec((1, g, l // tq, PAIR, tq), lambda bi, p, qi: (bi, p, 0, 0, 0))
    return pl.pallas_call(
        functools.partial(_attn_kernel, tq=tq),
        grid=(b, N_HEADS // g, l // tq),
        in_specs=[pl.BlockSpec((1, g, tq, LANES), lambda bi, p, qi: (bi, p, qi, 0)),
                  pre, pre_v, full, full_v, out_block],
        out_specs=out_block,
        out_shape=jax.ShapeDtypeStruct((b, l, D_BRANCH), BF16),
        scratch_shapes=[pltpu.VMEM((g, tq, tq), F32)],
        compiler_params=_params(("parallel", "parallel", "parallel")),
        name="fox_attn",
    )(qp, kpre, vpre, kp, vp, zf)


def _rwkv_kernel(rt_ref, at_ref, bt_ref, kt_ref, xv_ref, bh_ref, kh_ref, bonus_ref, z_ref, wend_ref,
                 gnw_ref, gnb_ref, ones_ref, z0_ref,
                 y_ref, zf_ref, state_ref, yacc_ref, *, tr):
    t = pl.program_id(1)
    last = t == pl.num_programs(1) - 1

    @pl.when(t == 0)
    def _():
        state_ref[...] = z0_ref[...]

    ones_bd = ones_ref[...]

    def seg_sum(x):
        return _dot(x.astype(BF16), ones_bd)

    operand = dict(rt=rt_ref, at=at_ref, bt=bt_ref, kt=kt_ref, xv=xv_ref, bh=bh_ref, kh=kh_ref)

    col = lax.broadcasted_iota(jnp.int32, (CHUNK, PAIR), 1)
    trow = lax.broadcasted_iota(jnp.int32, (CHUNK, PAIR), 0)
    tcol = col % CHUNK
    head_a = col < HEAD_DIM
    strict = trow > tcol
    incl = trow >= tcol
    eye_sbs = jnp.where(trow == tcol, 1.0, 0.0).astype(F32)
    sq_r = lax.broadcasted_iota(jnp.int32, (PAIR, PAIR), 0)
    sq_c = lax.broadcasted_iota(jnp.int32, (PAIR, PAIR), 1)
    same_head = (sq_r // HEAD_DIM) == (sq_c // HEAD_DIM)
    eye_sq = sq_r == sq_c

    def stack(x):
        zero = jnp.zeros_like(x)
        return jnp.concatenate([jnp.where(head_a, x, zero), jnp.where(head_a, zero, x)], axis=0)

    def fold(sq):
        sq = jnp.where(same_head, sq, 0.0)
        return sq[:CHUNK] + sq[CHUNK:]

    def tile(name, c, p):
        return operand[name][0, c * CHUNK:(c + 1) * CHUNK, p * PAIR:(p + 1) * PAIR]

    n_chunks = tr // CHUNK
    units =[(c, p) for c in range(n_chunks) for p in range(N_PAIRS)]
    nu = range(len(units))
    wc = [wend_ref[0, c:c + 1, p * PAIR:(p + 1) * PAIR] for c, p in units]
    rt, at, bt, kt, xv, bh, kh = ([tile(name, c, p) for c, p in units]
                                  for name in ("rt", "at", "bt", "kt", "xv", "bh", "kh"))
    bt_s = [stack(x) for x in bt]
    kt_s = [stack(x) for x in kt]
    xv_s = [stack(x) for x in xv]

    ar = [jnp.concatenate([at[u], rt[u]], axis=0) for u in nu]
    x_b = [_dot_nt(ar[u], bt_s[u]) for u in nu]
    x_k = [_dot_nt(ar[u], kt_s[u]) for u in nu]
    a_ab = [jnp.where(strict, x[:CHUNK], 0.0) for x in x_b]
    a_rb = [jnp.where(incl, x[CHUNK:], 0.0).astype(BF16) for x in x_b]
    a_ak = [jnp.where(strict, x[:CHUNK], 0.0).astype(BF16) for x in x_k]
    a_rk = [jnp.where(incl, x[CHUNK:], 0.0).astype(BF16) for x in x_k]

    ab = [a.astype(BF16) for a in a_ab]
    pw = [_dot(ab[u], stack(ab[u])) for u in nu]
    tinv = [eye_sbs + a_ab[u] for u in nu]
    for level in range(1, 6):
        pb = [x.astype(BF16) for x in pw]
        if level < 5:
            both = [_dot(jnp.concatenate([tinv[u].astype(BF16), pb[u]], axis=0), stack(pb[u])) for u in nu]
            tinv = [tinv[u] + both[u][:CHUNK] for u in nu]
            pw = [both[u][CHUNK:] for u in nu]
        else:
            tinv = [tinv[u] + _dot(tinv[u].astype(BF16), stack(pb[u])) for u in nu]
    tb = [x.astype(BF16) for x in tinv]

    ap = [_dot(tb[u], stack(at[u])).astype(BF16) for u in nu]
    akv = [_dot(a_ak[u], xv_s[u]).astype(BF16) for u in nu]
    vp = [_dot(tb[u], stack(akv[u])).astype(BF16) for u in nu]
    ap_s = [stack(x) for x in ap]
    vp_s = [stack(x) for x in vp]
    m_sbs = [fold(jnp.where(eye_sq, wc[u], 0.0) + _dot_tn(bh[u], ap[u])).astype(BF16) for u in nu]
    g_sbs = [fold(_dot_tn(jnp.concatenate([bh[u], kh[u]], axis=0), jnp.concatenate([vp[u], xv[u]], axis=0)))
             for u in nu]
    rp = [(rt[u].astype(F32) + _dot(a_rb[u], ap_s[u])).astype(BF16) for u in nu]
    y0 = [_dot(jnp.concatenate([a_rb[u], a_rk[u]], axis=1), jnp.concatenate([vp_s[u], xv_s[u]], axis=0))
          for u in nu]

    for u, (c, p) in enumerate(units):
        z_bd = stack(state_ref[p].astype(BF16))
        both = _dot(jnp.concatenate([rp[u], m_sbs[u]], axis=0), z_bd)
        yacc_ref[c * CHUNK:(c + 1) * CHUNK, p * PAIR:(p + 1) * PAIR] = both[:CHUNK] + y0[u]
        state_ref[p] = both[CHUNK:] + g_sbs[u]

    @pl.when(last)
    def _():
        zf_ref[0] = state_ref[...]

    inv_n = 1.0 / HEAD_DIM
    y = yacc_ref[...]
    mean = seg_sum(y) * inv_n
    d = y - mean
    var = seg_sum(d * d) * inv_n
    yn = d * lax.rsqrt(var + GN_EPS) * gnw_ref[...] + gnb_ref[...]
    z = z_ref[0].astype(F32)
    y_ref[0] = ((yn + bonus_ref[0].astype(F32)) * (z * _sigmoid(z))).astype(y_ref.dtype)


def _rwkv(ops, z, wend, gn_w, gn_b, ones_bd, z0, tr):
    b, l, _ = z.shape
    wide = pl.BlockSpec((1, tr, D_BRANCH), lambda bi, t: (bi, t, 0))
    row_w = pl.BlockSpec((1, D_BRANCH), lambda bi, t: (0, 0))
    ones = pl.BlockSpec((D_BRANCH, D_BRANCH), lambda bi, t: (0, 0))
    st_in = pl.BlockSpec((N_PAIRS, CHUNK, PAIR), lambda bi, t: (0, 0, 0))
    st_out = pl.BlockSpec((1, N_PAIRS, CHUNK, PAIR), lambda bi, t: (bi, 0, 0, 0))
    return pl.pallas_call(
        functools.partial(_rwkv_kernel, tr=tr),
        grid=(b, l // tr),
        in_specs=[wide] * (len(RWKV_OPERANDS) + 1)
                 + [pl.BlockSpec((1, tr // CHUNK, D_BRANCH), lambda bi, t: (bi, t, 0)),
                    row_w, row_w, ones, st_in],
        out_specs=[wide, st_out],
        out_shape=[jax.ShapeDtypeStruct((b, l, D_BRANCH), BF16),
                   jax.ShapeDtypeStruct((b, N_PAIRS, CHUNK, PAIR), F32)],
        scratch_shapes=[pltpu.VMEM((N_PAIRS, CHUNK, PAIR), F32), pltpu.VMEM((tr, D_BRANCH), F32)],
        compiler_params=_params(("parallel", "arbitrary")),
        name="rwkv",
    )(*ops, z, wend, gn_w, gn_b, ones_bd, z0)


def _out_proj_kernel(yf_ref, yr_ref, x_ref, w_ref, fnw_ref, out_ref):
    mix = _dot(yf_ref[...], w_ref[:D_BRANCH, :]) + _dot(yr_ref[...], w_ref[D_BRANCH:, :])
    h = x_ref[...] + mix
    out_ref[...] = h * lax.rsqrt(jnp.mean(h * h, axis=-1, keepdims=True) + NORM_EPS) * fnw_ref[...]


def _out_proj(yf, yr, x, w_out, fnw, tm):
    n = x.shape[0]
    half = pl.BlockSpec((tm, D_BRANCH), lambda i: (i, 0))
    full = pl.BlockSpec((tm, D_MODEL), lambda i: (i, 0))
    return pl.pallas_call(
        _out_proj_kernel,
        grid=(n // tm,),
        in_specs=[half, half, full,
                  pl.BlockSpec((D_MODEL, D_MODEL), lambda i: (0, 0)),
                  pl.BlockSpec((1, D_MODEL), lambda i: (0, 0))],
        out_specs=full,
        out_shape=jax.ShapeDtypeStruct((n, D_MODEL), F32),
        compiler_params=_params(("parallel",)),
        name="out_proj",
    )(yf, yr, x, w_out, fnw)


def _tiles(b, l):
    rows = b * l
    tm = 512 if rows % 512 == 0 else 256
    return tm, 512, 512, 512


def kernel(x, meta, norm_w, w_in, b_f, mu_shift, w0, w_up, a0, a_up, k_k, k_a, r_k, gn_w, gn_b,
           w_out, final_norm_w):
    b, l, d = x.shape
    assert d == D_MODEL and norm_w.shape[0] == 1 and l % 256 == 0
    tm, t_prep, t_attn, t_rwkv = _tiles(b, l)

    wi = w_in[0].astype(BF16)
    o = 0
    cols = {}
    for name, width in (("q", D_BRANCH), ("k", D_BRANCH), ("v", D_BRANCH), ("fl", N_HEADS), ("zf", D_BRANCH),
                        ("r", D_BRANCH), ("rk", D_BRANCH), ("rv", D_BRANCH), ("wd", RANK), ("ad", RANK),
                        ("zr", D_BRANCH)):
        cols[name] = wi[:, o:o + width]
        o += width
    w_all = jnp.concatenate(
        [cols[n] for n in ("q", "k", "v", "zf", "r", "rk", "rv", "zr", "wd", "ad", "fl")]
        + [jnp.zeros((D_MODEL, LANES - N_HEADS), BF16)], axis=1)

    row = lambda vec: vec.reshape(1, -1).astype(F32)
    mu = mu_shift[0]
    mu_r, mu_k, mu_v = (row(mu[i * D_BRANCH:(i + 1) * D_BRANCH]) for i in range(3))
    mu_wa = row(mu[3 * D_BRANCH:])
    zeros_up = jnp.zeros((RANK, D_BRANCH), F32)
    wup_pad = jnp.concatenate([w_up[0], zeros_up], axis=0)
    aup_pad = jnp.concatenate([zeros_up, a_up[0]], axis=0)
    hid = np.arange(D_BRANCH) // HEAD_DIM
    ones_bd = jnp.asarray(hid[:, None] == hid[None, :], BF16)
    shift_prm = (mu_r, mu_k, mu_v, mu_wa, row(w0[0]), wup_pad.astype(BF16), row(a0[0]), aup_pad.astype(BF16),
                 row(k_k[0]), row(k_a[0]), row(r_k[0]), ones_bd)
    gnw, gnb = row(gn_w[0]), row(gn_b[0])
    bf_pad = jnp.concatenate([b_f[0], jnp.zeros((LANES - N_HEADS,), F32)]).reshape(1, LANES)
    pq, pk = _select_matrices()
    nw = row(norm_w[0])
    n_ops = len(RWKV_OPERANDS)

    pre_rows = jnp.concatenate([jnp.zeros((PREFIX_ROWS - N_META, D_MODEL), F32), meta.astype(F32)], axis=0)
    zero_w = jnp.zeros((1, D_BRANCH), F32)
    pre = _in_proj(pre_rows, nw, w_all, shift_prm, (zero_w, zero_w, zero_w, jnp.zeros((1, LANES), F32)),
                   PREFIX_ROWS, 1)
    pq_, pk_, pv_, _, pzr_, pfl_ = pre[:6]
    pre_ops, pre_wend, last_raw = pre[6:6 + n_ops], pre[6 + n_ops], pre[7 + n_ops:]
    lead = lambda a: a[None]
    _, kpre, vpre, c_pre = _fox_prep(lead(pq_), lead(pk_), lead(pv_), lead(pfl_), bf_pad,
                                     jnp.zeros((1, LANES), F32), pq, pk, PREFIX_ROWS, PREFIX_ROWS - N_META)
    _, z_pre = _rwkv([lead(a) for a in pre_ops], lead(pzr_), lead(pre_wend), gnw, gnb, ones_bd,
                     jnp.zeros((N_PAIRS, CHUNK, PAIR), F32), PREFIX_ROWS)

    xf = x.reshape(b * l, D_MODEL)
    main = _in_proj(xf, nw, w_all, shift_prm, last_raw, tm, l // tm)
    q_, k_, v_, zf_, zr_, fl_ = main[:6]
    bl = lambda a: a.reshape(b, l, a.shape[-1])
    qp, kp, vp, _ = _fox_prep(bl(q_), bl(k_), bl(v_), bl(fl_), bf_pad, c_pre[0], pq, pk, t_prep, 0)
    y_fox = _fox_attn(qp, kp, vp, kpre, vpre, bl(zf_), t_attn)
    y_rwkv, _ = _rwkv([bl(a) for a in main[6:6 + n_ops]], bl(zr_),
                      main[6 + n_ops].reshape(b, l // CHUNK, D_BRANCH), gnw, gnb, ones_bd, z_pre[0], t_rwkv)
    out = _out_proj(y_fox.reshape(b * l, D_BRANCH), y_rwkv.reshape(b * l, D_BRANCH), xf,
                    w_out[0].astype(BF16), row(final_norm_w), tm)
    return out.reshape(b, l, D_MODEL)
```

```python
import functools

import numpy as np
import jax
import jax.numpy as jnp
from jax import lax
from jax.experimental import pallas as pl
from jax.experimental.pallas import tpu as pltpu

F32 = jnp.float32
BF16 = jnp.bfloat16

D_MODEL = 1024
N_META = 16
HEAD_DIM = 64
N_HEADS = 8
D_BRANCH = N_HEADS * HEAD_DIM
N_PAIRS = N_HEADS // 2
RANK = 64
NORM_EPS = 1e-6
GN_EPS = 64e-5
KK_EPS = 1e-12
NEG = -1e30

LANES = 128
PREFIX_ROWS = 128
CHUNK = 64
SUB = 2 * CHUNK
PAIR = 2 * HEAD_DIM
VMEM_LIMIT = 56 * 1024 * 1024

N_WIDE = 8
COL_WA = N_WIDE * D_BRANCH
COL_FL = COL_WA + LANES
N_COLS = COL_FL + LANES

N_CPARTS = 3
ONES_LANE = (HEAD_DIM, 0)
ATTN_HEADS = 4


def _dot(a, b):
    return jnp.dot(a, b, preferred_element_type=F32)


def _pieces(x, n):
    out = []
    for _ in range(n - 1):
        p = x.astype(BF16)
        out.append(p)
        x = x - p.astype(F32)
    out.append(x.astype(BF16))
    return out


def _dot_exact_lhs(a_bf16, x, n):
    acc = None
    for p in _pieces(x, n):
        t = _dot(a_bf16, p)
        acc = t if acc is None else acc + t
    return acc


def _dot_x2(x, w_bf16):
    x_hi, x_lo = _pieces(x, 2)
    return _dot(x_hi, w_bf16) + _dot(x_lo, w_bf16)


def _dot_nt(a, b):
    return lax.dot_general(a, b, (((1,), (1,)), ((), ())), preferred_element_type=F32)


def _dot_tn(a, b):
    return lax.dot_general(a, b, (((0,), (0,)), ((), ())), preferred_element_type=F32)


def _softplus(x):
    return jnp.maximum(x, 0.0) + jnp.log(1.0 + jnp.exp(-jnp.abs(x)))


def _sigmoid(x):
    return 1.0 / (1.0 + jnp.exp(-x))


def _params(sem):
    return pltpu.CompilerParams(dimension_semantics=sem, vmem_limit_bytes=VMEM_LIMIT)


G_Q, G_K, G_V, G_ZF, G_R, G_RK, G_RV, G_ZR = range(N_WIDE)
RWKV_OPERANDS = ("rt", "at", "bt", "kt", "xv", "bh", "kh", "bonus")


def _in_proj_kernel(x_ref, nw_ref, w_ref,
                    mu_r_ref, mu_k_ref, mu_v_ref, mu_wa_ref, w0_ref, wup_ref,
                    a0_ref, aup_ref, kk_ref, ka_ref, rk_ref, ones_ref,
                    pr_ref, pk_ref, pv_ref, pwa_ref,
                    q_ref, k_ref, v_ref, zf_ref, zr_ref, fl_ref,
                    rt_ref, at_ref, bt_ref, kt_ref, xv_ref, bh_ref, kh_ref, bonus_ref, wend_ref,
                    lr_ref, lk_ref, lv_ref, lwa_ref,
                    sr_ref, sk_ref, sv_ref, swa_ref, *, tm, tiles_per_seq):
    i = pl.program_id(0)

    @pl.when(lax.rem(i, tiles_per_seq) == 0)
    def _():
        sr_ref[...] = pr_ref[...]
        sk_ref[...] = pk_ref[...]
        sv_ref[...] = pv_ref[...]
        swa_ref[...] = pwa_ref[...]

    x = x_ref[...]
    u = x * lax.rsqrt(jnp.mean(x * x, axis=-1, keepdims=True) + NORM_EPS) * nw_ref[...]
    ub = u.astype(BF16)

    def group(g):
        return _dot(ub, w_ref[:, g * D_BRANCH:(g + 1) * D_BRANCH])

    raw = [group(G_R), group(G_RK), group(G_RV), _dot(ub, w_ref[:, COL_WA:COL_WA + LANES])]
    ones_bd = ones_ref[...]

    def seg_sum(v):
        return _dot(v.astype(BF16), ones_bd)

    def shifted(cur, prev_row, mu_ref):
        first = lax.broadcasted_iota(jnp.int32, cur.shape, 0) == 0
        prev = jnp.where(first, prev_row, pltpu.roll(cur, 1, 0))
        return cur + mu_ref[...] * (prev - cur)

    r2 = lax.broadcasted_iota(jnp.int32, (SUB, SUB), 0)
    c2 = lax.broadcasted_iota(jnp.int32, (SUB, SUB), 1)
    tri = jnp.where((r2 >= c2) & (r2 // CHUNK == c2 // CHUNK), 1.0, 0.0).astype(BF16)
    outs = dict(zip(RWKV_OPERANDS, (rt_ref, at_ref, bt_ref, kt_ref, xv_ref, bh_ref, kh_ref, bonus_ref)))

    def rwkv_prepare():
        prev_rows = [sr_ref[...], sk_ref[...], sv_ref[...], swa_ref[...]]
        mus = (mu_r_ref, mu_k_ref, mu_v_ref, mu_wa_ref)
        xr, xk, xv, xwa = [shifted(c, p, mu) for c, p, mu in zip(raw, prev_rows, mus)]
        w_lin = w0_ref[...] + _dot_x2(jnp.tanh(xwa), wup_ref[...])
        a_lin = a0_ref[...] + _dot(xwa.astype(BF16), aup_ref[...])
        kk = xk * kk_ref[...]
        kk_ss = seg_sum(kk * kk)
        yield [c[tm - 1:tm, :] for c in raw]
        w = -_softplus(-w_lin) - 0.5
        ld = -jnp.exp(w)
        a = _sigmoid(a_lin)
        kk = kk * lax.rsqrt(kk_ss + KK_EPS)
        kmod = xk * (1.0 + (a - 1.0) * ka_ref[...])
        lw = jnp.concatenate([_dot_exact_lhs(tri, ld[sb * SUB:(sb + 1) * SUB, :], 2)
                              for sb in range(tm // SUB)], axis=0)
        rk_sum = seg_sum(xr * kmod * rk_ref[...])
        yield None
        w_inv = jnp.exp(-lw)
        bt = kk * a * w_inv
        kt = kmod * w_inv
        for name, val in (("rt", xr * jnp.exp(lw)), ("at", -kk * jnp.exp(lw - ld)), ("bt", bt), ("kt", kt),
                          ("xv", xv), ("bonus", rk_sum * xv)):
            outs[name][...] = val.astype(BF16)
        for c in range(tm // CHUNK):
            rows = slice(c * CHUNK, (c + 1) * CHUNK)
            w_c = jnp.exp(lw[(c + 1) * CHUNK - 1:(c + 1) * CHUNK, :])
            wend_ref[c:c + 1, :] = w_c
            bh_ref[rows, :] = (bt[rows, :] * w_c).astype(BF16)
            kh_ref[rows, :] = (kt[rows, :] * w_c).astype(BF16)
        yield None

    parts = rwkv_prepare()
    last_rows = next(parts)
    q_ref[...] = group(G_Q).astype(BF16)
    k_ref[...] = group(G_K).astype(BF16)
    next(parts)
    v_ref[...] = group(G_V).astype(BF16)
    zf_ref[...] = group(G_ZF).astype(BF16)
    next(parts)
    zr_ref[...] = group(G_ZR).astype(BF16)
    fl_ref[...] = _dot(ub, w_ref[:, COL_FL:COL_FL + LANES])

    for ref, rowv in zip((sr_ref, sk_ref, sv_ref, swa_ref), last_rows):
        ref[...] = rowv

    @pl.when(i == pl.num_programs(0) - 1)
    def _():
        for out, rowv in zip((lr_ref, lk_ref, lv_ref, lwa_ref), last_rows):
            out[...] = rowv


def _in_proj(rows, norm_w, w_all, prm, prev, tm, tiles_per_seq):
    n = rows.shape[0]
    tile = lambda width: pl.BlockSpec((tm, width), lambda i: (i, 0))
    const = lambda shape: pl.BlockSpec(shape, lambda i: (0,) * len(shape))
    row_w, row_n, up = const((1, D_BRANCH)), const((1, LANES)), const((LANES, D_BRANCH))
    wide = lambda dt: jax.ShapeDtypeStruct((n, D_BRANCH), dt)
    return pl.pallas_call(
        functools.partial(_in_proj_kernel, tm=tm, tiles_per_seq=tiles_per_seq),
        grid=(n // tm,),
        in_specs=[tile(D_MODEL), const((1, D_MODEL)),
                  pl.BlockSpec((D_MODEL, N_COLS), lambda i: (0, 0), pipeline_mode=pl.Buffered(1)),
                  row_w, row_w, row_w, row_n, row_w, up, row_w, up,
                  row_w, row_w, row_w, const((D_BRANCH, D_BRANCH)),
                  row_w, row_w, row_w, row_n],
        out_specs=[tile(D_BRANCH)] * 5 + [tile(LANES)] + [tile(D_BRANCH)] * len(RWKV_OPERANDS)
                  + [pl.BlockSpec((tm // CHUNK, D_BRANCH), lambda i: (i, 0)),
                     row_w, row_w, row_w, row_n],
        out_shape=[wide(BF16), wide(BF16), wide(BF16), wide(BF16), wide(BF16),
                   jax.ShapeDtypeStruct((n, LANES), F32)]
                  + [wide(BF16)] * len(RWKV_OPERANDS)
                  + [jax.ShapeDtypeStruct((n // CHUNK, D_BRANCH), F32)]
                  + [jax.ShapeDtypeStruct((1, D_BRANCH), F32)] * 3 + [jax.ShapeDtypeStruct((1, LANES), F32)],
        scratch_shapes=[pltpu.VMEM((1, D_BRANCH), F32), pltpu.VMEM((1, D_BRANCH), F32),
                        pltpu.VMEM((1, D_BRANCH), F32), pltpu.VMEM((1, LANES), F32)],
        compiler_params=_params(("arbitrary",)),
        name="in_proj",
    )(rows, norm_w, w_all, *prm, *prev)


def _select_matrices():
    pq = np.zeros((LANES, D_BRANCH), np.float32)
    pk = np.zeros((LANES, D_BRANCH), np.float32)
    for h in range(N_HEADS):
        base = (h // 2) * PAIR + (HEAD_DIM if h % 2 == 0 else 0)
        for part in range(N_CPARTS):
            pq[part * N_HEADS + h, base + part] = 1.0
            pq[N_CPARTS * N_HEADS + h, base + N_CPARTS + part] = 1.0
            pk[N_CPARTS * N_HEADS + h, base + part] = 1.0
            pk[part * N_HEADS + h, base + N_CPARTS + part] = -1.0
    return jnp.asarray(pq, BF16), jnp.asarray(pk, BF16)


def _fox_prep_kernel(q_ref, k_ref, v_ref, fl_ref, bf_ref, c0_ref, pq_ref, pk_ref,
                     qo_ref, ko_ref, vo_ref, cl_ref, carry_ref, *, tr, n_pad):
    i = pl.program_id(1)

    @pl.when(i == 0)
    def _():
        carry_ref[...] = c0_ref[...]

    x = fl_ref[0] + bf_ref[...]
    logf = jnp.minimum(x, 0.0) - jnp.log(1.0 + jnp.exp(-jnp.abs(x)))
    lane = lax.broadcasted_iota(jnp.int32, (tr, LANES), 1)
    row = lax.broadcasted_iota(jnp.int32, (tr, LANES), 0) + i * tr
    valid = lane < N_HEADS
    if n_pad:
        valid = valid & (row >= n_pad)
    logf = jnp.where(valid, logf, 0.0)
    r2 = lax.broadcasted_iota(jnp.int32, (tr, tr), 0)
    c2 = lax.broadcasted_iota(jnp.int32, (tr, tr), 1)
    tri = jnp.where(r2 >= c2, 1.0, 0.0).astype(BF16)
    cum = _dot_exact_lhs(tri, logf, 2) + carry_ref[...]
    carry_ref[...] = cum[tr - 1:tr, :]

    @pl.when(i == pl.num_programs(1) - 1)
    def _():
        cl_ref[0] = cum[tr - 1:tr, :]

    p1 = cum.astype(BF16).astype(F32)
    rem = cum - p1
    p2 = rem.astype(BF16).astype(F32)
    p3 = (rem - p2).astype(BF16).astype(F32)
    ones = jnp.where((lane >= N_CPARTS * N_HEADS) & (lane < (N_CPARTS + 1) * N_HEADS), 1.0, 0.0)
    cbits = p1 + pltpu.roll(p2, N_HEADS, 1) + pltpu.roll(p3, 2 * N_HEADS, 1) + ones
    cbits_k = cbits
    if n_pad:
        cbits_k = jnp.where((row < n_pad) & (lane < N_HEADS), -NEG, cbits)
    bias_q = _dot(cbits.astype(BF16), pq_ref[...]).astype(BF16)
    bias_k = _dot(cbits_k.astype(BF16), pk_ref[...]).astype(BF16)
    low = lane < HEAD_DIM
    sel_r = lax.broadcasted_iota(jnp.int32, (PAIR, PAIR), 0)
    sel_c = lax.broadcasted_iota(jnp.int32, (PAIR, PAIR), 1)
    eye = jnp.where(sel_r == sel_c, 1.0, 0.0).astype(BF16)
    vrow = lax.broadcasted_iota(jnp.int32, (PAIR, tr), 0)
    for pr in range(N_PAIRS):
        sl = slice(pr * PAIR, (pr + 1) * PAIR)
        q_sc = q_ref[0, :, sl] * jnp.asarray(0.125, BF16)
        k_pr = k_ref[0, :, sl]
        v_t = _dot_nt(eye, v_ref[0, :, sl])
        for half in range(2):
            h = 2 * pr + half
            own = low if half == 0 else jnp.logical_not(low)
            qo_ref[0, h] = jnp.where(own, q_sc, bias_q[:, sl])
            ko_ref[0, h] = jnp.where(own, k_pr, bias_k[:, sl])
            own_v = (vrow < HEAD_DIM) if half == 0 else (vrow >= HEAD_DIM)
            vo_ref[0, h, 0] = jnp.where(vrow == ONES_LANE[half], 1.0, jnp.where(own_v, v_t, 0.0)).astype(BF16)


def _fox_prep(q, k, v, fl, bf_pad, c0, pq, pk, tr, n_pad):
    b, l, _ = q.shape
    wide = pl.BlockSpec((1, tr, D_BRANCH), lambda bi, i: (bi, i, 0))
    head_out = pl.BlockSpec((1, N_HEADS, tr, LANES), lambda bi, i: (bi, 0, i, 0))
    row128 = pl.BlockSpec((1, LANES), lambda bi, i: (0, 0))
    sel = pl.BlockSpec((LANES, D_BRANCH), lambda bi, i: (0, 0))
    return pl.pallas_call(
        functools.partial(_fox_prep_kernel, tr=tr, n_pad=n_pad),
        grid=(b, l // tr),
        in_specs=[wide, wide, wide,
                  pl.BlockSpec((1, tr, LANES), lambda bi, i: (bi, i, 0)),
                  row128, row128, sel, sel],
        out_specs=[head_out, head_out,
                   pl.BlockSpec((1, N_HEADS, 1, PAIR, tr), lambda bi, i: (bi, 0, i, 0, 0)),
                   pl.BlockSpec((1, 1, LANES), lambda bi, i: (bi, 0, 0))],
        out_shape=[jax.ShapeDtypeStruct((b, N_HEADS, l, LANES), BF16)] * 2
                  + [jax.ShapeDtypeStruct((b, N_HEADS, l // tr, PAIR, tr), BF16),
                     jax.ShapeDtypeStruct((b, 1, LANES), F32)],
        scratch_shapes=[pltpu.VMEM((1, LANES), F32)],
        compiler_params=_params(("parallel", "arbitrary")),
        name="fox_prep",
    )(q, k, v, fl, bf_pad, c0, pq, pk)


def _attn_kernel(q_ref, kpre_ref, vpre_ref, k_ref, v_ref, z_ref, o_ref, s_ref, *, tq):
    qi = pl.program_id(2)
    heads = range(ATTN_HEADS)
    qs = [q_ref[0, h] for h in heads]

    def rowmax(s):
        return jnp.max(s, axis=0, keepdims=True)

    def k_chunk(h, idx):
        return k_ref[0, h, pl.ds(pl.multiple_of(idx * tq, tq), tq), :]

    def v_chunk(h, idx):
        return v_ref[0, h, idx]

    def scores(idx):
        return [_dot_nt(k_chunk(h, idx), qs[h]) for h in heads]

    def softmax_pv(carry, s, rmax, idx):
        out = []
        for h in heads:
            m, acc = carry[h]
            m_new = jnp.maximum(m, rmax[h])
            p = jnp.exp(s[h] - m_new).astype(BF16)
            out.append((m_new, jnp.exp(m - m_new) * acc + _dot(v_chunk(h, idx), p)))
        return out

    n_zero = PREFIX_ROWS - N_META
    s_pre = [_dot_nt(kpre_ref[0, h, n_zero:, :], qs[h]) for h in heads]
    key = lax.broadcasted_iota(jnp.int32, (tq, tq), 0)
    qry = lax.broadcasted_iota(jnp.int32, (tq, tq), 1)
    s_dia = [jnp.where(key <= qry, s, NEG) for s in scores(qi)]
    s_nxt = scores(0)
    carry = []
    for h in heads:
        m0 = jnp.maximum(rowmax(s_pre[h]), rowmax(s_dia[h]))
        p_pre = jnp.concatenate([jnp.zeros((n_zero, tq), BF16), jnp.exp(s_pre[h] - m0).astype(BF16)], axis=0)
        p_dia = jnp.exp(s_dia[h] - m0).astype(BF16)
        carry.append((m0, _dot(vpre_ref[0, h, 0], p_pre) + _dot(v_chunk(h, qi), p_dia)))
        s_ref[h] = s_nxt[h]
    rmax = [rowmax(s) for s in s_nxt]

    def step2(t, state):
        carry, rmax = state
        k0 = 2 * t
        s_a = [s_ref[h] for h in heads]
        s_b = scores(k0 + 1)
        carry = softmax_pv(carry, s_a, rmax, k0)
        rmax_b = [rowmax(s) for s in s_b]
        s_c = scores(k0 + 2)
        carry = softmax_pv(carry, s_b, rmax_b, k0 + 1)
        for h in heads:
            s_ref[h] = s_c[h]
        return carry, [rowmax(s) for s in s_c]

    def step1(k0, state):
        carry, rmax = state
        s_a = [s_ref[h] for h in heads]
        s_b = scores(k0 + 1)
        carry = softmax_pv(carry, s_a, rmax, k0)
        for h in heads:
            s_ref[h] = s_b[h]
        return carry, [rowmax(s) for s in s_b]

    steps = jnp.maximum(qi - 1, 0)
    pairs = steps // 2
    state = lax.fori_loop(0, pairs, step2, (carry, rmax))
    carry, rmax = lax.fori_loop(2 * pairs, steps, step1, state)

    some = qi > 0
    last = jnp.maximum(qi - 1, 0)
    s_fin = [jnp.where(some, s_ref[h], NEG) for h in heads]
    r_fin = [jnp.where(some, r, NEG) for r in rmax]
    accs = [acc for _, acc in softmax_pv(carry, s_fin, r_fin, last)]
    vrow = lax.broadcasted_iota(jnp.int32, (PAIR, tq), 0)
    for pr in range(ATTN_HEADS // 2):
        acc0, acc1 = accs[2 * pr], accs[2 * pr + 1]
        l0 = acc0[ONES_LANE[0]:ONES_LANE[0] + 1, :]
        l1 = acc1[ONES_LANE[1]:ONES_LANE[1] + 1, :]
        o_t = jnp.where(vrow < HEAD_DIM, acc0 / l0, acc1 / l1)
        z = z_ref[0, :, pr * PAIR:(pr + 1) * PAIR].astype(F32)
        o_ref[0, :, pr * PAIR:(pr + 1) * PAIR] = (o_t.T * (z * _sigmoid(z))).astype(o_ref.dtype)


def _fox_attn(qp, kp, vp, kpre, vpre, zf, tq):
    b, _, l, _ = qp.shape
    g = ATTN_HEADS
    out_block = pl.BlockSpec((1, tq, g * HEAD_DIM), lambda bi, p, qi: (bi, qi, p))
    assert vp.shape[-1] == tq and vpre.shape[-1] == PREFIX_ROWS
    pre = pl.BlockSpec((1, g, PREFIX_ROWS, LANES), lambda bi, p, qi: (0, p, 0, 0))
    pre_v = pl.BlockSpec((1, g, 1, PAIR, PREFIX_ROWS), lambda bi, p, qi: (0, p, 0, 0, 0))
    full = pl.BlockSpec((1, g, l, LANES), lambda bi, p, qi: (bi, p, 0, 0))
    full_v = pl.BlockSpec((1, g, l // tq, PAIR, tq), lambda bi, p, qi: (bi, p, 0, 0, 0))
    return pl.pallas_call(
        functools.partial(_attn_kernel, tq=tq),
        grid=(b, N_HEADS // g, l // tq),
        in_specs=[pl.BlockSpec((1, g, tq, LANES), lambda bi, p, qi: (bi, p, qi, 0)),
                  pre, pre_v, full, full_v, out_block],
        out_specs=out_block,
        out_shape=jax.ShapeDtypeStruct((b, l, D_BRANCH), BF16),
        scratch_shapes=[pltpu.VMEM((g, tq, tq), F32)],
        compiler_params=_params(("parallel", "parallel", "parallel")),
        name="fox_attn",
    )(qp, kpre, vpre, kp, vp, zf)


def _rwkv_kernel(rt_ref, at_ref, bt_ref, kt_ref, xv_ref, bh_ref, kh_ref, bonus_ref, z_ref, wend_ref,
                 gnw_ref, gnb_ref, ones_ref, z0_ref,
                 y_ref, zf_ref, state_ref, yacc_ref, *, tr):
    t = pl.program_id(1)
    last = t == pl.num_programs(1) - 1

    @pl.when(t == 0)
    def _():
        state_ref[...] = z0_ref[...]

    ones_bd = ones_ref[...]

    def seg_sum(x):
        return _dot(x.astype(BF16), ones_bd)

    operand = dict(rt=rt_ref, at=at_ref, bt=bt_ref, kt=kt_ref, xv=xv_ref, bh=bh_ref, kh=kh_ref)

    col = lax.broadcasted_iota(jnp.int32, (CHUNK, PAIR), 1)
    trow = lax.broadcasted_iota(jnp.int32, (CHUNK, PAIR), 0)
    tcol = col % CHUNK
    head_a = col < HEAD_DIM
    strict = trow > tcol
    incl = trow >= tcol
    eye_sbs = jnp.where(trow == tcol, 1.0, 0.0).astype(F32)
    sq_r = lax.broadcasted_iota(jnp.int32, (PAIR, PAIR), 0)
    sq_c = lax.broadcasted_iota(jnp.int32, (PAIR, PAIR), 1)
    same_head = (sq_r // HEAD_DIM) == (sq_c // HEAD_DIM)
    eye_sq = sq_r == sq_c

    def stack(x):
        zero = jnp.zeros_like(x)
        return jnp.concatenate([jnp.where(head_a, x, zero), jnp.where(head_a, zero, x)], axis=0)

    def fold(sq):
        sq = jnp.where(same_head, sq, 0.0)
        return sq[:CHUNK] + sq[CHUNK:]

    def tile(name, c, p):
        return operand[name][0, c * CHUNK:(c + 1) * CHUNK, p * PAIR:(p + 1) * PAIR]

    n_chunks = tr // CHUNK
    units =[(c, p) for c in range(n_chunks) for p in range(N_PAIRS)]
    nu = range(len(units))
    wc = [wend_ref[0, c:c + 1, p * PAIR:(p + 1) * PAIR] for c, p in units]
    rt, at, bt, kt, xv, bh, kh = ([tile(name, c, p) for c, p in units]
                                  for name in ("rt", "at", "bt", "kt", "xv", "bh", "kh"))
    bt_s = [stack(x) for x in bt]
    kt_s = [stack(x) for x in kt]
    xv_s = [stack(x) for x in xv]

    ar = [jnp.concatenate([at[u], rt[u]], axis=0) for u in nu]
    x_b = [_dot_nt(ar[u], bt_s[u]) for u in nu]
    x_k = [_dot_nt(ar[u], kt_s[u]) for u in nu]
    a_ab = [jnp.where(strict, x[:CHUNK], 0.0) for x in x_b]
    a_rb = [jnp.where(incl, x[CHUNK:], 0.0).astype(BF16) for x in x_b]
    a_ak = [jnp.where(strict, x[:CHUNK], 0.0).astype(BF16) for x in x_k]
    a_rk = [jnp.where(incl, x[CHUNK:], 0.0).astype(BF16) for x in x_k]

    ab = [a.astype(BF16) for a in a_ab]
    pw = [_dot(ab[u], stack(ab[u])) for u in nu]
    tinv = [eye_sbs + a_ab[u] for u in nu]
    for level in range(1, 6):
        pb = [x.astype(BF16) for x in pw]
        if level < 5:
            both = [_dot(jnp.concatenate([tinv[u].astype(BF16), pb[u]], axis=0), stack(pb[u])) for u in nu]
            tinv = [tinv[u] + both[u][:CHUNK] for u in nu]
            pw = [both[u][CHUNK:] for u in nu]
        else:
            tinv = [tinv[u] + _dot(tinv[u].astype(BF16), stack(pb[u])) for u in nu]
    tb = [x.astype(BF16) for x in tinv]

    ap = [_dot(tb[u], stack(at[u])).astype(BF16) for u in nu]
    akv = [_dot(a_ak[u], xv_s[u]).astype(BF16) for u in nu]
    vp = [_dot(tb[u], stack(akv[u])).astype(BF16) for u in nu]
    ap_s = [stack(x) for x in ap]
    vp_s = [stack(x) for x in vp]
    m_sbs = [fold(jnp.where(eye_sq, wc[u], 0.0) + _dot_tn(bh[u], ap[u])).astype(BF16) for u in nu]
    g_sbs = [fold(_dot_tn(jnp.concatenate([bh[u], kh[u]], axis=0), jnp.concatenate([vp[u], xv[u]], axis=0)))
             for u in nu]
    rp = [(rt[u].astype(F32) + _dot(a_rb[u], ap_s[u])).astype(BF16) for u in nu]
    y0 = [_dot(jnp.concatenate([a_rb[u], a_rk[u]], axis=1), jnp.concatenate([vp_s[u], xv_s[u]], axis=0))
          for u in nu]

    for u, (c, p) in enumerate(units):
        z_bd = stack(state_ref[p].astype(BF16))
        both = _dot(jnp.concatenate([rp[u], m_sbs[u]], axis=0), z_bd)
        yacc_ref[c * CHUNK:(c + 1) * CHUNK, p * PAIR:(p + 1) * PAIR] = both[:CHUNK] + y0[u]
        state_ref[p] = both[CHUNK:] + g_sbs[u]

    @pl.when(last)
    def _():
        zf_ref[0] = state_ref[...]

    inv_n = 1.0 / HEAD_DIM
    y = yacc_ref[...]
    mean = seg_sum(y) * inv_n
    d = y - mean
    var = seg_sum(d * d) * inv_n
    yn = d * lax.rsqrt(var + GN_EPS) * gnw_ref[...] + gnb_ref[...]
    z = z_ref[0].astype(F32)
    y_ref[0] = ((yn + bonus_ref[0].astype(F32)) * (z * _sigmoid(z))).astype(y_ref.dtype)


def _rwkv(ops, z, wend, gn_w, gn_b, ones_bd, z0, tr):
    b, l, _ = z.shape
    wide = pl.BlockSpec((1, tr, D_BRANCH), lambda bi, t: (bi, t, 0))
    row_w = pl.BlockSpec((1, D_BRANCH), lambda bi, t: (0, 0))
    ones = pl.BlockSpec((D_BRANCH, D_BRANCH), lambda bi, t: (0, 0))
    st_in = pl.BlockSpec((N_PAIRS, CHUNK, PAIR), lambda bi, t: (0, 0, 0))
    st_out = pl.BlockSpec((1, N_PAIRS, CHUNK, PAIR), lambda bi, t: (bi, 0, 0, 0))
    return pl.pallas_call(
        functools.partial(_rwkv_kernel, tr=tr),
        grid=(b, l // tr),
        in_specs=[wide] * (len(RWKV_OPERANDS) + 1)
                 + [pl.BlockSpec((1, tr // CHUNK, D_BRANCH), lambda bi, t: (bi, t, 0)),
                    row_w, row_w, ones, st_in],
        out_specs=[wide, st_out],
        out_shape=[jax.ShapeDtypeStruct((b, l, D_BRANCH), BF16),
                   jax.ShapeDtypeStruct((b, N_PAIRS, CHUNK, PAIR), F32)],
        scratch_shapes=[pltpu.VMEM((N_PAIRS, CHUNK, PAIR), F32), pltpu.VMEM((tr, D_BRANCH), F32)],
        compiler_params=_params(("parallel", "arbitrary")),
        name="rwkv",
    )(*ops, z, wend, gn_w, gn_b, ones_bd, z0)


def _out_proj_kernel(yf_ref, yr_ref, x_ref, w_ref, fnw_ref, out_ref):
    mix = _dot(yf_ref[...], w_ref[:D_BRANCH, :]) + _dot(yr_ref[...], w_ref[D_BRANCH:, :])
    h = x_ref[...] + mix
    out_ref[...] = h * lax.rsqrt(jnp.mean(h * h, axis=-1, keepdims=True) + NORM_EPS) * fnw_ref[...]


def _out_proj(yf, yr, x, w_out, fnw, tm):
    n = x.shape[0]
    half = pl.BlockSpec((tm, D_BRANCH), lambda i: (i, 0))
    full = pl.BlockSpec((tm, D_MODEL), lambda i: (i, 0))
    return pl.pallas_call(
        _out_proj_kernel,
        grid=(n // tm,),
        in_specs=[half, half, full,
                  pl.BlockSpec((D_MODEL, D_MODEL), lambda i: (0, 0)),
                  pl.BlockSpec((1, D_MODEL), lambda i: (0, 0))],
        out_specs=full,
        out_shape=jax.ShapeDtypeStruct((n, D_MODEL), F32),
        compiler_params=_params(("parallel",)),
        name="out_proj",
    )(yf, yr, x, w_out, fnw)


def _tiles(b, l):
    assert l % 512 == 0
    t_out = 1024 if (b * l) % 1024 == 0 else 512
    return 512, 512, 512, t_out


def kernel(x, meta, norm_w, w_in, b_f, mu_shift, w0, w_up, a0, a_up, k_k, k_a, r_k, gn_w, gn_b,
           w_out, final_norm_w):
    b, l, d = x.shape
    assert d == D_MODEL and norm_w.shape[0] == 1
    tm, t_attn, t_rwkv, t_out = _tiles(b, l)
    t_prep = t_attn

    wi = w_in[0].astype(BF16)
    o = 0
    cols = {}
    for name, width in (("q", D_BRANCH), ("k", D_BRANCH), ("v", D_BRANCH), ("fl", N_HEADS), ("zf", D_BRANCH),
                        ("r", D_BRANCH), ("rk", D_BRANCH), ("rv", D_BRANCH), ("wd", RANK), ("ad", RANK),
                        ("zr", D_BRANCH)):
        cols[name] = wi[:, o:o + width]
        o += width
    w_all = jnp.concatenate(
        [cols[n] for n in ("q", "k", "v", "zf", "r", "rk", "rv", "zr", "wd", "ad", "fl")]
        + [jnp.zeros((D_MODEL, LANES - N_HEADS), BF16)], axis=1)

    row = lambda vec: vec.reshape(1, -1).astype(F32)
    mu = mu_shift[0]
    mu_r, mu_k, mu_v = (row(mu[i * D_BRANCH:(i + 1) * D_BRANCH]) for i in range(3))
    mu_wa = row(mu[3 * D_BRANCH:])
    zeros_up = jnp.zeros((RANK, D_BRANCH), F32)
    wup_pad = jnp.concatenate([w_up[0], zeros_up], axis=0)
    aup_pad = jnp.concatenate([zeros_up, a_up[0]], axis=0)
    hid = np.arange(D_BRANCH) // HEAD_DIM
    ones_bd = jnp.asarray(hid[:, None] == hid[None, :], BF16)
    shift_prm = (mu_r, mu_k, mu_v, mu_wa, row(w0[0]), wup_pad.astype(BF16), row(a0[0]), aup_pad.astype(BF16),
                 row(k_k[0]), row(k_a[0]), row(r_k[0]), ones_bd)
    gnw, gnb = row(gn_w[0]), row(gn_b[0])
    bf_pad = jnp.concatenate([b_f[0], jnp.zeros((LANES - N_HEADS,), F32)]).reshape(1, LANES)
    pq, pk = _select_matrices()
    nw = row(norm_w[0])
    n_ops = len(RWKV_OPERANDS)

    pre_rows = jnp.concatenate([jnp.zeros((PREFIX_ROWS - N_META, D_MODEL), F32), meta.astype(F32)], axis=0)
    zero_w = jnp.zeros((1, D_BRANCH), F32)
    pre = _in_proj(pre_rows, nw, w_all, shift_prm, (zero_w, zero_w, zero_w, jnp.zeros((1, LANES), F32)),
                   PREFIX_ROWS, 1)
    pq_, pk_, pv_, _, pzr_, pfl_ = pre[:6]
    pre_ops, pre_wend, last_raw = pre[6:6 + n_ops], pre[6 + n_ops], pre[7 + n_ops:]
    lead = lambda a: a[None]
    _, kpre, vpre, c_pre = _fox_prep(lead(pq_), lead(pk_), lead(pv_), lead(pfl_), bf_pad,
                                     jnp.zeros((1, LANES), F32), pq, pk, PREFIX_ROWS, PREFIX_ROWS - N_META)
    _, z_pre = _rwkv([lead(a) for a in pre_ops], lead(pzr_), lead(pre_wend), gnw, gnb, ones_bd,
                     jnp.zeros((N_PAIRS, CHUNK, PAIR), F32), PREFIX_ROWS)

    xf = x.reshape(b * l, D_MODEL)
    main = _in_proj(xf, nw, w_all, shift_prm, last_raw, tm, l // tm)
    q_, k_, v_, zf_, zr_, fl_ = main[:6]
    bl = lambda a: a.reshape(b, l, a.shape[-1])
    qp, kp, vp, _ = _fox_prep(bl(q_), bl(k_), bl(v_), bl(fl_), bf_pad, c_pre[0], pq, pk, t_prep, 0)
    y_fox = _fox_attn(qp, kp, vp, kpre, vpre, bl(zf_), t_attn)
    y_rwkv, _ = _rwkv([bl(a) for a in main[6:6 + n_ops]], bl(zr_),
                      main[6 + n_ops].reshape(b, l // CHUNK, D_BRANCH), gnw, gnb, ones_bd, z_pre[0], t_rwkv)
    out = _out_proj(y_fox.reshape(b * l, D_BRANCH), y_rwkv.reshape(b * l, D_BRANCH), xf,
                    w_out[0].astype(BF16), row(final_norm_w), t_out)
    return out.reshape(b, l, D_MODEL)
```

```python
import functools

import numpy as np
import jax
import jax.numpy as jnp
from jax import lax
from jax.experimental import pallas as pl
from jax.experimental.pallas import tpu as pltpu

F32 = jnp.float32
BF16 = jnp.bfloat16

D_MODEL = 1024
N_META = 16
HEAD_DIM = 64
N_HEADS = 8
D_BRANCH = N_HEADS * HEAD_DIM
N_PAIRS = N_HEADS // 2
RANK = 64
NORM_EPS = 1e-6
GN_EPS = 64e-5
KK_EPS = 1e-12
NEG = -1e30

LANES = 128
PREFIX_ROWS = 128
CHUNK = 64
SUB = 2 * CHUNK
PAIR = 2 * HEAD_DIM
VMEM_LIMIT = 56 * 1024 * 1024

N_WIDE = 8
COL_WA = N_WIDE * D_BRANCH
COL_FL = COL_WA + LANES
N_COLS = COL_FL + LANES

N_CPARTS = 3
ONES_LANE = (HEAD_DIM, 0)
ATTN_HEADS = 4


def _dot(a, b):
    return jnp.dot(a, b, preferred_element_type=F32)


def _pieces(x, n):
    out = []
    for _ in range(n - 1):
        p = x.astype(BF16)
        out.append(p)
        x = x - p.astype(F32)
    out.append(x.astype(BF16))
    return out


def _dot_exact_lhs(a_bf16, x, n):
    acc = None
    for p in _pieces(x, n):
        t = _dot(a_bf16, p)
        acc = t if acc is None else acc + t
    return acc


def _dot_x2(x, w_bf16):
    x_hi, x_lo = _pieces(x, 2)
    return _dot(x_hi, w_bf16) + _dot(x_lo, w_bf16)


def _dot_nt(a, b):
    return lax.dot_general(a, b, (((1,), (1,)), ((), ())), preferred_element_type=F32)


def _dot_tn(a, b):
    return lax.dot_general(a, b, (((0,), (0,)), ((), ())), preferred_element_type=F32)


def _softplus(x):
    return jnp.maximum(x, 0.0) + jnp.log(1.0 + jnp.exp(-jnp.abs(x)))


def _sigmoid(x):
    return 1.0 / (1.0 + jnp.exp(-x))


def _params(sem):
    return pltpu.CompilerParams(dimension_semantics=sem, vmem_limit_bytes=VMEM_LIMIT)


G_Q, G_K, G_V, G_ZF, G_R, G_RK, G_RV, G_ZR = range(N_WIDE)
RWKV_OPERANDS = ("rt", "at", "bt", "kt", "xv", "bh", "kh", "bonus")


def _in_proj_kernel(x_ref, nw_ref, w_ref,
                    mu_r_ref, mu_k_ref, mu_v_ref, mu_wa_ref, w0_ref, wup_ref,
                    a0_ref, aup_ref, kk_ref, ka_ref, rk_ref, ones_ref,
                    pr_ref, pk_ref, pv_ref, pwa_ref,
                    q_ref, k_ref, v_ref, zf_ref, zr_ref, fl_ref,
                    rt_ref, at_ref, bt_ref, kt_ref, xv_ref, bh_ref, kh_ref, bonus_ref, wend_ref,
                    lr_ref, lk_ref, lv_ref, lwa_ref,
                    sr_ref, sk_ref, sv_ref, swa_ref, *, tm, tiles_per_seq):
    i = pl.program_id(0)

    @pl.when(lax.rem(i, tiles_per_seq) == 0)
    def _():
        sr_ref[...] = pr_ref[...]
        sk_ref[...] = pk_ref[...]
        sv_ref[...] = pv_ref[...]
        swa_ref[...] = pwa_ref[...]

    x = x_ref[...]
    u = x * lax.rsqrt(jnp.mean(x * x, axis=-1, keepdims=True) + NORM_EPS) * nw_ref[...]
    ub = u.astype(BF16)

    def group(g):
        return _dot(ub, w_ref[:, g * D_BRANCH:(g + 1) * D_BRANCH])

    raw = [group(G_R), group(G_RK), group(G_RV), _dot(ub, w_ref[:, COL_WA:COL_WA + LANES])]
    ones_bd = ones_ref[...]

    def seg_sum(v):
        return _dot(v.astype(BF16), ones_bd)

    def shifted(cur, prev_row, mu_ref):
        first = lax.broadcasted_iota(jnp.int32, cur.shape, 0) == 0
        prev = jnp.where(first, prev_row, pltpu.roll(cur, 1, 0))
        return cur + mu_ref[...] * (prev - cur)

    r2 = lax.broadcasted_iota(jnp.int32, (SUB, SUB), 0)
    c2 = lax.broadcasted_iota(jnp.int32, (SUB, SUB), 1)
    tri = jnp.where((r2 >= c2) & (r2 // CHUNK == c2 // CHUNK), 1.0, 0.0).astype(BF16)
    outs = dict(zip(RWKV_OPERANDS, (rt_ref, at_ref, bt_ref, kt_ref, xv_ref, bh_ref, kh_ref, bonus_ref)))

    def rwkv_prepare():
        prev_rows = [sr_ref[...], sk_ref[...], sv_ref[...], swa_ref[...]]
        mus = (mu_r_ref, mu_k_ref, mu_v_ref, mu_wa_ref)
        xr, xk, xv, xwa = [shifted(c, p, mu) for c, p, mu in zip(raw, prev_rows, mus)]
        w_lin = w0_ref[...] + _dot_x2(jnp.tanh(xwa), wup_ref[...])
        a_lin = a0_ref[...] + _dot(xwa.astype(BF16), aup_ref[...])
        kk = xk * kk_ref[...]
        kk_ss = seg_sum(kk * kk)
        yield [c[tm - 1:tm, :] for c in raw]
        w = -_softplus(-w_lin) - 0.5
        ld = -jnp.exp(w)
        a = _sigmoid(a_lin)
        kk = kk * lax.rsqrt(kk_ss + KK_EPS)
        kmod = xk * (1.0 + (a - 1.0) * ka_ref[...])
        lw = jnp.concatenate([_dot_exact_lhs(tri, ld[sb * SUB:(sb + 1) * SUB, :], 2)
                              for sb in range(tm // SUB)], axis=0)
        rk_sum = seg_sum(xr * kmod * rk_ref[...])
        yield None
        w_inv = jnp.exp(-lw)
        bt = kk * a * w_inv
        kt = kmod * w_inv
        for name, val in (("rt", xr * jnp.exp(lw)), ("at", -kk * jnp.exp(lw - ld)), ("bt", bt), ("kt", kt),
                          ("xv", xv), ("bonus", rk_sum * xv)):
            outs[name][...] = val.astype(BF16)
        for c in range(tm // CHUNK):
            rows = slice(c * CHUNK, (c + 1) * CHUNK)
            w_c = jnp.exp(lw[(c + 1) * CHUNK - 1:(c + 1) * CHUNK, :])
            wend_ref[c:c + 1, :] = w_c
            bh_ref[rows, :] = (bt[rows, :] * w_c).astype(BF16)
            kh_ref[rows, :] = (kt[rows, :] * w_c).astype(BF16)
        yield None

    parts = rwkv_prepare()
    last_rows = next(parts)
    q_ref[...] = group(G_Q).astype(BF16)
    k_ref[...] = group(G_K).astype(BF16)
    next(parts)
    v_ref[...] = group(G_V).astype(BF16)
    zf_ref[...] = group(G_ZF).astype(BF16)
    next(parts)
    zr_ref[...] = group(G_ZR).astype(BF16)
    fl_ref[...] = _dot(ub, w_ref[:, COL_FL:COL_FL + LANES])

    for ref, rowv in zip((sr_ref, sk_ref, sv_ref, swa_ref), last_rows):
        ref[...] = rowv

    @pl.when(i == pl.num_programs(0) - 1)
    def _():
        for out, rowv in zip((lr_ref, lk_ref, lv_ref, lwa_ref), last_rows):
            out[...] = rowv


def _in_proj(rows, norm_w, w_all, prm, prev, tm, tiles_per_seq):
    n = rows.shape[0]
    tile = lambda width: pl.BlockSpec((tm, width), lambda i: (i, 0))
    const = lambda shape: pl.BlockSpec(shape, lambda i: (0,) * len(shape))
    row_w, row_n, up = const((1, D_BRANCH)), const((1, LANES)), const((LANES, D_BRANCH))
    wide = lambda dt: jax.ShapeDtypeStruct((n, D_BRANCH), dt)
    return pl.pallas_call(
        functools.partial(_in_proj_kernel, tm=tm, tiles_per_seq=tiles_per_seq),
        grid=(n // tm,),
        in_specs=[tile(D_MODEL), const((1, D_MODEL)),
                  pl.BlockSpec((D_MODEL, N_COLS), lambda i: (0, 0), pipeline_mode=pl.Buffered(1)),
                  row_w, row_w, row_w, row_n, row_w, up, row_w, up,
                  row_w, row_w, row_w, const((D_BRANCH, D_BRANCH)),
                  row_w, row_w, row_w, row_n],
        out_specs=[tile(D_BRANCH)] * 5 + [tile(LANES)] + [tile(D_BRANCH)] * len(RWKV_OPERANDS)
                  + [pl.BlockSpec((tm // CHUNK, D_BRANCH), lambda i: (i, 0)),
                     row_w, row_w, row_w, row_n],
        out_shape=[wide(BF16), wide(BF16), wide(BF16), wide(BF16), wide(BF16),
                   jax.ShapeDtypeStruct((n, LANES), F32)]
                  + [wide(BF16)] * len(RWKV_OPERANDS)
                  + [jax.ShapeDtypeStruct((n // CHUNK, D_BRANCH), F32)]
                  + [jax.ShapeDtypeStruct((1, D_BRANCH), F32)] * 3 + [jax.ShapeDtypeStruct((1, LANES), F32)],
        scratch_shapes=[pltpu.VMEM((1, D_BRANCH), F32), pltpu.VMEM((1, D_BRANCH), F32),
                        pltpu.VMEM((1, D_BRANCH), F32), pltpu.VMEM((1, LANES), F32)],
        compiler_params=_params(("arbitrary",)),
        name="in_proj",
    )(rows, norm_w, w_all, *prm, *prev)


def _select_matrices():
    pq = np.zeros((LANES, D_BRANCH), np.float32)
    pk = np.zeros((LANES, D_BRANCH), np.float32)
    for h in range(N_HEADS):
        base = (h // 2) * PAIR + (HEAD_DIM if h % 2 == 0 else 0)
        for part in range(N_CPARTS):
            pq[part * N_HEADS + h, base + part] = 1.0
            pq[N_CPARTS * N_HEADS + h, base + N_CPARTS + part] = 1.0
            pk[N_CPARTS * N_HEADS + h, base + part] = 1.0
            pk[part * N_HEADS + h, base + N_CPARTS + part] = -1.0
    return jnp.asarray(pq, BF16), jnp.asarray(pk, BF16)


def _fox_prep_kernel(q_ref, k_ref, v_ref, fl_ref, bf_ref, c0_ref, pq_ref, pk_ref,
                     qo_ref, ko_ref, vo_ref, cl_ref, carry_ref, *, tr, tc, n_pad):
    i = pl.program_id(1)

    @pl.when(i == 0)
    def _():
        carry_ref[...] = c0_ref[...]

    lane = lax.broadcasted_iota(jnp.int32, (tc, LANES), 1)
    row0 = lax.broadcasted_iota(jnp.int32, (tc, LANES), 0) + i * tr
    r2 = lax.broadcasted_iota(jnp.int32, (tc, tc), 0)
    c2 = lax.broadcasted_iota(jnp.int32, (tc, tc), 1)
    tri = jnp.where(r2 >= c2, 1.0, 0.0).astype(BF16)
    ones = jnp.where((lane >= N_CPARTS * N_HEADS) & (lane < (N_CPARTS + 1) * N_HEADS), 1.0, 0.0)
    low = lane < HEAD_DIM
    sel_r = lax.broadcasted_iota(jnp.int32, (PAIR, PAIR), 0)
    sel_c = lax.broadcasted_iota(jnp.int32, (PAIR, PAIR), 1)
    eye = jnp.where(sel_r == sel_c, 1.0, 0.0).astype(BF16)
    vrow = lax.broadcasted_iota(jnp.int32, (PAIR, tc), 0)

    carry = carry_ref[...]
    for sb in range(tr // tc):
        rows = slice(sb * tc, (sb + 1) * tc)
        row = row0 + sb * tc
        x = fl_ref[0, rows, :] + bf_ref[...]
        logf = jnp.minimum(x, 0.0) - jnp.log(1.0 + jnp.exp(-jnp.abs(x)))
        valid = lane < N_HEADS
        if n_pad:
            valid = valid & (row >= n_pad)
        logf = jnp.where(valid, logf, 0.0)
        cum = _dot_exact_lhs(tri, logf, 2) + carry
        carry = cum[tc - 1:tc, :]

        p1 = cum.astype(BF16).astype(F32)
        rem = cum - p1
        p2 = rem.astype(BF16).astype(F32)
        p3 = (rem - p2).astype(BF16).astype(F32)
        cbits = p1 + pltpu.roll(p2, N_HEADS, 1) + pltpu.roll(p3, 2 * N_HEADS, 1) + ones
        cbits_k = cbits
        if n_pad:
            cbits_k = jnp.where((row < n_pad) & (lane < N_HEADS), -NEG, cbits)
        bias_q = _dot(cbits.astype(BF16), pq_ref[...]).astype(BF16)
        bias_k = _dot(cbits_k.astype(BF16), pk_ref[...]).astype(BF16)
        for pr in range(N_PAIRS):
            sl = slice(pr * PAIR, (pr + 1) * PAIR)
            q_sc = q_ref[0, rows, sl] * jnp.asarray(0.125, BF16)
            k_pr = k_ref[0, rows, sl]
            v_t = _dot_nt(eye, v_ref[0, rows, sl])
            for half in range(2):
                h = 2 * pr + half
                own = low if half == 0 else jnp.logical_not(low)
                qo_ref[0, h, rows, :] = jnp.where(own, q_sc, bias_q[:, sl])
                ko_ref[0, h, rows, :] = jnp.where(own, k_pr, bias_k[:, sl])
                own_v = (vrow < HEAD_DIM) if half == 0 else (vrow >= HEAD_DIM)
                vo_ref[0, h, sb] = jnp.where(vrow == ONES_LANE[half], 1.0,
                                             jnp.where(own_v, v_t, 0.0)).astype(BF16)
    carry_ref[...] = carry

    @pl.when(i == pl.num_programs(1) - 1)
    def _():
        cl_ref[0] = carry


def _fox_prep(q, k, v, fl, bf_pad, c0, pq, pk, tr, tc, n_pad):
    b, l, _ = q.shape
    wide = pl.BlockSpec((1, tr, D_BRANCH), lambda bi, i: (bi, i, 0))
    head_out = pl.BlockSpec((1, N_HEADS, tr, LANES), lambda bi, i: (bi, 0, i, 0))
    row128 = pl.BlockSpec((1, LANES), lambda bi, i: (0, 0))
    sel = pl.BlockSpec((LANES, D_BRANCH), lambda bi, i: (0, 0))
    return pl.pallas_call(
        functools.partial(_fox_prep_kernel, tr=tr, tc=tc, n_pad=n_pad),
        grid=(b, l // tr),
        in_specs=[wide, wide, wide,
                  pl.BlockSpec((1, tr, LANES), lambda bi, i: (bi, i, 0)),
                  row128, row128, sel, sel],
        out_specs=[head_out, head_out,
                   pl.BlockSpec((1, N_HEADS, tr // tc, PAIR, tc), lambda bi, i: (bi, 0, i, 0, 0)),
                   pl.BlockSpec((1, 1, LANES), lambda bi, i: (bi, 0, 0))],
        out_shape=[jax.ShapeDtypeStruct((b, N_HEADS, l, LANES), BF16)] * 2
                  + [jax.ShapeDtypeStruct((b, N_HEADS, l // tc, PAIR, tc), BF16),
                     jax.ShapeDtypeStruct((b, 1, LANES), F32)],
        scratch_shapes=[pltpu.VMEM((1, LANES), F32)],
        compiler_params=_params(("parallel", "arbitrary")),
        name="fox_prep",
    )(q, k, v, fl, bf_pad, c0, pq, pk)


def _attn_kernel(q_ref, kpre_ref, vpre_ref, k_ref, v_ref, z_ref, o_ref, s_ref, *, tq):
    qi = pl.program_id(2)
    heads = range(ATTN_HEADS)
    qs = [q_ref[0, h] for h in heads]

    def rowmax(s):
        return jnp.max(s, axis=0, keepdims=True)

    def k_chunk(h, idx):
        return k_ref[0, h, pl.ds(pl.multiple_of(idx * tq, tq), tq), :]

    def v_chunk(h, idx):
        return v_ref[0, h, idx]

    def scores(idx):
        return [_dot_nt(k_chunk(h, idx), qs[h]) for h in heads]

    def softmax_pv(carry, s, rmax, idx):
        out = []
        for h in heads:
            m, acc = carry[h]
            m_new = jnp.maximum(m, rmax[h])
            p = jnp.exp(s[h] - m_new).astype(BF16)
            out.append((m_new, jnp.exp(m - m_new) * acc + _dot(v_chunk(h, idx), p)))
        return out

    n_zero = PREFIX_ROWS - N_META
    s_pre = [_dot_nt(kpre_ref[0, h, n_zero:, :], qs[h]) for h in heads]
    key = lax.broadcasted_iota(jnp.int32, (tq, tq), 0)
    qry = lax.broadcasted_iota(jnp.int32, (tq, tq), 1)
    s_dia = [jnp.where(key <= qry, s, NEG) for s in scores(qi)]
    s_nxt = scores(0)
    carry = []
    for h in heads:
        m0 = jnp.maximum(rowmax(s_pre[h]), rowmax(s_dia[h]))
        p_pre = jnp.concatenate([jnp.zeros((n_zero, tq), BF16), jnp.exp(s_pre[h] - m0).astype(BF16)], axis=0)
        p_dia = jnp.exp(s_dia[h] - m0).astype(BF16)
        carry.append((m0, _dot(vpre_ref[0, h, 0], p_pre) + _dot(v_chunk(h, qi), p_dia)))
        s_ref[h] = s_nxt[h]
    rmax = [rowmax(s) for s in s_nxt]

    def step2(t, state):
        carry, rmax = state
        k0 = 2 * t
        s_a = [s_ref[h] for h in heads]
        s_b = scores(k0 + 1)
        carry = softmax_pv(carry, s_a, rmax, k0)
        rmax_b = [rowmax(s) for s in s_b]
        s_c = scores(k0 + 2)
        carry = softmax_pv(carry, s_b, rmax_b, k0 + 1)
        for h in heads:
            s_ref[h] = s_c[h]
        return carry, [rowmax(s) for s in s_c]

    def step1(k0, state):
        carry, rmax = state
        s_a = [s_ref[h] for h in heads]
        s_b = scores(k0 + 1)
        carry = softmax_pv(carry, s_a, rmax, k0)
        for h in heads:
            s_ref[h] = s_b[h]
        return carry, [rowmax(s) for s in s_b]

    steps = jnp.maximum(qi - 1, 0)
    pairs = steps // 2
    state = lax.fori_loop(0, pairs, step2, (carry, rmax))
    carry, rmax = lax.fori_loop(2 * pairs, steps, step1, state)

    some = qi > 0
    last = jnp.maximum(qi - 1, 0)
    s_fin = [jnp.where(some, s_ref[h], NEG) for h in heads]
    r_fin = [jnp.where(some, r, NEG) for r in rmax]
    accs = [acc for _, acc in softmax_pv(carry, s_fin, r_fin, last)]
    vrow = lax.broadcasted_iota(jnp.int32, (PAIR, tq), 0)
    for pr in range(ATTN_HEADS // 2):
        acc0, acc1 = accs[2 * pr], accs[2 * pr + 1]
        l0 = acc0[ONES_LANE[0]:ONES_LANE[0] + 1, :]
        l1 = acc1[ONES_LANE[1]:ONES_LANE[1] + 1, :]
        o_t = jnp.where(vrow < HEAD_DIM, acc0 / l0, acc1 / l1)
        z = z_ref[0, :, pr * PAIR:(pr + 1) * PAIR].astype(F32)
        o_ref[0, :, pr * PAIR:(pr + 1) * PAIR] = (o_t.T * (z * _sigmoid(z))).astype(o_ref.dtype)


def _fox_attn(qp, kp, vp, kpre, vpre, zf, tq):
    b, _, l, _ = qp.shape
    g = ATTN_HEADS
    out_block = pl.BlockSpec((1, tq, g * HEAD_DIM), lambda bi, p, qi: (bi, qi, p))
    assert vp.shape[-1] == tq and vpre.shape[-1] == PREFIX_ROWS
    pre = pl.BlockSpec((1, g, PREFIX_ROWS, LANES), lambda bi, p, qi: (0, p, 0, 0))
    pre_v = pl.BlockSpec((1, g, 1, PAIR, PREFIX_ROWS), lambda bi, p, qi: (0, p, 0, 0, 0))
    full = pl.BlockSpec((1, g, l, LANES), lambda bi, p, qi: (bi, p, 0, 0))
    full_v = pl.BlockSpec((1, g, l // tq, PAIR, tq), lambda bi, p, qi: (bi, p, 0, 0, 0))
    return pl.pallas_call(
        functools.partial(_attn_kernel, tq=tq),
        grid=(b, N_HEADS // g, l // tq),
        in_specs=[pl.BlockSpec((1, g, tq, LANES), lambda bi, p, qi: (bi, p, qi, 0)),
                  pre, pre_v, full, full_v, out_block],
        out_specs=out_block,
        out_shape=jax.ShapeDtypeStruct((b, l, D_BRANCH), BF16),
        scratch_shapes=[pltpu.VMEM((g, tq, tq), F32)],
        compiler_params=_params(("parallel", "parallel", "parallel")),
        name="fox_attn",
    )(qp, kpre, vpre, kp, vp, zf)


def _rwkv_kernel(rt_ref, at_ref, bt_ref, kt_ref, xv_ref, bh_ref, kh_ref, bonus_ref, z_ref, wend_ref,
                 gnw_ref, gnb_ref, ones_ref, z0_ref,
                 y_ref, zf_ref, state_ref, yacc_ref, *, tr):
    t = pl.program_id(1)
    last = t == pl.num_programs(1) - 1

    @pl.when(t == 0)
    def _():
        state_ref[...] = z0_ref[...]

    ones_bd = ones_ref[...]

    def seg_sum(x):
        return _dot(x.astype(BF16), ones_bd)

    operand = dict(rt=rt_ref, at=at_ref, bt=bt_ref, kt=kt_ref, xv=xv_ref, bh=bh_ref, kh=kh_ref)

    col = lax.broadcasted_iota(jnp.int32, (CHUNK, PAIR), 1)
    trow = lax.broadcasted_iota(jnp.int32, (CHUNK, PAIR), 0)
    tcol = col % CHUNK
    head_a = col < HEAD_DIM
    strict = trow > tcol
    incl = trow >= tcol
    eye_sbs = jnp.where(trow == tcol, 1.0, 0.0).astype(F32)
    sq_r = lax.broadcasted_iota(jnp.int32, (PAIR, PAIR), 0)
    sq_c = lax.broadcasted_iota(jnp.int32, (PAIR, PAIR), 1)
    same_head = (sq_r // HEAD_DIM) == (sq_c // HEAD_DIM)
    eye_sq = sq_r == sq_c

    def stack(x):
        zero = jnp.zeros_like(x)
        return jnp.concatenate([jnp.where(head_a, x, zero), jnp.where(head_a, zero, x)], axis=0)

    def fold(sq):
        sq = jnp.where(same_head, sq, 0.0)
        return sq[:CHUNK] + sq[CHUNK:]

    def tile(name, c, p):
        return operand[name][0, c * CHUNK:(c + 1) * CHUNK, p * PAIR:(p + 1) * PAIR]

    n_chunks = tr // CHUNK
    units =[(c, p) for c in range(n_chunks) for p in range(N_PAIRS)]
    nu = range(len(units))
    wc = [wend_ref[0, c:c + 1, p * PAIR:(p + 1) * PAIR] for c, p in units]
    rt, at, bt, kt, xv, bh, kh = ([tile(name, c, p) for c, p in units]
                                  for name in ("rt", "at", "bt", "kt", "xv", "bh", "kh"))
    bt_s = [stack(x) for x in bt]
    kt_s = [stack(x) for x in kt]
    xv_s = [stack(x) for x in xv]

    ar = [jnp.concatenate([at[u], rt[u]], axis=0) for u in nu]
    x_b = [_dot_nt(ar[u], bt_s[u]) for u in nu]
    x_k = [_dot_nt(ar[u], kt_s[u]) for u in nu]
    a_ab = [jnp.where(strict, x[:CHUNK], 0.0) for x in x_b]
    a_rb = [jnp.where(incl, x[CHUNK:], 0.0).astype(BF16) for x in x_b]
    a_ak = [jnp.where(strict, x[:CHUNK], 0.0).astype(BF16) for x in x_k]
    a_rk = [jnp.where(incl, x[CHUNK:], 0.0).astype(BF16) for x in x_k]

    ab = [a.astype(BF16) for a in a_ab]
    pw = [_dot(ab[u], stack(ab[u])) for u in nu]
    tinv = [eye_sbs + a_ab[u] for u in nu]
    for level in range(1, 6):
        pb = [x.astype(BF16) for x in pw]
        if level < 5:
            both = [_dot(jnp.concatenate([tinv[u].astype(BF16), pb[u]], axis=0), stack(pb[u])) for u in nu]
            tinv = [tinv[u] + both[u][:CHUNK] for u in nu]
            pw = [both[u][CHUNK:] for u in nu]
        else:
            tinv = [tinv[u] + _dot(tinv[u].astype(BF16), stack(pb[u])) for u in nu]
    tb = [x.astype(BF16) for x in tinv]

    ap = [_dot(tb[u], stack(at[u])).astype(BF16) for u in nu]
    akv = [_dot(a_ak[u], xv_s[u]).astype(BF16) for u in nu]
    vp = [_dot(tb[u], stack(akv[u])).astype(BF16) for u in nu]
    ap_s = [stack(x) for x in ap]
    vp_s = [stack(x) for x in vp]
    m_sbs = [fold(jnp.where(eye_sq, wc[u], 0.0) + _dot_tn(bh[u], ap[u])).astype(BF16) for u in nu]
    g_sbs = [fold(_dot_tn(jnp.concatenate([bh[u], kh[u]], axis=0), jnp.concatenate([vp[u], xv[u]], axis=0)))
             for u in nu]
    rp = [(rt[u].astype(F32) + _dot(a_rb[u], ap_s[u])).astype(BF16) for u in nu]
    y0 = [_dot(jnp.concatenate([a_rb[u], a_rk[u]], axis=1), jnp.concatenate([vp_s[u], xv_s[u]], axis=0))
          for u in nu]

    for u, (c, p) in enumerate(units):
        z_bd = stack(state_ref[p].astype(BF16))
        both = _dot(jnp.concatenate([rp[u], m_sbs[u]], axis=0), z_bd)
        yacc_ref[c * CHUNK:(c + 1) * CHUNK, p * PAIR:(p + 1) * PAIR] = both[:CHUNK] + y0[u]
        state_ref[p] = both[CHUNK:] + g_sbs[u]

    @pl.when(last)
    def _():
        zf_ref[0] = state_ref[...]

    inv_n = 1.0 / HEAD_DIM
    y = yacc_ref[...]
    mean = seg_sum(y) * inv_n
    d = y - mean
    var = seg_sum(d * d) * inv_n
    yn = d * lax.rsqrt(var + GN_EPS) * gnw_ref[...] + gnb_ref[...]
    z = z_ref[0].astype(F32)
    y_ref[0] = ((yn + bonus_ref[0].astype(F32)) * (z * _sigmoid(z))).astype(y_ref.dtype)


def _rwkv(ops, z, wend, gn_w, gn_b, ones_bd, z0, tr):
    b, l, _ = z.shape
    wide = pl.BlockSpec((1, tr, D_BRANCH), lambda bi, t: (bi, t, 0))
    row_w = pl.BlockSpec((1, D_BRANCH), lambda bi, t: (0, 0))
    ones = pl.BlockSpec((D_BRANCH, D_BRANCH), lambda bi, t: (0, 0))
    st_in = pl.BlockSpec((N_PAIRS, CHUNK, PAIR), lambda bi, t: (0, 0, 0))
    st_out = pl.BlockSpec((1, N_PAIRS, CHUNK, PAIR), lambda bi, t: (bi, 0, 0, 0))
    return pl.pallas_call(
        functools.partial(_rwkv_kernel, tr=tr),
        grid=(b, l // tr),
        in_specs=[wide] * (len(RWKV_OPERANDS) + 1)
                 + [pl.BlockSpec((1, tr // CHUNK, D_BRANCH), lambda bi, t: (bi, t, 0)),
                    row_w, row_w, ones, st_in],
        out_specs=[wide, st_out],
        out_shape=[jax.ShapeDtypeStruct((b, l, D_BRANCH), BF16),
                   jax.ShapeDtypeStruct((b, N_PAIRS, CHUNK, PAIR), F32)],
        scratch_shapes=[pltpu.VMEM((N_PAIRS, CHUNK, PAIR), F32), pltpu.VMEM((tr, D_BRANCH), F32)],
        compiler_params=_params(("parallel", "arbitrary")),
        name="rwkv",
    )(*ops, z, wend, gn_w, gn_b, ones_bd, z0)


def _out_proj_kernel(yf_ref, yr_ref, x_ref, w_ref, fnw_ref, out_ref):
    mix = _dot(yf_ref[...], w_ref[:D_BRANCH, :]) + _dot(yr_ref[...], w_ref[D_BRANCH:, :])
    h = x_ref[...] + mix
    out_ref[...] = h * lax.rsqrt(jnp.mean(h * h, axis=-1, keepdims=True) + NORM_EPS) * fnw_ref[...]


def _out_proj(yf, yr, x, w_out, fnw, tm):
    n = x.shape[0]
    half = pl.BlockSpec((tm, D_BRANCH), lambda i: (i, 0))
    full = pl.BlockSpec((tm, D_MODEL), lambda i: (i, 0))
    return pl.pallas_call(
        _out_proj_kernel,
        grid=(n // tm,),
        in_specs=[half, half, full,
                  pl.BlockSpec((D_MODEL, D_MODEL), lambda i: (0, 0)),
                  pl.BlockSpec((1, D_MODEL), lambda i: (0, 0))],
        out_specs=full,
        out_shape=jax.ShapeDtypeStruct((n, D_MODEL), F32),
        compiler_params=_params(("parallel",)),
        name="out_proj",
    )(yf, yr, x, w_out, fnw)


def _tiles(b, l):
    assert l % 512 == 0
    t_prep = 1024 if l % 1024 == 0 else 512
    t_out = 2048 if (b * l) % 2048 == 0 else 512
    return 512, 512, 512, t_prep, t_out


def kernel(x, meta, norm_w, w_in, b_f, mu_shift, w0, w_up, a0, a_up, k_k, k_a, r_k, gn_w, gn_b,
           w_out, final_norm_w):
    b, l, d = x.shape
    assert d == D_MODEL and norm_w.shape[0] == 1
    tm, t_attn, t_rwkv, t_prep, t_out = _tiles(b, l)

    wi = w_in[0].astype(BF16)
    o = 0
    cols = {}
    for name, width in (("q", D_BRANCH), ("k", D_BRANCH), ("v", D_BRANCH), ("fl", N_HEADS), ("zf", D_BRANCH),
                        ("r", D_BRANCH), ("rk", D_BRANCH), ("rv", D_BRANCH), ("wd", RANK), ("ad", RANK),
                        ("zr", D_BRANCH)):
        cols[name] = wi[:, o:o + width]
        o += width
    w_all = jnp.concatenate(
        [cols[n] for n in ("q", "k", "v", "zf", "r", "rk", "rv", "zr", "wd", "ad", "fl")]
        + [jnp.zeros((D_MODEL, LANES - N_HEADS), BF16)], axis=1)

    row = lambda vec: vec.reshape(1, -1).astype(F32)
    mu = mu_shift[0]
    mu_r, mu_k, mu_v = (row(mu[i * D_BRANCH:(i + 1) * D_BRANCH]) for i in range(3))
    mu_wa = row(mu[3 * D_BRANCH:])
    zeros_up = jnp.zeros((RANK, D_BRANCH), F32)
    wup_pad = jnp.concatenate([w_up[0], zeros_up], axis=0)
    aup_pad = jnp.concatenate([zeros_up, a_up[0]], axis=0)
    hid = np.arange(D_BRANCH) // HEAD_DIM
    ones_bd = jnp.asarray(hid[:, None] == hid[None, :], BF16)
    shift_prm = (mu_r, mu_k, mu_v, mu_wa, row(w0[0]), wup_pad.astype(BF16), row(a0[0]), aup_pad.astype(BF16),
                 row(k_k[0]), row(k_a[0]), row(r_k[0]), ones_bd)
    gnw, gnb = row(gn_w[0]), row(gn_b[0])
    bf_pad = jnp.concatenate([b_f[0], jnp.zeros((LANES - N_HEADS,), F32)]).reshape(1, LANES)
    pq, pk = _select_matrices()
    nw = row(norm_w[0])
    n_ops = len(RWKV_OPERANDS)

    pre_rows = jnp.concatenate([jnp.zeros((PREFIX_ROWS - N_META, D_MODEL), F32), meta.astype(F32)], axis=0)
    zero_w = jnp.zeros((1, D_BRANCH), F32)
    pre = _in_proj(pre_rows, nw, w_all, shift_prm, (zero_w, zero_w, zero_w, jnp.zeros((1, LANES), F32)),
                   PREFIX_ROWS, 1)
    pq_, pk_, pv_, _, pzr_, pfl_ = pre[:6]
    pre_ops, pre_wend, last_raw = pre[6:6 + n_ops], pre[6 + n_ops], pre[7 + n_ops:]
    lead = lambda a: a[None]
    _, kpre, vpre, c_pre = _fox_prep(lead(pq_), lead(pk_), lead(pv_), lead(pfl_), bf_pad,
                                     jnp.zeros((1, LANES), F32), pq, pk, PREFIX_ROWS, PREFIX_ROWS,
                                     PREFIX_ROWS - N_META)
    _, z_pre = _rwkv([lead(a) for a in pre_ops], lead(pzr_), lead(pre_wend), gnw, gnb, ones_bd,
                     jnp.zeros((N_PAIRS, CHUNK, PAIR), F32), PREFIX_ROWS)

    xf = x.reshape(b * l, D_MODEL)
    main = _in_proj(xf, nw, w_all, shift_prm, last_raw, tm, l // tm)
    q_, k_, v_, zf_, zr_, fl_ = main[:6]
    bl = lambda a: a.reshape(b, l, a.shape[-1])
    qp, kp, vp, _ = _fox_prep(bl(q_), bl(k_), bl(v_), bl(fl_), bf_pad, c_pre[0], pq, pk, t_prep, t_attn, 0)
    y_fox = _fox_attn(qp, kp, vp, kpre, vpre, bl(zf_), t_attn)
    y_rwkv, _ = _rwkv([bl(a) for a in main[6:6 + n_ops]], bl(zr_),
                      main[6 + n_ops].reshape(b, l // CHUNK, D_BRANCH), gnw, gnb, ones_bd, z_pre[0], t_rwkv)
    out = _out_proj(y_fox.reshape(b * l, D_BRANCH), y_rwkv.reshape(b * l, D_BRANCH), xf,
                    w_out[0].astype(BF16), row(final_norm_w), t_out)
    return out.reshape(b, l, D_MODEL)
```

```python
import functools

import numpy as np
import jax
import jax.numpy as jnp
from jax import lax
from jax.experimental import pallas as pl
from jax.experimental.pallas import tpu as pltpu

F32 = jnp.float32
BF16 = jnp.bfloat16

D_MODEL = 1024
N_META = 16
HEAD_DIM = 64
N_HEADS = 8
D_BRANCH = N_HEADS * HEAD_DIM
N_PAIRS = N_HEADS // 2
RANK = 64
NORM_EPS = 1e-6
GN_EPS = 64e-5
KK_EPS = 1e-12
NEG = -1e30

LANES = 128
PREFIX_ROWS = 128
CHUNK = 64
SUB = 2 * CHUNK
PAIR = 2 * HEAD_DIM
VMEM_LIMIT = 56 * 1024 * 1024

N_WIDE = 8
COL_WA = N_WIDE * D_BRANCH
COL_FL = COL_WA + LANES
N_COLS = COL_FL + LANES

N_CPARTS = 3
ONES_LANE = (HEAD_DIM, 0)
ATTN_HEADS = 8


def _dot(a, b):
    return jnp.dot(a, b, preferred_element_type=F32)


def _pieces(x, n):
    out = []
    for _ in range(n - 1):
        p = x.astype(BF16)
        out.append(p)
        x = x - p.astype(F32)
    out.append(x.astype(BF16))
    return out


def _dot_exact_lhs(a_bf16, x, n):
    acc = None
    for p in _pieces(x, n):
        t = _dot(a_bf16, p)
        acc = t if acc is None else acc + t
    return acc


def _dot_x2(x, w_bf16):
    x_hi, x_lo = _pieces(x, 2)
    return _dot(x_hi, w_bf16) + _dot(x_lo, w_bf16)


def _dot_nt(a, b):
    return lax.dot_general(a, b, (((1,), (1,)), ((), ())), preferred_element_type=F32)


def _dot_tn(a, b):
    return lax.dot_general(a, b, (((0,), (0,)), ((), ())), preferred_element_type=F32)


def _softplus(x):
    return jnp.maximum(x, 0.0) + jnp.log(1.0 + jnp.exp(-jnp.abs(x)))


def _sigmoid(x):
    return 1.0 / (1.0 + jnp.exp(-x))


def _params(sem):
    return pltpu.CompilerParams(dimension_semantics=sem, vmem_limit_bytes=VMEM_LIMIT)


G_Q, G_K, G_V, G_ZF, G_R, G_RK, G_RV, G_ZR = range(N_WIDE)
RWKV_OPERANDS = ("rt", "at", "bt", "kt", "xv", "bh", "kh", "bonus")


def _in_proj_kernel(x_ref, nw_ref, w_ref,
                    mu_r_ref, mu_k_ref, mu_v_ref, mu_wa_ref, w0_ref, wup_ref,
                    a0_ref, aup_ref, kk_ref, ka_ref, rk_ref, ones_ref,
                    pr_ref, pk_ref, pv_ref, pwa_ref,
                    q_ref, k_ref, v_ref, zf_ref, zr_ref, fl_ref,
                    rt_ref, at_ref, bt_ref, kt_ref, xv_ref, bh_ref, kh_ref, bonus_ref, wend_ref,
                    lr_ref, lk_ref, lv_ref, lwa_ref,
                    sr_ref, sk_ref, sv_ref, swa_ref, *, tm, tiles_per_seq):
    i = pl.program_id(0)

    @pl.when(lax.rem(i, tiles_per_seq) == 0)
    def _():
        sr_ref[...] = pr_ref[...]
        sk_ref[...] = pk_ref[...]
        sv_ref[...] = pv_ref[...]
        swa_ref[...] = pwa_ref[...]

    x = x_ref[...]
    u = x * lax.rsqrt(jnp.mean(x * x, axis=-1, keepdims=True) + NORM_EPS) * nw_ref[...]
    ub = u.astype(BF16)

    def group(g):
        return _dot(ub, w_ref[:, g * D_BRANCH:(g + 1) * D_BRANCH])

    raw = [group(G_R), group(G_RK), group(G_RV), _dot(ub, w_ref[:, COL_WA:COL_WA + LANES])]
    ones_bd = ones_ref[...]

    def seg_sum(v):
        return _dot(v.astype(BF16), ones_bd)

    def shifted(cur, prev_row, mu_ref):
        first = lax.broadcasted_iota(jnp.int32, cur.shape, 0) == 0
        prev = jnp.where(first, prev_row, pltpu.roll(cur, 1, 0))
        return cur + mu_ref[...] * (prev - cur)

    r2 = lax.broadcasted_iota(jnp.int32, (SUB, SUB), 0)
    c2 = lax.broadcasted_iota(jnp.int32, (SUB, SUB), 1)
    tri = jnp.where((r2 >= c2) & (r2 // CHUNK == c2 // CHUNK), 1.0, 0.0).astype(BF16)
    outs = dict(zip(RWKV_OPERANDS, (rt_ref, at_ref, bt_ref, kt_ref, xv_ref, bh_ref, kh_ref, bonus_ref)))

    def rwkv_prepare():
        prev_rows = [sr_ref[...], sk_ref[...], sv_ref[...], swa_ref[...]]
        mus = (mu_r_ref, mu_k_ref, mu_v_ref, mu_wa_ref)
        xr, xk, xv, xwa = [shifted(c, p, mu) for c, p, mu in zip(raw, prev_rows, mus)]
        w_lin = w0_ref[...] + _dot_x2(jnp.tanh(xwa), wup_ref[...])
        a_lin = a0_ref[...] + _dot(xwa.astype(BF16), aup_ref[...])
        kk = xk * kk_ref[...]
        kk_ss = seg_sum(kk * kk)
        yield [c[tm - 1:tm, :] for c in raw]
        w = -_softplus(-w_lin) - 0.5
        ld = -jnp.exp(w)
        a = _sigmoid(a_lin)
        kk = kk * lax.rsqrt(kk_ss + KK_EPS)
        kmod = xk * (1.0 + (a - 1.0) * ka_ref[...])
        lw = jnp.concatenate([_dot_exact_lhs(tri, ld[sb * SUB:(sb + 1) * SUB, :], 2)
                              for sb in range(tm // SUB)], axis=0)
        rk_sum = seg_sum(xr * kmod * rk_ref[...])
        yield None
        w_inv = jnp.exp(-lw)
        bt = kk * a * w_inv
        kt = kmod * w_inv
        for name, val in (("rt", xr * jnp.exp(lw)), ("at", -kk * jnp.exp(lw - ld)), ("bt", bt), ("kt", kt),
                          ("xv", xv), ("bonus", rk_sum * xv)):
            outs[name][...] = val.astype(BF16)
        for c in range(tm // CHUNK):
            rows = slice(c * CHUNK, (c + 1) * CHUNK)
            w_c = jnp.exp(lw[(c + 1) * CHUNK - 1:(c + 1) * CHUNK, :])
            wend_ref[c:c + 1, :] = w_c
            bh_ref[rows, :] = (bt[rows, :] * w_c).astype(BF16)
            kh_ref[rows, :] = (kt[rows, :] * w_c).astype(BF16)
        yield None

    parts = rwkv_prepare()
    last_rows = next(parts)
    q_ref[...] = group(G_Q).astype(BF16)
    k_ref[...] = group(G_K).astype(BF16)
    next(parts)
    v_ref[...] = group(G_V).astype(BF16)
    zf_ref[...] = group(G_ZF).astype(BF16)
    next(parts)
    zr_ref[...] = group(G_ZR).astype(BF16)
    fl_ref[...] = _dot(ub, w_ref[:, COL_FL:COL_FL + LANES])

    for ref, rowv in zip((sr_ref, sk_ref, sv_ref, swa_ref), last_rows):
        ref[...] = rowv

    @pl.when(i == pl.num_programs(0) - 1)
    def _():
        for out, rowv in zip((lr_ref, lk_ref, lv_ref, lwa_ref), last_rows):
            out[...] = rowv


def _in_proj(rows, norm_w, w_all, prm, prev, tm, tiles_per_seq):
    n = rows.shape[0]
    tile = lambda width: pl.BlockSpec((tm, width), lambda i: (i, 0))
    const = lambda shape: pl.BlockSpec(shape, lambda i: (0,) * len(shape))
    row_w, row_n, up = const((1, D_BRANCH)), const((1, LANES)), const((LANES, D_BRANCH))
    wide = lambda dt: jax.ShapeDtypeStruct((n, D_BRANCH), dt)
    return pl.pallas_call(
        functools.partial(_in_proj_kernel, tm=tm, tiles_per_seq=tiles_per_seq),
        grid=(n // tm,),
        in_specs=[tile(D_MODEL), const((1, D_MODEL)),
                  pl.BlockSpec((D_MODEL, N_COLS), lambda i: (0, 0), pipeline_mode=pl.Buffered(1)),
                  row_w, row_w, row_w, row_n, row_w, up, row_w, up,
                  row_w, row_w, row_w, const((D_BRANCH, D_BRANCH)),
                  row_w, row_w, row_w, row_n],
        out_specs=[tile(D_BRANCH)] * 5 + [tile(LANES)] + [tile(D_BRANCH)] * len(RWKV_OPERANDS)
                  + [pl.BlockSpec((tm // CHUNK, D_BRANCH), lambda i: (i, 0)),
                     row_w, row_w, row_w, row_n],
        out_shape=[wide(BF16), wide(BF16), wide(BF16), wide(BF16), wide(BF16),
                   jax.ShapeDtypeStruct((n, LANES), F32)]
                  + [wide(BF16)] * len(RWKV_OPERANDS)
                  + [jax.ShapeDtypeStruct((n // CHUNK, D_BRANCH), F32)]
                  + [jax.ShapeDtypeStruct((1, D_BRANCH), F32)] * 3 + [jax.ShapeDtypeStruct((1, LANES), F32)],
        scratch_shapes=[pltpu.VMEM((1, D_BRANCH), F32), pltpu.VMEM((1, D_BRANCH), F32),
                        pltpu.VMEM((1, D_BRANCH), F32), pltpu.VMEM((1, LANES), F32)],
        compiler_params=_params(("arbitrary",)),
        name="in_proj",
    )(rows, norm_w, w_all, *prm, *prev)


def _select_matrices():
    pq = np.zeros((LANES, D_BRANCH), np.float32)
    pk = np.zeros((LANES, D_BRANCH), np.float32)
    for h in range(N_HEADS):
        base = (h // 2) * PAIR + (HEAD_DIM if h % 2 == 0 else 0)
        for part in range(N_CPARTS):
            pq[part * N_HEADS + h, base + part] = 1.0
            pq[N_CPARTS * N_HEADS + h, base + N_CPARTS + part] = 1.0
            pk[N_CPARTS * N_HEADS + h, base + part] = 1.0
            pk[part * N_HEADS + h, base + N_CPARTS + part] = -1.0
    return jnp.asarray(pq, BF16), jnp.asarray(pk, BF16)


def _fox_prep_kernel(q_ref, k_ref, v_ref, fl_ref, bf_ref, c0_ref, pq_ref, pk_ref,
                     qo_ref, ko_ref, vo_ref, cl_ref, carry_ref, *, tr, tc, n_pad):
    i = pl.program_id(1)

    @pl.when(i == 0)
    def _():
        carry_ref[...] = c0_ref[...]

    lane = lax.broadcasted_iota(jnp.int32, (tc, LANES), 1)
    row0 = lax.broadcasted_iota(jnp.int32, (tc, LANES), 0) + i * tr
    r2 = lax.broadcasted_iota(jnp.int32, (tc, tc), 0)
    c2 = lax.broadcasted_iota(jnp.int32, (tc, tc), 1)
    tri = jnp.where(r2 >= c2, 1.0, 0.0).astype(BF16)
    ones = jnp.where((lane >= N_CPARTS * N_HEADS) & (lane < (N_CPARTS + 1) * N_HEADS), 1.0, 0.0)
    low = lane < HEAD_DIM
    sel_r = lax.broadcasted_iota(jnp.int32, (PAIR, PAIR), 0)
    sel_c = lax.broadcasted_iota(jnp.int32, (PAIR, PAIR), 1)
    eye = jnp.where(sel_r == sel_c, 1.0, 0.0).astype(BF16)
    vrow = lax.broadcasted_iota(jnp.int32, (PAIR, tc), 0)

    carry = carry_ref[...]
    for sb in range(tr // tc):
        rows = slice(sb * tc, (sb + 1) * tc)
        row = row0 + sb * tc
        x = fl_ref[0, rows, :] + bf_ref[...]
        logf = jnp.minimum(x, 0.0) - jnp.log(1.0 + jnp.exp(-jnp.abs(x)))
        valid = lane < N_HEADS
        if n_pad:
            valid = valid & (row >= n_pad)
        logf = jnp.where(valid, logf, 0.0)
        cum = _dot_exact_lhs(tri, logf, 2) + carry
        carry = cum[tc - 1:tc, :]

        p1 = cum.astype(BF16).astype(F32)
        rem = cum - p1
        p2 = rem.astype(BF16).astype(F32)
        p3 = (rem - p2).astype(BF16).astype(F32)
        cbits = p1 + pltpu.roll(p2, N_HEADS, 1) + pltpu.roll(p3, 2 * N_HEADS, 1) + ones
        cbits_k = cbits
        if n_pad:
            cbits_k = jnp.where((row < n_pad) & (lane < N_HEADS), -NEG, cbits)
        bias_q = _dot(cbits.astype(BF16), pq_ref[...]).astype(BF16)
        bias_k = _dot(cbits_k.astype(BF16), pk_ref[...]).astype(BF16)
        for pr in range(N_PAIRS):
            sl = slice(pr * PAIR, (pr + 1) * PAIR)
            q_sc = q_ref[0, rows, sl] * jnp.asarray(0.125, BF16)
            k_pr = k_ref[0, rows, sl]
            v_t = _dot_nt(eye, v_ref[0, rows, sl])
            for half in range(2):
                h = 2 * pr + half
                own = low if half == 0 else jnp.logical_not(low)
                qo_ref[0, h, rows, :] = jnp.where(own, q_sc, bias_q[:, sl])
                ko_ref[0, h, rows, :] = jnp.where(own, k_pr, bias_k[:, sl])
                own_v = (vrow < HEAD_DIM) if half == 0 else (vrow >= HEAD_DIM)
                vo_ref[0, h, sb] = jnp.where(vrow == ONES_LANE[half], 1.0,
                                             jnp.where(own_v, v_t, 0.0)).astype(BF16)
    carry_ref[...] = carry

    @pl.when(i == pl.num_programs(1) - 1)
    def _():
        cl_ref[0] = carry


def _fox_prep(q, k, v, fl, bf_pad, c0, pq, pk, tr, tc, n_pad):
    b, l, _ = q.shape
    wide = pl.BlockSpec((1, tr, D_BRANCH), lambda bi, i: (bi, i, 0))
    head_out = pl.BlockSpec((1, N_HEADS, tr, LANES), lambda bi, i: (bi, 0, i, 0))
    row128 = pl.BlockSpec((1, LANES), lambda bi, i: (0, 0))
    sel = pl.BlockSpec((LANES, D_BRANCH), lambda bi, i: (0, 0))
    return pl.pallas_call(
        functools.partial(_fox_prep_kernel, tr=tr, tc=tc, n_pad=n_pad),
        grid=(b, l // tr),
        in_specs=[wide, wide, wide,
                  pl.BlockSpec((1, tr, LANES), lambda bi, i: (bi, i, 0)),
                  row128, row128, sel, sel],
        out_specs=[head_out, head_out,
                   pl.BlockSpec((1, N_HEADS, tr // tc, PAIR, tc), lambda bi, i: (bi, 0, i, 0, 0)),
                   pl.BlockSpec((1, 1, LANES), lambda bi, i: (bi, 0, 0))],
        out_shape=[jax.ShapeDtypeStruct((b, N_HEADS, l, LANES), BF16)] * 2
                  + [jax.ShapeDtypeStruct((b, N_HEADS, l // tc, PAIR, tc), BF16),
                     jax.ShapeDtypeStruct((b, 1, LANES), F32)],
        scratch_shapes=[pltpu.VMEM((1, LANES), F32)],
        compiler_params=_params(("parallel", "arbitrary")),
        name="fox_prep",
    )(q, k, v, fl, bf_pad, c0, pq, pk)


def _attn_kernel(q_ref, kpre_ref, vpre_ref, k_ref, v_ref, z_ref, o_ref, s_ref, *, tq):
    qi = pl.program_id(2)
    heads = range(ATTN_HEADS)
    qs = [q_ref[0, h] for h in heads]

    def rowmax(s):
        return jnp.max(s, axis=0, keepdims=True)

    def k_chunk(h, idx):
        return k_ref[0, h, pl.ds(pl.multiple_of(idx * tq, tq), tq), :]

    def v_chunk(h, idx):
        return v_ref[0, h, idx]

    def scores(idx):
        return [_dot_nt(k_chunk(h, idx), qs[h]) for h in heads]

    def softmax_pv(carry, s, rmax, idx):
        out = []
        for h in heads:
            m, acc = carry[h]
            m_new = jnp.maximum(m, rmax[h])
            p = jnp.exp(s[h] - m_new).astype(BF16)
            out.append((m_new, jnp.exp(m - m_new) * acc + _dot(v_chunk(h, idx), p)))
        return out

    n_zero = PREFIX_ROWS - N_META
    s_pre = [_dot_nt(kpre_ref[0, h, n_zero:, :], qs[h]) for h in heads]
    key = lax.broadcasted_iota(jnp.int32, (tq, tq), 0)
    qry = lax.broadcasted_iota(jnp.int32, (tq, tq), 1)
    s_dia = [jnp.where(key <= qry, s, NEG) for s in scores(qi)]
    s_nxt = scores(0)
    carry = []
    for h in heads:
        m0 = jnp.maximum(rowmax(s_pre[h]), rowmax(s_dia[h]))
        p_pre = jnp.concatenate([jnp.zeros((n_zero, tq), BF16), jnp.exp(s_pre[h] - m0).astype(BF16)], axis=0)
        p_dia = jnp.exp(s_dia[h] - m0).astype(BF16)
        carry.append((m0, _dot(vpre_ref[0, h, 0], p_pre) + _dot(v_chunk(h, qi), p_dia)))
        s_ref[h] = s_nxt[h]
    rmax = [rowmax(s) for s in s_nxt]

    def step2(t, state):
        carry, rmax = state
        k0 = 2 * t
        s_a = [s_ref[h] for h in heads]
        s_b = scores(k0 + 1)
        carry = softmax_pv(carry, s_a, rmax, k0)
        rmax_b = [rowmax(s) for s in s_b]
        s_c = scores(k0 + 2)
        carry = softmax_pv(carry, s_b, rmax_b, k0 + 1)
        for h in heads:
            s_ref[h] = s_c[h]
        return carry, [rowmax(s) for s in s_c]

    def step1(k0, state):
        carry, rmax = state
        s_a = [s_ref[h] for h in heads]
        s_b = scores(k0 + 1)
        carry = softmax_pv(carry, s_a, rmax, k0)
        for h in heads:
            s_ref[h] = s_b[h]
        return carry, [rowmax(s) for s in s_b]

    steps = jnp.maximum(qi - 1, 0)
    pairs = steps // 2
    state = lax.fori_loop(0, pairs, step2, (carry, rmax))
    carry, rmax = lax.fori_loop(2 * pairs, steps, step1, state)

    some = qi > 0
    last = jnp.maximum(qi - 1, 0)
    s_fin = [jnp.where(some, s_ref[h], NEG) for h in heads]
    r_fin = [jnp.where(some, r, NEG) for r in rmax]
    accs = [acc for _, acc in softmax_pv(carry, s_fin, r_fin, last)]
    vrow = lax.broadcasted_iota(jnp.int32, (PAIR, tq), 0)
    for pr in range(ATTN_HEADS // 2):
        acc0, acc1 = accs[2 * pr], accs[2 * pr + 1]
        l0 = acc0[ONES_LANE[0]:ONES_LANE[0] + 1, :]
        l1 = acc1[ONES_LANE[1]:ONES_LANE[1] + 1, :]
        o_t = jnp.where(vrow < HEAD_DIM, acc0 / l0, acc1 / l1)
        z = z_ref[0, :, pr * PAIR:(pr + 1) * PAIR].astype(F32)
        o_ref[0, :, pr * PAIR:(pr + 1) * PAIR] = (o_t.T * (z * _sigmoid(z))).astype(o_ref.dtype)


def _fox_attn(qp, kp, vp, kpre, vpre, zf, tq):
    b, _, l, _ = qp.shape
    g = ATTN_HEADS
    out_block = pl.BlockSpec((1, tq, g * HEAD_DIM), lambda bi, p, qi: (bi, qi, p))
    assert vp.shape[-1] == tq and vpre.shape[-1] == PREFIX_ROWS
    pre = pl.BlockSpec((1, g, PREFIX_ROWS, LANES), lambda bi, p, qi: (0, p, 0, 0))
    pre_v = pl.BlockSpec((1, g, 1, PAIR, PREFIX_ROWS), lambda bi, p, qi: (0, p, 0, 0, 0))
    full = pl.BlockSpec((1, g, l, LANES), lambda bi, p, qi: (bi, p, 0, 0))
    full_v = pl.BlockSpec((1, g, l // tq, PAIR, tq), lambda bi, p, qi: (bi, p, 0, 0, 0),
                          pipeline_mode=pl.Buffered(1))
    return pl.pallas_call(
        functools.partial(_attn_kernel, tq=tq),
        grid=(b, N_HEADS // g, l // tq),
        in_specs=[pl.BlockSpec((1, g, tq, LANES), lambda bi, p, qi: (bi, p, qi, 0)),
                  pre, pre_v, full, full_v, out_block],
        out_specs=out_block,
        out_shape=jax.ShapeDtypeStruct((b, l, D_BRANCH), BF16),
        scratch_shapes=[pltpu.VMEM((g, tq, tq), F32)],
        compiler_params=_params(("parallel", "parallel", "parallel")),
        name="fox_attn",
    )(qp, kpre, vpre, kp, vp, zf)


def _rwkv_kernel(rt_ref, at_ref, bt_ref, kt_ref, xv_ref, bh_ref, kh_ref, bonus_ref, z_ref, wend_ref,
                 gnw_ref, gnb_ref, ones_ref, z0_ref,
                 y_ref, zf_ref, state_ref, yacc_ref, *, tr):
    t = pl.program_id(1)
    last = t == pl.num_programs(1) - 1

    @pl.when(t == 0)
    def _():
        state_ref[...] = z0_ref[...]

    ones_bd = ones_ref[...]

    def seg_sum(x):
        return _dot(x.astype(BF16), ones_bd)

    operand = dict(rt=rt_ref, at=at_ref, bt=bt_ref, kt=kt_ref, xv=xv_ref, bh=bh_ref, kh=kh_ref)

    col = lax.broadcasted_iota(jnp.int32, (CHUNK, PAIR), 1)
    trow = lax.broadcasted_iota(jnp.int32, (CHUNK, PAIR), 0)
    tcol = col % CHUNK
    head_a = col < HEAD_DIM
    strict = trow > tcol
    incl = trow >= tcol
    eye_sbs = jnp.where(trow == tcol, 1.0, 0.0).astype(F32)
    sq_r = lax.broadcasted_iota(jnp.int32, (PAIR, PAIR), 0)
    sq_c = lax.broadcasted_iota(jnp.int32, (PAIR, PAIR), 1)
    same_head = (sq_r // HEAD_DIM) == (sq_c // HEAD_DIM)
    eye_sq = sq_r == sq_c

    def stack(x):
        zero = jnp.zeros_like(x)
        return jnp.concatenate([jnp.where(head_a, x, zero), jnp.where(head_a, zero, x)], axis=0)

    def fold(sq):
        sq = jnp.where(same_head, sq, 0.0)
        return sq[:CHUNK] + sq[CHUNK:]

    def tile(name, c, p):
        return operand[name][0, c * CHUNK:(c + 1) * CHUNK, p * PAIR:(p + 1) * PAIR]

    n_chunks = tr // CHUNK
    units =[(c, p) for c in range(n_chunks) for p in range(N_PAIRS)]
    nu = range(len(units))
    wc = [wend_ref[0, c:c + 1, p * PAIR:(p + 1) * PAIR] for c, p in units]
    rt, at, bt, kt, xv, bh, kh = ([tile(name, c, p) for c, p in units]
                                  for name in ("rt", "at", "bt", "kt", "xv", "bh", "kh"))
    bt_s = [stack(x) for x in bt]
    kt_s = [stack(x) for x in kt]
    xv_s = [stack(x) for x in xv]

    ar = [jnp.concatenate([at[u], rt[u]], axis=0) for u in nu]
    x_b = [_dot_nt(ar[u], bt_s[u]) for u in nu]
    x_k = [_dot_nt(ar[u], kt_s[u]) for u in nu]
    a_ab = [jnp.where(strict, x[:CHUNK], 0.0) for x in x_b]
    a_rb = [jnp.where(incl, x[CHUNK:], 0.0).astype(BF16) for x in x_b]
    a_ak = [jnp.where(strict, x[:CHUNK], 0.0).astype(BF16) for x in x_k]
    a_rk = [jnp.where(incl, x[CHUNK:], 0.0).astype(BF16) for x in x_k]

    ab = [a.astype(BF16) for a in a_ab]
    pw = [_dot(ab[u], stack(ab[u])) for u in nu]
    tinv = [eye_sbs + a_ab[u] for u in nu]
    for level in range(1, 6):
        pb = [x.astype(BF16) for x in pw]
        if level < 5:
            both = [_dot(jnp.concatenate([tinv[u].astype(BF16), pb[u]], axis=0), stack(pb[u])) for u in nu]
            tinv = [tinv[u] + both[u][:CHUNK] for u in nu]
            pw = [both[u][CHUNK:] for u in nu]
        else:
            tinv = [tinv[u] + _dot(tinv[u].astype(BF16), stack(pb[u])) for u in nu]
    tb = [x.astype(BF16) for x in tinv]

    ap = [_dot(tb[u], stack(at[u])).astype(BF16) for u in nu]
    akv = [_dot(a_ak[u], xv_s[u]).astype(BF16) for u in nu]
    vp = [_dot(tb[u], stack(akv[u])).astype(BF16) for u in nu]
    ap_s = [stack(x) for x in ap]
    vp_s = [stack(x) for x in vp]
    m_sbs = [fold(jnp.where(eye_sq, wc[u], 0.0) + _dot_tn(bh[u], ap[u])).astype(BF16) for u in nu]
    g_sbs = [fold(_dot_tn(jnp.concatenate([bh[u], kh[u]], axis=0), jnp.concatenate([vp[u], xv[u]], axis=0)))
             for u in nu]
    rp = [(rt[u].astype(F32) + _dot(a_rb[u], ap_s[u])).astype(BF16) for u in nu]
    y0 = [_dot(jnp.concatenate([a_rb[u], a_rk[u]], axis=1), jnp.concatenate([vp_s[u], xv_s[u]], axis=0))
          for u in nu]

    for u, (c, p) in enumerate(units):
        z_bd = stack(state_ref[p].astype(BF16))
        both = _dot(jnp.concatenate([rp[u], m_sbs[u]], axis=0), z_bd)
        yacc_ref[c * CHUNK:(c + 1) * CHUNK, p * PAIR:(p + 1) * PAIR] = both[:CHUNK] + y0[u]
        state_ref[p] = both[CHUNK:] + g_sbs[u]

    @pl.when(last)
    def _():
        zf_ref[0] = state_ref[...]

    inv_n = 1.0 / HEAD_DIM
    y = yacc_ref[...]
    mean = seg_sum(y) * inv_n
    d = y - mean
    var = seg_sum(d * d) * inv_n
    yn = d * lax.rsqrt(var + GN_EPS) * gnw_ref[...] + gnb_ref[...]
    z = z_ref[0].astype(F32)
    y_ref[0] = ((yn + bonus_ref[0].astype(F32)) * (z * _sigmoid(z))).astype(y_ref.dtype)


def _rwkv(ops, z, wend, gn_w, gn_b, ones_bd, z0, tr):
    b, l, _ = z.shape
    wide = pl.BlockSpec((1, tr, D_BRANCH), lambda bi, t: (bi, t, 0))
    row_w = pl.BlockSpec((1, D_BRANCH), lambda bi, t: (0, 0))
    ones = pl.BlockSpec((D_BRANCH, D_BRANCH), lambda bi, t: (0, 0))
    st_in = pl.BlockSpec((N_PAIRS, CHUNK, PAIR), lambda bi, t: (0, 0, 0))
    st_out = pl.BlockSpec((1, N_PAIRS, CHUNK, PAIR), lambda bi, t: (bi, 0, 0, 0))
    return pl.pallas_call(
        functools.partial(_rwkv_kernel, tr=tr),
        grid=(b, l // tr),
        in_specs=[wide] * (len(RWKV_OPERANDS) + 1)
                 + [pl.BlockSpec((1, tr // CHUNK, D_BRANCH), lambda bi, t: (bi, t, 0)),
                    row_w, row_w, ones, st_in],
        out_specs=[wide, st_out],
        out_shape=[jax.ShapeDtypeStruct((b, l, D_BRANCH), BF16),
                   jax.ShapeDtypeStruct((b, N_PAIRS, CHUNK, PAIR), F32)],
        scratch_shapes=[pltpu.VMEM((N_PAIRS, CHUNK, PAIR), F32), pltpu.VMEM((tr, D_BRANCH), F32)],
        compiler_params=_params(("parallel", "arbitrary")),
        name="rwkv",
    )(*ops, z, wend, gn_w, gn_b, ones_bd, z0)


def _out_proj_kernel(yf_ref, yr_ref, x_ref, w_ref, fnw_ref, out_ref):
    mix = _dot(yf_ref[...], w_ref[:D_BRANCH, :]) + _dot(yr_ref[...], w_ref[D_BRANCH:, :])
    h = x_ref[...] + mix
    out_ref[...] = h * lax.rsqrt(jnp.mean(h * h, axis=-1, keepdims=True) + NORM_EPS) * fnw_ref[...]


def _out_proj(yf, yr, x, w_out, fnw, tm):
    n = x.shape[0]
    half = pl.BlockSpec((tm, D_BRANCH), lambda i: (i, 0))
    full = pl.BlockSpec((tm, D_MODEL), lambda i: (i, 0))
    return pl.pallas_call(
        _out_proj_kernel,
        grid=(n // tm,),
        in_specs=[half, half, full,
                  pl.BlockSpec((D_MODEL, D_MODEL), lambda i: (0, 0)),
                  pl.BlockSpec((1, D_MODEL), lambda i: (0, 0))],
        out_specs=full,
        out_shape=jax.ShapeDtypeStruct((n, D_MODEL), F32),
        compiler_params=_params(("parallel",)),
        name="out_proj",
    )(yf, yr, x, w_out, fnw)


def _tiles(b, l):
    assert l % 512 == 0
    t_prep = 1024 if l % 1024 == 0 else 512
    t_out = 2048 if (b * l) % 2048 == 0 else 512
    return 512, 512, 512, t_prep, t_out


def kernel(x, meta, norm_w, w_in, b_f, mu_shift, w0, w_up, a0, a_up, k_k, k_a, r_k, gn_w, gn_b,
           w_out, final_norm_w):
    b, l, d = x.shape
    assert d == D_MODEL and norm_w.shape[0] == 1
    tm, t_attn, t_rwkv, t_prep, t_out = _tiles(b, l)

    wi = w_in[0].astype(BF16)
    o = 0
    cols = {}
    for name, width in (("q", D_BRANCH), ("k", D_BRANCH), ("v", D_BRANCH), ("fl", N_HEADS), ("zf", D_BRANCH),
                        ("r", D_BRANCH), ("rk", D_BRANCH), ("rv", D_BRANCH), ("wd", RANK), ("ad", RANK),
                        ("zr", D_BRANCH)):
        cols[name] = wi[:, o:o + width]
        o += width
    w_all = jnp.concatenate(
        [cols[n] for n in ("q", "k", "v", "zf", "r", "rk", "rv", "zr", "wd", "ad", "fl")]
        + [jnp.zeros((D_MODEL, LANES - N_HEADS), BF16)], axis=1)

    row = lambda vec: vec.reshape(1, -1).astype(F32)
    mu = mu_shift[0]
    mu_r, mu_k, mu_v = (row(mu[i * D_BRANCH:(i + 1) * D_BRANCH]) for i in range(3))
    mu_wa = row(mu[3 * D_BRANCH:])
    zeros_up = jnp.zeros((RANK, D_BRANCH), F32)
    wup_pad = jnp.concatenate([w_up[0], zeros_up], axis=0)
    aup_pad = jnp.concatenate([zeros_up, a_up[0]], axis=0)
    hid = np.arange(D_BRANCH) // HEAD_DIM
    ones_bd = jnp.asarray(hid[:, None] == hid[None, :], BF16)
    shift_prm = (mu_r, mu_k, mu_v, mu_wa, row(w0[0]), wup_pad.astype(BF16), row(a0[0]), aup_pad.astype(BF16),
                 row(k_k[0]), row(k_a[0]), row(r_k[0]), ones_bd)
    gnw, gnb = row(gn_w[0]), row(gn_b[0])
    bf_pad = jnp.concatenate([b_f[0], jnp.zeros((LANES - N_HEADS,), F32)]).reshape(1, LANES)
    pq, pk = _select_matrices()
    nw = row(norm_w[0])
    n_ops = len(RWKV_OPERANDS)

    pre_rows = jnp.concatenate([jnp.zeros((PREFIX_ROWS - N_META, D_MODEL), F32), meta.astype(F32)], axis=0)
    zero_w = jnp.zeros((1, D_BRANCH), F32)
    pre = _in_proj(pre_rows, nw, w_all, shift_prm, (zero_w, zero_w, zero_w, jnp.zeros((1, LANES), F32)),
                   PREFIX_ROWS, 1)
    pq_, pk_, pv_, _, pzr_, pfl_ = pre[:6]
    pre_ops, pre_wend, last_raw = pre[6:6 + n_ops], pre[6 + n_ops], pre[7 + n_ops:]
    lead = lambda a: a[None]
    _, kpre, vpre, c_pre = _fox_prep(lead(pq_), lead(pk_), lead(pv_), lead(pfl_), bf_pad,
                                     jnp.zeros((1, LANES), F32), pq, pk, PREFIX_ROWS, PREFIX_ROWS,
                                     PREFIX_ROWS - N_META)
    _, z_pre = _rwkv([lead(a) for a in pre_ops], lead(pzr_), lead(pre_wend), gnw, gnb, ones_bd,
                     jnp.zeros((N_PAIRS, CHUNK, PAIR), F32), PREFIX_ROWS)

    xf = x.reshape(b * l, D_MODEL)
    main = _in_proj(xf, nw, w_all, shift_prm, last_raw, tm, l // tm)
    q_, k_, v_, zf_, zr_, fl_ = main[:6]
    bl = lambda a: a.reshape(b, l, a.shape[-1])
    qp, kp, vp, _ = _fox_prep(bl(q_), bl(k_), bl(v_), bl(fl_), bf_pad, c_pre[0], pq, pk, t_prep, t_attn, 0)
    y_fox = _fox_attn(qp, kp, vp, kpre, vpre, bl(zf_), t_attn)
    y_rwkv, _ = _rwkv([bl(a) for a in main[6:6 + n_ops]], bl(zr_),
                      main[6 + n_ops].reshape(b, l // CHUNK, D_BRANCH), gnw, gnb, ones_bd, z_pre[0], t_rwkv)
    out = _out_proj(y_fox.reshape(b * l, D_BRANCH), y_rwkv.reshape(b * l, D_BRANCH), xf,
                    w_out[0].astype(BF16), row(final_norm_w), t_out)
    return out.reshape(b, l, D_MODEL)
```

```python
import functools

import numpy as np
import jax
import jax.numpy as jnp
from jax import lax
from jax.experimental import pallas as pl
from jax.experimental.pallas import tpu as pltpu

F32 = jnp.float32
BF16 = jnp.bfloat16

D_MODEL = 1024
N_META = 16
HEAD_DIM = 64
N_HEADS = 8
D_BRANCH = N_HEADS * HEAD_DIM
N_PAIRS = N_HEADS // 2
RANK = 64
NORM_EPS = 1e-6
GN_EPS = 64e-5
KK_EPS = 1e-12
NEG = -1e30

LANES = 128
PREFIX_ROWS = 128
CHUNK = 64
SUB = 2 * CHUNK
PAIR = 2 * HEAD_DIM
VMEM_LIMIT = 56 * 1024 * 1024

N_WIDE = 8
COL_WA = N_WIDE * D_BRANCH
COL_FL = COL_WA + LANES
N_COLS = COL_FL + LANES

N_CPARTS = 3
ONES_LANE = (HEAD_DIM, 0)
ATTN_HEADS = 4


def _dot(a, b):
    return jnp.dot(a, b, preferred_element_type=F32)


def _pieces(x, n):
    out = []
    for _ in range(n - 1):
        p = x.astype(BF16)
        out.append(p)
        x = x - p.astype(F32)
    out.append(x.astype(BF16))
    return out


def _dot_exact_lhs(a_bf16, x, n):
    acc = None
    for p in _pieces(x, n):
        t = _dot(a_bf16, p)
        acc = t if acc is None else acc + t
    return acc


def _dot_x2(x, w_bf16):
    x_hi, x_lo = _pieces(x, 2)
    return _dot(x_hi, w_bf16) + _dot(x_lo, w_bf16)


def _dot_nt(a, b):
    return lax.dot_general(a, b, (((1,), (1,)), ((), ())), preferred_element_type=F32)


def _dot_tn(a, b):
    return lax.dot_general(a, b, (((0,), (0,)), ((), ())), preferred_element_type=F32)


def _softplus(x):
    return jnp.maximum(x, 0.0) + jnp.log(1.0 + jnp.exp(-jnp.abs(x)))


def _sigmoid(x):
    return 1.0 / (1.0 + jnp.exp(-x))


def _params(sem):
    return pltpu.CompilerParams(dimension_semantics=sem, vmem_limit_bytes=VMEM_LIMIT)


G_Q, G_K, G_V, G_ZF, G_R, G_RK, G_RV, G_ZR = range(N_WIDE)
RWKV_OPERANDS = ("rt", "at", "bt", "kt", "xv", "bh", "kh", "bonus")


def _in_proj_kernel(x_ref, nw_ref, w_ref,
                    mu_r_ref, mu_k_ref, mu_v_ref, mu_wa_ref, w0_ref, wup_ref,
                    a0_ref, aup_ref, kk_ref, ka_ref, rk_ref, ones_ref,
                    pr_ref, pk_ref, pv_ref, pwa_ref,
                    q_ref, k_ref, v_ref, zf_ref, zr_ref, fl_ref,
                    rt_ref, at_ref, bt_ref, kt_ref, xv_ref, bh_ref, kh_ref, bonus_ref, wend_ref,
                    lr_ref, lk_ref, lv_ref, lwa_ref,
                    sr_ref, sk_ref, sv_ref, swa_ref, *, tm, tiles_per_seq):
    i = pl.program_id(0)

    @pl.when(lax.rem(i, tiles_per_seq) == 0)
    def _():
        sr_ref[...] = pr_ref[...]
        sk_ref[...] = pk_ref[...]
        sv_ref[...] = pv_ref[...]
        swa_ref[...] = pwa_ref[...]

    x = x_ref[...]
    u = x * lax.rsqrt(jnp.mean(x * x, axis=-1, keepdims=True) + NORM_EPS) * nw_ref[...]
    ub = u.astype(BF16)

    def group(g):
        return _dot(ub, w_ref[:, g * D_BRANCH:(g + 1) * D_BRANCH])

    raw = [group(G_R), group(G_RK), group(G_RV), _dot(ub, w_ref[:, COL_WA:COL_WA + LANES])]
    ones_bd = ones_ref[...]

    def seg_sum(v):
        return _dot(v.astype(BF16), ones_bd)

    def shifted(cur, prev_row, mu_ref):
        first = lax.broadcasted_iota(jnp.int32, cur.shape, 0) == 0
        prev = jnp.where(first, prev_row, pltpu.roll(cur, 1, 0))
        return cur + mu_ref[...] * (prev - cur)

    r2 = lax.broadcasted_iota(jnp.int32, (SUB, SUB), 0)
    c2 = lax.broadcasted_iota(jnp.int32, (SUB, SUB), 1)
    tri = jnp.where((r2 >= c2) & (r2 // CHUNK == c2 // CHUNK), 1.0, 0.0).astype(BF16)
    outs = dict(zip(RWKV_OPERANDS, (rt_ref, at_ref, bt_ref, kt_ref, xv_ref, bh_ref, kh_ref, bonus_ref)))

    def rwkv_prepare():
        prev_rows = [sr_ref[...], sk_ref[...], sv_ref[...], swa_ref[...]]
        mus = (mu_r_ref, mu_k_ref, mu_v_ref, mu_wa_ref)
        xr, xk, xv, xwa = [shifted(c, p, mu) for c, p, mu in zip(raw, prev_rows, mus)]
        w_lin = w0_ref[...] + _dot_x2(jnp.tanh(xwa), wup_ref[...])
        a_lin = a0_ref[...] + _dot(xwa.astype(BF16), aup_ref[...])
        kk = xk * kk_ref[...]
        kk_ss = seg_sum(kk * kk)
        yield [c[tm - 1:tm, :] for c in raw]
        w = -_softplus(-w_lin) - 0.5
        ld = -jnp.exp(w)
        a = _sigmoid(a_lin)
        kk = kk * lax.rsqrt(kk_ss + KK_EPS)
        kmod = xk * (1.0 + (a - 1.0) * ka_ref[...])
        lw = jnp.concatenate([_dot_exact_lhs(tri, ld[sb * SUB:(sb + 1) * SUB, :], 2)
                              for sb in range(tm // SUB)], axis=0)
        rk_sum = seg_sum(xr * kmod * rk_ref[...])
        yield None
        w_inv = jnp.exp(-lw)
        bt = kk * a * w_inv
        kt = kmod * w_inv
        for name, val in (("rt", xr * jnp.exp(lw)), ("at", -kk * jnp.exp(lw - ld)), ("bt", bt), ("kt", kt),
                          ("xv", xv), ("bonus", rk_sum * xv)):
            outs[name][...] = val.astype(BF16)
        for c in range(tm // CHUNK):
            rows = slice(c * CHUNK, (c + 1) * CHUNK)
            w_c = jnp.exp(lw[(c + 1) * CHUNK - 1:(c + 1) * CHUNK, :])
            wend_ref[c:c + 1, :] = w_c
            bh_ref[rows, :] = (bt[rows, :] * w_c).astype(BF16)
            kh_ref[rows, :] = (kt[rows, :] * w_c).astype(BF16)
        yield None

    parts = rwkv_prepare()
    last_rows = next(parts)
    q_ref[...] = group(G_Q).astype(BF16)
    k_ref[...] = group(G_K).astype(BF16)
    next(parts)
    v_ref[...] = group(G_V).astype(BF16)
    zf_ref[...] = group(G_ZF).astype(BF16)
    next(parts)
    zr_ref[...] = group(G_ZR).astype(BF16)
    fl_ref[...] = _dot(ub, w_ref[:, COL_FL:COL_FL + LANES])

    for ref, rowv in zip((sr_ref, sk_ref, sv_ref, swa_ref), last_rows):
        ref[...] = rowv

    @pl.when(i == pl.num_programs(0) - 1)
    def _():
        for out, rowv in zip((lr_ref, lk_ref, lv_ref, lwa_ref), last_rows):
            out[...] = rowv


def _in_proj(rows, norm_w, w_all, prm, prev, tm, tiles_per_seq):
    n = rows.shape[0]
    tile = lambda width: pl.BlockSpec((tm, width), lambda i: (i, 0))
    const = lambda shape: pl.BlockSpec(shape, lambda i: (0,) * len(shape))
    row_w, row_n, up = const((1, D_BRANCH)), const((1, LANES)), const((LANES, D_BRANCH))
    wide = lambda dt: jax.ShapeDtypeStruct((n, D_BRANCH), dt)
    return pl.pallas_call(
        functools.partial(_in_proj_kernel, tm=tm, tiles_per_seq=tiles_per_seq),
        grid=(n // tm,),
        in_specs=[tile(D_MODEL), const((1, D_MODEL)),
                  pl.BlockSpec((D_MODEL, N_COLS), lambda i: (0, 0), pipeline_mode=pl.Buffered(1)),
                  row_w, row_w, row_w, row_n, row_w, up, row_w, up,
                  row_w, row_w, row_w, const((D_BRANCH, D_BRANCH)),
                  row_w, row_w, row_w, row_n],
        out_specs=[tile(D_BRANCH)] * 5 + [tile(LANES)] + [tile(D_BRANCH)] * len(RWKV_OPERANDS)
                  + [pl.BlockSpec((tm // CHUNK, D_BRANCH), lambda i: (i, 0)),
                     row_w, row_w, row_w, row_n],
        out_shape=[wide(BF16), wide(BF16), wide(BF16), wide(BF16), wide(BF16),
                   jax.ShapeDtypeStruct((n, LANES), F32)]
                  + [wide(BF16)] * len(RWKV_OPERANDS)
                  + [jax.ShapeDtypeStruct((n // CHUNK, D_BRANCH), F32)]
                  + [jax.ShapeDtypeStruct((1, D_BRANCH), F32)] * 3 + [jax.ShapeDtypeStruct((1, LANES), F32)],
        scratch_shapes=[pltpu.VMEM((1, D_BRANCH), F32), pltpu.VMEM((1, D_BRANCH), F32),
                        pltpu.VMEM((1, D_BRANCH), F32), pltpu.VMEM((1, LANES), F32)],
        compiler_params=_params(("arbitrary",)),
        name="in_proj",
    )(rows, norm_w, w_all, *prm, *prev)


def _select_matrices():
    pq = np.zeros((LANES, D_BRANCH), np.float32)
    pk = np.zeros((LANES, D_BRANCH), np.float32)
    for h in range(N_HEADS):
        base = (h // 2) * PAIR + (HEAD_DIM if h % 2 == 0 else 0)
        for part in range(N_CPARTS):
            pq[part * N_HEADS + h, base + part] = 1.0
            pq[N_CPARTS * N_HEADS + h, base + N_CPARTS + part] = 1.0
            pk[N_CPARTS * N_HEADS + h, base + part] = 1.0
            pk[part * N_HEADS + h, base + N_CPARTS + part] = -1.0
    return jnp.asarray(pq, BF16), jnp.asarray(pk, BF16)


def _fox_prep_kernel(q_ref, k_ref, v_ref, fl_ref, bf_ref, c0_ref, pq_ref, pk_ref,
                     qo_ref, ko_ref, vo_ref, cl_ref, carry_ref, *, tr, tc, n_pad):
    i = pl.program_id(1)

    @pl.when(i == 0)
    def _():
        carry_ref[...] = c0_ref[...]

    lane = lax.broadcasted_iota(jnp.int32, (tc, LANES), 1)
    row0 = lax.broadcasted_iota(jnp.int32, (tc, LANES), 0) + i * tr
    r2 = lax.broadcasted_iota(jnp.int32, (tc, tc), 0)
    c2 = lax.broadcasted_iota(jnp.int32, (tc, tc), 1)
    tri = jnp.where(r2 >= c2, 1.0, 0.0).astype(BF16)
    ones = jnp.where((lane >= N_CPARTS * N_HEADS) & (lane < (N_CPARTS + 1) * N_HEADS), 1.0, 0.0)
    low = lane < HEAD_DIM
    sel_r = lax.broadcasted_iota(jnp.int32, (PAIR, PAIR), 0)
    sel_c = lax.broadcasted_iota(jnp.int32, (PAIR, PAIR), 1)
    eye = jnp.where(sel_r == sel_c, 1.0, 0.0).astype(BF16)
    vrow = lax.broadcasted_iota(jnp.int32, (PAIR, tc), 0)

    carry = carry_ref[...]
    for sb in range(tr // tc):
        rows = slice(sb * tc, (sb + 1) * tc)
        row = row0 + sb * tc
        x = fl_ref[0, rows, :] + bf_ref[...]
        logf = jnp.minimum(x, 0.0) - jnp.log(1.0 + jnp.exp(-jnp.abs(x)))
        valid = lane < N_HEADS
        if n_pad:
            valid = valid & (row >= n_pad)
        logf = jnp.where(valid, logf, 0.0)
        cum = _dot_exact_lhs(tri, logf, 2) + carry
        carry = cum[tc - 1:tc, :]

        p1 = cum.astype(BF16).astype(F32)
        rem = cum - p1
        p2 = rem.astype(BF16).astype(F32)
        p3 = (rem - p2).astype(BF16).astype(F32)
        cbits = p1 + pltpu.roll(p2, N_HEADS, 1) + pltpu.roll(p3, 2 * N_HEADS, 1) + ones
        cbits_k = cbits
        if n_pad:
            cbits_k = jnp.where((row < n_pad) & (lane < N_HEADS), -NEG, cbits)
        bias_q = _dot(cbits.astype(BF16), pq_ref[...]).astype(BF16)
        bias_k = _dot(cbits_k.astype(BF16), pk_ref[...]).astype(BF16)
        for pr in range(N_PAIRS):
            sl = slice(pr * PAIR, (pr + 1) * PAIR)
            q_sc = q_ref[0, rows, sl] * jnp.asarray(0.125, BF16)
            k_pr = k_ref[0, rows, sl]
            v_t = _dot_nt(eye, v_ref[0, rows, sl])
            for half in range(2):
                h = 2 * pr + half
                own = low if half == 0 else jnp.logical_not(low)
                qo_ref[0, h, rows, :] = jnp.where(own, q_sc, bias_q[:, sl])
                ko_ref[0, h, rows, :] = jnp.where(own, k_pr, bias_k[:, sl])
                own_v = (vrow < HEAD_DIM) if half == 0 else (vrow >= HEAD_DIM)
                vo_ref[0, h, sb] = jnp.where(vrow == ONES_LANE[half], 1.0,
                                             jnp.where(own_v, v_t, 0.0)).astype(BF16)
    carry_ref[...] = carry

    @pl.when(i == pl.num_programs(1) - 1)
    def _():
        cl_ref[0] = carry


def _fox_prep(q, k, v, fl, bf_pad, c0, pq, pk, tr, tc, n_pad):
    b, l, _ = q.shape
    wide = pl.BlockSpec((1, tr, D_BRANCH), lambda bi, i: (bi, i, 0))
    head_out = pl.BlockSpec((1, N_HEADS, tr, LANES), lambda bi, i: (bi, 0, i, 0))
    row128 = pl.BlockSpec((1, LANES), lambda bi, i: (0, 0))
    sel = pl.BlockSpec((LANES, D_BRANCH), lambda bi, i: (0, 0))
    return pl.pallas_call(
        functools.partial(_fox_prep_kernel, tr=tr, tc=tc, n_pad=n_pad),
        grid=(b, l // tr),
        in_specs=[wide, wide, wide,
                  pl.BlockSpec((1, tr, LANES), lambda bi, i: (bi, i, 0)),
                  row128, row128, sel, sel],
        out_specs=[head_out, head_out,
                   pl.BlockSpec((1, N_HEADS, tr // tc, PAIR, tc), lambda bi, i: (bi, 0, i, 0, 0)),
                   pl.BlockSpec((1, 1, LANES), lambda bi, i: (bi, 0, 0))],
        out_shape=[jax.ShapeDtypeStruct((b, N_HEADS, l, LANES), BF16)] * 2
                  + [jax.ShapeDtypeStruct((b, N_HEADS, l // tc, PAIR, tc), BF16),
                     jax.ShapeDtypeStruct((b, 1, LANES), F32)],
        scratch_shapes=[pltpu.VMEM((1, LANES), F32)],
        compiler_params=_params(("parallel", "arbitrary")),
        name="fox_prep",
    )(q, k, v, fl, bf_pad, c0, pq, pk)


def _attn_kernel(q_ref, kpre_ref, vpre_ref, k_ref, v_ref, z_ref, o_ref, s_ref, *, tq):
    qi = pl.program_id(2)
    heads = range(ATTN_HEADS)
    qs = [q_ref[0, h] for h in heads]

    def rowmax(s):
        return jnp.max(s, axis=0, keepdims=True)

    def k_chunk(h, idx):
        return k_ref[0, h, pl.ds(pl.multiple_of(idx * tq, tq), tq), :]

    def v_chunk(h, idx):
        return v_ref[0, h, idx]

    def scores(idx):
        return [_dot_nt(k_chunk(h, idx), qs[h]) for h in heads]

    def softmax_pv(carry, s, rmax, idx):
        out = []
        for h in heads:
            m, acc = carry[h]
            m_new = jnp.maximum(m, rmax[h])
            p = jnp.exp(s[h] - m_new).astype(BF16)
            out.append((m_new, jnp.exp(m - m_new) * acc + _dot(v_chunk(h, idx), p)))
        return out

    n_zero = PREFIX_ROWS - N_META
    s_pre = [_dot_nt(kpre_ref[0, h, n_zero:, :], qs[h]) for h in heads]
    key = lax.broadcasted_iota(jnp.int32, (tq, tq), 0)
    qry = lax.broadcasted_iota(jnp.int32, (tq, tq), 1)
    s_dia = [jnp.where(key <= qry, s, NEG) for s in scores(qi)]
    s_nxt = scores(0)
    carry = []
    for h in heads:
        m0 = jnp.maximum(rowmax(s_pre[h]), rowmax(s_dia[h]))
        p_pre = jnp.concatenate([jnp.zeros((n_zero, tq), BF16), jnp.exp(s_pre[h] - m0).astype(BF16)], axis=0)
        p_dia = jnp.exp(s_dia[h] - m0).astype(BF16)
        carry.append((m0, _dot(vpre_ref[0, h, 0], p_pre) + _dot(v_chunk(h, qi), p_dia)))
        s_ref[h] = s_nxt[h]
    rmax = [rowmax(s) for s in s_nxt]

    def step2(t, state):
        carry, rmax = state
        k0 = 2 * t
        s_a = [s_ref[h] for h in heads]
        s_b = scores(k0 + 1)
        carry = softmax_pv(carry, s_a, rmax, k0)
        rmax_b = [rowmax(s) for s in s_b]
        s_c = scores(k0 + 2)
        carry = softmax_pv(carry, s_b, rmax_b, k0 + 1)
        for h in heads:
            s_ref[h] = s_c[h]
        return carry, [rowmax(s) for s in s_c]

    def step1(k0, state):
        carry, rmax = state
        s_a = [s_ref[h] for h in heads]
        s_b = scores(k0 + 1)
        carry = softmax_pv(carry, s_a, rmax, k0)
        for h in heads:
            s_ref[h] = s_b[h]
        return carry, [rowmax(s) for s in s_b]

    steps = jnp.maximum(qi - 1, 0)
    pairs = steps // 2
    state = lax.fori_loop(0, pairs, step2, (carry, rmax))
    carry, rmax = lax.fori_loop(2 * pairs, steps, step1, state)

    some = qi > 0
    last = jnp.maximum(qi - 1, 0)
    s_fin = [jnp.where(some, s_ref[h], NEG) for h in heads]
    r_fin = [jnp.where(some, r, NEG) for r in rmax]
    accs = [acc for _, acc in softmax_pv(carry, s_fin, r_fin, last)]
    vrow = lax.broadcasted_iota(jnp.int32, (PAIR, tq), 0)
    for pr in range(ATTN_HEADS // 2):
        acc0, acc1 = accs[2 * pr], accs[2 * pr + 1]
        l0 = acc0[ONES_LANE[0]:ONES_LANE[0] + 1, :]
        l1 = acc1[ONES_LANE[1]:ONES_LANE[1] + 1, :]
        o_t = jnp.where(vrow < HEAD_DIM, acc0 / l0, acc1 / l1)
        z = z_ref[0, :, pr * PAIR:(pr + 1) * PAIR].astype(F32)
        o_ref[0, :, pr * PAIR:(pr + 1) * PAIR] = (o_t.T * (z * _sigmoid(z))).astype(o_ref.dtype)


def _fox_attn(qp, kp, vp, kpre, vpre, zf, tq):
    b, _, l, _ = qp.shape
    g = ATTN_HEADS
    out_block = pl.BlockSpec((1, tq, g * HEAD_DIM), lambda bi, p, qi: (bi, qi, p))
    assert vp.shape[-1] == tq and vpre.shape[-1] == PREFIX_ROWS
    pre = pl.BlockSpec((1, g, PREFIX_ROWS, LANES), lambda bi, p, qi: (0, p, 0, 0))
    pre_v = pl.BlockSpec((1, g, 1, PAIR, PREFIX_ROWS), lambda bi, p, qi: (0, p, 0, 0, 0))
    full = pl.BlockSpec((1, g, l, LANES), lambda bi, p, qi: (bi, p, 0, 0))
    full_v = pl.BlockSpec((1, g, l // tq, PAIR, tq), lambda bi, p, qi: (bi, p, 0, 0, 0))
    return pl.pallas_call(
        functools.partial(_attn_kernel, tq=tq),
        grid=(b, N_HEADS // g, l // tq),
        in_specs=[pl.BlockSpec((1, g, tq, LANES), lambda bi, p, qi: (bi, p, qi, 0)),
                  pre, pre_v, full, full_v, out_block],
        out_specs=out_block,
        out_shape=jax.ShapeDtypeStruct((b, l, D_BRANCH), BF16),
        scratch_shapes=[pltpu.VMEM((g, tq, tq), F32)],
        compiler_params=_params(("parallel", "parallel", "parallel")),
        name="fox_attn",
    )(qp, kpre, vpre, kp, vp, zf)


def _rwkv_kernel(rt_ref, at_ref, bt_ref, kt_ref, xv_ref, bh_ref, kh_ref, bonus_ref, z_ref, wend_ref,
                 gnw_ref, gnb_ref, ones_ref, z0_ref, *rest, tr, project):
    if project:
        yf_ref, x_ref, wout_ref, fnw_ref, out_ref, zf_ref, state_ref, yacc_ref = rest
    else:
        out_ref, zf_ref, state_ref, yacc_ref = rest
    t = pl.program_id(1)
    last = t == pl.num_programs(1) - 1

    @pl.when(t == 0)
    def _():
        state_ref[...] = z0_ref[...]

    ones_bd = ones_ref[...]

    def seg_sum(x):
        return _dot(x.astype(BF16), ones_bd)

    operand = dict(rt=rt_ref, at=at_ref, bt=bt_ref, kt=kt_ref, xv=xv_ref, bh=bh_ref, kh=kh_ref)

    col = lax.broadcasted_iota(jnp.int32, (CHUNK, PAIR), 1)
    trow = lax.broadcasted_iota(jnp.int32, (CHUNK, PAIR), 0)
    tcol = col % CHUNK
    head_a = col < HEAD_DIM
    strict = trow > tcol
    incl = trow >= tcol
    eye_sbs = jnp.where(trow == tcol, 1.0, 0.0).astype(F32)
    sq_r = lax.broadcasted_iota(jnp.int32, (PAIR, PAIR), 0)
    sq_c = lax.broadcasted_iota(jnp.int32, (PAIR, PAIR), 1)
    same_head = (sq_r // HEAD_DIM) == (sq_c // HEAD_DIM)
    eye_sq = sq_r == sq_c

    def stack(x):
        zero = jnp.zeros_like(x)
        return jnp.concatenate([jnp.where(head_a, x, zero), jnp.where(head_a, zero, x)], axis=0)

    def fold(sq):
        sq = jnp.where(same_head, sq, 0.0)
        return sq[:CHUNK] + sq[CHUNK:]

    def tile(name, c, p):
        return operand[name][0, c * CHUNK:(c + 1) * CHUNK, p * PAIR:(p + 1) * PAIR]

    n_chunks = tr // CHUNK
    units =[(c, p) for c in range(n_chunks) for p in range(N_PAIRS)]
    nu = range(len(units))
    wc = [wend_ref[0, c:c + 1, p * PAIR:(p + 1) * PAIR] for c, p in units]
    rt, at, bt, kt, xv, bh, kh = ([tile(name, c, p) for c, p in units]
                                  for name in ("rt", "at", "bt", "kt", "xv", "bh", "kh"))
    bt_s = [stack(x) for x in bt]
    kt_s = [stack(x) for x in kt]
    xv_s = [stack(x) for x in xv]

    ar = [jnp.concatenate([at[u], rt[u]], axis=0) for u in nu]
    x_b = [_dot_nt(ar[u], bt_s[u]) for u in nu]
    x_k = [_dot_nt(ar[u], kt_s[u]) for u in nu]
    a_ab = [jnp.where(strict, x[:CHUNK], 0.0) for x in x_b]
    a_rb = [jnp.where(incl, x[CHUNK:], 0.0).astype(BF16) for x in x_b]
    a_ak = [jnp.where(strict, x[:CHUNK], 0.0).astype(BF16) for x in x_k]
    a_rk = [jnp.where(incl, x[CHUNK:], 0.0).astype(BF16) for x in x_k]

    ab = [a.astype(BF16) for a in a_ab]
    pw = [_dot(ab[u], stack(ab[u])) for u in nu]
    tinv = [eye_sbs + a_ab[u] for u in nu]
    for level in range(1, 6):
        pb = [x.astype(BF16) for x in pw]
        if level < 5:
            both = [_dot(jnp.concatenate([tinv[u].astype(BF16), pb[u]], axis=0), stack(pb[u])) for u in nu]
            tinv = [tinv[u] + both[u][:CHUNK] for u in nu]
            pw = [both[u][CHUNK:] for u in nu]
        else:
            tinv = [tinv[u] + _dot(tinv[u].astype(BF16), stack(pb[u])) for u in nu]
    tb = [x.astype(BF16) for x in tinv]

    ap = [_dot(tb[u], stack(at[u])).astype(BF16) for u in nu]
    akv = [_dot(a_ak[u], xv_s[u]).astype(BF16) for u in nu]
    vp = [_dot(tb[u], stack(akv[u])).astype(BF16) for u in nu]
    ap_s = [stack(x) for x in ap]
    vp_s = [stack(x) for x in vp]
    m_sbs = [fold(jnp.where(eye_sq, wc[u], 0.0) + _dot_tn(bh[u], ap[u])).astype(BF16) for u in nu]
    g_sbs = [fold(_dot_tn(jnp.concatenate([bh[u], kh[u]], axis=0), jnp.concatenate([vp[u], xv[u]], axis=0)))
             for u in nu]
    rp = [(rt[u].astype(F32) + _dot(a_rb[u], ap_s[u])).astype(BF16) for u in nu]
    y0 = [_dot(jnp.concatenate([a_rb[u], a_rk[u]], axis=1), jnp.concatenate([vp_s[u], xv_s[u]], axis=0))
          for u in nu]

    for u, (c, p) in enumerate(units):
        z_bd = stack(state_ref[p].astype(BF16))
        both = _dot(jnp.concatenate([rp[u], m_sbs[u]], axis=0), z_bd)
        yacc_ref[c * CHUNK:(c + 1) * CHUNK, p * PAIR:(p + 1) * PAIR] = both[:CHUNK] + y0[u]
        state_ref[p] = both[CHUNK:] + g_sbs[u]

    @pl.when(last)
    def _():
        zf_ref[0] = state_ref[...]

    inv_n = 1.0 / HEAD_DIM
    y = yacc_ref[...]
    mean = seg_sum(y) * inv_n
    d = y - mean
    var = seg_sum(d * d) * inv_n
    yn = d * lax.rsqrt(var + GN_EPS) * gnw_ref[...] + gnb_ref[...]
    z = z_ref[0].astype(F32)
    y_rwkv = ((yn + bonus_ref[0].astype(F32)) * (z * _sigmoid(z))).astype(BF16)
    if project:
        mix = _dot(yf_ref[0], wout_ref[:D_BRANCH, :]) + _dot(y_rwkv, wout_ref[D_BRANCH:, :])
        h = x_ref[0] + mix
        out_ref[0] = h * lax.rsqrt(jnp.mean(h * h, axis=-1, keepdims=True) + NORM_EPS) * fnw_ref[...]
    else:
        out_ref[0] = y_rwkv


def _rwkv(ops, z, wend, gn_w, gn_b, ones_bd, z0, tr, finish=None):
    b, l, _ = z.shape
    wide = pl.BlockSpec((1, tr, D_BRANCH), lambda bi, t: (bi, t, 0))
    row_w = pl.BlockSpec((1, D_BRANCH), lambda bi, t: (0, 0))
    ones = pl.BlockSpec((D_BRANCH, D_BRANCH), lambda bi, t: (0, 0))
    st_in = pl.BlockSpec((N_PAIRS, CHUNK, PAIR), lambda bi, t: (0, 0, 0))
    st_out = pl.BlockSpec((1, N_PAIRS, CHUNK, PAIR), lambda bi, t: (bi, 0, 0, 0))
    in_specs = ([wide] * (len(RWKV_OPERANDS) + 1)
                + [pl.BlockSpec((1, tr // CHUNK, D_BRANCH), lambda bi, t: (bi, t, 0)), row_w, row_w, ones, st_in])
    args = [*ops, z, wend, gn_w, gn_b, ones_bd, z0]
    if finish is None:
        out_spec, out_sds = wide, jax.ShapeDtypeStruct((b, l, D_BRANCH), BF16)
    else:
        full = pl.BlockSpec((1, tr, D_MODEL), lambda bi, t: (bi, t, 0))
        in_specs += [wide, full, pl.BlockSpec((D_MODEL, D_MODEL), lambda bi, t: (0, 0)),
                     pl.BlockSpec((1, D_MODEL), lambda bi, t: (0, 0))]
        args += list(finish)
        out_spec, out_sds = full, jax.ShapeDtypeStruct((b, l, D_MODEL), F32)
    return pl.pallas_call(
        functools.partial(_rwkv_kernel, tr=tr, project=finish is not None),
        grid=(b, l // tr),
        in_specs=in_specs,
        out_specs=[out_spec, st_out],
        out_shape=[out_sds, jax.ShapeDtypeStruct((b, N_PAIRS, CHUNK, PAIR), F32)],
        scratch_shapes=[pltpu.VMEM((N_PAIRS, CHUNK, PAIR), F32), pltpu.VMEM((tr, D_BRANCH), F32)],
        compiler_params=_params(("parallel", "arbitrary")),
        name="rwkv",
    )(*args)


def _out_proj_kernel(yf_ref, yr_ref, x_ref, w_ref, fnw_ref, out_ref):
    mix = _dot(yf_ref[...], w_ref[:D_BRANCH, :]) + _dot(yr_ref[...], w_ref[D_BRANCH:, :])
    h = x_ref[...] + mix
    out_ref[...] = h * lax.rsqrt(jnp.mean(h * h, axis=-1, keepdims=True) + NORM_EPS) * fnw_ref[...]


def _out_proj(yf, yr, x, w_out, fnw, tm):
    n = x.shape[0]
    half = pl.BlockSpec((tm, D_BRANCH), lambda i: (i, 0))
    full = pl.BlockSpec((tm, D_MODEL), lambda i: (i, 0))
    return pl.pallas_call(
        _out_proj_kernel,
        grid=(n // tm,),
        in_specs=[half, half, full,
                  pl.BlockSpec((D_MODEL, D_MODEL), lambda i: (0, 0)),
                  pl.BlockSpec((1, D_MODEL), lambda i: (0, 0))],
        out_specs=full,
        out_shape=jax.ShapeDtypeStruct((n, D_MODEL), F32),
        compiler_params=_params(("parallel",)),
        name="out_proj",
    )(yf, yr, x, w_out, fnw)


def _tiles(b, l):
    assert l % 512 == 0
    t_prep = 1024 if l % 1024 == 0 else 512
    t_out = 2048 if (b * l) % 2048 == 0 else 512
    return 512, 512, 512, t_prep, t_out


def kernel(x, meta, norm_w, w_in, b_f, mu_shift, w0, w_up, a0, a_up, k_k, k_a, r_k, gn_w, gn_b,
           w_out, final_norm_w):
    b, l, d = x.shape
    assert d == D_MODEL and norm_w.shape[0] == 1
    tm, t_attn, t_rwkv, t_prep, t_out = _tiles(b, l)

    wi = w_in[0].astype(BF16)
    o = 0
    cols = {}
    for name, width in (("q", D_BRANCH), ("k", D_BRANCH), ("v", D_BRANCH), ("fl", N_HEADS), ("zf", D_BRANCH),
                        ("r", D_BRANCH), ("rk", D_BRANCH), ("rv", D_BRANCH), ("wd", RANK), ("ad", RANK),
                        ("zr", D_BRANCH)):
        cols[name] = wi[:, o:o + width]
        o += width
    w_all = jnp.concatenate(
        [cols[n] for n in ("q", "k", "v", "zf", "r", "rk", "rv", "zr", "wd", "ad", "fl")]
        + [jnp.zeros((D_MODEL, LANES - N_HEADS), BF16)], axis=1)

    row = lambda vec: vec.reshape(1, -1).astype(F32)
    mu = mu_shift[0]
    mu_r, mu_k, mu_v = (row(mu[i * D_BRANCH:(i + 1) * D_BRANCH]) for i in range(3))
    mu_wa = row(mu[3 * D_BRANCH:])
    zeros_up = jnp.zeros((RANK, D_BRANCH), F32)
    wup_pad = jnp.concatenate([w_up[0], zeros_up], axis=0)
    aup_pad = jnp.concatenate([zeros_up, a_up[0]], axis=0)
    hid = np.arange(D_BRANCH) // HEAD_DIM
    ones_bd = jnp.asarray(hid[:, None] == hid[None, :], BF16)
    shift_prm = (mu_r, mu_k, mu_v, mu_wa, row(w0[0]), wup_pad.astype(BF16), row(a0[0]), aup_pad.astype(BF16),
                 row(k_k[0]), row(k_a[0]), row(r_k[0]), ones_bd)
    gnw, gnb = row(gn_w[0]), row(gn_b[0])
    bf_pad = jnp.concatenate([b_f[0], jnp.zeros((LANES - N_HEADS,), F32)]).reshape(1, LANES)
    pq, pk = _select_matrices()
    nw = row(norm_w[0])
    n_ops = len(RWKV_OPERANDS)

    pre_rows = jnp.concatenate([jnp.zeros((PREFIX_ROWS - N_META, D_MODEL), F32), meta.astype(F32)], axis=0)
    zero_w = jnp.zeros((1, D_BRANCH), F32)
    pre = _in_proj(pre_rows, nw, w_all, shift_prm, (zero_w, zero_w, zero_w, jnp.zeros((1, LANES), F32)),
                   PREFIX_ROWS, 1)
    pq_, pk_, pv_, _, pzr_, pfl_ = pre[:6]
    pre_ops, pre_wend, last_raw = pre[6:6 + n_ops], pre[6 + n_ops], pre[7 + n_ops:]
    lead = lambda a: a[None]
    _, kpre, vpre, c_pre = _fox_prep(lead(pq_), lead(pk_), lead(pv_), lead(pfl_), bf_pad,
                                     jnp.zeros((1, LANES), F32), pq, pk, PREFIX_ROWS, PREFIX_ROWS,
                                     PREFIX_ROWS - N_META)
    _, z_pre = _rwkv([lead(a) for a in pre_ops], lead(pzr_), lead(pre_wend), gnw, gnb, ones_bd,
                     jnp.zeros((N_PAIRS, CHUNK, PAIR), F32), PREFIX_ROWS)

    xf = x.reshape(b * l, D_MODEL)
    main = _in_proj(xf, nw, w_all, shift_prm, last_raw, tm, l // tm)
    q_, k_, v_, zf_, zr_, fl_ = main[:6]
    bl = lambda a: a.reshape(b, l, a.shape[-1])
    qp, kp, vp, _ = _fox_prep(bl(q_), bl(k_), bl(v_), bl(fl_), bf_pad, c_pre[0], pq, pk, t_prep, t_attn, 0)
    y_fox = _fox_attn(qp, kp, vp, kpre, vpre, bl(zf_), t_attn)
    out, _ = _rwkv([bl(a) for a in main[6:6 + n_ops]], bl(zr_),
                   main[6 + n_ops].reshape(b, l // CHUNK, D_BRANCH), gnw, gnb, ones_bd, z_pre[0], t_rwkv,
                   finish=(y_fox, x, w_out[0].astype(BF16), row(final_norm_w)))
    return out
```

```python
import functools

import numpy as np
import jax
import jax.numpy as jnp
from jax import lax
from jax.experimental import pallas as pl
from jax.experimental.pallas import tpu as pltpu

F32 = jnp.float32
BF16 = jnp.bfloat16

D_MODEL = 1024
N_META = 16
HEAD_DIM = 64
N_HEADS = 8
D_BRANCH = N_HEADS * HEAD_DIM
N_PAIRS = N_HEADS // 2
RANK = 64
NORM_EPS = 1e-6
GN_EPS = 64e-5
KK_EPS = 1e-12
NEG = -1e30

LANES = 128
PREFIX_ROWS = 128
CHUNK = 64
SUB = 2 * CHUNK
PAIR = 2 * HEAD_DIM
VMEM_LIMIT = 56 * 1024 * 1024

N_WIDE = 8
COL_WA = N_WIDE * D_BRANCH
COL_FL = COL_WA + LANES
N_COLS = COL_FL + LANES

N_CPARTS = 3
ONES_LANE = (HEAD_DIM, 0)
ATTN_HEADS = 4


def _dot(a, b):
    return jnp.dot(a, b, preferred_element_type=F32)


def _pieces(x, n):
    out = []
    for _ in range(n - 1):
        p = x.astype(BF16)
        out.append(p)
        x = x - p.astype(F32)
    out.append(x.astype(BF16))
    return out


def _dot_exact_lhs(a_bf16, x, n):
    acc = None
    for p in _pieces(x, n):
        t = _dot(a_bf16, p)
        acc = t if acc is None else acc + t
    return acc


def _dot_x2(x, w_bf16):
    x_hi, x_lo = _pieces(x, 2)
    return _dot(x_hi, w_bf16) + _dot(x_lo, w_bf16)


def _dot_nt(a, b):
    return lax.dot_general(a, b, (((1,), (1,)), ((), ())), preferred_element_type=F32)


def _dot_tn(a, b):
    return lax.dot_general(a, b, (((0,), (0,)), ((), ())), preferred_element_type=F32)


def _softplus(x):
    return jnp.maximum(x, 0.0) + jnp.log(1.0 + jnp.exp(-jnp.abs(x)))


def _sigmoid(x):
    return 1.0 / (1.0 + jnp.exp(-x))


def _params(sem):
    return pltpu.CompilerParams(dimension_semantics=sem, vmem_limit_bytes=VMEM_LIMIT)


G_Q, G_K, G_V, G_ZF, G_R, G_RK, G_RV, G_ZR = range(N_WIDE)
RWKV_OPERANDS = ("rt", "at", "bt", "kt", "xv", "bh", "kh", "bonus")


def _in_proj_kernel(x_ref, nw_ref, w_ref,
                    mu_r_ref, mu_k_ref, mu_v_ref, mu_wa_ref, w0_ref, wup_ref,
                    a0_ref, aup_ref, kk_ref, ka_ref, rk_ref, ones_ref,
                    pr_ref, pk_ref, pv_ref, pwa_ref,
                    q_ref, k_ref, v_ref, zf_ref, zr_ref, fl_ref,
                    rt_ref, at_ref, bt_ref, kt_ref, xv_ref, bh_ref, kh_ref, bonus_ref, wend_ref,
                    lr_ref, lk_ref, lv_ref, lwa_ref,
                    sr_ref, sk_ref, sv_ref, swa_ref, *, tm, tiles_per_seq):
    i = pl.program_id(0)

    @pl.when(lax.rem(i, tiles_per_seq) == 0)
    def _():
        sr_ref[...] = pr_ref[...]
        sk_ref[...] = pk_ref[...]
        sv_ref[...] = pv_ref[...]
        swa_ref[...] = pwa_ref[...]

    x = x_ref[...]
    u = x * lax.rsqrt(jnp.mean(x * x, axis=-1, keepdims=True) + NORM_EPS) * nw_ref[...]
    ub = u.astype(BF16)

    def group(g):
        return _dot(ub, w_ref[:, g * D_BRANCH:(g + 1) * D_BRANCH])

    raw = [group(G_R), group(G_RK), group(G_RV), _dot(ub, w_ref[:, COL_WA:COL_WA + LANES])]
    ones_bd = ones_ref[...]

    def seg_sum(v):
        return _dot(v.astype(BF16), ones_bd)

    def shifted(cur, prev_row, mu_ref):
        first = lax.broadcasted_iota(jnp.int32, cur.shape, 0) == 0
        prev = jnp.where(first, prev_row, pltpu.roll(cur, 1, 0))
        return cur + mu_ref[...] * (prev - cur)

    r2 = lax.broadcasted_iota(jnp.int32, (SUB, SUB), 0)
    c2 = lax.broadcasted_iota(jnp.int32, (SUB, SUB), 1)
    tri = jnp.where((r2 >= c2) & (r2 // CHUNK == c2 // CHUNK), 1.0, 0.0).astype(BF16)
    outs = dict(zip(RWKV_OPERANDS, (rt_ref, at_ref, bt_ref, kt_ref, xv_ref, bh_ref, kh_ref, bonus_ref)))

    def rwkv_prepare():
        prev_rows = [sr_ref[...], sk_ref[...], sv_ref[...], swa_ref[...]]
        mus = (mu_r_ref, mu_k_ref, mu_v_ref, mu_wa_ref)
        xr, xk, xv, xwa = [shifted(c, p, mu) for c, p, mu in zip(raw, prev_rows, mus)]
        w_lin = w0_ref[...] + _dot_x2(jnp.tanh(xwa), wup_ref[...])
        a_lin = a0_ref[...] + _dot(xwa.astype(BF16), aup_ref[...])
        kk = xk * kk_ref[...]
        kk_ss = seg_sum(kk * kk)
        yield [c[tm - 1:tm, :] for c in raw]
        w = -_softplus(-w_lin) - 0.5
        ld = -jnp.exp(w)
        a = _sigmoid(a_lin)
        kk = kk * lax.rsqrt(kk_ss + KK_EPS)
        kmod = xk * (1.0 + (a - 1.0) * ka_ref[...])
        lw = jnp.concatenate([_dot_exact_lhs(tri, ld[sb * SUB:(sb + 1) * SUB, :], 2)
                              for sb in range(tm // SUB)], axis=0)
        rk_sum = seg_sum(xr * kmod * rk_ref[...])
        yield None
        w_inv = jnp.exp(-lw)
        bt = kk * a * w_inv
        kt = kmod * w_inv
        for name, val in (("rt", xr * jnp.exp(lw)), ("at", -kk * jnp.exp(lw - ld)), ("bt", bt), ("kt", kt),
                          ("xv", xv), ("bonus", rk_sum * xv)):
            outs[name][...] = val.astype(BF16)
        for c in range(tm // CHUNK):
            rows = slice(c * CHUNK, (c + 1) * CHUNK)
            w_c = jnp.exp(lw[(c + 1) * CHUNK - 1:(c + 1) * CHUNK, :])
            wend_ref[c:c + 1, :] = w_c
            bh_ref[rows, :] = (bt[rows, :] * w_c).astype(BF16)
            kh_ref[rows, :] = (kt[rows, :] * w_c).astype(BF16)
        yield None

    parts = rwkv_prepare()
    last_rows = next(parts)
    q_ref[...] = group(G_Q).astype(BF16)
    k_ref[...] = group(G_K).astype(BF16)
    next(parts)
    v_ref[...] = group(G_V).astype(BF16)
    zf_ref[...] = group(G_ZF).astype(BF16)
    next(parts)
    zr_ref[...] = group(G_ZR).astype(BF16)
    fl_ref[...] = _dot(ub, w_ref[:, COL_FL:COL_FL + LANES])

    for ref, rowv in zip((sr_ref, sk_ref, sv_ref, swa_ref), last_rows):
        ref[...] = rowv

    @pl.when(i == pl.num_programs(0) - 1)
    def _():
        for out, rowv in zip((lr_ref, lk_ref, lv_ref, lwa_ref), last_rows):
            out[...] = rowv


def _in_proj(rows, norm_w, w_all, prm, prev, tm, tiles_per_seq):
    n = rows.shape[0]
    tile = lambda width: pl.BlockSpec((tm, width), lambda i: (i, 0))
    const = lambda shape: pl.BlockSpec(shape, lambda i: (0,) * len(shape))
    row_w, row_n, up = const((1, D_BRANCH)), const((1, LANES)), const((LANES, D_BRANCH))
    wide = lambda dt: jax.ShapeDtypeStruct((n, D_BRANCH), dt)
    return pl.pallas_call(
        functools.partial(_in_proj_kernel, tm=tm, tiles_per_seq=tiles_per_seq),
        grid=(n // tm,),
        in_specs=[tile(D_MODEL), const((1, D_MODEL)),
                  pl.BlockSpec((D_MODEL, N_COLS), lambda i: (0, 0), pipeline_mode=pl.Buffered(1)),
                  row_w, row_w, row_w, row_n, row_w, up, row_w, up,
                  row_w, row_w, row_w, const((D_BRANCH, D_BRANCH)),
                  row_w, row_w, row_w, row_n],
        out_specs=[tile(D_BRANCH)] * 5 + [tile(LANES)] + [tile(D_BRANCH)] * len(RWKV_OPERANDS)
                  + [pl.BlockSpec((tm // CHUNK, D_BRANCH), lambda i: (i, 0)),
                     row_w, row_w, row_w, row_n],
        out_shape=[wide(BF16), wide(BF16), wide(BF16), wide(BF16), wide(BF16),
                   jax.ShapeDtypeStruct((n, LANES), F32)]
                  + [wide(BF16)] * len(RWKV_OPERANDS)
                  + [jax.ShapeDtypeStruct((n // CHUNK, D_BRANCH), F32)]
                  + [jax.ShapeDtypeStruct((1, D_BRANCH), F32)] * 3 + [jax.ShapeDtypeStruct((1, LANES), F32)],
        scratch_shapes=[pltpu.VMEM((1, D_BRANCH), F32), pltpu.VMEM((1, D_BRANCH), F32),
                        pltpu.VMEM((1, D_BRANCH), F32), pltpu.VMEM((1, LANES), F32)],
        compiler_params=_params(("arbitrary",)),
        name="in_proj",
    )(rows, norm_w, w_all, *prm, *prev)


def _select_matrices():
    pq = np.zeros((LANES, D_BRANCH), np.float32)
    pk = np.zeros((LANES, D_BRANCH), np.float32)
    for h in range(N_HEADS):
        base = (h // 2) * PAIR + (HEAD_DIM if h % 2 == 0 else 0)
        for part in range(N_CPARTS):
            pq[part * N_HEADS + h, base + part] = 1.0
            pq[N_CPARTS * N_HEADS + h, base + N_CPARTS + part] = 1.0
            pk[N_CPARTS * N_HEADS + h, base + part] = 1.0
            pk[part * N_HEADS + h, base + N_CPARTS + part] = -1.0
    return jnp.asarray(pq, BF16), jnp.asarray(pk, BF16)


def _fox_prep_kernel(q_ref, k_ref, v_ref, fl_ref, bf_ref, c0_ref, pq_ref, pk_ref,
                     qo_ref, ko_ref, vo_ref, cl_ref, carry_ref, *, tr, tc, n_pad):
    i = pl.program_id(1)

    @pl.when(i == 0)
    def _():
        carry_ref[...] = c0_ref[...]

    lane = lax.broadcasted_iota(jnp.int32, (tc, LANES), 1)
    row0 = lax.broadcasted_iota(jnp.int32, (tc, LANES), 0) + i * tr
    r2 = lax.broadcasted_iota(jnp.int32, (tc, tc), 0)
    c2 = lax.broadcasted_iota(jnp.int32, (tc, tc), 1)
    tri = jnp.where(r2 >= c2, 1.0, 0.0).astype(BF16)
    ones = jnp.where((lane >= N_CPARTS * N_HEADS) & (lane < (N_CPARTS + 1) * N_HEADS), 1.0, 0.0)
    low = lane < HEAD_DIM
    sel_r = lax.broadcasted_iota(jnp.int32, (PAIR, PAIR), 0)
    sel_c = lax.broadcasted_iota(jnp.int32, (PAIR, PAIR), 1)
    eye = jnp.where(sel_r == sel_c, 1.0, 0.0).astype(BF16)
    vrow = lax.broadcasted_iota(jnp.int32, (PAIR, tc), 0)

    carry = carry_ref[...]
    for sb in range(tr // tc):
        rows = slice(sb * tc, (sb + 1) * tc)
        row = row0 + sb * tc
        x = fl_ref[0, rows, :] + bf_ref[...]
        logf = jnp.minimum(x, 0.0) - jnp.log(1.0 + jnp.exp(-jnp.abs(x)))
        valid = lane < N_HEADS
        if n_pad:
            valid = valid & (row >= n_pad)
        logf = jnp.where(valid, logf, 0.0)
        cum = _dot_exact_lhs(tri, logf, 2) + carry
        carry = cum[tc - 1:tc, :]

        p1 = cum.astype(BF16).astype(F32)
        rem = cum - p1
        p2 = rem.astype(BF16).astype(F32)
        p3 = (rem - p2).astype(BF16).astype(F32)
        cbits = p1 + pltpu.roll(p2, N_HEADS, 1) + pltpu.roll(p3, 2 * N_HEADS, 1) + ones
        cbits_k = cbits
        if n_pad:
            cbits_k = jnp.where((row < n_pad) & (lane < N_HEADS), -NEG, cbits)
        bias_q = _dot(cbits.astype(BF16), pq_ref[...]).astype(BF16)
        bias_k = _dot(cbits_k.astype(BF16), pk_ref[...]).astype(BF16)
        for pr in range(N_PAIRS):
            sl = slice(pr * PAIR, (pr + 1) * PAIR)
            q_sc = q_ref[0, rows, sl] * jnp.asarray(0.125, BF16)
            k_pr = k_ref[0, rows, sl]
            v_t = _dot_nt(eye, v_ref[0, rows, sl])
            for half in range(2):
                h = 2 * pr + half
                own = low if half == 0 else jnp.logical_not(low)
                qo_ref[0, h, rows, :] = jnp.where(own, q_sc, bias_q[:, sl])
                ko_ref[0, h, rows, :] = jnp.where(own, k_pr, bias_k[:, sl])
                own_v = (vrow < HEAD_DIM) if half == 0 else (vrow >= HEAD_DIM)
                vo_ref[0, h, sb] = jnp.where(vrow == ONES_LANE[half], 1.0,
                                             jnp.where(own_v, v_t, 0.0)).astype(BF16)
    carry_ref[...] = carry

    @pl.when(i == pl.num_programs(1) - 1)
    def _():
        cl_ref[0] = carry


def _fox_prep(q, k, v, fl, bf_pad, c0, pq, pk, tr, tc, n_pad):
    b, l, _ = q.shape
    wide = pl.BlockSpec((1, tr, D_BRANCH), lambda bi, i: (bi, i, 0))
    head_out = pl.BlockSpec((1, N_HEADS, tr, LANES), lambda bi, i: (bi, 0, i, 0))
    row128 = pl.BlockSpec((1, LANES), lambda bi, i: (0, 0))
    sel = pl.BlockSpec((LANES, D_BRANCH), lambda bi, i: (0, 0))
    return pl.pallas_call(
        functools.partial(_fox_prep_kernel, tr=tr, tc=tc, n_pad=n_pad),
        grid=(b, l // tr),
        in_specs=[wide, wide, wide,
                  pl.BlockSpec((1, tr, LANES), lambda bi, i: (bi, i, 0)),
                  row128, row128, sel, sel],
        out_specs=[head_out, head_out,
                   pl.BlockSpec((1, N_HEADS, tr // tc, PAIR, tc), lambda bi, i: (bi, 0, i, 0, 0)),
                   pl.BlockSpec((1, 1, LANES), lambda bi, i: (bi, 0, 0))],
        out_shape=[jax.ShapeDtypeStruct((b, N_HEADS, l, LANES), BF16)] * 2
                  + [jax.ShapeDtypeStruct((b, N_HEADS, l // tc, PAIR, tc), BF16),
                     jax.ShapeDtypeStruct((b, 1, LANES), F32)],
        scratch_shapes=[pltpu.VMEM((1, LANES), F32)],
        compiler_params=_params(("parallel", "arbitrary")),
        name="fox_prep",
    )(q, k, v, fl, bf_pad, c0, pq, pk)


def _attn_kernel(q_ref, kpre_ref, vpre_ref, k_ref, v_ref, z_ref, o_ref, s_ref, *, tq):
    qi = pl.program_id(2)
    heads = range(ATTN_HEADS)
    qs = [q_ref[0, h] for h in heads]

    def rowmax(s):
        return jnp.max(s, axis=0, keepdims=True)

    def k_chunk(h, idx):
        return k_ref[0, h, pl.ds(pl.multiple_of(idx * tq, tq), tq), :]

    def v_chunk(h, idx):
        return v_ref[0, h, idx]

    def scores(idx):
        return [_dot_nt(k_chunk(h, idx), qs[h]) for h in heads]

    def softmax_pv(carry, s, rmax, idx):
        out = []
        for h in heads:
            m, acc = carry[h]
            m_new = jnp.maximum(m, rmax[h])
            p = jnp.exp(s[h] - m_new).astype(BF16)
            out.append((m_new, jnp.exp(m - m_new) * acc + _dot(v_chunk(h, idx), p)))
        return out

    n_zero = PREFIX_ROWS - N_META
    s_pre = [_dot_nt(kpre_ref[0, h, n_zero:, :], qs[h]) for h in heads]
    key = lax.broadcasted_iota(jnp.int32, (tq, tq), 0)
    qry = lax.broadcasted_iota(jnp.int32, (tq, tq), 1)
    s_dia = [jnp.where(key <= qry, s, NEG) for s in scores(qi)]
    s_nxt = scores(0)
    carry = []
    for h in heads:
        m0 = jnp.maximum(rowmax(s_pre[h]), rowmax(s_dia[h]))
        p_pre = jnp.concatenate([jnp.zeros((n_zero, tq), BF16), jnp.exp(s_pre[h] - m0).astype(BF16)], axis=0)
        p_dia = jnp.exp(s_dia[h] - m0).astype(BF16)
        carry.append((m0, _dot(vpre_ref[0, h, 0], p_pre) + _dot(v_chunk(h, qi), p_dia)))
        s_ref[h] = s_nxt[h]
    rmax = [rowmax(s) for s in s_nxt]

    def step2(t, state):
        carry, rmax = state
        k0 = 2 * t
        s_a = [s_ref[h] for h in heads]
        s_b = scores(k0 + 1)
        carry = softmax_pv(carry, s_a, rmax, k0)
        rmax_b = [rowmax(s) for s in s_b]
        s_c = scores(k0 + 2)
        carry = softmax_pv(carry, s_b, rmax_b, k0 + 1)
        for h in heads:
            s_ref[h] = s_c[h]
        return carry, [rowmax(s) for s in s_c]

    def step1(k0, state):
        carry, rmax = state
        s_a = [s_ref[h] for h in heads]
        s_b = scores(k0 + 1)
        carry = softmax_pv(carry, s_a, rmax, k0)
        for h in heads:
            s_ref[h] = s_b[h]
        return carry, [rowmax(s) for s in s_b]

    steps = jnp.maximum(qi - 1, 0)
    pairs = steps // 2
    state = lax.fori_loop(0, pairs, step2, (carry, rmax))
    carry, rmax = lax.fori_loop(2 * pairs, steps, step1, state)

    some = qi > 0
    last = jnp.maximum(qi - 1, 0)
    s_fin = [jnp.where(some, s_ref[h], NEG) for h in heads]
    r_fin = [jnp.where(some, r, NEG) for r in rmax]
    accs = [acc for _, acc in softmax_pv(carry, s_fin, r_fin, last)]
    vrow = lax.broadcasted_iota(jnp.int32, (PAIR, tq), 0)
    for pr in range(ATTN_HEADS // 2):
        acc0, acc1 = accs[2 * pr], accs[2 * pr + 1]
        l0 = acc0[ONES_LANE[0]:ONES_LANE[0] + 1, :]
        l1 = acc1[ONES_LANE[1]:ONES_LANE[1] + 1, :]
        o_t = jnp.where(vrow < HEAD_DIM, acc0 / l0, acc1 / l1)
        z = z_ref[0, :, pr * PAIR:(pr + 1) * PAIR].astype(F32)
        o_ref[0, :, pr * PAIR:(pr + 1) * PAIR] = (o_t.T * (z * _sigmoid(z))).astype(o_ref.dtype)


def _fox_attn(qp, kp, vp, kpre, vpre, zf, tq):
    b, _, l, _ = qp.shape
    g = ATTN_HEADS
    out_block = pl.BlockSpec((1, tq, g * HEAD_DIM), lambda bi, p, qi: (bi, qi, p))
    assert vp.shape[-1] == tq and vpre.shape[-1] == PREFIX_ROWS
    pre = pl.BlockSpec((1, g, PREFIX_ROWS, LANES), lambda bi, p, qi: (0, p, 0, 0))
    pre_v = pl.BlockSpec((1, g, 1, PAIR, PREFIX_ROWS), lambda bi, p, qi: (0, p, 0, 0, 0))
    full = pl.BlockSpec((1, g, l, LANES), lambda bi, p, qi: (bi, p, 0, 0))
    full_v = pl.BlockSpec((1, g, l // tq, PAIR, tq), lambda bi, p, qi: (bi, p, 0, 0, 0))
    return pl.pallas_call(
        functools.partial(_attn_kernel, tq=tq),
        grid=(b, N_HEADS // g, l // tq),
        in_specs=[pl.BlockSpec((1, g, tq, LANES), lambda bi, p, qi: (bi, p, qi, 0)),
                  pre, pre_v, full, full_v, out_block],
        out_specs=out_block,
        out_shape=jax.ShapeDtypeStruct((b, l, D_BRANCH), BF16),
        scratch_shapes=[pltpu.VMEM((g, tq, tq), F32)],
        compiler_params=_params(("parallel", "parallel", "parallel")),
        name="fox_attn",
    )(qp, kpre, vpre, kp, vp, zf)


def _rwkv_kernel(rt_ref, at_ref, bt_ref, kt_ref, xv_ref, bh_ref, kh_ref, bonus_ref, z_ref, wend_ref,
                 gnw_ref, gnb_ref, ones_ref, z0_ref, *rest, tr, project):
    if project:
        yf_ref, x_ref, wout_ref, fnw_ref, out_ref, zf_ref, state_ref, yacc_ref = rest
    else:
        out_ref, zf_ref, state_ref, yacc_ref = rest
    t = pl.program_id(1)
    last = t == pl.num_programs(1) - 1

    @pl.when(t == 0)
    def _():
        state_ref[...] = z0_ref[...]

    ones_bd = ones_ref[...]

    def seg_sum(x):
        return _dot(x.astype(BF16), ones_bd)

    operand = dict(rt=rt_ref, at=at_ref, bt=bt_ref, kt=kt_ref, xv=xv_ref, bh=bh_ref, kh=kh_ref)

    col = lax.broadcasted_iota(jnp.int32, (CHUNK, PAIR), 1)
    trow = lax.broadcasted_iota(jnp.int32, (CHUNK, PAIR), 0)
    tcol = col % CHUNK
    head_a = col < HEAD_DIM
    strict = trow > tcol
    incl = trow >= tcol
    eye_sbs = jnp.where(trow == tcol, 1.0, 0.0).astype(F32)
    sq_r = lax.broadcasted_iota(jnp.int32, (PAIR, PAIR), 0)
    sq_c = lax.broadcasted_iota(jnp.int32, (PAIR, PAIR), 1)
    same_head = (sq_r // HEAD_DIM) == (sq_c // HEAD_DIM)
    eye_sq = sq_r == sq_c

    def stack(x):
        zero = jnp.zeros_like(x)
        return jnp.concatenate([jnp.where(head_a, x, zero), jnp.where(head_a, zero, x)], axis=0)

    def fold(sq):
        sq = jnp.where(same_head, sq, 0.0)
        return sq[:CHUNK] + sq[CHUNK:]

    def tile(name, c, p):
        return operand[name][0, c * CHUNK:(c + 1) * CHUNK, p * PAIR:(p + 1) * PAIR]

    n_chunks = tr // CHUNK
    units =[(c, p) for c in range(n_chunks) for p in range(N_PAIRS)]
    nu = range(len(units))
    wc = [wend_ref[0, c:c + 1, p * PAIR:(p + 1) * PAIR] for c, p in units]
    rt, at, bt, kt, xv, bh, kh = ([tile(name, c, p) for c, p in units]
                                  for name in ("rt", "at", "bt", "kt", "xv", "bh", "kh"))
    bt_s = [stack(x) for x in bt]
    kt_s = [stack(x) for x in kt]
    xv_s = [stack(x) for x in xv]

    ar = [jnp.concatenate([at[u], rt[u]], axis=0) for u in nu]
    x_b = [_dot_nt(ar[u], bt_s[u]) for u in nu]
    x_k = [_dot_nt(ar[u], kt_s[u]) for u in nu]
    a_ab = [jnp.where(strict, x[:CHUNK], 0.0) for x in x_b]
    a_rb = [jnp.where(incl, x[CHUNK:], 0.0).astype(BF16) for x in x_b]
    a_ak = [jnp.where(strict, x[:CHUNK], 0.0).astype(BF16) for x in x_k]
    a_rk = [jnp.where(incl, x[CHUNK:], 0.0).astype(BF16) for x in x_k]

    ab = [a.astype(BF16) for a in a_ab]
    pw = [_dot(ab[u], stack(ab[u])) for u in nu]
    tinv = [eye_sbs + a_ab[u] for u in nu]
    for level in range(1, 6):
        pb = [x.astype(BF16) for x in pw]
        if level < 5:
            both = [_dot(jnp.concatenate([tinv[u].astype(BF16), pb[u]], axis=0), stack(pb[u])) for u in nu]
            tinv = [tinv[u] + both[u][:CHUNK] for u in nu]
            pw = [both[u][CHUNK:] for u in nu]
        else:
            tinv = [tinv[u] + _dot(tinv[u].astype(BF16), stack(pb[u])) for u in nu]
    tb = [x.astype(BF16) for x in tinv]

    ap = [_dot(tb[u], stack(at[u])).astype(BF16) for u in nu]
    akv = [_dot(a_ak[u], xv_s[u]).astype(BF16) for u in nu]
    vp = [_dot(tb[u], stack(akv[u])).astype(BF16) for u in nu]
    ap_s = [stack(x) for x in ap]
    vp_s = [stack(x) for x in vp]
    m_sbs = [fold(jnp.where(eye_sq, wc[u], 0.0) + _dot_tn(bh[u], ap[u])).astype(BF16) for u in nu]
    g_sbs = [fold(_dot_tn(jnp.concatenate([bh[u], kh[u]], axis=0), jnp.concatenate([vp[u], xv[u]], axis=0)))
             for u in nu]
    rp = [(rt[u].astype(F32) + _dot(a_rb[u], ap_s[u])).astype(BF16) for u in nu]
    y0 = [_dot(jnp.concatenate([a_rb[u], a_rk[u]], axis=1), jnp.concatenate([vp_s[u], xv_s[u]], axis=0))
          for u in nu]

    for u, (c, p) in enumerate(units):
        z_bd = stack(state_ref[p].astype(BF16))
        both = _dot(jnp.concatenate([rp[u], m_sbs[u]], axis=0), z_bd)
        yacc_ref[c * CHUNK:(c + 1) * CHUNK, p * PAIR:(p + 1) * PAIR] = both[:CHUNK] + y0[u]
        state_ref[p] = both[CHUNK:] + g_sbs[u]

    @pl.when(last)
    def _():
        zf_ref[0] = state_ref[...]

    inv_n = 1.0 / HEAD_DIM
    y = yacc_ref[...]
    mean = seg_sum(y) * inv_n
    d = y - mean
    var = seg_sum(d * d) * inv_n
    yn = d * lax.rsqrt(var + GN_EPS) * gnw_ref[...] + gnb_ref[...]
    z = z_ref[0].astype(F32)
    y_rwkv = ((yn + bonus_ref[0].astype(F32)) * (z * _sigmoid(z))).astype(BF16)
    if project:
        mix = _dot(yf_ref[0], wout_ref[:D_BRANCH, :]) + _dot(y_rwkv, wout_ref[D_BRANCH:, :])
        h = x_ref[0] + mix
        out_ref[0] = h * lax.rsqrt(jnp.mean(h * h, axis=-1, keepdims=True) + NORM_EPS) * fnw_ref[...]
    else:
        out_ref[0] = y_rwkv


def _rwkv(ops, z, wend, gn_w, gn_b, ones_bd, z0, tr, finish=None):
    b, l, _ = z.shape
    wide = pl.BlockSpec((1, tr, D_BRANCH), lambda bi, t: (bi, t, 0))
    row_w = pl.BlockSpec((1, D_BRANCH), lambda bi, t: (0, 0))
    ones = pl.BlockSpec((D_BRANCH, D_BRANCH), lambda bi, t: (0, 0))
    st_in = pl.BlockSpec((N_PAIRS, CHUNK, PAIR), lambda bi, t: (0, 0, 0))
    st_out = pl.BlockSpec((1, N_PAIRS, CHUNK, PAIR), lambda bi, t: (bi, 0, 0, 0))
    in_specs = ([wide] * (len(RWKV_OPERANDS) + 1)
                + [pl.BlockSpec((1, tr // CHUNK, D_BRANCH), lambda bi, t: (bi, t, 0)), row_w, row_w, ones, st_in])
    args = [*ops, z, wend, gn_w, gn_b, ones_bd, z0]
    if finish is None:
        out_spec, out_sds = wide, jax.ShapeDtypeStruct((b, l, D_BRANCH), BF16)
    else:
        full = pl.BlockSpec((1, tr, D_MODEL), lambda bi, t: (bi, t, 0))
        in_specs += [wide, full, pl.BlockSpec((D_MODEL, D_MODEL), lambda bi, t: (0, 0)),
                     pl.BlockSpec((1, D_MODEL), lambda bi, t: (0, 0))]
        args += list(finish)
        out_spec, out_sds = full, jax.ShapeDtypeStruct((b, l, D_MODEL), F32)
    return pl.pallas_call(
        functools.partial(_rwkv_kernel, tr=tr, project=finish is not None),
        grid=(b, l // tr),
        in_specs=in_specs,
        out_specs=[out_spec, st_out],
        out_shape=[out_sds, jax.ShapeDtypeStruct((b, N_PAIRS, CHUNK, PAIR), F32)],
        scratch_shapes=[pltpu.VMEM((N_PAIRS, CHUNK, PAIR), F32), pltpu.VMEM((tr, D_BRANCH), F32)],
        compiler_params=_params(("parallel", "arbitrary")),
        name="rwkv",
    )(*args)


def _tiles(b, l):
    assert l % 512 == 0
    t_prep = 1024 if l % 1024 == 0 else 512
    return 512, 512, 512, t_prep


def kernel(x, meta, norm_w, w_in, b_f, mu_shift, w0, w_up, a0, a_up, k_k, k_a, r_k, gn_w, gn_b,
           w_out, final_norm_w):
    b, l, d = x.shape
    assert d == D_MODEL and norm_w.shape[0] == 1
    tm, t_attn, t_rwkv, t_prep = _tiles(b, l)

    wi = w_in[0].astype(BF16)
    o = 0
    cols = {}
    for name, width in (("q", D_BRANCH), ("k", D_BRANCH), ("v", D_BRANCH), ("fl", N_HEADS), ("zf", D_BRANCH),
                        ("r", D_BRANCH), ("rk", D_BRANCH), ("rv", D_BRANCH), ("wd", RANK), ("ad", RANK),
                        ("zr", D_BRANCH)):
        cols[name] = wi[:, o:o + width]
        o += width
    w_all = jnp.concatenate(
        [cols[n] for n in ("q", "k", "v", "zf", "r", "rk", "rv", "zr", "wd", "ad", "fl")]
        + [jnp.zeros((D_MODEL, LANES - N_HEADS), BF16)], axis=1)

    row = lambda vec: vec.reshape(1, -1).astype(F32)
    mu = mu_shift[0]
    mu_r, mu_k, mu_v = (row(mu[i * D_BRANCH:(i + 1) * D_BRANCH]) for i in range(3))
    mu_wa = row(mu[3 * D_BRANCH:])
    zeros_up = jnp.zeros((RANK, D_BRANCH), F32)
    wup_pad = jnp.concatenate([w_up[0], zeros_up], axis=0)
    aup_pad = jnp.concatenate([zeros_up, a_up[0]], axis=0)
    hid = np.arange(D_BRANCH) // HEAD_DIM
    ones_bd = jnp.asarray(hid[:, None] == hid[None, :], BF16)
    shift_prm = (mu_r, mu_k, mu_v, mu_wa, row(w0[0]), wup_pad.astype(BF16), row(a0[0]), aup_pad.astype(BF16),
                 row(k_k[0]), row(k_a[0]), row(r_k[0]), ones_bd)
    gnw, gnb = row(gn_w[0]), row(gn_b[0])
    bf_pad = jnp.concatenate([b_f[0], jnp.zeros((LANES - N_HEADS,), F32)]).reshape(1, LANES)
    pq, pk = _select_matrices()
    nw = row(norm_w[0])
    n_ops = len(RWKV_OPERANDS)

    pre_rows = jnp.concatenate([jnp.zeros((PREFIX_ROWS - N_META, D_MODEL), F32), meta.astype(F32)], axis=0)
    zero_w = jnp.zeros((1, D_BRANCH), F32)
    pre = _in_proj(pre_rows, nw, w_all, shift_prm, (zero_w, zero_w, zero_w, jnp.zeros((1, LANES), F32)),
                   PREFIX_ROWS, 1)
    pq_, pk_, pv_, _, pzr_, pfl_ = pre[:6]
    pre_ops, pre_wend, last_raw = pre[6:6 + n_ops], pre[6 + n_ops], pre[7 + n_ops:]
    lead = lambda a: a[None]
    _, kpre, vpre, c_pre = _fox_prep(lead(pq_), lead(pk_), lead(pv_), lead(pfl_), bf_pad,
                                     jnp.zeros((1, LANES), F32), pq, pk, PREFIX_ROWS, PREFIX_ROWS,
                                     PREFIX_ROWS - N_META)
    _, z_pre = _rwkv([lead(a) for a in pre_ops], lead(pzr_), lead(pre_wend), gnw, gnb, ones_bd,
                     jnp.zeros((N_PAIRS, CHUNK, PAIR), F32), PREFIX_ROWS)

    xf = x.reshape(b * l, D_MODEL)
    main = _in_proj(xf, nw, w_all, shift_prm, last_raw, tm, l // tm)
    q_, k_, v_, zf_, zr_, fl_ = main[:6]
    bl = lambda a: a.reshape(b, l, a.shape[-1])
    qp, kp, vp, _ = _fox_prep(bl(q_), bl(k_), bl(v_), bl(fl_), bf_pad, c_pre[0], pq, pk, t_prep, t_attn, 0)
    y_fox = _fox_attn(qp, kp, vp, kpre, vpre, bl(zf_), t_attn)
    out, _ = _rwkv([bl(a) for a in main[6:6 + n_ops]], bl(zr_),
                   main[6 + n_ops].reshape(b, l // CHUNK, D_BRANCH), gnw, gnb, ones_bd, z_pre[0], t_rwkv,
                   finish=(y_fox, x, w_out[0].astype(BF16), row(final_norm_w)))
    return out
```

```python
import functools

import numpy as np
import jax
import jax.numpy as jnp
from jax import lax
from jax.experimental import pallas as pl
from jax.experimental.pallas import tpu as pltpu

F32 = jnp.float32
BF16 = jnp.bfloat16

D_MODEL = 1024
N_META = 16
HEAD_DIM = 64
N_HEADS = 8
D_BRANCH = N_HEADS * HEAD_DIM
N_PAIRS = N_HEADS // 2
RANK = 64
NORM_EPS = 1e-6
GN_EPS = 64e-5
KK_EPS = 1e-12
NEG = -1e30

LANES = 128
PREFIX_ROWS = 128
CHUNK = 64
SUB = 2 * CHUNK
PAIR = 2 * HEAD_DIM
VMEM_LIMIT = 56 * 1024 * 1024

N_WIDE = 8
COL_WA = N_WIDE * D_BRANCH
COL_FL = COL_WA + LANES
N_COLS = COL_FL + LANES

N_CPARTS = 3
ONES_LANE = (HEAD_DIM, 0)
ATTN_HEADS = 4


def _dot(a, b):
    return jnp.dot(a, b, preferred_element_type=F32)


def _pieces(x, n):
    out = []
    for _ in range(n - 1):
        p = x.astype(BF16)
        out.append(p)
        x = x - p.astype(F32)
    out.append(x.astype(BF16))
    return out


def _dot_exact_lhs(a_bf16, x, n):
    acc = None
    for p in _pieces(x, n):
        t = _dot(a_bf16, p)
        acc = t if acc is None else acc + t
    return acc


def _dot_x2(x, w_bf16):
    x_hi, x_lo = _pieces(x, 2)
    return _dot(x_hi, w_bf16) + _dot(x_lo, w_bf16)


def _dot_nt(a, b):
    return lax.dot_general(a, b, (((1,), (1,)), ((), ())), preferred_element_type=F32)


def _dot_tn(a, b):
    return lax.dot_general(a, b, (((0,), (0,)), ((), ())), preferred_element_type=F32)


def _softplus(x):
    return jnp.maximum(x, 0.0) + jnp.log(1.0 + jnp.exp(-jnp.abs(x)))


def _sigmoid(x):
    return 1.0 / (1.0 + jnp.exp(-x))


def _params(sem):
    return pltpu.CompilerParams(dimension_semantics=sem, vmem_limit_bytes=VMEM_LIMIT)


G_Q, G_K, G_V, G_ZF, G_R, G_RK, G_RV, G_ZR = range(N_WIDE)
RWKV_OPERANDS = ("rt", "at", "bt", "kt", "xv", "bh", "kh", "bonus")


def _in_proj_kernel(x_ref, nw_ref, w_ref,
                    mu_r_ref, mu_k_ref, mu_v_ref, mu_wa_ref, w0_ref, wup_ref,
                    a0_ref, aup_ref, kk_ref, ka_ref, rk_ref, ones_ref,
                    pr_ref, pk_ref, pv_ref, pwa_ref, bf_ref, c0_ref, selq_ref, selk_ref,
                    qo_ref, ko_ref, vo_ref, zf_ref, zr_ref,
                    rt_ref, at_ref, bt_ref, kt_ref, xv_ref, bh_ref, kh_ref, bonus_ref, wend_ref,
                    lr_ref, lk_ref, lv_ref, lwa_ref, cl_ref,
                    sr_ref, sk_ref, sv_ref, swa_ref, sc_ref, *, tm, tiles_per_seq, n_pad):
    i = pl.program_id(0)
    seq_tile = lax.rem(i, tiles_per_seq)

    @pl.when(seq_tile == 0)
    def _():
        sr_ref[...] = pr_ref[...]
        sk_ref[...] = pk_ref[...]
        sv_ref[...] = pv_ref[...]
        swa_ref[...] = pwa_ref[...]
        sc_ref[...] = c0_ref[...]

    x = x_ref[...]
    u = x * lax.rsqrt(jnp.mean(x * x, axis=-1, keepdims=True) + NORM_EPS) * nw_ref[...]
    ub = u.astype(BF16)

    def group(g):
        return _dot(ub, w_ref[:, g * D_BRANCH:(g + 1) * D_BRANCH])

    raw = [group(G_R), group(G_RK), group(G_RV), _dot(ub, w_ref[:, COL_WA:COL_WA + LANES])]
    ones_bd = ones_ref[...]

    def seg_sum(v):
        return _dot(v.astype(BF16), ones_bd)

    def shifted(cur, prev_row, mu_ref):
        first = lax.broadcasted_iota(jnp.int32, cur.shape, 0) == 0
        prev = jnp.where(first, prev_row, pltpu.roll(cur, 1, 0))
        return cur + mu_ref[...] * (prev - cur)

    r2 = lax.broadcasted_iota(jnp.int32, (SUB, SUB), 0)
    c2 = lax.broadcasted_iota(jnp.int32, (SUB, SUB), 1)
    tri = jnp.where((r2 >= c2) & (r2 // CHUNK == c2 // CHUNK), 1.0, 0.0).astype(BF16)
    outs = dict(zip(RWKV_OPERANDS, (rt_ref, at_ref, bt_ref, kt_ref, xv_ref, bh_ref, kh_ref, bonus_ref)))

    def rwkv_prepare():
        prev_rows = [sr_ref[...], sk_ref[...], sv_ref[...], swa_ref[...]]
        mus = (mu_r_ref, mu_k_ref, mu_v_ref, mu_wa_ref)
        xr, xk, xv, xwa = [shifted(c, p, mu) for c, p, mu in zip(raw, prev_rows, mus)]
        w_lin = w0_ref[...] + _dot_x2(jnp.tanh(xwa), wup_ref[...])
        a_lin = a0_ref[...] + _dot(xwa.astype(BF16), aup_ref[...])
        kk = xk * kk_ref[...]
        kk_ss = seg_sum(kk * kk)
        yield [c[tm - 1:tm, :] for c in raw]
        w = -_softplus(-w_lin) - 0.5
        ld = -jnp.exp(w)
        a = _sigmoid(a_lin)
        kk = kk * lax.rsqrt(kk_ss + KK_EPS)
        kmod = xk * (1.0 + (a - 1.0) * ka_ref[...])
        lw = jnp.concatenate([_dot_exact_lhs(tri, ld[sb * SUB:(sb + 1) * SUB, :], 2)
                              for sb in range(tm // SUB)], axis=0)
        rk_sum = seg_sum(xr * kmod * rk_ref[...])
        yield None
        w_inv = jnp.exp(-lw)
        bt = kk * a * w_inv
        kt = kmod * w_inv
        for name, val in (("rt", xr * jnp.exp(lw)), ("at", -kk * jnp.exp(lw - ld)), ("bt", bt), ("kt", kt),
                          ("xv", xv), ("bonus", rk_sum * xv)):
            outs[name][...] = val.astype(BF16)
        for c in range(tm // CHUNK):
            rows = slice(c * CHUNK, (c + 1) * CHUNK)
            w_c = jnp.exp(lw[(c + 1) * CHUNK - 1:(c + 1) * CHUNK, :])
            wend_ref[c:c + 1, :] = w_c
            bh_ref[rows, :] = (bt[rows, :] * w_c).astype(BF16)
            kh_ref[rows, :] = (kt[rows, :] * w_c).astype(BF16)
        yield None

    parts = rwkv_prepare()
    last_rows = next(parts)
    q = group(G_Q).astype(BF16)
    k = group(G_K).astype(BF16)
    next(parts)
    v = group(G_V).astype(BF16)
    zf_ref[...] = group(G_ZF).astype(BF16)
    next(parts)
    zr_ref[...] = group(G_ZR).astype(BF16)
    fl = _dot(ub, w_ref[:, COL_FL:COL_FL + LANES])
    c_last = _fox_operands(q, k, v, fl, bf_ref[...], sc_ref[...], selq_ref[...], selk_ref[...],
                           seq_tile * tm, n_pad, qo_ref, ko_ref, vo_ref)
    sc_ref[...] = c_last

    for ref, rowv in zip((sr_ref, sk_ref, sv_ref, swa_ref), last_rows):
        ref[...] = rowv

    @pl.when(i == pl.num_programs(0) - 1)
    def _():
        for out, rowv in zip((lr_ref, lk_ref, lv_ref, lwa_ref), last_rows):
            out[...] = rowv
        cl_ref[...] = c_last


def _in_proj(rows, norm_w, w_all, prm, prev, fox_prm, tm, tiles_per_seq, n_pad):
    n = rows.shape[0]
    n_seq = n // (tm * tiles_per_seq)
    tile = lambda width: pl.BlockSpec((tm, width), lambda i: (i, 0))
    const = lambda shape: pl.BlockSpec(shape, lambda i: (0,) * len(shape))
    row_w, row_n, up = const((1, D_BRANCH)), const((1, LANES)), const((LANES, D_BRANCH))
    wide = lambda dt: jax.ShapeDtypeStruct((n, D_BRANCH), dt)
    per_head = pl.BlockSpec((1, N_HEADS, tm, LANES), lambda i: (i // tiles_per_seq, 0, i % tiles_per_seq, 0))
    per_head_t = pl.BlockSpec((1, N_HEADS, 1, PAIR, tm),
                              lambda i: (i // tiles_per_seq, 0, i % tiles_per_seq, 0, 0))
    return pl.pallas_call(
        functools.partial(_in_proj_kernel, tm=tm, tiles_per_seq=tiles_per_seq, n_pad=n_pad),
        grid=(n // tm,),
        in_specs=[tile(D_MODEL), const((1, D_MODEL)),
                  pl.BlockSpec((D_MODEL, N_COLS), lambda i: (0, 0), pipeline_mode=pl.Buffered(1)),
                  row_w, row_w, row_w, row_n, row_w, up, row_w, up,
                  row_w, row_w, row_w, const((D_BRANCH, D_BRANCH)),
                  row_w, row_w, row_w, row_n, row_n, row_n, up, up],
        out_specs=[per_head, per_head, per_head_t, tile(D_BRANCH), tile(D_BRANCH)]
                  + [tile(D_BRANCH)] * len(RWKV_OPERANDS)
                  + [pl.BlockSpec((tm // CHUNK, D_BRANCH), lambda i: (i, 0)),
                     row_w, row_w, row_w, row_n, row_n],
        out_shape=[jax.ShapeDtypeStruct((n_seq, N_HEADS, tiles_per_seq * tm, LANES), BF16)] * 2
                  + [jax.ShapeDtypeStruct((n_seq, N_HEADS, tiles_per_seq, PAIR, tm), BF16),
                     wide(BF16), wide(BF16)]
                  + [wide(BF16)] * len(RWKV_OPERANDS)
                  + [jax.ShapeDtypeStruct((n // CHUNK, D_BRANCH), F32)]
                  + [jax.ShapeDtypeStruct((1, D_BRANCH), F32)] * 3 + [jax.ShapeDtypeStruct((1, LANES), F32)] * 2,
        scratch_shapes=[pltpu.VMEM((1, D_BRANCH), F32), pltpu.VMEM((1, D_BRANCH), F32),
                        pltpu.VMEM((1, D_BRANCH), F32), pltpu.VMEM((1, LANES), F32),
                        pltpu.VMEM((1, LANES), F32)],
        compiler_params=_params(("arbitrary",)),
        name="in_proj",
    )(rows, norm_w, w_all, *prm, *prev, *fox_prm)


def _select_matrices():
    pq = np.zeros((LANES, D_BRANCH), np.float32)
    pk = np.zeros((LANES, D_BRANCH), np.float32)
    for h in range(N_HEADS):
        base = (h // 2) * PAIR + (HEAD_DIM if h % 2 == 0 else 0)
        for part in range(N_CPARTS):
            pq[part * N_HEADS + h, base + part] = 1.0
            pq[N_CPARTS * N_HEADS + h, base + N_CPARTS + part] = 1.0
            pk[N_CPARTS * N_HEADS + h, base + part] = 1.0
            pk[part * N_HEADS + h, base + N_CPARTS + part] = -1.0
    return jnp.asarray(pq, BF16), jnp.asarray(pk, BF16)


def _fox_operands(q, k, v, fl, bf, carry, pq, pk, row_base, n_pad, qo_ref, ko_ref, vo_ref):
    tc = q.shape[0]
    lane = lax.broadcasted_iota(jnp.int32, (tc, LANES), 1)
    row = lax.broadcasted_iota(jnp.int32, (tc, LANES), 0) + row_base
    r2 = lax.broadcasted_iota(jnp.int32, (tc, tc), 0)
    c2 = lax.broadcasted_iota(jnp.int32, (tc, tc), 1)
    tri = jnp.where(r2 >= c2, 1.0, 0.0).astype(BF16)
    ones = jnp.where((lane >= N_CPARTS * N_HEADS) & (lane < (N_CPARTS + 1) * N_HEADS), 1.0, 0.0)
    low = lane < HEAD_DIM
    sel_r = lax.broadcasted_iota(jnp.int32, (PAIR, PAIR), 0)
    sel_c = lax.broadcasted_iota(jnp.int32, (PAIR, PAIR), 1)
    eye = jnp.where(sel_r == sel_c, 1.0, 0.0).astype(BF16)
    vrow = lax.broadcasted_iota(jnp.int32, (PAIR, tc), 0)

    x = fl + bf
    logf = jnp.minimum(x, 0.0) - jnp.log(1.0 + jnp.exp(-jnp.abs(x)))
    valid = lane < N_HEADS
    if n_pad:
        valid = valid & (row >= n_pad)
    logf = jnp.where(valid, logf, 0.0)
    cum = _dot_exact_lhs(tri, logf, 2) + carry

    p1 = cum.astype(BF16).astype(F32)
    rem = cum - p1
    p2 = rem.astype(BF16).astype(F32)
    p3 = (rem - p2).astype(BF16).astype(F32)
    cbits = p1 + pltpu.roll(p2, N_HEADS, 1) + pltpu.roll(p3, 2 * N_HEADS, 1) + ones
    cbits_k = cbits
    if n_pad:
        cbits_k = jnp.where((row < n_pad) & (lane < N_HEADS), -NEG, cbits)
    bias_q = _dot(cbits.astype(BF16), pq).astype(BF16)
    bias_k = _dot(cbits_k.astype(BF16), pk).astype(BF16)
    for pr in range(N_PAIRS):
        sl = slice(pr * PAIR, (pr + 1) * PAIR)
        q_sc = q[:, sl] * jnp.asarray(0.125, BF16)
        v_t = _dot_nt(eye, v[:, sl])
        for half in range(2):
            h = 2 * pr + half
            own = low if half == 0 else jnp.logical_not(low)
            qo_ref[0, h] = jnp.where(own, q_sc, bias_q[:, sl])
            ko_ref[0, h] = jnp.where(own, k[:, sl], bias_k[:, sl])
            own_v = (vrow < HEAD_DIM) if half == 0 else (vrow >= HEAD_DIM)
            vo_ref[0, h, 0] = jnp.where(vrow == ONES_LANE[half], 1.0, jnp.where(own_v, v_t, 0.0)).astype(BF16)
    return cum[tc - 1:tc, :]


def _attn_kernel(q_ref, kpre_ref, vpre_ref, k_ref, v_ref, z_ref, o_ref, s_ref, *, tq):
    qi = pl.program_id(2)
    heads = range(ATTN_HEADS)
    qs = [q_ref[0, h] for h in heads]

    def rowmax(s):
        return jnp.max(s, axis=0, keepdims=True)

    def k_chunk(h, idx):
        return k_ref[0, h, pl.ds(pl.multiple_of(idx * tq, tq), tq), :]

    def v_chunk(h, idx):
        return v_ref[0, h, idx]

    def scores(idx):
        return [_dot_nt(k_chunk(h, idx), qs[h]) for h in heads]

    def softmax_pv(carry, s, rmax, idx):
        out = []
        for h in heads:
            m, acc = carry[h]
            m_new = jnp.maximum(m, rmax[h])
            p = jnp.exp(s[h] - m_new).astype(BF16)
            out.append((m_new, jnp.exp(m - m_new) * acc + _dot(v_chunk(h, idx), p)))
        return out

    n_zero = PREFIX_ROWS - N_META
    s_pre = [_dot_nt(kpre_ref[0, h, n_zero:, :], qs[h]) for h in heads]
    key = lax.broadcasted_iota(jnp.int32, (tq, tq), 0)
    qry = lax.broadcasted_iota(jnp.int32, (tq, tq), 1)
    s_dia = [jnp.where(key <= qry, s, NEG) for s in scores(qi)]
    s_nxt = scores(0)
    carry = []
    for h in heads:
        m0 = jnp.maximum(rowmax(s_pre[h]), rowmax(s_dia[h]))
        p_pre = jnp.concatenate([jnp.zeros((n_zero, tq), BF16), jnp.exp(s_pre[h] - m0).astype(BF16)], axis=0)
        p_dia = jnp.exp(s_dia[h] - m0).astype(BF16)
        carry.append((m0, _dot(vpre_ref[0, h, 0], p_pre) + _dot(v_chunk(h, qi), p_dia)))
        s_ref[h] = s_nxt[h]
    rmax = [rowmax(s) for s in s_nxt]

    def step2(t, state):
        carry, rmax = state
        k0 = 2 * t
        s_a = [s_ref[h] for h in heads]
        s_b = scores(k0 + 1)
        carry = softmax_pv(carry, s_a, rmax, k0)
        rmax_b = [rowmax(s) for s in s_b]
        s_c = scores(k0 + 2)
        carry = softmax_pv(carry, s_b, rmax_b, k0 + 1)
        for h in heads:
            s_ref[h] = s_c[h]
        return carry, [rowmax(s) for s in s_c]

    def step1(k0, state):
        carry, rmax = state
        s_a = [s_ref[h] for h in heads]
        s_b = scores(k0 + 1)
        carry = softmax_pv(carry, s_a, rmax, k0)
        for h in heads:
            s_ref[h] = s_b[h]
        return carry, [rowmax(s) for s in s_b]

    steps = jnp.maximum(qi - 1, 0)
    pairs = steps // 2
    state = lax.fori_loop(0, pairs, step2, (carry, rmax))
    carry, rmax = lax.fori_loop(2 * pairs, steps, step1, state)

    some = qi > 0
    last = jnp.maximum(qi - 1, 0)
    s_fin = [jnp.where(some, s_ref[h], NEG) for h in heads]
    r_fin = [jnp.where(some, r, NEG) for r in rmax]
    accs = [acc for _, acc in softmax_pv(carry, s_fin, r_fin, last)]
    vrow = lax.broadcasted_iota(jnp.int32, (PAIR, tq), 0)
    for pr in range(ATTN_HEADS // 2):
        acc0, acc1 = accs[2 * pr], accs[2 * pr + 1]
        l0 = acc0[ONES_LANE[0]:ONES_LANE[0] + 1, :]
        l1 = acc1[ONES_LANE[1]:ONES_LANE[1] + 1, :]
        o_t = jnp.where(vrow < HEAD_DIM, acc0 / l0, acc1 / l1)
        z = z_ref[0, :, pr * PAIR:(pr + 1) * PAIR].astype(F32)
        o_ref[0, :, pr * PAIR:(pr + 1) * PAIR] = (o_t.T * (z * _sigmoid(z))).astype(o_ref.dtype)


def _fox_attn(qp, kp, vp, kpre, vpre, zf, tq):
    b, _, l, _ = qp.shape
    g = ATTN_HEADS
    out_block = pl.BlockSpec((1, tq, g * HEAD_DIM), lambda bi, p, qi: (bi, qi, p))
    assert vp.shape[-1] == tq and vpre.shape[-1] == PREFIX_ROWS
    pre = pl.BlockSpec((1, g, PREFIX_ROWS, LANES), lambda bi, p, qi: (0, p, 0, 0))
    pre_v = pl.BlockSpec((1, g, 1, PAIR, PREFIX_ROWS), lambda bi, p, qi: (0, p, 0, 0, 0))
    full = pl.BlockSpec((1, g, l, LANES), lambda bi, p, qi: (bi, p, 0, 0))
    full_v = pl.BlockSpec((1, g, l // tq, PAIR, tq), lambda bi, p, qi: (bi, p, 0, 0, 0))
    return pl.pallas_call(
        functools.partial(_attn_kernel, tq=tq),
        grid=(b, N_HEADS // g, l // tq),
        in_specs=[pl.BlockSpec((1, g, tq, LANES), lambda bi, p, qi: (bi, p, qi, 0)),
                  pre, pre_v, full, full_v, out_block],
        out_specs=out_block,
        out_shape=jax.ShapeDtypeStruct((b, l, D_BRANCH), BF16),
        scratch_shapes=[pltpu.VMEM((g, tq, tq), F32)],
        compiler_params=_params(("parallel", "parallel", "parallel")),
        name="fox_attn",
    )(qp, kpre, vpre, kp, vp, zf)


def _rwkv_kernel(rt_ref, at_ref, bt_ref, kt_ref, xv_ref, bh_ref, kh_ref, bonus_ref, z_ref, wend_ref,
                 gnw_ref, gnb_ref, ones_ref, z0_ref, *rest, tr, project):
    if project:
        yf_ref, x_ref, wout_ref, fnw_ref, out_ref, zf_ref, state_ref, yacc_ref = rest
    else:
        out_ref, zf_ref, state_ref, yacc_ref = rest
    t = pl.program_id(1)
    last = t == pl.num_programs(1) - 1

    @pl.when(t == 0)
    def _():
        state_ref[...] = z0_ref[...]

    ones_bd = ones_ref[...]

    def seg_sum(x):
        return _dot(x.astype(BF16), ones_bd)

    operand = dict(rt=rt_ref, at=at_ref, bt=bt_ref, kt=kt_ref, xv=xv_ref, bh=bh_ref, kh=kh_ref)

    col = lax.broadcasted_iota(jnp.int32, (CHUNK, PAIR), 1)
    trow = lax.broadcasted_iota(jnp.int32, (CHUNK, PAIR), 0)
    tcol = col % CHUNK
    head_a = col < HEAD_DIM
    strict = trow > tcol
    incl = trow >= tcol
    eye_sbs = jnp.where(trow == tcol, 1.0, 0.0).astype(F32)
    sq_r = lax.broadcasted_iota(jnp.int32, (PAIR, PAIR), 0)
    sq_c = lax.broadcasted_iota(jnp.int32, (PAIR, PAIR), 1)
    same_head = (sq_r // HEAD_DIM) == (sq_c // HEAD_DIM)
    eye_sq = sq_r == sq_c

    def stack(x):
        zero = jnp.zeros_like(x)
        return jnp.concatenate([jnp.where(head_a, x, zero), jnp.where(head_a, zero, x)], axis=0)

    def fold(sq):
        sq = jnp.where(same_head, sq, 0.0)
        return sq[:CHUNK] + sq[CHUNK:]

    def tile(name, c, p):
        return operand[name][0, c * CHUNK:(c + 1) * CHUNK, p * PAIR:(p + 1) * PAIR]

    n_chunks = tr // CHUNK
    units =[(c, p) for c in range(n_chunks) for p in range(N_PAIRS)]
    nu = range(len(units))
    wc = [wend_ref[0, c:c + 1, p * PAIR:(p + 1) * PAIR] for c, p in units]
    rt, at, bt, kt, xv, bh, kh = ([tile(name, c, p) for c, p in units]
                                  for name in ("rt", "at", "bt", "kt", "xv", "bh", "kh"))
    bt_s = [stack(x) for x in bt]
    kt_s = [stack(x) for x in kt]
    xv_s = [stack(x) for x in xv]

    ar = [jnp.concatenate([at[u], rt[u]], axis=0) for u in nu]
    x_b = [_dot_nt(ar[u], bt_s[u]) for u in nu]
    x_k = [_dot_nt(ar[u], kt_s[u]) for u in nu]
    a_ab = [jnp.where(strict, x[:CHUNK], 0.0) for x in x_b]
    a_rb = [jnp.where(incl, x[CHUNK:], 0.0).astype(BF16) for x in x_b]
    a_ak = [jnp.where(strict, x[:CHUNK], 0.0).astype(BF16) for x in x_k]
    a_rk = [jnp.where(incl, x[CHUNK:], 0.0).astype(BF16) for x in x_k]

    ab = [a.astype(BF16) for a in a_ab]
    pw = [_dot(ab[u], stack(ab[u])) for u in nu]
    tinv = [eye_sbs + a_ab[u] for u in nu]
    for level in range(1, 6):
        pb = [x.astype(BF16) for x in pw]
        if level < 5:
            both = [_dot(jnp.concatenate([tinv[u].astype(BF16), pb[u]], axis=0), stack(pb[u])) for u in nu]
            tinv = [tinv[u] + both[u][:CHUNK] for u in nu]
            pw = [both[u][CHUNK:] for u in nu]
        else:
            tinv = [tinv[u] + _dot(tinv[u].astype(BF16), stack(pb[u])) for u in nu]
    tb = [x.astype(BF16) for x in tinv]

    ap = [_dot(tb[u], stack(at[u])).astype(BF16) for u in nu]
    akv = [_dot(a_ak[u], xv_s[u]).astype(BF16) for u in nu]
    vp = [_dot(tb[u], stack(akv[u])).astype(BF16) for u in nu]
    ap_s = [stack(x) for x in ap]
    vp_s = [stack(x) for x in vp]
    m_sbs = [fold(jnp.where(eye_sq, wc[u], 0.0) + _dot_tn(bh[u], ap[u])).astype(BF16) for u in nu]
    g_sbs = [fold(_dot_tn(jnp.concatenate([bh[u], kh[u]], axis=0), jnp.concatenate([vp[u], xv[u]], axis=0)))
             for u in nu]
    rp = [(rt[u].astype(F32) + _dot(a_rb[u], ap_s[u])).astype(BF16) for u in nu]
    y0 = [_dot(jnp.concatenate([a_rb[u], a_rk[u]], axis=1), jnp.concatenate([vp_s[u], xv_s[u]], axis=0))
          for u in nu]

    for u, (c, p) in enumerate(units):
        z_bd = stack(state_ref[p].astype(BF16))
        both = _dot(jnp.concatenate([rp[u], m_sbs[u]], axis=0), z_bd)
        yacc_ref[c * CHUNK:(c + 1) * CHUNK, p * PAIR:(p + 1) * PAIR] = both[:CHUNK] + y0[u]
        state_ref[p] = both[CHUNK:] + g_sbs[u]

    @pl.when(last)
    def _():
        zf_ref[0] = state_ref[...]

    inv_n = 1.0 / HEAD_DIM
    y = yacc_ref[...]
    mean = seg_sum(y) * inv_n
    d = y - mean
    var = seg_sum(d * d) * inv_n
    yn = d * lax.rsqrt(var + GN_EPS) * gnw_ref[...] + gnb_ref[...]
    z = z_ref[0].astype(F32)
    y_rwkv = ((yn + bonus_ref[0].astype(F32)) * (z * _sigmoid(z))).astype(BF16)
    if project:
        mix = _dot(yf_ref[0], wout_ref[:D_BRANCH, :]) + _dot(y_rwkv, wout_ref[D_BRANCH:, :])
        h = x_ref[0] + mix
        out_ref[0] = h * lax.rsqrt(jnp.mean(h * h, axis=-1, keepdims=True) + NORM_EPS) * fnw_ref[...]
    else:
        out_ref[0] = y_rwkv


def _rwkv(ops, z, wend, gn_w, gn_b, ones_bd, z0, tr, finish=None):
    b, l, _ = z.shape
    wide = pl.BlockSpec((1, tr, D_BRANCH), lambda bi, t: (bi, t, 0))
    row_w = pl.BlockSpec((1, D_BRANCH), lambda bi, t: (0, 0))
    ones = pl.BlockSpec((D_BRANCH, D_BRANCH), lambda bi, t: (0, 0))
    st_in = pl.BlockSpec((N_PAIRS, CHUNK, PAIR), lambda bi, t: (0, 0, 0))
    st_out = pl.BlockSpec((1, N_PAIRS, CHUNK, PAIR), lambda bi, t: (bi, 0, 0, 0))
    in_specs = ([wide] * (len(RWKV_OPERANDS) + 1)
                + [pl.BlockSpec((1, tr // CHUNK, D_BRANCH), lambda bi, t: (bi, t, 0)), row_w, row_w, ones, st_in])
    args = [*ops, z, wend, gn_w, gn_b, ones_bd, z0]
    if finish is None:
        out_spec, out_sds = wide, jax.ShapeDtypeStruct((b, l, D_BRANCH), BF16)
    else:
        full = pl.BlockSpec((1, tr, D_MODEL), lambda bi, t: (bi, t, 0))
        in_specs += [wide, full, pl.BlockSpec((D_MODEL, D_MODEL), lambda bi, t: (0, 0)),
                     pl.BlockSpec((1, D_MODEL), lambda bi, t: (0, 0))]
        args += list(finish)
        out_spec, out_sds = full, jax.ShapeDtypeStruct((b, l, D_MODEL), F32)
    return pl.pallas_call(
        functools.partial(_rwkv_kernel, tr=tr, project=finish is not None),
        grid=(b, l // tr),
        in_specs=in_specs,
        out_specs=[out_spec, st_out],
        out_shape=[out_sds, jax.ShapeDtypeStruct((b, N_PAIRS, CHUNK, PAIR), F32)],
        scratch_shapes=[pltpu.VMEM((N_PAIRS, CHUNK, PAIR), F32), pltpu.VMEM((tr, D_BRANCH), F32)],
        compiler_params=_params(("parallel", "arbitrary")),
        name="rwkv",
    )(*args)


def _tiles(l):
    assert l % 512 == 0
    return 512, 512, 512


def kernel(x, meta, norm_w, w_in, b_f, mu_shift, w0, w_up, a0, a_up, k_k, k_a, r_k, gn_w, gn_b,
           w_out, final_norm_w):
    b, l, d = x.shape
    assert d == D_MODEL and norm_w.shape[0] == 1
    tm, t_attn, t_rwkv = _tiles(l)

    wi = w_in[0].astype(BF16)
    o = 0
    cols = {}
    for name, width in (("q", D_BRANCH), ("k", D_BRANCH), ("v", D_BRANCH), ("fl", N_HEADS), ("zf", D_BRANCH),
                        ("r", D_BRANCH), ("rk", D_BRANCH), ("rv", D_BRANCH), ("wd", RANK), ("ad", RANK),
                        ("zr", D_BRANCH)):
        cols[name] = wi[:, o:o + width]
        o += width
    w_all = jnp.concatenate(
        [cols[n] for n in ("q", "k", "v", "zf", "r", "rk", "rv", "zr", "wd", "ad", "fl")]
        + [jnp.zeros((D_MODEL, LANES - N_HEADS), BF16)], axis=1)

    row = lambda vec: vec.reshape(1, -1).astype(F32)
    mu = mu_shift[0]
    mu_r, mu_k, mu_v = (row(mu[i * D_BRANCH:(i + 1) * D_BRANCH]) for i in range(3))
    mu_wa = row(mu[3 * D_BRANCH:])
    zeros_up = jnp.zeros((RANK, D_BRANCH), F32)
    wup_pad = jnp.concatenate([w_up[0], zeros_up], axis=0)
    aup_pad = jnp.concatenate([zeros_up, a_up[0]], axis=0)
    hid = np.arange(D_BRANCH) // HEAD_DIM
    ones_bd = jnp.asarray(hid[:, None] == hid[None, :], BF16)
    shift_prm = (mu_r, mu_k, mu_v, mu_wa, row(w0[0]), wup_pad.astype(BF16), row(a0[0]), aup_pad.astype(BF16),
                 row(k_k[0]), row(k_a[0]), row(r_k[0]), ones_bd)
    gnw, gnb = row(gn_w[0]), row(gn_b[0])
    bf_pad = jnp.concatenate([b_f[0], jnp.zeros((LANES - N_HEADS,), F32)]).reshape(1, LANES)
    pq, pk = _select_matrices()
    nw = row(norm_w[0])
    n_ops = len(RWKV_OPERANDS)

    pre_rows = jnp.concatenate([jnp.zeros((PREFIX_ROWS - N_META, D_MODEL), F32), meta.astype(F32)], axis=0)
    zero_w, zero_n = jnp.zeros((1, D_BRANCH), F32), jnp.zeros((1, LANES), F32)
    pre = _in_proj(pre_rows, nw, w_all, shift_prm, (zero_w, zero_w, zero_w, zero_n),
                   (bf_pad, zero_n, pq, pk), PREFIX_ROWS, 1, PREFIX_ROWS - N_META)
    _, kpre, vpre, _, pzr_ = pre[:5]
    pre_ops, pre_wend = pre[5:5 + n_ops], pre[5 + n_ops]
    last_raw, c_pre = pre[6 + n_ops:10 + n_ops], pre[10 + n_ops]
    lead = lambda a: a[None]
    _, z_pre = _rwkv([lead(a) for a in pre_ops], lead(pzr_), lead(pre_wend), gnw, gnb, ones_bd,
                     jnp.zeros((N_PAIRS, CHUNK, PAIR), F32), PREFIX_ROWS)

    main = _in_proj(x.reshape(b * l, D_MODEL), nw, w_all, shift_prm, last_raw, (bf_pad, c_pre, pq, pk),
                    tm, l // tm, 0)
    qp, kp, vp, zf_, zr_ = main[:5]
    bl = lambda a: a.reshape(b, l, a.shape[-1])
    y_fox = _fox_attn(qp, kp, vp, kpre, vpre, bl(zf_), t_attn)
    out, _ = _rwkv([bl(a) for a in main[5:5 + n_ops]], bl(zr_),
                   main[5 + n_ops].reshape(b, l // CHUNK, D_BRANCH), gnw, gnb, ones_bd, z_pre[0], t_rwkv,
                   finish=(y_fox, x, w_out[0].astype(BF16), row(final_norm_w)))
    return out
```

```python
import functools

import numpy as np
import jax
import jax.numpy as jnp
from jax import lax
from jax.experimental import pallas as pl
from jax.experimental.pallas import tpu as pltpu

F32 = jnp.float32
BF16 = jnp.bfloat16

D_MODEL = 1024
N_META = 16
HEAD_DIM = 64
N_HEADS = 8
D_BRANCH = N_HEADS * HEAD_DIM
N_PAIRS = N_HEADS // 2
RANK = 64
NORM_EPS = 1e-6
GN_EPS = 64e-5
KK_EPS = 1e-12
NEG = -1e30

LANES = 128
PREFIX_ROWS = 128
CHUNK = 64
SUB = 2 * CHUNK
PAIR = 2 * HEAD_DIM
VMEM_LIMIT = 56 * 1024 * 1024

N_WIDE = 8
COL_WA = N_WIDE * D_BRANCH
COL_FL = COL_WA + LANES
N_COLS = COL_FL + LANES

N_CPARTS = 3
ONES_LANE = (HEAD_DIM, 0)
ATTN_HEADS = 4


def _dot(a, b):
    return jnp.dot(a, b, preferred_element_type=F32)


def _pieces(x, n):
    out = []
    for _ in range(n - 1):
        p = x.astype(BF16)
        out.append(p)
        x = x - p.astype(F32)
    out.append(x.astype(BF16))
    return out


def _dot_exact_lhs(a_bf16, x, n):
    acc = None
    for p in _pieces(x, n):
        t = _dot(a_bf16, p)
        acc = t if acc is None else acc + t
    return acc


def _dot_x2(x, w_bf16):
    x_hi, x_lo = _pieces(x, 2)
    return _dot(x_hi, w_bf16) + _dot(x_lo, w_bf16)


def _dot_nt(a, b):
    return lax.dot_general(a, b, (((1,), (1,)), ((), ())), preferred_element_type=F32)


def _dot_tn(a, b):
    return lax.dot_general(a, b, (((0,), (0,)), ((), ())), preferred_element_type=F32)


def _softplus(x):
    return jnp.maximum(x, 0.0) + jnp.log(1.0 + jnp.exp(-jnp.abs(x)))


def _sigmoid(x):
    return 1.0 / (1.0 + jnp.exp(-x))


def _params(sem):
    return pltpu.CompilerParams(dimension_semantics=sem, vmem_limit_bytes=VMEM_LIMIT)


G_Q, G_K, G_V, G_ZF, G_R, G_RK, G_RV, G_ZR = range(N_WIDE)
RWKV_OPERANDS = ("rt", "at", "bt", "kt", "xv", "bh", "kh", "bonus")


def _in_proj_kernel(x_ref, nw_ref, w_ref,
                    mu_r_ref, mu_k_ref, mu_v_ref, mu_wa_ref, w0_ref, wup_ref,
                    a0_ref, aup_ref, kk_ref, ka_ref, rk_ref, ones_ref,
                    pr_ref, pk_ref, pv_ref, pwa_ref, bf_ref, c0_ref, selq_ref, selk_ref,
                    qo_ref, ko_ref, vo_ref, zf_ref, zr_ref,
                    rt_ref, at_ref, bt_ref, kt_ref, xv_ref, bh_ref, kh_ref, bonus_ref, wend_ref,
                    lr_ref, lk_ref, lv_ref, lwa_ref, cl_ref,
                    sr_ref, sk_ref, sv_ref, swa_ref, sc_ref, *, tm, tiles_per_seq, n_pad):
    i = pl.program_id(0)
    seq_tile = lax.rem(i, tiles_per_seq)

    @pl.when(seq_tile == 0)
    def _():
        sr_ref[...] = pr_ref[...]
        sk_ref[...] = pk_ref[...]
        sv_ref[...] = pv_ref[...]
        swa_ref[...] = pwa_ref[...]
        sc_ref[...] = c0_ref[...]

    x = x_ref[...]
    u = x * lax.rsqrt(jnp.mean(x * x, axis=-1, keepdims=True) + NORM_EPS) * nw_ref[...]
    ub = u.astype(BF16)

    def group(g):
        return _dot(ub, w_ref[:, g * D_BRANCH:(g + 1) * D_BRANCH])

    raw = [group(G_R), group(G_RK), group(G_RV), _dot(ub, w_ref[:, COL_WA:COL_WA + LANES])]
    ones_bd = ones_ref[...]

    def seg_sum(v):
        return _dot(v.astype(BF16), ones_bd)

    def shifted(cur, prev_row, mu_ref):
        first = lax.broadcasted_iota(jnp.int32, cur.shape, 0) == 0
        prev = jnp.where(first, prev_row, pltpu.roll(cur, 1, 0))
        return cur + mu_ref[...] * (prev - cur)

    r2 = lax.broadcasted_iota(jnp.int32, (SUB, SUB), 0)
    c2 = lax.broadcasted_iota(jnp.int32, (SUB, SUB), 1)
    tri = jnp.where((r2 >= c2) & (r2 // CHUNK == c2 // CHUNK), 1.0, 0.0).astype(BF16)
    outs = dict(zip(RWKV_OPERANDS, (rt_ref, at_ref, bt_ref, kt_ref, xv_ref, bh_ref, kh_ref, bonus_ref)))

    def rwkv_prepare():
        prev_rows = [sr_ref[...], sk_ref[...], sv_ref[...], swa_ref[...]]
        mus = (mu_r_ref, mu_k_ref, mu_v_ref, mu_wa_ref)
        xr, xk, xv, xwa = [shifted(c, p, mu) for c, p, mu in zip(raw, prev_rows, mus)]
        w_lin = w0_ref[...] + _dot_x2(jnp.tanh(xwa), wup_ref[...])
        a_lin = a0_ref[...] + _dot(xwa.astype(BF16), aup_ref[...])
        kk = xk * kk_ref[...]
        kk_ss = seg_sum(kk * kk)
        yield [c[tm - 1:tm, :] for c in raw]
        w = -_softplus(-w_lin) - 0.5
        ld = -jnp.exp(w)
        a = _sigmoid(a_lin)
        kk = kk * lax.rsqrt(kk_ss + KK_EPS)
        kmod = xk * (1.0 + (a - 1.0) * ka_ref[...])
        lw = jnp.concatenate([_dot_exact_lhs(tri, ld[sb * SUB:(sb + 1) * SUB, :], 2)
                              for sb in range(tm // SUB)], axis=0)
        rk_sum = seg_sum(xr * kmod * rk_ref[...])
        yield None
        w_inv = jnp.exp(-lw)
        bt = kk * a * w_inv
        kt = kmod * w_inv
        for name, val in (("rt", xr * jnp.exp(lw)), ("at", -kk * jnp.exp(lw - ld)), ("bt", bt), ("kt", kt),
                          ("xv", xv), ("bonus", rk_sum * xv)):
            outs[name][...] = val.astype(BF16)
        for c in range(tm // CHUNK):
            rows = slice(c * CHUNK, (c + 1) * CHUNK)
            w_c = jnp.exp(lw[(c + 1) * CHUNK - 1:(c + 1) * CHUNK, :])
            wend_ref[c:c + 1, :] = w_c
            bh_ref[rows, :] = (bt[rows, :] * w_c).astype(BF16)
            kh_ref[rows, :] = (kt[rows, :] * w_c).astype(BF16)
        yield None

    parts = rwkv_prepare()
    last_rows = next(parts)
    q = group(G_Q).astype(BF16)
    k = group(G_K).astype(BF16)
    next(parts)
    v = group(G_V).astype(BF16)
    zf_ref[...] = group(G_ZF).astype(BF16)
    next(parts)
    zr_ref[...] = group(G_ZR).astype(BF16)
    fl = _dot(ub, w_ref[:, COL_FL:COL_FL + LANES])
    bias_q, bias_k, c_last = _fox_bias(fl, bf_ref[...], sc_ref[...], selq_ref[...], selk_ref[...],
                                       seq_tile * tm, n_pad)
    sc_ref[...] = c_last
    _fox_write(q, k, v, bias_q, bias_k, qo_ref, ko_ref, vo_ref)

    for ref, rowv in zip((sr_ref, sk_ref, sv_ref, swa_ref), last_rows):
        ref[...] = rowv

    @pl.when(i == pl.num_programs(0) - 1)
    def _():
        for out, rowv in zip((lr_ref, lk_ref, lv_ref, lwa_ref), last_rows):
            out[...] = rowv
        cl_ref[...] = c_last


def _in_proj(rows, norm_w, w_all, prm, prev, fox_prm, tm, tiles_per_seq, n_pad):
    n = rows.shape[0]
    n_seq = n // (tm * tiles_per_seq)
    tile = lambda width: pl.BlockSpec((tm, width), lambda i: (i, 0))
    const = lambda shape: pl.BlockSpec(shape, lambda i: (0,) * len(shape))
    row_w, row_n, up = const((1, D_BRANCH)), const((1, LANES)), const((LANES, D_BRANCH))
    wide = lambda dt: jax.ShapeDtypeStruct((n, D_BRANCH), dt)
    per_head = pl.BlockSpec((1, N_HEADS, tm, LANES), lambda i: (i // tiles_per_seq, 0, i % tiles_per_seq, 0))
    per_head_t = pl.BlockSpec((1, N_HEADS, 1, PAIR, tm),
                              lambda i: (i // tiles_per_seq, 0, i % tiles_per_seq, 0, 0))
    return pl.pallas_call(
        functools.partial(_in_proj_kernel, tm=tm, tiles_per_seq=tiles_per_seq, n_pad=n_pad),
        grid=(n // tm,),
        in_specs=[tile(D_MODEL), const((1, D_MODEL)),
                  pl.BlockSpec((D_MODEL, N_COLS), lambda i: (0, 0), pipeline_mode=pl.Buffered(1)),
                  row_w, row_w, row_w, row_n, row_w, up, row_w, up,
                  row_w, row_w, row_w, const((D_BRANCH, D_BRANCH)),
                  row_w, row_w, row_w, row_n, row_n, row_n, up, up],
        out_specs=[per_head, per_head, per_head_t, tile(D_BRANCH), tile(D_BRANCH)]
                  + [tile(D_BRANCH)] * len(RWKV_OPERANDS)
                  + [pl.BlockSpec((tm // CHUNK, D_BRANCH), lambda i: (i, 0)),
                     row_w, row_w, row_w, row_n, row_n],
        out_shape=[jax.ShapeDtypeStruct((n_seq, N_HEADS, tiles_per_seq * tm, LANES), BF16)] * 2
                  + [jax.ShapeDtypeStruct((n_seq, N_HEADS, tiles_per_seq, PAIR, tm), BF16),
                     wide(BF16), wide(BF16)]
                  + [wide(BF16)] * len(RWKV_OPERANDS)
                  + [jax.ShapeDtypeStruct((n // CHUNK, D_BRANCH), F32)]
                  + [jax.ShapeDtypeStruct((1, D_BRANCH), F32)] * 3 + [jax.ShapeDtypeStruct((1, LANES), F32)] * 2,
        scratch_shapes=[pltpu.VMEM((1, D_BRANCH), F32), pltpu.VMEM((1, D_BRANCH), F32),
                        pltpu.VMEM((1, D_BRANCH), F32), pltpu.VMEM((1, LANES), F32),
                        pltpu.VMEM((1, LANES), F32)],
        compiler_params=_params(("arbitrary",)),
        name="in_proj",
    )(rows, norm_w, w_all, *prm, *prev, *fox_prm)


def _select_matrices():
    pq = np.zeros((LANES, D_BRANCH), np.float32)
    pk = np.zeros((LANES, D_BRANCH), np.float32)
    for h in range(N_HEADS):
        base = (h // 2) * PAIR + (HEAD_DIM if h % 2 == 0 else 0)
        for part in range(N_CPARTS):
            pq[part * N_HEADS + h, base + part] = 1.0
            pq[N_CPARTS * N_HEADS + h, base + N_CPARTS + part] = 1.0
            pk[N_CPARTS * N_HEADS + h, base + part] = 1.0
            pk[part * N_HEADS + h, base + N_CPARTS + part] = -1.0
    return jnp.asarray(pq, BF16), jnp.asarray(pk, BF16)


def _fox_bias(fl, bf, carry, pq, pk, row_base, n_pad):
    tc = fl.shape[0]
    lane = lax.broadcasted_iota(jnp.int32, (tc, LANES), 1)
    row = lax.broadcasted_iota(jnp.int32, (tc, LANES), 0) + row_base
    r2 = lax.broadcasted_iota(jnp.int32, (tc, tc), 0)
    c2 = lax.broadcasted_iota(jnp.int32, (tc, tc), 1)
    tri = jnp.where(r2 >= c2, 1.0, 0.0).astype(BF16)
    ones = jnp.where((lane >= N_CPARTS * N_HEADS) & (lane < (N_CPARTS + 1) * N_HEADS), 1.0, 0.0)

    x = fl + bf
    logf = jnp.minimum(x, 0.0) - jnp.log(1.0 + jnp.exp(-jnp.abs(x)))
    valid = lane < N_HEADS
    if n_pad:
        valid = valid & (row >= n_pad)
    logf = jnp.where(valid, logf, 0.0)
    cum = _dot_exact_lhs(tri, logf, 2) + carry

    p1 = cum.astype(BF16).astype(F32)
    rem = cum - p1
    p2 = rem.astype(BF16).astype(F32)
    p3 = (rem - p2).astype(BF16).astype(F32)
    cbits = p1 + pltpu.roll(p2, N_HEADS, 1) + pltpu.roll(p3, 2 * N_HEADS, 1) + ones
    cbits_k = cbits
    if n_pad:
        cbits_k = jnp.where((row < n_pad) & (lane < N_HEADS), -NEG, cbits)
    bias_q = _dot(cbits.astype(BF16), pq).astype(BF16)
    bias_k = _dot(cbits_k.astype(BF16), pk).astype(BF16)
    return bias_q, bias_k, cum[tc - 1:tc, :]


def _fox_write(q, k, v, bias_q, bias_k, qo_ref, ko_ref, vo_ref):
    tc = q.shape[0]
    lane = lax.broadcasted_iota(jnp.int32, (tc, LANES), 1)
    low = lane < HEAD_DIM
    sel_r = lax.broadcasted_iota(jnp.int32, (PAIR, PAIR), 0)
    sel_c = lax.broadcasted_iota(jnp.int32, (PAIR, PAIR), 1)
    eye = jnp.where(sel_r == sel_c, 1.0, 0.0).astype(BF16)
    vrow = lax.broadcasted_iota(jnp.int32, (PAIR, tc), 0)
    for pr in range(N_PAIRS):
        sl = slice(pr * PAIR, (pr + 1) * PAIR)
        q_sc = q[:, sl] * jnp.asarray(0.125, BF16)
        v_t = _dot_nt(eye, v[:, sl])
        for half in range(2):
            h = 2 * pr + half
            own = low if half == 0 else jnp.logical_not(low)
            qo_ref[0, h] = jnp.where(own, q_sc, bias_q[:, sl])
            ko_ref[0, h] = jnp.where(own, k[:, sl], bias_k[:, sl])
            own_v = (vrow < HEAD_DIM) if half == 0 else (vrow >= HEAD_DIM)
            vo_ref[0, h, 0] = jnp.where(vrow == ONES_LANE[half], 1.0, jnp.where(own_v, v_t, 0.0)).astype(BF16)


def _attn_kernel(q_ref, kpre_ref, vpre_ref, k_ref, v_ref, z_ref, o_ref, s_ref, *, tq):
    qi = pl.program_id(2)
    heads = range(ATTN_HEADS)
    qs = [q_ref[0, h] for h in heads]

    def rowmax(s):
        return jnp.max(s, axis=0, keepdims=True)

    def k_chunk(h, idx):
        return k_ref[0, h, pl.ds(pl.multiple_of(idx * tq, tq), tq), :]

    def v_chunk(h, idx):
        return v_ref[0, h, idx]

    def scores(idx):
        return [_dot_nt(k_chunk(h, idx), qs[h]) for h in heads]

    def softmax_pv(carry, s, rmax, idx):
        out = []
        for h in heads:
            m, acc = carry[h]
            m_new = jnp.maximum(m, rmax[h])
            p = jnp.exp(s[h] - m_new).astype(BF16)
            out.append((m_new, jnp.exp(m - m_new) * acc + _dot(v_chunk(h, idx), p)))
        return out

    n_zero = PREFIX_ROWS - N_META
    s_pre = [_dot_nt(kpre_ref[0, h, n_zero:, :], qs[h]) for h in heads]
    key = lax.broadcasted_iota(jnp.int32, (tq, tq), 0)
    qry = lax.broadcasted_iota(jnp.int32, (tq, tq), 1)
    s_dia = [jnp.where(key <= qry, s, NEG) for s in scores(qi)]
    s_nxt = scores(0)
    carry = []
    for h in heads:
        m0 = jnp.maximum(rowmax(s_pre[h]), rowmax(s_dia[h]))
        p_pre = jnp.concatenate([jnp.zeros((n_zero, tq), BF16), jnp.exp(s_pre[h] - m0).astype(BF16)], axis=0)
        p_dia = jnp.exp(s_dia[h] - m0).astype(BF16)
        carry.append((m0, _dot(vpre_ref[0, h, 0], p_pre) + _dot(v_chunk(h, qi), p_dia)))
        s_ref[h] = s_nxt[h]
    rmax = [rowmax(s) for s in s_nxt]

    def step2(t, state):
        carry, rmax = state
        k0 = 2 * t
        s_a = [s_ref[h] for h in heads]
        s_b = scores(k0 + 1)
        carry = softmax_pv(carry, s_a, rmax, k0)
        rmax_b = [rowmax(s) for s in s_b]
        s_c = scores(k0 + 2)
        carry = softmax_pv(carry, s_b, rmax_b, k0 + 1)
        for h in heads:
            s_ref[h] = s_c[h]
        return carry, [rowmax(s) for s in s_c]

    def step1(k0, state):
        carry, rmax = state
        s_a = [s_ref[h] for h in heads]
        s_b = scores(k0 + 1)
        carry = softmax_pv(carry, s_a, rmax, k0)
        for h in heads:
            s_ref[h] = s_b[h]
        return carry, [rowmax(s) for s in s_b]

    steps = jnp.maximum(qi - 1, 0)
    pairs = steps // 2
    state = lax.fori_loop(0, pairs, step2, (carry, rmax))
    carry, rmax = lax.fori_loop(2 * pairs, steps, step1, state)

    some = qi > 0
    last = jnp.maximum(qi - 1, 0)
    s_fin = [jnp.where(some, s_ref[h], NEG) for h in heads]
    r_fin = [jnp.where(some, r, NEG) for r in rmax]
    accs = [acc for _, acc in softmax_pv(carry, s_fin, r_fin, last)]
    vrow = lax.broadcasted_iota(jnp.int32, (PAIR, tq), 0)
    for pr in range(ATTN_HEADS // 2):
        acc0, acc1 = accs[2 * pr], accs[2 * pr + 1]
        l0 = acc0[ONES_LANE[0]:ONES_LANE[0] + 1, :]
        l1 = acc1[ONES_LANE[1]:ONES_LANE[1] + 1, :]
        o_t = jnp.where(vrow < HEAD_DIM, acc0 / l0, acc1 / l1)
        z = z_ref[0, :, pr * PAIR:(pr + 1) * PAIR].astype(F32)
        o_ref[0, :, pr * PAIR:(pr + 1) * PAIR] = (o_t.T * (z * _sigmoid(z))).astype(o_ref.dtype)


def _fox_attn(qp, kp, vp, kpre, vpre, zf, tq):
    b, _, l, _ = qp.shape
    g = ATTN_HEADS
    out_block = pl.BlockSpec((1, tq, g * HEAD_DIM), lambda bi, p, qi: (bi, qi, p))
    assert vp.shape[-1] == tq and vpre.shape[-1] == PREFIX_ROWS
    pre = pl.BlockSpec((1, g, PREFIX_ROWS, LANES), lambda bi, p, qi: (0, p, 0, 0))
    pre_v = pl.BlockSpec((1, g, 1, PAIR, PREFIX_ROWS), lambda bi, p, qi: (0, p, 0, 0, 0))
    full = pl.BlockSpec((1, g, l, LANES), lambda bi, p, qi: (bi, p, 0, 0))
    full_v = pl.BlockSpec((1, g, l // tq, PAIR, tq), lambda bi, p, qi: (bi, p, 0, 0, 0))
    return pl.pallas_call(
        functools.partial(_attn_kernel, tq=tq),
        grid=(b, N_HEADS // g, l // tq),
        in_specs=[pl.BlockSpec((1, g, tq, LANES), lambda bi, p, qi: (bi, p, qi, 0)),
                  pre, pre_v, full, full_v, out_block],
        out_specs=out_block,
        out_shape=jax.ShapeDtypeStruct((b, l, D_BRANCH), BF16),
        scratch_shapes=[pltpu.VMEM((g, tq, tq), F32)],
        compiler_params=_params(("parallel", "parallel", "parallel")),
        name="fox_attn",
    )(qp, kpre, vpre, kp, vp, zf)


def _rwkv_kernel(rt_ref, at_ref, bt_ref, kt_ref, xv_ref, bh_ref, kh_ref, bonus_ref, z_ref, wend_ref,
                 gnw_ref, gnb_ref, ones_ref, z0_ref, *rest, tr, project):
    if project:
        yf_ref, x_ref, wout_ref, fnw_ref, out_ref, zf_ref, state_ref, yacc_ref = rest
    else:
        out_ref, zf_ref, state_ref, yacc_ref = rest
    t = pl.program_id(1)
    last = t == pl.num_programs(1) - 1

    @pl.when(t == 0)
    def _():
        state_ref[...] = z0_ref[...]

    ones_bd = ones_ref[...]

    def seg_sum(x):
        return _dot(x.astype(BF16), ones_bd)

    operand = dict(rt=rt_ref, at=at_ref, bt=bt_ref, kt=kt_ref, xv=xv_ref, bh=bh_ref, kh=kh_ref)

    col = lax.broadcasted_iota(jnp.int32, (CHUNK, PAIR), 1)
    trow = lax.broadcasted_iota(jnp.int32, (CHUNK, PAIR), 0)
    tcol = col % CHUNK
    head_a = col < HEAD_DIM
    strict = trow > tcol
    incl = trow >= tcol
    eye_sbs = jnp.where(trow == tcol, 1.0, 0.0).astype(F32)
    sq_r = lax.broadcasted_iota(jnp.int32, (PAIR, PAIR), 0)
    sq_c = lax.broadcasted_iota(jnp.int32, (PAIR, PAIR), 1)
    same_head = (sq_r // HEAD_DIM) == (sq_c // HEAD_DIM)
    eye_sq = sq_r == sq_c

    def stack(x):
        zero = jnp.zeros_like(x)
        return jnp.concatenate([jnp.where(head_a, x, zero), jnp.where(head_a, zero, x)], axis=0)

    def fold(sq):
        sq = jnp.where(same_head, sq, 0.0)
        return sq[:CHUNK] + sq[CHUNK:]

    def tile(name, c, p):
        return operand[name][0, c * CHUNK:(c + 1) * CHUNK, p * PAIR:(p + 1) * PAIR]

    n_chunks = tr // CHUNK
    units =[(c, p) for c in range(n_chunks) for p in range(N_PAIRS)]
    nu = range(len(units))
    wc = [wend_ref[0, c:c + 1, p * PAIR:(p + 1) * PAIR] for c, p in units]
    rt, at, bt, kt, xv, bh, kh = ([tile(name, c, p) for c, p in units]
                                  for name in ("rt", "at", "bt", "kt", "xv", "bh", "kh"))
    bt_s = [stack(x) for x in bt]
    kt_s = [stack(x) for x in kt]
    xv_s = [stack(x) for x in xv]

    ar = [jnp.concatenate([at[u], rt[u]], axis=0) for u in nu]
    x_b = [_dot_nt(ar[u], bt_s[u]) for u in nu]
    x_k = [_dot_nt(ar[u], kt_s[u]) for u in nu]
    a_ab = [jnp.where(strict, x[:CHUNK], 0.0) for x in x_b]
    a_rb = [jnp.where(incl, x[CHUNK:], 0.0).astype(BF16) for x in x_b]
    a_ak = [jnp.where(strict, x[:CHUNK], 0.0).astype(BF16) for x in x_k]
    a_rk = [jnp.where(incl, x[CHUNK:], 0.0).astype(BF16) for x in x_k]

    ab = [a.astype(BF16) for a in a_ab]
    pw = [_dot(ab[u], stack(ab[u])) for u in nu]
    tinv = [eye_sbs + a_ab[u] for u in nu]
    for level in range(1, 6):
        pb = [x.astype(BF16) for x in pw]
        if level < 5:
            both = [_dot(jnp.concatenate([tinv[u].astype(BF16), pb[u]], axis=0), stack(pb[u])) for u in nu]
            tinv = [tinv[u] + both[u][:CHUNK] for u in nu]
            pw = [both[u][CHUNK:] for u in nu]
        else:
            tinv = [tinv[u] + _dot(tinv[u].astype(BF16), stack(pb[u])) for u in nu]
    tb = [x.astype(BF16) for x in tinv]

    ap = [_dot(tb[u], stack(at[u])).astype(BF16) for u in nu]
    akv = [_dot(a_ak[u], xv_s[u]).astype(BF16) for u in nu]
    vp = [_dot(tb[u], stack(akv[u])).astype(BF16) for u in nu]
    ap_s = [stack(x) for x in ap]
    vp_s = [stack(x) for x in vp]
    m_sbs = [fold(jnp.where(eye_sq, wc[u], 0.0) + _dot_tn(bh[u], ap[u])).astype(BF16) for u in nu]
    g_sbs = [fold(_dot_tn(jnp.concatenate([bh[u], kh[u]], axis=0), jnp.concatenate([vp[u], xv[u]], axis=0)))
             for u in nu]
    rp = [(rt[u].astype(F32) + _dot(a_rb[u], ap_s[u])).astype(BF16) for u in nu]
    y0 = [_dot(jnp.concatenate([a_rb[u], a_rk[u]], axis=1), jnp.concatenate([vp_s[u], xv_s[u]], axis=0))
          for u in nu]

    for u, (c, p) in enumerate(units):
        z_bd = stack(state_ref[p].astype(BF16))
        both = _dot(jnp.concatenate([rp[u], m_sbs[u]], axis=0), z_bd)
        yacc_ref[c * CHUNK:(c + 1) * CHUNK, p * PAIR:(p + 1) * PAIR] = both[:CHUNK] + y0[u]
        state_ref[p] = both[CHUNK:] + g_sbs[u]

    @pl.when(last)
    def _():
        zf_ref[0] = state_ref[...]

    inv_n = 1.0 / HEAD_DIM
    y = yacc_ref[...]
    mean = seg_sum(y) * inv_n
    d = y - mean
    var = seg_sum(d * d) * inv_n
    yn = d * lax.rsqrt(var + GN_EPS) * gnw_ref[...] + gnb_ref[...]
    z = z_ref[0].astype(F32)
    y_rwkv = ((yn + bonus_ref[0].astype(F32)) * (z * _sigmoid(z))).astype(BF16)
    if project:
        mix = _dot(yf_ref[0], wout_ref[:D_BRANCH, :]) + _dot(y_rwkv, wout_ref[D_BRANCH:, :])
        h = x_ref[0] + mix
        out_ref[0] = h * lax.rsqrt(jnp.mean(h * h, axis=-1, keepdims=True) + NORM_EPS) * fnw_ref[...]
    else:
        out_ref[0] = y_rwkv


def _rwkv(ops, z, wend, gn_w, gn_b, ones_bd, z0, tr, finish=None):
    b, l, _ = z.shape
    wide = pl.BlockSpec((1, tr, D_BRANCH), lambda bi, t: (bi, t, 0))
    row_w = pl.BlockSpec((1, D_BRANCH), lambda bi, t: (0, 0))
    ones = pl.BlockSpec((D_BRANCH, D_BRANCH), lambda bi, t: (0, 0))
    st_in = pl.BlockSpec((N_PAIRS, CHUNK, PAIR), lambda bi, t: (0, 0, 0))
    st_out = pl.BlockSpec((1, N_PAIRS, CHUNK, PAIR), lambda bi, t: (bi, 0, 0, 0))
    in_specs = ([wide] * (len(RWKV_OPERANDS) + 1)
                + [pl.BlockSpec((1, tr // CHUNK, D_BRANCH), lambda bi, t: (bi, t, 0)), row_w, row_w, ones, st_in])
    args = [*ops, z, wend, gn_w, gn_b, ones_bd, z0]
    if finish is None:
        out_spec, out_sds = wide, jax.ShapeDtypeStruct((b, l, D_BRANCH), BF16)
    else:
        full = pl.BlockSpec((1, tr, D_MODEL), lambda bi, t: (bi, t, 0))
        in_specs += [wide, full, pl.BlockSpec((D_MODEL, D_MODEL), lambda bi, t: (0, 0)),
                     pl.BlockSpec((1, D_MODEL), lambda bi, t: (0, 0))]
        args += list(finish)
        out_spec, out_sds = full, jax.ShapeDtypeStruct((b, l, D_MODEL), F32)
    return pl.pallas_call(
        functools.partial(_rwkv_kernel, tr=tr, project=finish is not None),
        grid=(b, l // tr),
        in_specs=in_specs,
        out_specs=[out_spec, st_out],
        out_shape=[out_sds, jax.ShapeDtypeStruct((b, N_PAIRS, CHUNK, PAIR), F32)],
        scratch_shapes=[pltpu.VMEM((N_PAIRS, CHUNK, PAIR), F32), pltpu.VMEM((tr, D_BRANCH), F32)],
        compiler_params=_params(("parallel", "arbitrary")),
        name="rwkv",
    )(*args)


def _tiles(l):
    assert l % 512 == 0
    return 512, 512, 512


def kernel(x, meta, norm_w, w_in, b_f, mu_shift, w0, w_up, a0, a_up, k_k, k_a, r_k, gn_w, gn_b,
           w_out, final_norm_w):
    b, l, d = x.shape
    assert d == D_MODEL and norm_w.shape[0] == 1
    tm, t_attn, t_rwkv = _tiles(l)

    wi = w_in[0].astype(BF16)
    o = 0
    cols = {}
    for name, width in (("q", D_BRANCH), ("k", D_BRANCH), ("v", D_BRANCH), ("fl", N_HEADS), ("zf", D_BRANCH),
                        ("r", D_BRANCH), ("rk", D_BRANCH), ("rv", D_BRANCH), ("wd", RANK), ("ad", RANK),
                        ("zr", D_BRANCH)):
        cols[name] = wi[:, o:o + width]
        o += width
    w_all = jnp.concatenate(
        [cols[n] for n in ("q", "k", "v", "zf", "r", "rk", "rv", "zr", "wd", "ad", "fl")]
        + [jnp.zeros((D_MODEL, LANES - N_HEADS), BF16)], axis=1)

    row = lambda vec: vec.reshape(1, -1).astype(F32)
    mu = mu_shift[0]
    mu_r, mu_k, mu_v = (row(mu[i * D_BRANCH:(i + 1) * D_BRANCH]) for i in range(3))
    mu_wa = row(mu[3 * D_BRANCH:])
    zeros_up = jnp.zeros((RANK, D_BRANCH), F32)
    wup_pad = jnp.concatenate([w_up[0], zeros_up], axis=0)
    aup_pad = jnp.concatenate([zeros_up, a_up[0]], axis=0)
    hid = np.arange(D_BRANCH) // HEAD_DIM
    ones_bd = jnp.asarray(hid[:, None] == hid[None, :], BF16)
    shift_prm = (mu_r, mu_k, mu_v, mu_wa, row(w0[0]), wup_pad.astype(BF16), row(a0[0]), aup_pad.astype(BF16),
                 row(k_k[0]), row(k_a[0]), row(r_k[0]), ones_bd)
    gnw, gnb = row(gn_w[0]), row(gn_b[0])
    bf_pad = jnp.concatenate([b_f[0], jnp.zeros((LANES - N_HEADS,), F32)]).reshape(1, LANES)
    pq, pk = _select_matrices()
    nw = row(norm_w[0])
    n_ops = len(RWKV_OPERANDS)

    pre_rows = jnp.concatenate([jnp.zeros((PREFIX_ROWS - N_META, D_MODEL), F32), meta.astype(F32)], axis=0)
    zero_w, zero_n = jnp.zeros((1, D_BRANCH), F32), jnp.zeros((1, LANES), F32)
    pre = _in_proj(pre_rows, nw, w_all, shift_prm, (zero_w, zero_w, zero_w, zero_n),
                   (bf_pad, zero_n, pq, pk), PREFIX_ROWS, 1, PREFIX_ROWS - N_META)
    _, kpre, vpre, _, pzr_ = pre[:5]
    pre_ops, pre_wend = pre[5:5 + n_ops], pre[5 + n_ops]
    last_raw, c_pre = pre[6 + n_ops:10 + n_ops], pre[10 + n_ops]
    lead = lambda a: a[None]
    _, z_pre = _rwkv([lead(a) for a in pre_ops], lead(pzr_), lead(pre_wend), gnw, gnb, ones_bd,
                     jnp.zeros((N_PAIRS, CHUNK, PAIR), F32), PREFIX_ROWS)

    main = _in_proj(x.reshape(b * l, D_MODEL), nw, w_all, shift_prm, last_raw, (bf_pad, c_pre, pq, pk),
                    tm, l // tm, 0)
    qp, kp, vp, zf_, zr_ = main[:5]
    bl = lambda a: a.reshape(b, l, a.shape[-1])
    y_fox = _fox_attn(qp, kp, vp, kpre, vpre, bl(zf_), t_attn)
    out, _ = _rwkv([bl(a) for a in main[5:5 + n_ops]], bl(zr_),
                   main[5 + n_ops].reshape(b, l // CHUNK, D_BRANCH), gnw, gnb, ones_bd, z_pre[0], t_rwkv,
                   finish=(y_fox, x, w_out[0].astype(BF16), row(final_norm_w)))
    return out
```

```python
import functools

import numpy as np
import jax
import jax.numpy as jnp
from jax import lax
from jax.experimental import pallas as pl
from jax.experimental.pallas import tpu as pltpu

F32 = jnp.float32
BF16 = jnp.bfloat16

D_MODEL = 1024
N_META = 16
HEAD_DIM = 64
N_HEADS = 8
D_BRANCH = N_HEADS * HEAD_DIM
N_PAIRS = N_HEADS // 2
RANK = 64
NORM_EPS = 1e-6
GN_EPS = 64e-5
KK_EPS = 1e-12
NEG = -1e30

LANES = 128
PREFIX_ROWS = 128
CHUNK = 64
SUB = 2 * CHUNK
PAIR = 2 * HEAD_DIM
VMEM_LIMIT = 56 * 1024 * 1024

N_WIDE = 8
COL_WA = N_WIDE * D_BRANCH
COL_FL = COL_WA + LANES
N_COLS = COL_FL + LANES

N_CPARTS = 3
ONES_LANE = (HEAD_DIM, 0)
ATTN_HEADS = 4


def _dot(a, b):
    return jnp.dot(a, b, preferred_element_type=F32)


def _pieces(x, n):
    out = []
    for _ in range(n - 1):
        p = x.astype(BF16)
        out.append(p)
        x = x - p.astype(F32)
    out.append(x.astype(BF16))
    return out


def _dot_exact_lhs(a_bf16, x, n):
    acc = None
    for p in _pieces(x, n):
        t = _dot(a_bf16, p)
        acc = t if acc is None else acc + t
    return acc


def _dot_x2(x, w_bf16):
    x_hi, x_lo = _pieces(x, 2)
    return _dot(x_hi, w_bf16) + _dot(x_lo, w_bf16)


def _dot_nt(a, b):
    return lax.dot_general(a, b, (((1,), (1,)), ((), ())), preferred_element_type=F32)


def _dot_tn(a, b):
    return lax.dot_general(a, b, (((0,), (0,)), ((), ())), preferred_element_type=F32)


def _softplus(x):
    return jnp.maximum(x, 0.0) + jnp.log(1.0 + jnp.exp(-jnp.abs(x)))


def _sigmoid(x):
    return 1.0 / (1.0 + jnp.exp(-x))


def _params(sem):
    return pltpu.CompilerParams(dimension_semantics=sem, vmem_limit_bytes=VMEM_LIMIT)


G_Q, G_K, G_V, G_ZF, G_R, G_RK, G_RV, G_ZR = range(N_WIDE)
RWKV_OPERANDS = ("rt", "at", "bt", "kt", "xv", "bh", "kh", "bonus")


def _in_proj_kernel(x_ref, nw_ref, w_ref,
                    mu_r_ref, mu_k_ref, mu_v_ref, mu_wa_ref, w0_ref, wup_ref,
                    a0_ref, aup_ref, kk_ref, ka_ref, rk_ref, ones_ref,
                    pr_ref, pk_ref, pv_ref, pwa_ref, bf_ref, c0_ref, selq_ref, selk_ref,
                    qo_ref, ko_ref, vo_ref, zf_ref, zr_ref,
                    rt_ref, at_ref, bt_ref, kt_ref, xv_ref, bh_ref, kh_ref, bonus_ref, wend_ref,
                    lr_ref, lk_ref, lv_ref, lwa_ref, cl_ref,
                    sr_ref, sk_ref, sv_ref, swa_ref, sc_ref, *, tm, tiles_per_seq, n_pad):
    i = pl.program_id(0)
    seq_tile = lax.rem(i, tiles_per_seq)

    @pl.when(seq_tile == 0)
    def _():
        sr_ref[...] = pr_ref[...]
        sk_ref[...] = pk_ref[...]
        sv_ref[...] = pv_ref[...]
        swa_ref[...] = pwa_ref[...]
        sc_ref[...] = c0_ref[...]

    x = x_ref[...]
    u = x * lax.rsqrt(jnp.mean(x * x, axis=-1, keepdims=True) + NORM_EPS) * nw_ref[...]
    ub = u.astype(BF16)

    def group(g):
        return _dot(ub, w_ref[:, g * D_BRANCH:(g + 1) * D_BRANCH])

    raw = [group(G_R), group(G_RK), group(G_RV), _dot(ub, w_ref[:, COL_WA:COL_WA + LANES])]
    ones_bd = ones_ref[...]

    def seg_sum(v):
        return _dot(v.astype(BF16), ones_bd)

    def shifted(cur, prev_row, mu_ref):
        first = lax.broadcasted_iota(jnp.int32, cur.shape, 0) == 0
        prev = jnp.where(first, prev_row, pltpu.roll(cur, 1, 0))
        return cur + mu_ref[...] * (prev - cur)

    r2 = lax.broadcasted_iota(jnp.int32, (SUB, SUB), 0)
    c2 = lax.broadcasted_iota(jnp.int32, (SUB, SUB), 1)
    tri = jnp.where((r2 >= c2) & (r2 // CHUNK == c2 // CHUNK), 1.0, 0.0).astype(BF16)
    outs = dict(zip(RWKV_OPERANDS, (rt_ref, at_ref, bt_ref, kt_ref, xv_ref, bh_ref, kh_ref, bonus_ref)))

    def rwkv_prepare():
        prev_rows = [sr_ref[...], sk_ref[...], sv_ref[...], swa_ref[...]]
        mus = (mu_r_ref, mu_k_ref, mu_v_ref, mu_wa_ref)
        xr, xk, xv, xwa = [shifted(c, p, mu) for c, p, mu in zip(raw, prev_rows, mus)]
        w_lin = w0_ref[...] + _dot_x2(jnp.tanh(xwa), wup_ref[...])
        a_lin = a0_ref[...] + _dot(xwa.astype(BF16), aup_ref[...])
        kk = xk * kk_ref[...]
        kk_ss = seg_sum(kk * kk)
        yield [c[tm - 1:tm, :] for c in raw]
        w = -_softplus(-w_lin) - 0.5
        ld = -jnp.exp(w)
        a = _sigmoid(a_lin)
        kk = kk * lax.rsqrt(kk_ss + KK_EPS)
        kmod = xk * (1.0 + (a - 1.0) * ka_ref[...])
        lw = jnp.concatenate([_dot_exact_lhs(tri, ld[sb * SUB:(sb + 1) * SUB, :], 2)
                              for sb in range(tm // SUB)], axis=0)
        rk_sum = seg_sum(xr * kmod * rk_ref[...])
        yield None
        w_inv = jnp.exp(-lw)
        bt = kk * a * w_inv
        kt = kmod * w_inv
        for name, val in (("rt", xr * jnp.exp(lw)), ("at", -kk * jnp.exp(lw - ld)), ("bt", bt), ("kt", kt),
                          ("xv", xv), ("bonus", rk_sum * xv)):
            outs[name][...] = val.astype(BF16)
        for c in range(tm // CHUNK):
            rows = slice(c * CHUNK, (c + 1) * CHUNK)
            w_c = jnp.exp(lw[(c + 1) * CHUNK - 1:(c + 1) * CHUNK, :])
            wend_ref[c:c + 1, :] = w_c
            bh_ref[rows, :] = (bt[rows, :] * w_c).astype(BF16)
            kh_ref[rows, :] = (kt[rows, :] * w_c).astype(BF16)
        yield None

    parts = rwkv_prepare()
    last_rows = next(parts)
    q = group(G_Q).astype(BF16)
    k = group(G_K).astype(BF16)
    next(parts)
    v = group(G_V).astype(BF16)
    zf_ref[...] = group(G_ZF).astype(BF16)
    next(parts)
    zr_ref[...] = group(G_ZR).astype(BF16)
    fl = _dot(ub, w_ref[:, COL_FL:COL_FL + LANES])
    bias_q, bias_k, c_last = _fox_bias(fl, bf_ref[...], sc_ref[...], selq_ref[...], selk_ref[...],
                                       seq_tile * tm, n_pad)
    sc_ref[...] = c_last
    _fox_write(q, k, v, bias_q, bias_k, qo_ref, ko_ref, vo_ref)

    for ref, rowv in zip((sr_ref, sk_ref, sv_ref, swa_ref), last_rows):
        ref[...] = rowv

    @pl.when(i == pl.num_programs(0) - 1)
    def _():
        for out, rowv in zip((lr_ref, lk_ref, lv_ref, lwa_ref), last_rows):
            out[...] = rowv
        cl_ref[...] = c_last


def _in_proj(rows, norm_w, w_all, prm, prev, fox_prm, tm, tiles_per_seq, n_pad):
    n = rows.shape[0]
    n_seq = n // (tm * tiles_per_seq)
    tile = lambda width: pl.BlockSpec((tm, width), lambda i: (i, 0))
    const = lambda shape: pl.BlockSpec(shape, lambda i: (0,) * len(shape))
    row_w, row_n, up = const((1, D_BRANCH)), const((1, LANES)), const((LANES, D_BRANCH))
    wide = lambda dt: jax.ShapeDtypeStruct((n, D_BRANCH), dt)
    per_head = pl.BlockSpec((1, N_HEADS, tm, LANES), lambda i: (i // tiles_per_seq, 0, i % tiles_per_seq, 0))
    per_head_t = pl.BlockSpec((1, N_HEADS, 1, PAIR, tm),
                              lambda i: (i // tiles_per_seq, 0, i % tiles_per_seq, 0, 0))
    return pl.pallas_call(
        functools.partial(_in_proj_kernel, tm=tm, tiles_per_seq=tiles_per_seq, n_pad=n_pad),
        grid=(n // tm,),
        in_specs=[tile(D_MODEL), const((1, D_MODEL)),
                  pl.BlockSpec((D_MODEL, N_COLS), lambda i: (0, 0), pipeline_mode=pl.Buffered(1)),
                  row_w, row_w, row_w, row_n, row_w, up, row_w, up,
                  row_w, row_w, row_w, const((D_BRANCH, D_BRANCH)),
                  row_w, row_w, row_w, row_n, row_n, row_n, up, up],
        out_specs=[per_head, per_head, per_head_t, tile(D_BRANCH), tile(D_BRANCH)]
                  + [tile(D_BRANCH)] * len(RWKV_OPERANDS)
                  + [pl.BlockSpec((tm // CHUNK, D_BRANCH), lambda i: (i, 0)),
                     row_w, row_w, row_w, row_n, row_n],
        out_shape=[jax.ShapeDtypeStruct((n_seq, N_HEADS, tiles_per_seq * tm, LANES), BF16)] * 2
                  + [jax.ShapeDtypeStruct((n_seq, N_HEADS, tiles_per_seq, PAIR, tm), BF16),
                     wide(BF16), wide(BF16)]
                  + [wide(BF16)] * len(RWKV_OPERANDS)
                  + [jax.ShapeDtypeStruct((n // CHUNK, D_BRANCH), F32)]
                  + [jax.ShapeDtypeStruct((1, D_BRANCH), F32)] * 3 + [jax.ShapeDtypeStruct((1, LANES), F32)] * 2,
        scratch_shapes=[pltpu.VMEM((1, D_BRANCH), F32), pltpu.VMEM((1, D_BRANCH), F32),
                        pltpu.VMEM((1, D_BRANCH), F32), pltpu.VMEM((1, LANES), F32),
                        pltpu.VMEM((1, LANES), F32)],
        compiler_params=_params(("arbitrary",)),
        name="in_proj",
    )(rows, norm_w, w_all, *prm, *prev, *fox_prm)


def _select_matrices():
    pq = np.zeros((LANES, D_BRANCH), np.float32)
    pk = np.zeros((LANES, D_BRANCH), np.float32)
    for h in range(N_HEADS):
        base = (h // 2) * PAIR + (HEAD_DIM if h % 2 == 0 else 0)
        for part in range(N_CPARTS):
            pq[part * N_HEADS + h, base + part] = 1.0
            pq[N_CPARTS * N_HEADS + h, base + N_CPARTS + part] = 1.0
            pk[N_CPARTS * N_HEADS + h, base + part] = 1.0
            pk[part * N_HEADS + h, base + N_CPARTS + part] = -1.0
    return jnp.asarray(pq, BF16), jnp.asarray(pk, BF16)


def _fox_bias(fl, bf, carry, pq, pk, row_base, n_pad):
    tc = fl.shape[0]
    lane = lax.broadcasted_iota(jnp.int32, (tc, LANES), 1)
    row = lax.broadcasted_iota(jnp.int32, (tc, LANES), 0) + row_base
    r2 = lax.broadcasted_iota(jnp.int32, (tc, tc), 0)
    c2 = lax.broadcasted_iota(jnp.int32, (tc, tc), 1)
    tri = jnp.where(r2 >= c2, 1.0, 0.0).astype(BF16)
    ones = jnp.where((lane >= N_CPARTS * N_HEADS) & (lane < (N_CPARTS + 1) * N_HEADS), 1.0, 0.0)

    x = fl + bf
    logf = jnp.minimum(x, 0.0) - jnp.log(1.0 + jnp.exp(-jnp.abs(x)))
    valid = lane < N_HEADS
    if n_pad:
        valid = valid & (row >= n_pad)
    logf = jnp.where(valid, logf, 0.0)
    cum = _dot_exact_lhs(tri, logf, 2) + carry

    p1 = cum.astype(BF16).astype(F32)
    rem = cum - p1
    p2 = rem.astype(BF16).astype(F32)
    p3 = (rem - p2).astype(BF16).astype(F32)
    cbits = p1 + pltpu.roll(p2, N_HEADS, 1) + pltpu.roll(p3, 2 * N_HEADS, 1) + ones
    cbits_k = cbits
    if n_pad:
        cbits_k = jnp.where((row < n_pad) & (lane < N_HEADS), -NEG, cbits)
    bias_q = _dot(cbits.astype(BF16), pq).astype(BF16)
    bias_k = _dot(cbits_k.astype(BF16), pk).astype(BF16)
    return bias_q, bias_k, cum[tc - 1:tc, :]


def _fox_write(q, k, v, bias_q, bias_k, qo_ref, ko_ref, vo_ref):
    tc = q.shape[0]
    lane = lax.broadcasted_iota(jnp.int32, (tc, LANES), 1)
    low = lane < HEAD_DIM
    sel_r = lax.broadcasted_iota(jnp.int32, (PAIR, PAIR), 0)
    sel_c = lax.broadcasted_iota(jnp.int32, (PAIR, PAIR), 1)
    eye = jnp.where(sel_r == sel_c, 1.0, 0.0).astype(BF16)
    vrow = lax.broadcasted_iota(jnp.int32, (PAIR, tc), 0)
    for pr in range(N_PAIRS):
        sl = slice(pr * PAIR, (pr + 1) * PAIR)
        q_sc = q[:, sl] * jnp.asarray(0.125, BF16)
        v_t = _dot_nt(eye, v[:, sl])
        for half in range(2):
            h = 2 * pr + half
            own = low if half == 0 else jnp.logical_not(low)
            qo_ref[0, h] = jnp.where(own, q_sc, bias_q[:, sl])
            ko_ref[0, h] = jnp.where(own, k[:, sl], bias_k[:, sl])
            own_v = (vrow < HEAD_DIM) if half == 0 else (vrow >= HEAD_DIM)
            vo_ref[0, h, 0] = jnp.where(vrow == ONES_LANE[half], 1.0, jnp.where(own_v, v_t, 0.0)).astype(BF16)


def _attn_kernel(q_ref, kpre_ref, vpre_ref, k_ref, v_ref, z_ref, o_ref, s_ref, *, tq):
    qi = pl.program_id(2)
    heads = range(ATTN_HEADS)
    qs = [q_ref[0, h] for h in heads]

    def rowmax(s):
        return jnp.max(s, axis=0, keepdims=True)

    def k_chunk(h, idx):
        return k_ref[0, h, pl.ds(pl.multiple_of(idx * tq, tq), tq), :]

    def v_chunk(h, idx):
        return v_ref[0, h, idx]

    def scores(idx):
        return [_dot_nt(k_chunk(h, idx), qs[h]) for h in heads]

    def softmax_pv(carry, s, rmax, idx):
        out = []
        for h in heads:
            m, acc = carry[h]
            m_new = jnp.maximum(m, rmax[h])
            p = jnp.exp(s[h] - m_new).astype(BF16)
            out.append((m_new, jnp.exp(m - m_new) * acc + _dot(v_chunk(h, idx), p)))
        return out

    n_zero = PREFIX_ROWS - N_META
    s_pre = [_dot_nt(kpre_ref[0, h, n_zero:, :], qs[h]) for h in heads]
    key = lax.broadcasted_iota(jnp.int32, (tq, tq), 0)
    qry = lax.broadcasted_iota(jnp.int32, (tq, tq), 1)
    s_dia = [jnp.where(key <= qry, s, NEG) for s in scores(qi)]
    s_nxt = scores(0)
    carry = []
    for h in heads:
        m0 = jnp.maximum(rowmax(s_pre[h]), rowmax(s_dia[h]))
        p_pre = jnp.concatenate([jnp.zeros((n_zero, tq), BF16), jnp.exp(s_pre[h] - m0).astype(BF16)], axis=0)
        p_dia = jnp.exp(s_dia[h] - m0).astype(BF16)
        carry.append((m0, _dot(vpre_ref[0, h, 0], p_pre) + _dot(v_chunk(h, qi), p_dia)))
        s_ref[h] = s_nxt[h]
    rmax = [rowmax(s) for s in s_nxt]

    def step2(t, state):
        carry, rmax = state
        k0 = 2 * t
        s_a = [s_ref[h] for h in heads]
        s_b = scores(k0 + 1)
        carry = softmax_pv(carry, s_a, rmax, k0)
        rmax_b = [rowmax(s) for s in s_b]
        s_c = scores(k0 + 2)
        carry = softmax_pv(carry, s_b, rmax_b, k0 + 1)
        for h in heads:
            s_ref[h] = s_c[h]
        return carry, [rowmax(s) for s in s_c]

    def step1(k0, state):
        carry, rmax = state
        s_a = [s_ref[h] for h in heads]
        s_b = scores(k0 + 1)
        carry = softmax_pv(carry, s_a, rmax, k0)
        for h in heads:
            s_ref[h] = s_b[h]
        return carry, [rowmax(s) for s in s_b]

    steps = jnp.maximum(qi - 1, 0)
    pairs = steps // 2
    state = lax.fori_loop(0, pairs, step2, (carry, rmax))
    carry, rmax = lax.fori_loop(2 * pairs, steps, step1, state)

    some = qi > 0
    last = jnp.maximum(qi - 1, 0)
    s_fin = [jnp.where(some, s_ref[h], NEG) for h in heads]
    r_fin = [jnp.where(some, r, NEG) for r in rmax]
    accs = [acc for _, acc in softmax_pv(carry, s_fin, r_fin, last)]
    vrow = lax.broadcasted_iota(jnp.int32, (PAIR, tq), 0)
    for pr in range(ATTN_HEADS // 2):
        acc0, acc1 = accs[2 * pr], accs[2 * pr + 1]
        l0 = acc0[ONES_LANE[0]:ONES_LANE[0] + 1, :]
        l1 = acc1[ONES_LANE[1]:ONES_LANE[1] + 1, :]
        o_t = jnp.where(vrow < HEAD_DIM, acc0 / l0, acc1 / l1)
        z = z_ref[0, :, pr * PAIR:(pr + 1) * PAIR].astype(F32)
        o_ref[0, :, pr * PAIR:(pr + 1) * PAIR] = (o_t.T * (z * _sigmoid(z))).astype(o_ref.dtype)


def _fox_attn(qp, kp, vp, kpre, vpre, zf, tq):
    b, _, l, _ = qp.shape
    g = ATTN_HEADS
    out_block = pl.BlockSpec((1, tq, g * HEAD_DIM), lambda bi, p, qi: (bi, qi, p))
    assert vp.shape[-1] == tq and vpre.shape[-1] == PREFIX_ROWS
    pre = pl.BlockSpec((1, g, PREFIX_ROWS, LANES), lambda bi, p, qi: (0, p, 0, 0))
    pre_v = pl.BlockSpec((1, g, 1, PAIR, PREFIX_ROWS), lambda bi, p, qi: (0, p, 0, 0, 0))
    full = pl.BlockSpec((1, g, l, LANES), lambda bi, p, qi: (bi, p, 0, 0))
    full_v = pl.BlockSpec((1, g, l // tq, PAIR, tq), lambda bi, p, qi: (bi, p, 0, 0, 0))
    return pl.pallas_call(
        functools.partial(_attn_kernel, tq=tq),
        grid=(b, N_HEADS // g, l // tq),
        in_specs=[pl.BlockSpec((1, g, tq, LANES), lambda bi, p, qi: (bi, p, qi, 0)),
                  pre, pre_v, full, full_v, out_block],
        out_specs=out_block,
        out_shape=jax.ShapeDtypeStruct((b, l, D_BRANCH), BF16),
        scratch_shapes=[pltpu.VMEM((g, tq, tq), F32)],
        compiler_params=_params(("parallel", "parallel", "parallel")),
        name="fox_attn",
    )(qp, kpre, vpre, kp, vp, zf)


def _rwkv_kernel(rt_ref, at_ref, bt_ref, kt_ref, xv_ref, bh_ref, kh_ref, bonus_ref, z_ref, wend_ref,
                 gnw_ref, gnb_ref, ones_ref, z0_ref, *rest, tr, project):
    if project:
        yf_ref, x_ref, wout_ref, fnw_ref, out_ref, zf_ref, state_ref, yacc_ref = rest
    else:
        out_ref, zf_ref, state_ref, yacc_ref = rest
    t = pl.program_id(1)
    last = t == pl.num_programs(1) - 1

    @pl.when(t == 0)
    def _():
        state_ref[...] = z0_ref[...]

    ones_bd = ones_ref[...]

    def seg_sum(x):
        return _dot(x.astype(BF16), ones_bd)

    operand = dict(rt=rt_ref, at=at_ref, bt=bt_ref, kt=kt_ref, xv=xv_ref, bh=bh_ref, kh=kh_ref)

    col = lax.broadcasted_iota(jnp.int32, (CHUNK, PAIR), 1)
    trow = lax.broadcasted_iota(jnp.int32, (CHUNK, PAIR), 0)
    tcol = col % CHUNK
    head_a = col < HEAD_DIM
    strict = trow > tcol
    incl = trow >= tcol
    eye_sbs = jnp.where(trow == tcol, 1.0, 0.0).astype(F32)
    sq_r = lax.broadcasted_iota(jnp.int32, (PAIR, PAIR), 0)
    sq_c = lax.broadcasted_iota(jnp.int32, (PAIR, PAIR), 1)
    same_head = (sq_r // HEAD_DIM) == (sq_c // HEAD_DIM)
    eye_sq = sq_r == sq_c

    def stack(x):
        zero = jnp.zeros_like(x)
        return jnp.concatenate([jnp.where(head_a, x, zero), jnp.where(head_a, zero, x)], axis=0)

    def fold(sq):
        sq = jnp.where(same_head, sq, 0.0)
        return sq[:CHUNK] + sq[CHUNK:]

    def tile(name, c, p):
        return operand[name][0, c * CHUNK:(c + 1) * CHUNK, p * PAIR:(p + 1) * PAIR]

    n_chunks = tr // CHUNK
    units =[(c, p) for c in range(n_chunks) for p in range(N_PAIRS)]
    nu = range(len(units))
    wc = [wend_ref[0, c:c + 1, p * PAIR:(p + 1) * PAIR] for c, p in units]
    rt, at, bt, kt, xv, bh, kh = ([tile(name, c, p) for c, p in units]
                                  for name in ("rt", "at", "bt", "kt", "xv", "bh", "kh"))
    bt_s = [stack(x) for x in bt]
    kt_s = [stack(x) for x in kt]
    xv_s = [stack(x) for x in xv]

    ar = [jnp.concatenate([at[u], rt[u]], axis=0) for u in nu]
    x_b = [_dot_nt(ar[u], bt_s[u]) for u in nu]
    x_k = [_dot_nt(ar[u], kt_s[u]) for u in nu]
    a_ab = [jnp.where(strict, x[:CHUNK], 0.0) for x in x_b]
    a_rb = [jnp.where(incl, x[CHUNK:], 0.0).astype(BF16) for x in x_b]
    a_ak = [jnp.where(strict, x[:CHUNK], 0.0).astype(BF16) for x in x_k]
    a_rk = [jnp.where(incl, x[CHUNK:], 0.0).astype(BF16) for x in x_k]

    ab = [a.astype(BF16) for a in a_ab]
    pw = [_dot(ab[u], stack(ab[u])) for u in nu]
    tinv = [eye_sbs + a_ab[u] for u in nu]
    for level in range(1, 6):
        pb = [x.astype(BF16) for x in pw]
        if level < 5:
            both = [_dot(jnp.concatenate([tinv[u].astype(BF16), pb[u]], axis=0), stack(pb[u])) for u in nu]
            tinv = [tinv[u] + both[u][:CHUNK] for u in nu]
            pw = [both[u][CHUNK:] for u in nu]
        else:
            tinv = [tinv[u] + _dot(tinv[u].astype(BF16), stack(pb[u])) for u in nu]
    tb = [x.astype(BF16) for x in tinv]

    ap = [_dot(tb[u], stack(at[u])).astype(BF16) for u in nu]
    akv = [_dot(a_ak[u], xv_s[u]).astype(BF16) for u in nu]
    vp = [_dot(tb[u], stack(akv[u])).astype(BF16) for u in nu]
    ap_s = [stack(x) for x in ap]
    vp_s = [stack(x) for x in vp]
    m_sbs = [fold(jnp.where(eye_sq, wc[u], 0.0) + _dot_tn(bh[u], ap[u])).astype(BF16) for u in nu]
    g_sbs = [fold(_dot_tn(jnp.concatenate([bh[u], kh[u]], axis=0), jnp.concatenate([vp[u], xv[u]], axis=0)))
             for u in nu]
    rp = [(rt[u].astype(F32) + _dot(a_rb[u], ap_s[u])).astype(BF16) for u in nu]
    y0 = [_dot(jnp.concatenate([a_rb[u], a_rk[u]], axis=1), jnp.concatenate([vp_s[u], xv_s[u]], axis=0))
          for u in nu]

    for u, (c, p) in enumerate(units):
        z_bd = stack(state_ref[p].astype(BF16))
        both = _dot(jnp.concatenate([rp[u], m_sbs[u]], axis=0), z_bd)
        yacc_ref[c * CHUNK:(c + 1) * CHUNK, p * PAIR:(p + 1) * PAIR] = both[:CHUNK] + y0[u]
        state_ref[p] = both[CHUNK:] + g_sbs[u]

    @pl.when(last)
    def _():
        zf_ref[0] = state_ref[...]

    inv_n = 1.0 / HEAD_DIM
    y = yacc_ref[...]
    mean = seg_sum(y) * inv_n
    d = y - mean
    var = seg_sum(d * d) * inv_n
    yn = d * lax.rsqrt(var + GN_EPS) * gnw_ref[...] + gnb_ref[...]
    z = z_ref[0].astype(F32)
    y_rwkv = ((yn + bonus_ref[0].astype(F32)) * (z * _sigmoid(z))).astype(BF16)
    if project:
        mix = _dot(yf_ref[0], wout_ref[:D_BRANCH, :]) + _dot(y_rwkv, wout_ref[D_BRANCH:, :])
        h = x_ref[0] + mix
        out_ref[0] = h * lax.rsqrt(jnp.mean(h * h, axis=-1, keepdims=True) + NORM_EPS) * fnw_ref[...]
    else:
        out_ref[0] = y_rwkv


def _rwkv(ops, z, wend, gn_w, gn_b, ones_bd, z0, tr, finish=None):
    b, l, _ = z.shape
    wide = pl.BlockSpec((1, tr, D_BRANCH), lambda bi, t: (bi, t, 0))
    row_w = pl.BlockSpec((1, D_BRANCH), lambda bi, t: (0, 0))
    ones = pl.BlockSpec((D_BRANCH, D_BRANCH), lambda bi, t: (0, 0))
    st_in = pl.BlockSpec((N_PAIRS, CHUNK, PAIR), lambda bi, t: (0, 0, 0))
    st_out = pl.BlockSpec((1, N_PAIRS, CHUNK, PAIR), lambda bi, t: (bi, 0, 0, 0))
    in_specs = ([wide] * (len(RWKV_OPERANDS) + 1)
                + [pl.BlockSpec((1, tr // CHUNK, D_BRANCH), lambda bi, t: (bi, t, 0)), row_w, row_w, ones, st_in])
    args = [*ops, z, wend, gn_w, gn_b, ones_bd, z0]
    if finish is None:
        out_spec, out_sds = wide, jax.ShapeDtypeStruct((b, l, D_BRANCH), BF16)
    else:
        full = pl.BlockSpec((1, tr, D_MODEL), lambda bi, t: (bi, t, 0))
        in_specs += [wide, full, pl.BlockSpec((D_MODEL, D_MODEL), lambda bi, t: (0, 0)),
                     pl.BlockSpec((1, D_MODEL), lambda bi, t: (0, 0))]
        args += list(finish)
        out_spec, out_sds = full, jax.ShapeDtypeStruct((b, l, D_MODEL), F32)
    return pl.pallas_call(
        functools.partial(_rwkv_kernel, tr=tr, project=finish is not None),
        grid=(b, l // tr),
        in_specs=in_specs,
        out_specs=[out_spec, st_out],
        out_shape=[out_sds, jax.ShapeDtypeStruct((b, N_PAIRS, CHUNK, PAIR), F32)],
        scratch_shapes=[pltpu.VMEM((N_PAIRS, CHUNK, PAIR), F32), pltpu.VMEM((tr, D_BRANCH), F32)],
        compiler_params=_params(("parallel", "arbitrary")),
        name="rwkv",
    )(*args)


_SRC_COLS = (("q", D_BRANCH), ("k", D_BRANCH), ("v", D_BRANCH), ("fl", N_HEADS), ("zf", D_BRANCH),
             ("r", D_BRANCH), ("rk", D_BRANCH), ("rv", D_BRANCH), ("wd", RANK), ("ad", RANK), ("zr", D_BRANCH))
_DST_ORDER = ("q", "k", "v", "zf", "r", "rk", "rv", "zr", "wd", "ad", "fl")


def _repack_kernel(w_ref, o_ref):
    src, off = {}, 0
    for name, width in _SRC_COLS:
        src[name] = (off, width)
        off += width
    o_ref[:, COL_FL:] = jnp.zeros((o_ref.shape[0], LANES), BF16)
    dst = 0
    for name in _DST_ORDER:
        s, width = src[name]
        o_ref[:, dst:dst + width] = w_ref[:, s:s + width].astype(BF16)
        dst += width


def _repack_weights(w):
    rows = 256
    return pl.pallas_call(
        _repack_kernel,
        grid=(D_MODEL // rows,),
        in_specs=[pl.BlockSpec((rows, w.shape[1]), lambda i: (i, 0))],
        out_specs=pl.BlockSpec((rows, N_COLS), lambda i: (i, 0)),
        out_shape=jax.ShapeDtypeStruct((D_MODEL, N_COLS), BF16),
        compiler_params=_params(("parallel",)),
        name="repack_weights",
    )(w)


def _tiles(l):
    assert l % 512 == 0
    return 512, 512, 512


def kernel(x, meta, norm_w, w_in, b_f, mu_shift, w0, w_up, a0, a_up, k_k, k_a, r_k, gn_w, gn_b,
           w_out, final_norm_w):
    b, l, d = x.shape
    assert d == D_MODEL and norm_w.shape[0] == 1
    tm, t_attn, t_rwkv = _tiles(l)

    w_all = _repack_weights(w_in[0])

    row = lambda vec: vec.reshape(1, -1).astype(F32)
    mu = mu_shift[0]
    mu_r, mu_k, mu_v = (row(mu[i * D_BRANCH:(i + 1) * D_BRANCH]) for i in range(3))
    mu_wa = row(mu[3 * D_BRANCH:])
    zeros_up = jnp.zeros((RANK, D_BRANCH), F32)
    wup_pad = jnp.concatenate([w_up[0], zeros_up], axis=0)
    aup_pad = jnp.concatenate([zeros_up, a_up[0]], axis=0)
    hid = np.arange(D_BRANCH) // HEAD_DIM
    ones_bd = jnp.asarray(hid[:, None] == hid[None, :], BF16)
    shift_prm = (mu_r, mu_k, mu_v, mu_wa, row(w0[0]), wup_pad.astype(BF16), row(a0[0]), aup_pad.astype(BF16),
                 row(k_k[0]), row(k_a[0]), row(r_k[0]), ones_bd)
    gnw, gnb = row(gn_w[0]), row(gn_b[0])
    bf_pad = jnp.concatenate([b_f[0], jnp.zeros((LANES - N_HEADS,), F32)]).reshape(1, LANES)
    pq, pk = _select_matrices()
    nw = row(norm_w[0])
    n_ops = len(RWKV_OPERANDS)

    pre_rows = jnp.concatenate([jnp.zeros((PREFIX_ROWS - N_META, D_MODEL), F32), meta.astype(F32)], axis=0)
    zero_w, zero_n = jnp.zeros((1, D_BRANCH), F32), jnp.zeros((1, LANES), F32)
    pre = _in_proj(pre_rows, nw, w_all, shift_prm, (zero_w, zero_w, zero_w, zero_n),
                   (bf_pad, zero_n, pq, pk), PREFIX_ROWS, 1, PREFIX_ROWS - N_META)
    _, kpre, vpre, _, pzr_ = pre[:5]
    pre_ops, pre_wend = pre[5:5 + n_ops], pre[5 + n_ops]
    last_raw, c_pre = pre[6 + n_ops:10 + n_ops], pre[10 + n_ops]
    lead = lambda a: a[None]
    _, z_pre = _rwkv([lead(a) for a in pre_ops], lead(pzr_), lead(pre_wend), gnw, gnb, ones_bd,
                     jnp.zeros((N_PAIRS, CHUNK, PAIR), F32), PREFIX_ROWS)

    main = _in_proj(x.reshape(b * l, D_MODEL), nw, w_all, shift_prm, last_raw, (bf_pad, c_pre, pq, pk),
                    tm, l // tm, 0)
    qp, kp, vp, zf_, zr_ = main[:5]
    bl = lambda a: a.reshape(b, l, a.shape[-1])
    y_fox = _fox_attn(qp, kp, vp, kpre, vpre, bl(zf_), t_attn)
    out, _ = _rwkv([bl(a) for a in main[5:5 + n_ops]], bl(zr_),
                   main[5 + n_ops].reshape(b, l // CHUNK, D_BRANCH), gnw, gnb, ones_bd, z_pre[0], t_rwkv,
                   finish=(y_fox, x, w_out[0].astype(BF16), row(final_norm_w)))
    return out
```

```python
import functools

import numpy as np
import jax
import jax.numpy as jnp
from jax import lax
from jax.experimental import pallas as pl
from jax.experimental.pallas import tpu as pltpu

F32 = jnp.float32
BF16 = jnp.bfloat16

D_MODEL = 1024
N_META = 16
HEAD_DIM = 64
N_HEADS = 8
D_BRANCH = N_HEADS * HEAD_DIM
N_PAIRS = N_HEADS // 2
RANK = 64
NORM_EPS = 1e-6
GN_EPS = 64e-5
KK_EPS = 1e-12
NEG = -1e30

LANES = 128
PREFIX_ROWS = 128
CHUNK = 64
SUB = 2 * CHUNK
PAIR = 2 * HEAD_DIM
VMEM_LIMIT = 56 * 1024 * 1024

N_WIDE = 8
COL_WA = N_WIDE * D_BRANCH
COL_FL = COL_WA + LANES
N_COLS = COL_FL + LANES

N_CPARTS = 3
ONES_LANE = (HEAD_DIM, 0)
ATTN_HEADS = 4


def _dot(a, b):
    return jnp.dot(a, b, preferred_element_type=F32)


def _pieces(x, n):
    out = []
    for _ in range(n - 1):
        p = x.astype(BF16)
        out.append(p)
        x = x - p.astype(F32)
    out.append(x.astype(BF16))
    return out


def _dot_exact_lhs(a_bf16, x, n):
    acc = None
    for p in _pieces(x, n):
        t = _dot(a_bf16, p)
        acc = t if acc is None else acc + t
    return acc


def _dot_x2(x, w_bf16):
    x_hi, x_lo = _pieces(x, 2)
    return _dot(x_hi, w_bf16) + _dot(x_lo, w_bf16)


def _dot_nt(a, b):
    return lax.dot_general(a, b, (((1,), (1,)), ((), ())), preferred_element_type=F32)


def _dot_tn(a, b):
    return lax.dot_general(a, b, (((0,), (0,)), ((), ())), preferred_element_type=F32)


def _softplus(x):
    return jnp.maximum(x, 0.0) + jnp.log(1.0 + jnp.exp(-jnp.abs(x)))


def _sigmoid(x):
    return 1.0 / (1.0 + jnp.exp(-x))


def _params(sem):
    return pltpu.CompilerParams(dimension_semantics=sem, vmem_limit_bytes=VMEM_LIMIT)


G_Q, G_K, G_V, G_ZF, G_R, G_RK, G_RV, G_ZR = range(N_WIDE)
RWKV_OPERANDS = ("rt", "at", "bt", "kt", "xv", "bh", "kh", "bonus")


def _in_proj_kernel(x_ref, nw_ref, w_ref,
                    mu_r_ref, mu_k_ref, mu_v_ref, mu_wa_ref, w0_ref, wup_ref,
                    a0_ref, aup_ref, kk_ref, ka_ref, rk_ref, ones_ref,
                    pr_ref, pk_ref, pv_ref, pwa_ref, bf_ref, c0_ref, selq_ref, selk_ref,
                    qo_ref, ko_ref, vo_ref, zf_ref, zr_ref,
                    rt_ref, at_ref, bt_ref, kt_ref, xv_ref, bh_ref, kh_ref, bonus_ref, wend_ref,
                    lr_ref, lk_ref, lv_ref, lwa_ref, cl_ref,
                    sr_ref, sk_ref, sv_ref, swa_ref, sc_ref, *, tm, tiles_per_seq, n_pad):
    i = pl.program_id(0)
    seq_tile = lax.rem(i, tiles_per_seq)

    @pl.when(seq_tile == 0)
    def _():
        sr_ref[...] = pr_ref[...]
        sk_ref[...] = pk_ref[...]
        sv_ref[...] = pv_ref[...]
        swa_ref[...] = pwa_ref[...]
        sc_ref[...] = c0_ref[...]

    x = x_ref[...]
    u = x * lax.rsqrt(jnp.mean(x * x, axis=-1, keepdims=True) + NORM_EPS) * nw_ref[...]
    ub = u.astype(BF16)

    def group(g):
        return _dot(ub, w_ref[:, g * D_BRANCH:(g + 1) * D_BRANCH])

    raw = [group(G_R), group(G_RK), group(G_RV), _dot(ub, w_ref[:, COL_WA:COL_WA + LANES])]
    ones_bd = ones_ref[...]

    def seg_sum(v):
        return _dot(v.astype(BF16), ones_bd)

    def shifted(cur, prev_row, mu_ref):
        first = lax.broadcasted_iota(jnp.int32, cur.shape, 0) == 0
        prev = jnp.where(first, prev_row, pltpu.roll(cur, 1, 0))
        return cur + mu_ref[...] * (prev - cur)

    r2 = lax.broadcasted_iota(jnp.int32, (SUB, SUB), 0)
    c2 = lax.broadcasted_iota(jnp.int32, (SUB, SUB), 1)
    tri = jnp.where((r2 >= c2) & (r2 // CHUNK == c2 // CHUNK), 1.0, 0.0).astype(BF16)
    outs = dict(zip(RWKV_OPERANDS, (rt_ref, at_ref, bt_ref, kt_ref, xv_ref, bh_ref, kh_ref, bonus_ref)))

    def rwkv_prepare():
        prev_rows = [sr_ref[...], sk_ref[...], sv_ref[...], swa_ref[...]]
        mus = (mu_r_ref, mu_k_ref, mu_v_ref, mu_wa_ref)
        xr, xk, xv, xwa = [shifted(c, p, mu) for c, p, mu in zip(raw, prev_rows, mus)]
        w_lin = w0_ref[...] + _dot_x2(jnp.tanh(xwa), wup_ref[...])
        a_lin = a0_ref[...] + _dot(xwa.astype(BF16), aup_ref[...])
        kk = xk * kk_ref[...]
        kk_ss = seg_sum(kk * kk)
        yield [c[tm - 1:tm, :] for c in raw]
        w = -_softplus(-w_lin) - 0.5
        ld = -jnp.exp(w)
        a = _sigmoid(a_lin)
        kk = kk * lax.rsqrt(kk_ss + KK_EPS)
        kmod = xk * (1.0 + (a - 1.0) * ka_ref[...])
        lw = jnp.concatenate([_dot_exact_lhs(tri, ld[sb * SUB:(sb + 1) * SUB, :], 2)
                              for sb in range(tm // SUB)], axis=0)
        rk_sum = seg_sum(xr * kmod * rk_ref[...])
        yield None
        w_inv = jnp.exp(-lw)
        bt = kk * a * w_inv
        kt = kmod * w_inv
        for name, val in (("rt", xr * jnp.exp(lw)), ("at", -kk * jnp.exp(lw - ld)), ("bt", bt), ("kt", kt),
                          ("xv", xv), ("bonus", rk_sum * xv)):
            outs[name][...] = val.astype(BF16)
        for c in range(tm // CHUNK):
            rows = slice(c * CHUNK, (c + 1) * CHUNK)
            w_c = jnp.exp(lw[(c + 1) * CHUNK - 1:(c + 1) * CHUNK, :])
            wend_ref[c:c + 1, :] = w_c
            bh_ref[rows, :] = (bt[rows, :] * w_c).astype(BF16)
            kh_ref[rows, :] = (kt[rows, :] * w_c).astype(BF16)
        yield None

    parts = rwkv_prepare()
    last_rows = next(parts)
    q = group(G_Q).astype(BF16)
    k = group(G_K).astype(BF16)
    next(parts)
    v = group(G_V).astype(BF16)
    zf_ref[...] = group(G_ZF).astype(BF16)
    next(parts)
    zr_ref[...] = group(G_ZR).astype(BF16)
    fl = _dot(ub, w_ref[:, COL_FL:COL_FL + LANES])
    bias_q, bias_k, c_last = _fox_bias(fl, bf_ref[...], sc_ref[...], selq_ref[...], selk_ref[...],
                                       seq_tile * tm, n_pad)
    sc_ref[...] = c_last
    _fox_write(q, k, v, bias_q, bias_k, qo_ref, ko_ref, vo_ref)

    for ref, rowv in zip((sr_ref, sk_ref, sv_ref, swa_ref), last_rows):
        ref[...] = rowv

    @pl.when(i == pl.num_programs(0) - 1)
    def _():
        for out, rowv in zip((lr_ref, lk_ref, lv_ref, lwa_ref), last_rows):
            out[...] = rowv
        cl_ref[...] = c_last


def _in_proj(rows, norm_w, w_all, prm, prev, fox_prm, tm, tiles_per_seq, n_pad):
    n = rows.shape[0]
    n_seq = n // (tm * tiles_per_seq)
    tile = lambda width: pl.BlockSpec((tm, width), lambda i: (i, 0))
    const = lambda shape: pl.BlockSpec(shape, lambda i: (0,) * len(shape))
    row_w, row_n, up = const((1, D_BRANCH)), const((1, LANES)), const((LANES, D_BRANCH))
    wide = lambda dt: jax.ShapeDtypeStruct((n, D_BRANCH), dt)
    per_head = pl.BlockSpec((1, N_HEADS, tm, LANES), lambda i: (i // tiles_per_seq, 0, i % tiles_per_seq, 0))
    per_head_t = pl.BlockSpec((1, N_HEADS, 1, PAIR, tm),
                              lambda i: (i // tiles_per_seq, 0, i % tiles_per_seq, 0, 0))
    return pl.pallas_call(
        functools.partial(_in_proj_kernel, tm=tm, tiles_per_seq=tiles_per_seq, n_pad=n_pad),
        grid=(n // tm,),
        in_specs=[tile(D_MODEL), const((1, D_MODEL)),
                  pl.BlockSpec((D_MODEL, N_COLS), lambda i: (0, 0), pipeline_mode=pl.Buffered(1)),
                  row_w, row_w, row_w, row_n, row_w, up, row_w, up,
                  row_w, row_w, row_w, const((D_BRANCH, D_BRANCH)),
                  row_w, row_w, row_w, row_n, row_n, row_n, up, up],
        out_specs=[per_head, per_head, per_head_t, tile(D_BRANCH), tile(D_BRANCH)]
                  + [tile(D_BRANCH)] * len(RWKV_OPERANDS)
                  + [pl.BlockSpec((tm // CHUNK, D_BRANCH), lambda i: (i, 0)),
                     row_w, row_w, row_w, row_n, row_n],
        out_shape=[jax.ShapeDtypeStruct((n_seq, N_HEADS, tiles_per_seq * tm, LANES), BF16)] * 2
                  + [jax.ShapeDtypeStruct((n_seq, N_HEADS, tiles_per_seq, PAIR, tm), BF16),
                     wide(BF16), wide(BF16)]
                  + [wide(BF16)] * len(RWKV_OPERANDS)
                  + [jax.ShapeDtypeStruct((n // CHUNK, D_BRANCH), F32)]
                  + [jax.ShapeDtypeStruct((1, D_BRANCH), F32)] * 3 + [jax.ShapeDtypeStruct((1, LANES), F32)] * 2,
        scratch_shapes=[pltpu.VMEM((1, D_BRANCH), F32), pltpu.VMEM((1, D_BRANCH), F32),
                        pltpu.VMEM((1, D_BRANCH), F32), pltpu.VMEM((1, LANES), F32),
                        pltpu.VMEM((1, LANES), F32)],
        compiler_params=_params(("arbitrary",)),
        name="in_proj",
    )(rows, norm_w, w_all, *prm, *prev, *fox_prm)


def _select_matrices():
    pq = np.zeros((LANES, D_BRANCH), np.float32)
    pk = np.zeros((LANES, D_BRANCH), np.float32)
    for h in range(N_HEADS):
        base = (h // 2) * PAIR + (HEAD_DIM if h % 2 == 0 else 0)
        for part in range(N_CPARTS):
            pq[part * N_HEADS + h, base + part] = 1.0
            pq[N_CPARTS * N_HEADS + h, base + N_CPARTS + part] = 1.0
            pk[N_CPARTS * N_HEADS + h, base + part] = 1.0
            pk[part * N_HEADS + h, base + N_CPARTS + part] = -1.0
    return jnp.asarray(pq, BF16), jnp.asarray(pk, BF16)


def _fox_bias(fl, bf, carry, pq, pk, row_base, n_pad):
    tc = fl.shape[0]
    lane = lax.broadcasted_iota(jnp.int32, (tc, LANES), 1)
    row = lax.broadcasted_iota(jnp.int32, (tc, LANES), 0) + row_base
    r2 = lax.broadcasted_iota(jnp.int32, (tc, tc), 0)
    c2 = lax.broadcasted_iota(jnp.int32, (tc, tc), 1)
    tri = jnp.where(r2 >= c2, 1.0, 0.0).astype(BF16)
    ones = jnp.where((lane >= N_CPARTS * N_HEADS) & (lane < (N_CPARTS + 1) * N_HEADS), 1.0, 0.0)

    x = fl + bf
    logf = jnp.minimum(x, 0.0) - jnp.log(1.0 + jnp.exp(-jnp.abs(x)))
    valid = lane < N_HEADS
    if n_pad:
        valid = valid & (row >= n_pad)
    logf = jnp.where(valid, logf, 0.0)
    cum = _dot_exact_lhs(tri, logf, 2) + carry

    p1 = cum.astype(BF16).astype(F32)
    rem = cum - p1
    p2 = rem.astype(BF16).astype(F32)
    p3 = (rem - p2).astype(BF16).astype(F32)
    cbits = p1 + pltpu.roll(p2, N_HEADS, 1) + pltpu.roll(p3, 2 * N_HEADS, 1) + ones
    cbits_k = cbits
    if n_pad:
        cbits_k = jnp.where((row < n_pad) & (lane < N_HEADS), -NEG, cbits)
    bias_q = _dot(cbits.astype(BF16), pq).astype(BF16)
    bias_k = _dot(cbits_k.astype(BF16), pk).astype(BF16)
    return bias_q, bias_k, cum[tc - 1:tc, :]


def _fox_write(q, k, v, bias_q, bias_k, qo_ref, ko_ref, vo_ref):
    tc = q.shape[0]
    lane = lax.broadcasted_iota(jnp.int32, (tc, LANES), 1)
    low = lane < HEAD_DIM
    sel_r = lax.broadcasted_iota(jnp.int32, (PAIR, PAIR), 0)
    sel_c = lax.broadcasted_iota(jnp.int32, (PAIR, PAIR), 1)
    eye = jnp.where(sel_r == sel_c, 1.0, 0.0).astype(BF16)
    vrow = lax.broadcasted_iota(jnp.int32, (PAIR, tc), 0)
    for pr in range(N_PAIRS):
        sl = slice(pr * PAIR, (pr + 1) * PAIR)
        q_sc = q[:, sl] * jnp.asarray(0.125, BF16)
        v_t = _dot_nt(eye, v[:, sl])
        for half in range(2):
            h = 2 * pr + half
            own = low if half == 0 else jnp.logical_not(low)
            qo_ref[0, h] = jnp.where(own, q_sc, bias_q[:, sl])
            ko_ref[0, h] = jnp.where(own, k[:, sl], bias_k[:, sl])
            own_v = (vrow < HEAD_DIM) if half == 0 else (vrow >= HEAD_DIM)
            vo_ref[0, h, 0] = jnp.where(vrow == ONES_LANE[half], 1.0, jnp.where(own_v, v_t, 0.0)).astype(BF16)


def _attn_kernel(q_ref, kpre_ref, vpre_ref, k_ref, v_ref, z_ref, o_ref, s_ref, *, tq):
    qi = pl.program_id(2)
    heads = range(ATTN_HEADS)
    qs = [q_ref[0, h] for h in heads]

    def rowmax(s):
        return jnp.max(s, axis=0, keepdims=True)

    def k_chunk(h, idx):
        return k_ref[0, h, pl.ds(pl.multiple_of(idx * tq, tq), tq), :]

    def v_chunk(h, idx):
        return v_ref[0, h, idx]

    def scores(idx):
        return [_dot_nt(k_chunk(h, idx), qs[h]) for h in heads]

    def softmax_pv(carry, s, rmax, idx):
        out = []
        for h in heads:
            m, acc = carry[h]
            m_new = jnp.maximum(m, rmax[h])
            p = jnp.exp(s[h] - m_new).astype(BF16)
            out.append((m_new, jnp.exp(m - m_new) * acc + _dot(v_chunk(h, idx), p)))
        return out

    n_zero = PREFIX_ROWS - N_META
    s_pre = [_dot_nt(kpre_ref[0, h, n_zero:, :], qs[h]) for h in heads]
    key = lax.broadcasted_iota(jnp.int32, (tq, tq), 0)
    qry = lax.broadcasted_iota(jnp.int32, (tq, tq), 1)
    s_dia = [jnp.where(key <= qry, s, NEG) for s in scores(qi)]
    s_nxt = scores(0)
    carry = []
    for h in heads:
        m0 = jnp.maximum(rowmax(s_pre[h]), rowmax(s_dia[h]))
        p_pre = jnp.concatenate([jnp.zeros((n_zero, tq), BF16), jnp.exp(s_pre[h] - m0).astype(BF16)], axis=0)
        p_dia = jnp.exp(s_dia[h] - m0).astype(BF16)
        carry.append((m0, _dot(vpre_ref[0, h, 0], p_pre) + _dot(v_chunk(h, qi), p_dia)))
        s_ref[h] = s_nxt[h]
    rmax = [rowmax(s) for s in s_nxt]

    def step2(t, state):
        carry, rmax = state
        k0 = 2 * t
        s_a = [s_ref[h] for h in heads]
        s_b = scores(k0 + 1)
        carry = softmax_pv(carry, s_a, rmax, k0)
        rmax_b = [rowmax(s) for s in s_b]
        s_c = scores(k0 + 2)
        carry = softmax_pv(carry, s_b, rmax_b, k0 + 1)
        for h in heads:
            s_ref[h] = s_c[h]
        return carry, [rowmax(s) for s in s_c]

    def step1(k0, state):
        carry, rmax = state
        s_a = [s_ref[h] for h in heads]
        s_b = scores(k0 + 1)
        carry = softmax_pv(carry, s_a, rmax, k0)
        for h in heads:
            s_ref[h] = s_b[h]
        return carry, [rowmax(s) for s in s_b]

    steps = jnp.maximum(qi - 1, 0)
    pairs = steps // 2
    state = lax.fori_loop(0, pairs, step2, (carry, rmax))
    carry, rmax = lax.fori_loop(2 * pairs, steps, step1, state)

    some = qi > 0
    last = jnp.maximum(qi - 1, 0)
    s_fin = [jnp.where(some, s_ref[h], NEG) for h in heads]
    r_fin = [jnp.where(some, r, NEG) for r in rmax]
    accs = [acc for _, acc in softmax_pv(carry, s_fin, r_fin, last)]
    vrow = lax.broadcasted_iota(jnp.int32, (PAIR, tq), 0)
    for pr in range(ATTN_HEADS // 2):
        acc0, acc1 = accs[2 * pr], accs[2 * pr + 1]
        l0 = acc0[ONES_LANE[0]:ONES_LANE[0] + 1, :]
        l1 = acc1[ONES_LANE[1]:ONES_LANE[1] + 1, :]
        o_t = jnp.where(vrow < HEAD_DIM, acc0 / l0, acc1 / l1)
        z = z_ref[0, :, pr * PAIR:(pr + 1) * PAIR].astype(F32)
        o_ref[0, :, pr * PAIR:(pr + 1) * PAIR] = (o_t.T * (z * _sigmoid(z))).astype(o_ref.dtype)


def _fox_attn(qp, kp, vp, kpre, vpre, zf, tq):
    b, _, l, _ = qp.shape
    g = ATTN_HEADS
    out_block = pl.BlockSpec((1, tq, g * HEAD_DIM), lambda bi, p, qi: (bi, qi, p))
    assert vp.shape[-1] == tq and vpre.shape[-1] == PREFIX_ROWS
    pre = pl.BlockSpec((1, g, PREFIX_ROWS, LANES), lambda bi, p, qi: (0, p, 0, 0))
    pre_v = pl.BlockSpec((1, g, 1, PAIR, PREFIX_ROWS), lambda bi, p, qi: (0, p, 0, 0, 0))
    full = pl.BlockSpec((1, g, l, LANES), lambda bi, p, qi: (bi, p, 0, 0))
    full_v = pl.BlockSpec((1, g, l // tq, PAIR, tq), lambda bi, p, qi: (bi, p, 0, 0, 0))
    return pl.pallas_call(
        functools.partial(_attn_kernel, tq=tq),
        grid=(b, N_HEADS // g, l // tq),
        in_specs=[pl.BlockSpec((1, g, tq, LANES), lambda bi, p, qi: (bi, p, qi, 0)),
                  pre, pre_v, full, full_v, out_block],
        out_specs=out_block,
        out_shape=jax.ShapeDtypeStruct((b, l, D_BRANCH), BF16),
        scratch_shapes=[pltpu.VMEM((g, tq, tq), F32)],
        compiler_params=_params(("parallel", "parallel", "parallel")),
        name="fox_attn",
    )(qp, kpre, vpre, kp, vp, zf)


def _rwkv_kernel(rt_ref, at_ref, bt_ref, kt_ref, xv_ref, bh_ref, kh_ref, bonus_ref, z_ref, wend_ref,
                 gnw_ref, gnb_ref, ones_ref, z0_ref, *rest, tr, project):
    if project:
        yf_ref, x_ref, wout_ref, fnw_ref, out_ref, zf_ref, state_ref, yacc_ref = rest
    else:
        out_ref, zf_ref, state_ref, yacc_ref = rest
    t = pl.program_id(1)
    last = t == pl.num_programs(1) - 1

    @pl.when(t == 0)
    def _():
        state_ref[...] = z0_ref[...]

    ones_bd = ones_ref[...]

    def seg_sum(x):
        return _dot(x.astype(BF16), ones_bd)

    operand = dict(rt=rt_ref, at=at_ref, bt=bt_ref, kt=kt_ref, xv=xv_ref, bh=bh_ref, kh=kh_ref)

    col = lax.broadcasted_iota(jnp.int32, (CHUNK, PAIR), 1)
    trow = lax.broadcasted_iota(jnp.int32, (CHUNK, PAIR), 0)
    tcol = col % CHUNK
    head_a = col < HEAD_DIM
    strict = trow > tcol
    incl = trow >= tcol
    eye_sbs = jnp.where(trow == tcol, 1.0, 0.0).astype(F32)
    sq_r = lax.broadcasted_iota(jnp.int32, (PAIR, PAIR), 0)
    sq_c = lax.broadcasted_iota(jnp.int32, (PAIR, PAIR), 1)
    same_head = (sq_r // HEAD_DIM) == (sq_c // HEAD_DIM)
    eye_sq = sq_r == sq_c

    def stack(x):
        zero = jnp.zeros_like(x)
        return jnp.concatenate([jnp.where(head_a, x, zero), jnp.where(head_a, zero, x)], axis=0)

    def fold(sq):
        sq = jnp.where(same_head, sq, 0.0)
        return sq[:CHUNK] + sq[CHUNK:]

    def tile(name, c, p):
        return operand[name][0, c * CHUNK:(c + 1) * CHUNK, p * PAIR:(p + 1) * PAIR]

    n_chunks = tr // CHUNK
    units =[(c, p) for c in range(n_chunks) for p in range(N_PAIRS)]
    nu = range(len(units))
    wc = [wend_ref[0, c:c + 1, p * PAIR:(p + 1) * PAIR] for c, p in units]
    rt, at, bt, kt, xv, bh, kh = ([tile(name, c, p) for c, p in units]
                                  for name in ("rt", "at", "bt", "kt", "xv", "bh", "kh"))
    bt_s = [stack(x) for x in bt]
    kt_s = [stack(x) for x in kt]
    xv_s = [stack(x) for x in xv]

    ar = [jnp.concatenate([at[u], rt[u]], axis=0) for u in nu]
    x_b = [_dot_nt(ar[u], bt_s[u]) for u in nu]
    x_k = [_dot_nt(ar[u], kt_s[u]) for u in nu]
    a_ab = [jnp.where(strict, x[:CHUNK], 0.0) for x in x_b]
    a_rb = [jnp.where(incl, x[CHUNK:], 0.0).astype(BF16) for x in x_b]
    a_ak = [jnp.where(strict, x[:CHUNK], 0.0).astype(BF16) for x in x_k]
    a_rk = [jnp.where(incl, x[CHUNK:], 0.0).astype(BF16) for x in x_k]

    ab = [a.astype(BF16) for a in a_ab]
    pw = [_dot(ab[u], stack(ab[u])) for u in nu]
    tinv = [eye_sbs + a_ab[u] for u in nu]
    for level in range(1, 6):
        pb = [x.astype(BF16) for x in pw]
        if level < 5:
            both = [_dot(jnp.concatenate([tinv[u].astype(BF16), pb[u]], axis=0), stack(pb[u])) for u in nu]
            tinv = [tinv[u] + both[u][:CHUNK] for u in nu]
            pw = [both[u][CHUNK:] for u in nu]
        else:
            tinv = [tinv[u] + _dot(tinv[u].astype(BF16), stack(pb[u])) for u in nu]
    tb = [x.astype(BF16) for x in tinv]

    ap = [_dot(tb[u], stack(at[u])).astype(BF16) for u in nu]
    akv = [_dot(a_ak[u], xv_s[u]).astype(BF16) for u in nu]
    vp = [_dot(tb[u], stack(akv[u])).astype(BF16) for u in nu]
    ap_s = [stack(x) for x in ap]
    vp_s = [stack(x) for x in vp]
    m_sbs = [fold(jnp.where(eye_sq, wc[u], 0.0) + _dot_tn(bh[u], ap[u])).astype(BF16) for u in nu]
    g_sbs = [fold(_dot_tn(jnp.concatenate([bh[u], kh[u]], axis=0), jnp.concatenate([vp[u], xv[u]], axis=0)))
             for u in nu]
    rp = [(rt[u].astype(F32) + _dot(a_rb[u], ap_s[u])).astype(BF16) for u in nu]
    y0 = [_dot(jnp.concatenate([a_rb[u], a_rk[u]], axis=1), jnp.concatenate([vp_s[u], xv_s[u]], axis=0))
          for u in nu]

    for u, (c, p) in enumerate(units):
        z_bd = stack(state_ref[p].astype(BF16))
        both = _dot(jnp.concatenate([rp[u], m_sbs[u]], axis=0), z_bd)
        yacc_ref[c * CHUNK:(c + 1) * CHUNK, p * PAIR:(p + 1) * PAIR] = both[:CHUNK] + y0[u]
        state_ref[p] = both[CHUNK:] + g_sbs[u]

    @pl.when(last)
    def _():
        zf_ref[0] = state_ref[...]

    inv_n = 1.0 / HEAD_DIM
    y = yacc_ref[...]
    mean = seg_sum(y) * inv_n
    d = y - mean
    var = seg_sum(d * d) * inv_n
    yn = d * lax.rsqrt(var + GN_EPS) * gnw_ref[...] + gnb_ref[...]
    z = z_ref[0].astype(F32)
    y_rwkv = ((yn + bonus_ref[0].astype(F32)) * (z * _sigmoid(z))).astype(BF16)
    if project:
        mix = _dot(yf_ref[0], wout_ref[:D_BRANCH, :]) + _dot(y_rwkv, wout_ref[D_BRANCH:, :])
        h = x_ref[0] + mix
        out_ref[0] = h * lax.rsqrt(jnp.mean(h * h, axis=-1, keepdims=True) + NORM_EPS) * fnw_ref[...]
    else:
        out_ref[0] = y_rwkv


def _rwkv(ops, z, wend, gn_w, gn_b, ones_bd, z0, tr, finish=None):
    b, l, _ = z.shape
    wide = pl.BlockSpec((1, tr, D_BRANCH), lambda bi, t: (bi, t, 0))
    row_w = pl.BlockSpec((1, D_BRANCH), lambda bi, t: (0, 0))
    ones = pl.BlockSpec((D_BRANCH, D_BRANCH), lambda bi, t: (0, 0))
    st_in = pl.BlockSpec((N_PAIRS, CHUNK, PAIR), lambda bi, t: (0, 0, 0))
    st_out = pl.BlockSpec((1, N_PAIRS, CHUNK, PAIR), lambda bi, t: (bi, 0, 0, 0))
    in_specs = ([wide] * (len(RWKV_OPERANDS) + 1)
                + [pl.BlockSpec((1, tr // CHUNK, D_BRANCH), lambda bi, t: (bi, t, 0)), row_w, row_w, ones, st_in])
    args = [*ops, z, wend, gn_w, gn_b, ones_bd, z0]
    if finish is None:
        out_spec, out_sds = wide, jax.ShapeDtypeStruct((b, l, D_BRANCH), BF16)
    else:
        full = pl.BlockSpec((1, tr, D_MODEL), lambda bi, t: (bi, t, 0))
        in_specs += [wide, full, pl.BlockSpec((D_MODEL, D_MODEL), lambda bi, t: (0, 0)),
                     pl.BlockSpec((1, D_MODEL), lambda bi, t: (0, 0))]
        args += list(finish)
        out_spec, out_sds = full, jax.ShapeDtypeStruct((b, l, D_MODEL), F32)
    return pl.pallas_call(
        functools.partial(_rwkv_kernel, tr=tr, project=finish is not None),
        grid=(b, l // tr),
        in_specs=in_specs,
        out_specs=[out_spec, st_out],
        out_shape=[out_sds, jax.ShapeDtypeStruct((b, N_PAIRS, CHUNK, PAIR), F32)],
        scratch_shapes=[pltpu.VMEM((N_PAIRS, CHUNK, PAIR), F32), pltpu.VMEM((tr, D_BRANCH), F32)],
        compiler_params=_params(("parallel", "arbitrary")),
        name="rwkv",
    )(*args)


_SRC_COLS = (("q", D_BRANCH), ("k", D_BRANCH), ("v", D_BRANCH), ("fl", N_HEADS), ("zf", D_BRANCH),
             ("r", D_BRANCH), ("rk", D_BRANCH), ("rv", D_BRANCH), ("wd", RANK), ("ad", RANK), ("zr", D_BRANCH))
_DST_ORDER = ("q", "k", "v", "zf", "r", "rk", "rv", "zr", "wd", "ad", "fl")


def _repack_kernel(w_ref, o_ref):
    src, off = {}, 0
    for name, width in _SRC_COLS:
        src[name] = (off, width)
        off += width
    o_ref[:, COL_FL:] = jnp.zeros((o_ref.shape[0], LANES), BF16)
    dst = 0
    for name in _DST_ORDER:
        s, width = src[name]
        o_ref[:, dst:dst + width] = w_ref[:, s:s + width].astype(BF16)
        dst += width


def _repack_weights(w):
    rows = 256
    return pl.pallas_call(
        _repack_kernel,
        grid=(D_MODEL // rows,),
        in_specs=[pl.BlockSpec((None, rows, w.shape[2]), lambda i: (0, i, 0))],
        out_specs=pl.BlockSpec((rows, N_COLS), lambda i: (i, 0)),
        out_shape=jax.ShapeDtypeStruct((D_MODEL, N_COLS), BF16),
        compiler_params=_params(("parallel",)),
        name="repack_weights",
    )(w)


def _tiles(l):
    assert l % 512 == 0
    return 512, 512, 512


def kernel(x, meta, norm_w, w_in, b_f, mu_shift, w0, w_up, a0, a_up, k_k, k_a, r_k, gn_w, gn_b,
           w_out, final_norm_w):
    b, l, d = x.shape
    assert d == D_MODEL and norm_w.shape[0] == 1
    tm, t_attn, t_rwkv = _tiles(l)

    w_all = _repack_weights(w_in)

    row = lambda vec: vec.reshape(1, -1).astype(F32)
    mu = mu_shift[0]
    mu_r, mu_k, mu_v = (row(mu[i * D_BRANCH:(i + 1) * D_BRANCH]) for i in range(3))
    mu_wa = row(mu[3 * D_BRANCH:])
    zeros_up = jnp.zeros((RANK, D_BRANCH), F32)
    wup_pad = jnp.concatenate([w_up[0], zeros_up], axis=0)
    aup_pad = jnp.concatenate([zeros_up, a_up[0]], axis=0)
    hid = np.arange(D_BRANCH) // HEAD_DIM
    ones_bd = jnp.asarray(hid[:, None] == hid[None, :], BF16)
    shift_prm = (mu_r, mu_k, mu_v, mu_wa, row(w0[0]), wup_pad.astype(BF16), row(a0[0]), aup_pad.astype(BF16),
                 row(k_k[0]), row(k_a[0]), row(r_k[0]), ones_bd)
    gnw, gnb = row(gn_w[0]), row(gn_b[0])
    bf_pad = jnp.concatenate([b_f[0], jnp.zeros((LANES - N_HEADS,), F32)]).reshape(1, LANES)
    pq, pk = _select_matrices()
    nw = row(norm_w[0])
    n_ops = len(RWKV_OPERANDS)

    pre_rows = jnp.concatenate([jnp.zeros((PREFIX_ROWS - N_META, D_MODEL), F32), meta.astype(F32)], axis=0)
    zero_w, zero_n = jnp.zeros((1, D_BRANCH), F32), jnp.zeros((1, LANES), F32)
    pre = _in_proj(pre_rows, nw, w_all, shift_prm, (zero_w, zero_w, zero_w, zero_n),
                   (bf_pad, zero_n, pq, pk), PREFIX_ROWS, 1, PREFIX_ROWS - N_META)
    _, kpre, vpre, _, pzr_ = pre[:5]
    pre_ops, pre_wend = pre[5:5 + n_ops], pre[5 + n_ops]
    last_raw, c_pre = pre[6 + n_ops:10 + n_ops], pre[10 + n_ops]
    lead = lambda a: a[None]
    _, z_pre = _rwkv([lead(a) for a in pre_ops], lead(pzr_), lead(pre_wend), gnw, gnb, ones_bd,
                     jnp.zeros((N_PAIRS, CHUNK, PAIR), F32), PREFIX_ROWS)

    main = _in_proj(x.reshape(b * l, D_MODEL), nw, w_all, shift_prm, last_raw, (bf_pad, c_pre, pq, pk),
                    tm, l // tm, 0)
    qp, kp, vp, zf_, zr_ = main[:5]
    bl = lambda a: a.reshape(b, l, a.shape[-1])
    y_fox = _fox_attn(qp, kp, vp, kpre, vpre, bl(zf_), t_attn)
    out, _ = _rwkv([bl(a) for a in main[5:5 + n_ops]], bl(zr_),
                   main[5 + n_ops].reshape(b, l // CHUNK, D_BRANCH), gnw, gnb, ones_bd, z_pre[0], t_rwkv,
                   finish=(y_fox, x, w_out[0].astype(BF16), row(final_norm_w)))
    return out
```
